```python
import math
import jax, jax.numpy as jnp
from jax import lax
import numpy as np

D_MODEL = 1024
BATCH = 8
SEQ = 2048
DEPTH = 2
DEC_BATCH = 128
DEC_SEQ = 4
PAST_LEN = 16384
PAGE_SIZE = 128

MIX_WIDTH = 2 * D_MODEL
SSD_WIDTH = MIX_WIDTH // 2
SSD_HEAD_DIM = 64
SSD_HEADS = SSD_WIDTH // SSD_HEAD_DIM
SSD_GROUPS = 2
SSD_STATE = 128
SSD_CONV_DIM = SSD_WIDTH + 2 * SSD_GROUPS * SSD_STATE
GDN_WIDTH = MIX_WIDTH - SSD_WIDTH
GDN_HEAD_DIM = 128
GDN_HEADS = GDN_WIDTH // GDN_HEAD_DIM
GDN_KEY_WIDTH = GDN_HEADS * GDN_HEAD_DIM
GDN_CONV_DIM = 2 * GDN_KEY_WIDTH + GDN_WIDTH
CONV_K = 4
CHUNK = 64
NORM_EPS = 1e-6
ADA_DIM = 3 * D_MODEL
IN_SIZES = (SSD_WIDTH, SSD_CONV_DIM, SSD_HEADS, GDN_WIDTH, GDN_CONV_DIM, GDN_HEADS, GDN_HEADS)
IN_DIM = sum(IN_SIZES)

kernel_name = "hymba_ssd_gdn_adaln_step"


def _split(a, sizes):
    offs = np.cumsum(sizes)[:-1].tolist()
    return jnp.split(a, offs, axis=-1)


def _rmsnorm(x, w):
    xf = x.astype(jnp.float32)
    return xf * lax.rsqrt(jnp.mean(xf * xf, axis=-1, keepdims=True) + NORM_EPS) * w.astype(jnp.float32)


def _l2norm(x):
    return x * lax.rsqrt(jnp.sum(x * x, axis=-1, keepdims=True) + NORM_EPS)


def _chunk_len(L):
    return CHUNK if L % CHUNK == 0 else L


def _causal_conv(x, buf, w, b):
    L = x.shape[1]
    xp = jnp.concatenate([buf.astype(x.dtype), x], axis=1)
    y = xp[:, 0:L] * w[0]
    for k in range(1, CONV_K):
        y = y + xp[:, k:k + L] * w[k]
    if b is not None:
        y = y + b
    return jax.nn.silu(y), xp[:, L:]


def _ssd(x, dt, A, Bm, Cm, h0):
    Bsz, L, H, P = x.shape
    G, N = Bm.shape[2], Bm.shape[3]
    R = H // G
    Q = _chunk_len(L)
    nc = L // Q
    xc = (x * dt[..., None]).reshape(Bsz, nc, Q, G, R, P)
    a = (dt * A).reshape(Bsz, nc, Q, G, R)
    Bc = Bm.reshape(Bsz, nc, Q, G, N)
    Cc = Cm.reshape(Bsz, nc, Q, G, N)
    acum = jnp.cumsum(a, axis=2)
    tril = jnp.tril(jnp.ones((Q, Q), dtype=bool))[None, None, :, :, None, None]
    seg = acum[:, :, :, None] - acum[:, :, None, :]
    Lmat = jnp.exp(jnp.where(tril, seg, -jnp.inf))
    scores = jnp.einsum('bclgn,bcsgn->bcgls', Cc, Bc)
    y_diag = jnp.einsum('bcgls,bclsgr,bcsgrp->bclgrp', scores, Lmat, xc)
    decay_states = jnp.exp(acum[:, :, -1:] - acum)
    chunk_states = jnp.einsum('bclgn,bclgr,bclgrp->bcgrpn', Bc, decay_states, xc)
    chunk_decay = jnp.exp(acum[:, :, -1])

    def step(h, inp):
        s, d = inp
        return h * d[..., None, None] + s, h

    hT, h_in = lax.scan(step, h0.reshape(Bsz, G, R, P, N),
                        (jnp.moveaxis(chunk_states, 1, 0), jnp.moveaxis(chunk_decay, 1, 0)))
    h_in = jnp.moveaxis(h_in, 0, 1)
    y_off = jnp.einsum('bclgn,bcgrpn,bclgr->bclgrp', Cc, h_in, jnp.exp(acum))
    return (y_diag + y_off).reshape(Bsz, L, H, P), hT.reshape(Bsz, H, P, N)


def _gated_delta(q, k, v, g, beta, S0):
    Bsz, L, H, Dk = q.shape
    Dv = v.shape[-1]
    Q = _chunk_len(L)
    nc = L // Q
    to_c = lambda t: t.reshape(Bsz, nc, Q, H, -1).transpose(0, 3, 1, 2, 4)
    qc, kc, vc = to_c(q), to_c(k), to_c(v)
    gc = g.reshape(Bsz, nc, Q, H).transpose(0, 3, 1, 2)
    bc = beta.reshape(Bsz, nc, Q, H).transpose(0, 3, 1, 2)
    gcum = jnp.cumsum(gc, axis=-1)
    incl = jnp.tril(jnp.ones((Q, Q), dtype=bool))
    strict = jnp.tril(jnp.ones((Q, Q), dtype=bool), k=-1)
    decay = jnp.exp(jnp.where(incl, gcum[..., :, None] - gcum[..., None, :], -jnp.inf))
    kb = kc * bc[..., None]
    M = jnp.where(strict, jnp.einsum('bhcik,bhcjk->bhcij', kb, kc) * decay, 0.0)
    A = M + jnp.eye(Q, dtype=M.dtype)
    rhs = jnp.concatenate([vc * bc[..., None], kb * jnp.exp(gcum)[..., None]], axis=-1)
    sol = lax.linalg.triangular_solve(A, rhs, left_side=True, lower=True, unit_diagonal=True)
    u, w = sol[..., :Dv], sol[..., Dv:]
    attn = jnp.einsum('bhcik,bhcjk->bhcij', qc, kc) * decay
    q_dec = qc * jnp.exp(gcum)[..., None]
    k_dec = kc * jnp.exp(gcum[..., -1:] - gcum)[..., None]
    chunk_decay = jnp.exp(gcum[..., -1])

    def step(S, inp):
        u_c, w_c, at_c, qd_c, kd_c, d_c = inp
        v_new = u_c - jnp.einsum('bhik,bhkv->bhiv', w_c, S)
        o = jnp.einsum('bhik,bhkv->bhiv', qd_c, S) + jnp.einsum('bhij,bhjv->bhiv', at_c, v_new)
        S = S * d_c[..., None, None] + jnp.einsum('bhik,bhiv->bhkv', kd_c, v_new)
        return S, o

    mv = lambda t: jnp.moveaxis(t, 2, 0)
    ST, o = lax.scan(step, S0, (mv(u), mv(w), mv(attn), mv(q_dec), mv(k_dec), mv(chunk_decay)))
    o = o.transpose(1, 0, 3, 2, 4).reshape(Bsz, L, H, Dv)
    return o, ST


def _layer(x, c, conv_s, h_ssm, conv_g, S_gdn, norm_w, w_ada, b_ada, w_in, ssd_conv_w, ssd_conv_b,
           ssd_dt_bias, ssd_a_log, ssd_d, ssd_norm_w, gdn_conv_w, gdn_dt_bias, gdn_a_log, gdn_norm_w, w_out):
    f32 = jnp.float32
    Bsz, L, _ = x.shape
    mod = jax.nn.silu(c.astype(f32)) @ w_ada.astype(f32) + b_ada.astype(f32)
    shift, scale, gate = jnp.split(mod, 3, axis=-1)
    h = _rmsnorm(x, norm_w) * (1.0 + scale[:, None]) + shift[:, None]
    proj = h @ w_in.astype(f32)
    z_s, xbc_raw, dt_raw, z_g, qkv_raw, a_raw, b_raw = _split(proj, IN_SIZES)

    xbc, new_conv_s = _causal_conv(xbc_raw, conv_s.astype(f32), ssd_conv_w.astype(f32), ssd_conv_b.astype(f32))
    xs, Bm, Cm = _split(xbc, (SSD_WIDTH, SSD_GROUPS * SSD_STATE, SSD_GROUPS * SSD_STATE))
    dt = jax.nn.softplus(dt_raw + ssd_dt_bias.astype(f32))
    A = -jnp.exp(ssd_a_log.astype(f32))
    xh = xs.reshape(Bsz, L, SSD_HEADS, SSD_HEAD_DIM)
    y, new_h = _ssd(xh, dt, A, Bm.reshape(Bsz, L, SSD_GROUPS, SSD_STATE),
                    Cm.reshape(Bsz, L, SSD_GROUPS, SSD_STATE), h_ssm.astype(f32))
    y = (y + xh * ssd_d.astype(f32)[:, None]).reshape(Bsz, L, SSD_WIDTH) * jax.nn.silu(z_s)
    y = _rmsnorm(y.reshape(Bsz, L, SSD_GROUPS, SSD_WIDTH // SSD_GROUPS),
                 ssd_norm_w.reshape(SSD_GROUPS, SSD_WIDTH // SSD_GROUPS)).reshape(Bsz, L, SSD_WIDTH)

    qkv, new_conv_g = _causal_conv(qkv_raw, conv_g.astype(f32), gdn_conv_w.astype(f32), None)
    q, k, v = _split(qkv, (GDN_KEY_WIDTH, GDN_KEY_WIDTH, GDN_WIDTH))
    q = _l2norm(q.reshape(Bsz, L, GDN_HEADS, GDN_HEAD_DIM)) * (GDN_HEAD_DIM ** -0.5)
    k = _l2norm(k.reshape(Bsz, L, GDN_HEADS, GDN_HEAD_DIM))
    v = v.reshape(Bsz, L, GDN_HEADS, GDN_HEAD_DIM)
    beta = jax.nn.sigmoid(b_raw)
    g = -jnp.exp(gdn_a_log.astype(f32)) * jax.nn.softplus(a_raw + gdn_dt_bias.astype(f32))
    o, new_S = _gated_delta(q, k, v, g, beta, S_gdn.astype(f32))
    o = _rmsnorm(o, gdn_norm_w) * jax.nn.silu(z_g.reshape(Bsz, L, GDN_HEADS, GDN_HEAD_DIM))
    o = o.reshape(Bsz, L, GDN_WIDTH)

    mix = jnp.concatenate([y, o], axis=-1)
    out = x.astype(f32) + gate[:, None] * (mix @ w_out.astype(f32))
    return out.astype(x.dtype), new_conv_s, new_h, new_conv_g, new_S


def _trunk(x, c, states, weights, final_norm_w):
    new = ([], [], [], [])
    for l in range(DEPTH):
        x, *st = _layer(x, c, states[0][l], states[1][l], states[2][l], states[3][l], *[w[l] for w in weights])
        for lst, s, ref in zip(new, st, states):
            lst.append(s.astype(ref.dtype))
    y = _rmsnorm(x, final_norm_w).astype(x.dtype)
    return y, [jnp.stack(lst) for lst in new]


def setup_inputs(seed: int = 0) -> dict:
    key = jax.random.key(seed)
    ks = iter(jax.random.split(key, 32))
    f32 = jnp.float32
    nrm = lambda shape, s: jax.random.normal(next(ks), shape, f32) * s
    unif = lambda shape, lo, hi: jax.random.uniform(next(ks), shape, f32, lo, hi)

    def dt_bias(n):
        dt = jnp.exp(unif((DEPTH, n), math.log(1e-3), math.log(1e-1)))
        return dt + jnp.log(-jnp.expm1(-dt))

    return {
        "x_prompt": nrm((BATCH, SEQ, D_MODEL), 1.0),
        "x_sample": nrm((DEC_BATCH, DEC_SEQ, D_MODEL), 1.0),
        "state_ssd_conv": nrm((DEPTH, DEC_BATCH, CONV_K - 1, SSD_CONV_DIM), 1.0),
        "state_ssm": nrm((DEPTH, DEC_BATCH, SSD_HEADS, SSD_HEAD_DIM, SSD_STATE), 0.5),
        "state_gdn_conv": nrm((DEPTH, DEC_BATCH, CONV_K - 1, GDN_CONV_DIM), 1.0),
        "state_gdn": nrm((DEPTH, DEC_BATCH, GDN_HEADS, GDN_HEAD_DIM, GDN_HEAD_DIM), 0.5),
        "c_prompt": nrm((BATCH, D_MODEL), 1.0),
        "c_sample": nrm((DEC_BATCH, D_MODEL), 1.0),
        "norm_w": 1.0 + nrm((DEPTH, D_MODEL), 0.02),
        "w_ada": nrm((DEPTH, D_MODEL, ADA_DIM), 0.5 * D_MODEL ** -0.5),
        "b_ada": nrm((DEPTH, ADA_DIM), 0.02),
        "w_in": nrm((DEPTH, D_MODEL, IN_DIM), D_MODEL ** -0.5),
        "ssd_conv_w": nrm((DEPTH, CONV_K, SSD_CONV_DIM), CONV_K ** -0.5),
        "ssd_conv_b": nrm((DEPTH, SSD_CONV_DIM), 0.02),
        "ssd_dt_bias": dt_bias(SSD_HEADS),
        "ssd_a_log": jnp.log(unif((DEPTH, SSD_HEADS), 1.0, 16.0)),
        "ssd_d": 1.0 + nrm((DEPTH, SSD_HEADS), 0.02),
        "ssd_norm_w": 1.0 + nrm((DEPTH, SSD_WIDTH), 0.02),
        "gdn_conv_w": nrm((DEPTH, CONV_K, GDN_CONV_DIM), CONV_K ** -0.5),
        "gdn_dt_bias": dt_bias(GDN_HEADS),
        "gdn_a_log": jnp.log(unif((DEPTH, GDN_HEADS), 1.0, 16.0)),
        "gdn_norm_w": 1.0 + nrm((DEPTH, GDN_HEAD_DIM), 0.02),
        "w_out": nrm((DEPTH, MIX_WIDTH, D_MODEL), MIX_WIDTH ** -0.5),
        "final_norm_w": 1.0 + nrm((D_MODEL,), 0.02),
    }


def reference(x_prompt, x_sample, state_ssd_conv, state_ssm, state_gdn_conv, state_gdn, c_prompt, c_sample,
              norm_w, w_ada, b_ada, w_in, ssd_conv_w, ssd_conv_b, ssd_dt_bias, ssd_a_log, ssd_d, ssd_norm_w,
              gdn_conv_w, gdn_dt_bias, gdn_a_log, gdn_norm_w, w_out, final_norm_w):
    weights = (norm_w, w_ada, b_ada, w_in, ssd_conv_w, ssd_conv_b, ssd_dt_bias, ssd_a_log, ssd_d, ssd_norm_w,
               gdn_conv_w, gdn_dt_bias, gdn_a_log, gdn_norm_w, w_out)
    bp = x_prompt.shape[0]
    zero_states = (
        jnp.zeros((DEPTH, bp) + state_ssd_conv.shape[2:], state_ssd_conv.dtype),
        jnp.zeros((DEPTH, bp) + state_ssm.shape[2:], state_ssm.dtype),
        jnp.zeros((DEPTH, bp) + state_gdn_conv.shape[2:], state_gdn_conv.dtype),
        jnp.zeros((DEPTH, bp) + state_gdn.shape[2:], state_gdn.dtype),
    )
    y_prompt, sp = _trunk(x_prompt, c_prompt, zero_states, weights, final_norm_w)
    y_sample, ss = _trunk(x_sample, c_sample, (state_ssd_conv, state_ssm, state_gdn_conv, state_gdn),
                          weights, final_norm_w)
    return (y_prompt, y_sample, sp[0], sp[1], sp[2], sp[3], ss[0], ss[1], ss[2], ss[3])
```

```python
import functools
import math

import jax
import jax.numpy as jnp
from jax import lax
from jax.experimental import pallas as pl
from jax.experimental.pallas import tpu as pltpu

F32 = jnp.float32
BF16 = jnp.bfloat16

D_MODEL = 1024
DEPTH = 2
SSD_WIDTH = 1024
SSD_HEAD_DIM = 64
SSD_HEADS = 16
SSD_GROUPS = 2
SSD_STATE = 128
SSD_CONV_DIM = SSD_WIDTH + 2 * SSD_GROUPS * SSD_STATE
GDN_WIDTH = 1024
GDN_HEAD_DIM = 128
GDN_HEADS = 8
GDN_CONV_DIM = 3 * GDN_WIDTH
MIX_WIDTH = SSD_WIDTH + GDN_WIDTH
CONV_K = 4
NORM_EPS = 1e-6
ADA_DIM = 3 * D_MODEL
SMALL = 128
LANES = 128
SUBLANES = 8
INV_BLOCK = 16
VMEM_LIMIT = 56 * 1024 * 1024

_OFF_ZS = 0
_OFF_XBC = _OFF_ZS + SSD_WIDTH
_OFF_DT = _OFF_XBC + SSD_CONV_DIM
_OFF_ZG = _OFF_DT + SSD_HEADS
_OFF_QKV = _OFF_ZG + GDN_WIDTH
_OFF_A = _OFF_QKV + GDN_CONV_DIM
_OFF_B = _OFF_A + GDN_HEADS
_SM_DT = 0
_SM_A = SSD_HEADS
_SM_B = SSD_HEADS + GDN_HEADS


def _bdot(a, b):
    return jnp.dot(a.astype(BF16), b.astype(BF16), preferred_element_type=F32)


def _bdot_nt(a, b):
    return lax.dot_general(a.astype(BF16), b.astype(BF16), (((1,), (1,)), ((), ())),
                           preferred_element_type=F32)


def _terms(x, n):
    out = []
    r = x
    for i in range(n):
        t = r.astype(BF16)
        out.append(t)
        if i + 1 < n:
            r = r - t.astype(F32)
    return out


def _const_dot(c, x, n):
    acc = None
    for t in _terms(x, n):
        p = jnp.dot(c, t, preferred_element_type=F32)
        acc = p if acc is None else acc + p
    return acc


def _dot_const(x, c, n):
    acc = None
    for t in _terms(x, n):
        p = jnp.dot(t, c, preferred_element_type=F32)
        acc = p if acc is None else acc + p
    return acc


def _transpose_via_identity(eye, x, n):
    acc = None
    for t in _terms(x, n):
        p = lax.dot_general(eye, t, (((1,), (1,)), ((), ())), preferred_element_type=F32)
        acc = p if acc is None else acc + p
    return acc


def _softplus(x):
    return jnp.maximum(x, 0.0) + jnp.log1p(jnp.exp(-jnp.abs(x)))


def _silu(x):
    return x * jax.nn.sigmoid(x)


def _ada_kernel(c_ref, w_ref, b_ref, o_ref):
    s = _silu(c_ref[...])
    w = w_ref[0]
    s_hi, s_lo = _terms(s, 2)
    w_hi, w_lo = _terms(w, 2)
    acc = jnp.dot(s_hi, w_hi, preferred_element_type=F32)
    acc = acc + (jnp.dot(s_hi, w_lo, preferred_element_type=F32)
                 + jnp.dot(s_lo, w_hi, preferred_element_type=F32))
    o_ref[0] = acc + b_ref[0]


def _ada_call(c_all, w_ada, b_ada):
    rows = c_all.shape[0]
    tn = D_MODEL
    return pl.pallas_call(
        _ada_kernel,
        grid=(DEPTH, ADA_DIM // tn),
        in_specs=[
            pl.BlockSpec((rows, D_MODEL), lambda l, j: (0, 0)),
            pl.BlockSpec((1, D_MODEL, tn), lambda l, j: (l, 0, j)),
            pl.BlockSpec((1, 1, tn), lambda l, j: (l, 0, j)),
        ],
        out_specs=pl.BlockSpec((1, rows, tn), lambda l, j: (l, 0, j)),
        out_shape=jax.ShapeDtypeStruct((DEPTH, rows, ADA_DIM), F32),
        compiler_params=pltpu.CompilerParams(
            dimension_semantics=("arbitrary", "arbitrary"), vmem_limit_bytes=VMEM_LIMIT),
        name="ada_mod",
    )(c_all, w_ada, b_ada.reshape(DEPTH, 1, ADA_DIM))


def _inproj_kernel(x_ref, sc_ref, sh_ref, nw_ref, wzs_ref, wxbc_ref, wzg_ref, wqkv_ref, wsh_ref, wsl_ref,
                   zs_ref, xbc_ref, zg_ref, qkv_ref, sm_ref):
    x = x_ref[...]
    ms = jnp.mean(x * x, axis=-1, keepdims=True)
    h = x * lax.rsqrt(ms + NORM_EPS) * nw_ref[...]
    h = h * (1.0 + sc_ref[0]) + sh_ref[0]
    h_hi, h_lo = _terms(h, 2)
    zs_ref[...] = jnp.dot(h_hi, wzs_ref[...], preferred_element_type=F32)
    xbc_ref[...] = jnp.dot(h_hi, wxbc_ref[...], preferred_element_type=F32)
    zg_ref[...] = jnp.dot(h_hi, wzg_ref[...], preferred_element_type=F32)
    qkv_ref[...] = jnp.dot(h_hi, wqkv_ref[...], preferred_element_type=F32)
    sm = jnp.dot(h_hi, wsh_ref[...], preferred_element_type=F32)
    sm = sm + (jnp.dot(h_hi, wsl_ref[...], preferred_element_type=F32)
               + jnp.dot(h_lo, wsh_ref[...], preferred_element_type=F32))
    sm_ref[...] = sm


def _row_mod_spec(per_row, tm, steps_per_group):
    if per_row:
        return pl.BlockSpec((1, tm, D_MODEL), lambda i: (0, i, 0))
    return pl.BlockSpec((1, 1, D_MODEL), lambda i: (i // steps_per_group, 0, 0))


def _resident(shape):
    nd = len(shape)
    return pl.BlockSpec(shape, lambda i: (0,) * nd, pipeline_mode=pl.Buffered(1))


def _inproj_call(x2d, scale, shift, norm_w, w, tm, per_row, steps_per_group):
    T = x2d.shape[0]
    assert T % tm == 0
    widths = (SSD_WIDTH, SSD_CONV_DIM, GDN_WIDTH, GDN_CONV_DIM, SMALL)
    row = lambda wd: pl.BlockSpec((tm, wd), lambda i: (i, 0))
    return pl.pallas_call(
        _inproj_kernel,
        grid=(T // tm,),
        in_specs=[
            row(D_MODEL),
            _row_mod_spec(per_row, tm, steps_per_group),
            _row_mod_spec(per_row, tm, steps_per_group),
            _resident((1, D_MODEL)),
            _resident((D_MODEL, SSD_WIDTH)),
            _resident((D_MODEL, SSD_CONV_DIM)),
            _resident((D_MODEL, GDN_WIDTH)),
            _resident((D_MODEL, GDN_CONV_DIM)),
            _resident((D_MODEL, SMALL)),
            _resident((D_MODEL, SMALL)),
        ],
        out_specs=[row(wd) for wd in widths],
        out_shape=[jax.ShapeDtypeStruct((T, wd), F32) for wd in widths],
        compiler_params=pltpu.CompilerParams(
            dimension_semantics=("arbitrary",), vmem_limit_bytes=VMEM_LIMIT),
        name="in_proj",
    )(x2d, scale, shift, norm_w, w["zs"], w["xbc"], w["zg"], w["qkv"], w["sm_hi"], w["sm_lo"])


def _outproj_kernel(x_ref, mix_ref, gate_ref, w_ref, fw_ref, o_ref, *, final_norm):
    y = jnp.dot(mix_ref[...], w_ref[...], preferred_element_type=F32)
    out = x_ref[...] + gate_ref[0] * y
    if final_norm:
        ms = jnp.mean(out * out, axis=-1, keepdims=True)
        out = out * lax.rsqrt(ms + NORM_EPS) * fw_ref[...]
    o_ref[...] = out


def _outproj_call(x2d, mix2d, gate, w_out, final_w, tm, per_row, steps_per_group, final_norm):
    T = x2d.shape[0]
    return pl.pallas_call(
        functools.partial(_outproj_kernel, final_norm=final_norm),
        grid=(T // tm,),
        in_specs=[
            pl.BlockSpec((tm, D_MODEL), lambda i: (i, 0)),
            pl.BlockSpec((tm, MIX_WIDTH), lambda i: (i, 0)),
            _row_mod_spec(per_row, tm, steps_per_group),
            _resident((MIX_WIDTH, D_MODEL)),
            _resident((1, D_MODEL)),
        ],
        out_specs=pl.BlockSpec((tm, D_MODEL), lambda i: (i, 0)),
        out_shape=jax.ShapeDtypeStruct((T, D_MODEL), F32),
        compiler_params=pltpu.CompilerParams(
            dimension_semantics=("arbitrary",), vmem_limit_bytes=VMEM_LIMIT),
        name="out_proj",
    )(x2d, mix2d, gate, w_out, final_w)


def _unit_lower_solve(m_strict, rhs, q):
    nb = q // INV_BLOCK
    if nb > 1:
        r = lax.broadcasted_iota(jnp.int32, (q, q), 0) // INV_BLOCK
        c = lax.broadcasted_iota(jnp.int32, (q, q), 1) // INV_BLOCK
        d = jnp.where(r == c, m_strict, 0.0)
        o = m_strict - d
    else:
        d = m_strict
    n = -d
    e = d
    for _ in range(int(math.log2(INV_BLOCK)) - 1):
        e = _bdot(e, e)
        n = n + e + _bdot(n, e)
    y = rhs + _bdot(n, rhs)
    if nb > 1:
        p = o + _bdot(n, o)
        rr = -p
        e = p
        for _ in range(int(math.ceil(math.log2(nb))) - 1):
            e = _bdot(e, e)
            rr = rr + e + _bdot(rr, e)
        y = y + _bdot(rr, y)
    return y


def _mixer_kernel(zs_ref, xbc_ref, zg_ref, qkv_ref, sm_ref,
                  cs_ref, hs_ref, cg_ref, sg_ref,
                  scw_ref, scb_ref, gcw_ref, bias_ref, coef_ref, dexp_ref, snw_ref, gnw_ref,
                  mix_ref, ncs_ref, nhs_ref, ncg_ref, nsg_ref,
                  xps, xpg, xbc_c, qkv_c, st_s, st_g, y_s, pzs, pzg, psm,
                  *, q, lb, lv, nc):
    c_idx = pl.program_id(1)
    padded = lb < q
    pair = 2 * SSD_HEAD_DIM
    heads_per_group = SSD_HEADS // SSD_GROUPS
    gw = SSD_WIDTH // SSD_GROUPS

    @pl.when(c_idx == 0)
    def _load_state():
        xps[0:SUBLANES, :] = jnp.zeros((SUBLANES, SSD_CONV_DIM), F32)
        xpg[0:SUBLANES, :] = jnp.zeros((SUBLANES, GDN_CONV_DIM), F32)
        xps[SUBLANES - (CONV_K - 1):SUBLANES, :] = cs_ref[0]
        xpg[SUBLANES - (CONV_K - 1):SUBLANES, :] = cg_ref[0]
        for j in range(SSD_HEADS // 2):
            blk = hs_ref[0, 2 * j:2 * j + 2].reshape(pair, SSD_STATE)
            st_s[:, pair * j:pair * (j + 1)] = blk.T
        st_g[...] = sg_ref[0]

    def conv(x_ref, xp, w_ref, b_ref, out, width):
        if padded:
            xp[SUBLANES:SUBLANES + lb, :] = x_ref[0]
            xp[SUBLANES + lb:SUBLANES + q, :] = jnp.zeros((q - lb, width), F32)
        else:
            xp[SUBLANES:SUBLANES + q, :] = x_ref[0]
        base = SUBLANES - (CONV_K - 1)
        cw = 512
        for j in range(width // cw):
            cols = slice(j * cw, (j + 1) * cw)
            acc = xp[base:base + q, cols] * w_ref[0:1, cols]
            for k in range(1, CONV_K):
                acc = acc + xp[base + k:base + k + q, cols] * w_ref[k:k + 1, cols]
            if b_ref is not None:
                acc = acc + b_ref[:, cols]
            out[:, cols] = _silu(acc)

    conv(xbc_ref, xps, scw_ref, scb_ref, xbc_c, SSD_CONV_DIM)
    conv(qkv_ref, xpg, gcw_ref, None, qkv_c, GDN_CONV_DIM)

    @pl.when(c_idx == nc - 1)
    def _store_conv_state():
        ncs_ref[0] = xps[SUBLANES + lv - (CONV_K - 1):SUBLANES + lv, :]
        ncg_ref[0] = xpg[SUBLANES + lv - (CONV_K - 1):SUBLANES + lv, :]

    if nc > 1:
        xps[0:SUBLANES, :] = xps[q:q + SUBLANES, :]
        xpg[0:SUBLANES, :] = xpg[q:q + SUBLANES, :]

    if padded:
        for src, dst, wd in ((zs_ref, pzs, SSD_WIDTH), (zg_ref, pzg, GDN_WIDTH), (sm_ref, psm, SMALL)):
            dst[0:lb, :] = src[0]
            dst[lb:q, :] = jnp.zeros((q - lb, wd), F32)
        zs_v, zg_v, sm = pzs, pzg, psm[...]
    else:
        zs_v, zg_v, sm = zs_ref.at[0], zg_ref.at[0], sm_ref[0]

    sp = _softplus(sm + bias_ref[...])
    beta_all = jax.nn.sigmoid(sm)
    if padded:
        valid = lax.broadcasted_iota(jnp.int32, (q, SMALL), 0) < lv
        sp = jnp.where(valid, sp, 0.0)
        beta_all = jnp.where(valid, beta_all, 0.0)
    ag = sp * coef_ref[...]
    rq = lax.broadcasted_iota(jnp.int32, (q, q), 0)
    cq = lax.broadcasted_iota(jnp.int32, (q, q), 1)
    incl = rq >= cq
    strict = rq > cq
    tril = jnp.where(incl, 1.0, 0.0).astype(BF16)
    r128 = lax.broadcasted_iota(jnp.int32, (LANES, LANES), 0)
    c128 = lax.broadcasted_iota(jnp.int32, (LANES, LANES), 1)
    eye = jnp.where(r128 == c128, 1.0, 0.0).astype(BF16)
    cum = _const_dot(tril, ag, 3)
    cum_t = _transpose_via_identity(eye, cum, 3)
    sp_t = _transpose_via_identity(eye, sp, 3)
    ecum = jnp.exp(cum)
    to_end = jnp.exp(cum[q - 1:q, :] - cum)
    er = lax.broadcasted_iota(jnp.int32, (SMALL, SSD_WIDTH), 0)
    ec = lax.broadcasted_iota(jnp.int32, (SMALL, SSD_WIDTH), 1) // SSD_HEAD_DIM
    expand = jnp.where(er == ec, 1.0, 0.0).astype(BF16)
    p_ecum = _dot_const(ecum, expand, 2)
    p_dtend = _dot_const(sp * to_end, expand, 2)

    lane = lax.broadcasted_iota(jnp.int32, (q, pair), 1)
    lo_half = lane < SSD_HEAD_DIM
    for g in range(SSD_GROUPS):
        gcols = slice(g * gw, (g + 1) * gw)
        b_g = xbc_c[:, SSD_WIDTH + g * SSD_STATE:SSD_WIDTH + (g + 1) * SSD_STATE]
        c_g = xbc_c[:, SSD_WIDTH + (SSD_GROUPS + g) * SSD_STATE:SSD_WIDTH + (SSD_GROUPS + g + 1) * SSD_STATE]
        scores = _bdot_nt(c_g, b_g)
        b_gt = _transpose_via_identity(eye, b_g, 1)
        state = st_s[:, gcols]
        y_off = _bdot(c_g, state) * p_ecum[:, gcols]
        xcd = xbc_c[:, gcols] * p_dtend[:, gcols]
        st_s[:, gcols] = state * p_ecum[q - 1:q, gcols] + _bdot(b_gt, xcd)
        for jj in range(heads_per_group // 2):
            j = g * (heads_per_group // 2) + jj
            pcols = slice(pair * j, pair * (j + 1))
            x_pair = xbc_c[:, pcols]
            y_pair = y_off[:, pair * jj:pair * (jj + 1)] + x_pair * dexp_ref[:, pcols]
            for half in range(2):
                h = 2 * j + half
                seg = cum[:, h:h + 1] - cum_t[h:h + 1, :]
                lmat = jnp.exp(jnp.where(incl, seg, -jnp.inf))
                m_h = scores * lmat * sp_t[h:h + 1, :]
                x_h = jnp.where(lo_half if half == 0 else jnp.logical_not(lo_half), x_pair, 0.0)
                y_pair = y_pair + _bdot(m_h, x_h)
            y_s[:, pcols] = y_pair
        y_g = y_s[:, gcols] * _silu(zs_v[:, gcols])
        ms = jnp.mean(y_g * y_g, axis=-1, keepdims=True)
        y_g = y_g * lax.rsqrt(ms + NORM_EPS) * snw_ref[:, gcols]
        mix_ref[0, :, gcols] = y_g[0:lb].astype(mix_ref.dtype)

    for h in range(GDN_HEADS):
        hc = slice(h * GDN_HEAD_DIM, (h + 1) * GDN_HEAD_DIM)
        qh = qkv_c[:, hc]
        kh = qkv_c[:, GDN_WIDTH + h * GDN_HEAD_DIM:GDN_WIDTH + (h + 1) * GDN_HEAD_DIM]
        vh = qkv_c[:, 2 * GDN_WIDTH + h * GDN_HEAD_DIM:2 * GDN_WIDTH + (h + 1) * GDN_HEAD_DIM]
        qh = qh * lax.rsqrt(jnp.sum(qh * qh, axis=-1, keepdims=True) + NORM_EPS) * (GDN_HEAD_DIM ** -0.5)
        kh = kh * lax.rsqrt(jnp.sum(kh * kh, axis=-1, keepdims=True) + NORM_EPS)
        la = _SM_A + h
        bcol = beta_all[:, _SM_B + h:_SM_B + h + 1]
        eg = ecum[:, la:la + 1]
        kb = kh * bcol
        dec = jnp.exp(jnp.where(incl, cum[:, la:la + 1] - cum_t[la:la + 1, :], -jnp.inf))
        kq = _bdot_nt(jnp.concatenate([kb, qh], axis=0), kh)
        m_strict = jnp.where(strict, kq[0:q] * dec, 0.0)
        attn = kq[q:2 * q] * dec
        rhs = jnp.concatenate([vh * bcol, kb * eg], axis=1)
        sol = _unit_lower_solve(m_strict, rhs, q)
        u = sol[:, 0:GDN_HEAD_DIM]
        w = sol[:, GDN_HEAD_DIM:2 * GDN_HEAD_DIM]
        s_h = st_g[h]
        ws_qs = _bdot(jnp.concatenate([w, qh * eg], axis=0), s_h)
        v_new = u - ws_qs[0:q]
        o = ws_qs[q:2 * q] + _bdot(attn, v_new)
        kd_t = _transpose_via_identity(eye, kh * to_end[:, la:la + 1], 1)
        st_g[h] = s_h * ecum[q - 1:q, la:la + 1] + _bdot(kd_t, v_new)
        ms = jnp.mean(o * o, axis=-1, keepdims=True)
        o = o * lax.rsqrt(ms + NORM_EPS) * gnw_ref[...] * _silu(zg_v[:, hc])
        mix_ref[0, :, SSD_WIDTH + h * GDN_HEAD_DIM:SSD_WIDTH + (h + 1) * GDN_HEAD_DIM] = (
            o[0:lb].astype(mix_ref.dtype))

    @pl.when(c_idx == nc - 1)
    def _store_state():
        for j in range(SSD_HEADS // 2):
            blk = st_s[:, pair * j:pair * (j + 1)].T
            nhs_ref[0, 2 * j:2 * j + 2] = blk.reshape(2, SSD_HEAD_DIM, SSD_STATE)
        nsg_ref[0] = st_g[...]


def _mixer_call(zs, xbc, zg, qkv, sm, conv_s, h_ssm, conv_g, s_gdn, p, q, lb):
    B, L, _ = zs.shape
    assert L % lb == 0 and lb <= q and q % INV_BLOCK == 0
    nc = L // lb
    lv = lb
    assert lv >= CONV_K - 1 and (nc == 1 or lb == q)
    tok = lambda wd: pl.BlockSpec((1, lb, wd), lambda b, c: (b, c, 0))
    per_b = lambda shape: pl.BlockSpec((1,) + shape, lambda b, c: (b,) + (0,) * len(shape))
    const = lambda shape: pl.BlockSpec(shape, lambda b, c: (0,) * len(shape))
    pad_rows = q if lb < q else SUBLANES
    out_shape = [
        jax.ShapeDtypeStruct((B, L, MIX_WIDTH), BF16),
        jax.ShapeDtypeStruct((B, CONV_K - 1, SSD_CONV_DIM), F32),
        jax.ShapeDtypeStruct((B, SSD_HEADS, SSD_HEAD_DIM, SSD_STATE), F32),
        jax.ShapeDtypeStruct((B, CONV_K - 1, GDN_CONV_DIM), F32),
        jax.ShapeDtypeStruct((B, GDN_HEADS, GDN_HEAD_DIM, GDN_HEAD_DIM), F32),
    ]
    return pl.pallas_call(
        functools.partial(_mixer_kernel, q=q, lb=lb, lv=lv, nc=nc),
        grid=(B, nc),
        in_specs=[
            tok(SSD_WIDTH), tok(SSD_CONV_DIM), tok(GDN_WIDTH), tok(GDN_CONV_DIM), tok(SMALL),
            per_b((CONV_K - 1, SSD_CONV_DIM)),
            per_b((SSD_HEADS, SSD_HEAD_DIM, SSD_STATE)),
            per_b((CONV_K - 1, GDN_CONV_DIM)),
            per_b((GDN_HEADS, GDN_HEAD_DIM, GDN_HEAD_DIM)),
            const((CONV_K, SSD_CONV_DIM)), const((1, SSD_CONV_DIM)), const((CONV_K, GDN_CONV_DIM)),
            const((1, SMALL)), const((1, SMALL)), const((1, SSD_WIDTH)), const((1, SSD_WIDTH)),
            const((1, GDN_HEAD_DIM)),
        ],
        out_specs=[
            tok(MIX_WIDTH),
            per_b((CONV_K - 1, SSD_CONV_DIM)),
            per_b((SSD_HEADS, SSD_HEAD_DIM, SSD_STATE)),
            per_b((CONV_K - 1, GDN_CONV_DIM)),
            per_b((GDN_HEADS, GDN_HEAD_DIM, GDN_HEAD_DIM)),
        ],
        out_shape=out_shape,
        scratch_shapes=[
            pltpu.VMEM((q + SUBLANES, SSD_CONV_DIM), F32),
            pltpu.VMEM((q + SUBLANES, GDN_CONV_DIM), F32),
            pltpu.VMEM((q, SSD_CONV_DIM), F32),
            pltpu.VMEM((q, GDN_CONV_DIM), F32),
            pltpu.VMEM((SSD_STATE, SSD_WIDTH), F32),
            pltpu.VMEM((GDN_HEADS, GDN_HEAD_DIM, GDN_HEAD_DIM), F32),
            pltpu.VMEM((q, SSD_WIDTH), F32),
            pltpu.VMEM((pad_rows, SSD_WIDTH), F32),
            pltpu.VMEM((pad_rows, GDN_WIDTH), F32),
            pltpu.VMEM((pad_rows, SMALL), F32),
        ],
        compiler_params=pltpu.CompilerParams(
            dimension_semantics=("arbitrary", "arbitrary"), vmem_limit_bytes=VMEM_LIMIT),
        name="mixer",
    )(zs, xbc, zg, qkv, sm, conv_s, h_ssm, conv_g, s_gdn,
      p["scw"], p["scb"], p["gcw"], p["bias"], p["coef"], p["dexp"], p["snw"], p["gnw"])


def _prep_layer(l, norm_w, w_in, ssd_conv_w, ssd_conv_b, ssd_dt_bias, ssd_a_log, ssd_d, ssd_norm_w,
                gdn_conv_w, gdn_dt_bias, gdn_a_log, gdn_norm_w, w_out):
    wi = w_in[l]
    small = jnp.concatenate(
        [wi[:, _OFF_DT:_OFF_DT + SSD_HEADS], wi[:, _OFF_A:_OFF_A + GDN_HEADS], wi[:, _OFF_B:_OFF_B + GDN_HEADS],
         jnp.zeros((D_MODEL, SMALL - SSD_HEADS - 2 * GDN_HEADS), F32)], axis=1)
    sm_hi = small.astype(BF16)
    sm_lo = (small - sm_hi.astype(F32)).astype(BF16)
    pad = jnp.zeros((SMALL - SSD_HEADS - GDN_HEADS,), F32)
    return {
        "norm_w": norm_w[l].reshape(1, D_MODEL),
        "zs": wi[:, _OFF_ZS:_OFF_ZS + SSD_WIDTH].astype(BF16),
        "xbc": wi[:, _OFF_XBC:_OFF_XBC + SSD_CONV_DIM].astype(BF16),
        "zg": wi[:, _OFF_ZG:_OFF_ZG + GDN_WIDTH].astype(BF16),
        "qkv": wi[:, _OFF_QKV:_OFF_QKV + GDN_CONV_DIM].astype(BF16),
        "sm_hi": sm_hi,
        "sm_lo": sm_lo,
        "scw": ssd_conv_w[l],
        "scb": ssd_conv_b[l].reshape(1, SSD_CONV_DIM),
        "gcw": gdn_conv_w[l],
        "bias": jnp.concatenate([ssd_dt_bias[l], gdn_dt_bias[l], pad]).reshape(1, SMALL),
        "coef": jnp.concatenate([-jnp.exp(ssd_a_log[l]), -jnp.exp(gdn_a_log[l]), pad]).reshape(1, SMALL),
        "dexp": jnp.repeat(ssd_d[l], SSD_HEAD_DIM).reshape(1, SSD_WIDTH),
        "snw": ssd_norm_w[l].reshape(1, SSD_WIDTH),
        "gnw": gdn_norm_w[l].reshape(1, GDN_HEAD_DIM),
        "w_out": w_out[l].astype(BF16),
    }


def _trunk(x, mod, states, params, final_w, q, lb, tm):
    B, L, _ = x.shape
    T = B * L
    per_row = L < SUBLANES
    steps_per_group = max(L // tm, 1)
    x2d = x.reshape(T, D_MODEL)
    new_states = ([], [], [], [])
    for l in range(DEPTH):
        p = params[l]
        shift, scale, gate = (mod[l, :, i * D_MODEL:(i + 1) * D_MODEL] for i in range(3))
        if per_row:
            rep = lambda a: jnp.repeat(a, L, axis=0).reshape(1, T, D_MODEL)
        else:
            rep = lambda a: a.reshape(B, 1, D_MODEL)
        zs, xbc, zg, qkv, sm = _inproj_call(x2d, rep(scale), rep(shift), p["norm_w"], p, tm, per_row,
                                            steps_per_group)
        r3 = lambda a: a.reshape(B, L, a.shape[-1])
        mix, ncs, nhs, ncg, nsg = _mixer_call(r3(zs), r3(xbc), r3(zg), r3(qkv), r3(sm),
                                              states[0][l], states[1][l], states[2][l], states[3][l], p, q, lb)
        x2d = _outproj_call(x2d, mix.reshape(T, MIX_WIDTH), rep(gate), p["w_out"], final_w, tm, per_row,
                            steps_per_group, final_norm=(l == DEPTH - 1))
        for lst, s in zip(new_states, (ncs, nhs, ncg, nsg)):
            lst.append(s)
    return x2d.reshape(B, L, D_MODEL), [jnp.stack(lst) for lst in new_states]


def kernel(x_prompt, x_sample, state_ssd_conv, state_ssm, state_gdn_conv, state_gdn, c_prompt, c_sample,
           norm_w, w_ada, b_ada, w_in, ssd_conv_w, ssd_conv_b, ssd_dt_bias, ssd_a_log, ssd_d, ssd_norm_w,
           gdn_conv_w, gdn_dt_bias, gdn_a_log, gdn_norm_w, w_out, final_norm_w):
    bp = x_prompt.shape[0]
    bs = x_sample.shape[0]
    params = [_prep_layer(l, norm_w, w_in, ssd_conv_w, ssd_conv_b, ssd_dt_bias, ssd_a_log, ssd_d, ssd_norm_w,
                          gdn_conv_w, gdn_dt_bias, gdn_a_log, gdn_norm_w, w_out) for l in range(DEPTH)]
    final_w = final_norm_w.reshape(1, D_MODEL)
    mod = _ada_call(jnp.concatenate([c_prompt, c_sample], axis=0), w_ada, b_ada)
    zero_states = (
        jnp.zeros((DEPTH, bp) + state_ssd_conv.shape[2:], state_ssd_conv.dtype),
        jnp.zeros((DEPTH, bp) + state_ssm.shape[2:], state_ssm.dtype),
        jnp.zeros((DEPTH, bp) + state_gdn_conv.shape[2:], state_gdn_conv.dtype),
        jnp.zeros((DEPTH, bp) + state_gdn.shape[2:], state_gdn.dtype),
    )
    y_p, sp = _trunk(x_prompt, mod[:, :bp], zero_states, params, final_w, q=64, lb=64, tm=512)
    y_s, ss = _trunk(x_sample, mod[:, bp:], (state_ssd_conv, state_ssm, state_gdn_conv, state_gdn),
                     params, final_w, q=16, lb=x_sample.shape[1], tm=x_sample.shape[0] * x_sample.shape[1])
    return (y_p, y_s, sp[0], sp[1], sp[2], sp[3], ss[0], ss[1], ss[2], ss[3])
```

```python
import functools
import math

import jax
import jax.numpy as jnp
from jax import lax
from jax.experimental import pallas as pl
from jax.experimental.pallas import tpu as pltpu

F32 = jnp.float32
BF16 = jnp.bfloat16

D_MODEL = 1024
DEPTH = 2
SSD_WIDTH = 1024
SSD_HEAD_DIM = 64
SSD_HEADS = 16
SSD_GROUPS = 2
SSD_STATE = 128
SSD_CONV_DIM = SSD_WIDTH + 2 * SSD_GROUPS * SSD_STATE
GDN_WIDTH = 1024
GDN_HEAD_DIM = 128
GDN_HEADS = 8
GDN_CONV_DIM = 3 * GDN_WIDTH
MIX_WIDTH = SSD_WIDTH + GDN_WIDTH
CONV_K = 4
NORM_EPS = 1e-6
ADA_DIM = 3 * D_MODEL
SMALL = 128
LANES = 128
SUBLANES = 8
INV_BLOCK = 16
VMEM_LIMIT = 56 * 1024 * 1024

_OFF_ZS = 0
_OFF_XBC = _OFF_ZS + SSD_WIDTH
_OFF_DT = _OFF_XBC + SSD_CONV_DIM
_OFF_ZG = _OFF_DT + SSD_HEADS
_OFF_QKV = _OFF_ZG + GDN_WIDTH
_OFF_A = _OFF_QKV + GDN_CONV_DIM
_OFF_B = _OFF_A + GDN_HEADS
_SM_DT = 0
_SM_A = SSD_HEADS
_SM_B = SSD_HEADS + GDN_HEADS


def _bdot(a, b):
    return jnp.dot(a.astype(BF16), b.astype(BF16), preferred_element_type=F32)


def _bdot_nt(a, b):
    return lax.dot_general(a.astype(BF16), b.astype(BF16), (((1,), (1,)), ((), ())),
                           preferred_element_type=F32)


def _terms(x, n):
    out = []
    r = x
    for i in range(n):
        t = r.astype(BF16)
        out.append(t)
        if i + 1 < n:
            r = r - t.astype(F32)
    return out


def _const_dot(c, x, n):
    acc = None
    for t in _terms(x, n):
        p = jnp.dot(c, t, preferred_element_type=F32)
        acc = p if acc is None else acc + p
    return acc


def _dot_const(x, c, n):
    acc = None
    for t in _terms(x, n):
        p = jnp.dot(t, c, preferred_element_type=F32)
        acc = p if acc is None else acc + p
    return acc


def _transpose_via_identity(eye, x, n):
    acc = None
    for t in _terms(x, n):
        p = lax.dot_general(eye, t, (((1,), (1,)), ((), ())), preferred_element_type=F32)
        acc = p if acc is None else acc + p
    return acc


def _softplus(x):
    return jnp.maximum(x, 0.0) + jnp.log1p(jnp.exp(-jnp.abs(x)))


def _silu(x):
    return x * jax.nn.sigmoid(x)


def _ada_kernel(c_ref, w_ref, b_ref, o_ref):
    s = _silu(c_ref[...])
    w = w_ref[0]
    s_hi, s_lo = _terms(s, 2)
    w_hi, w_lo = _terms(w, 2)
    acc = jnp.dot(s_hi, w_hi, preferred_element_type=F32)
    acc = acc + (jnp.dot(s_hi, w_lo, preferred_element_type=F32)
                 + jnp.dot(s_lo, w_hi, preferred_element_type=F32))
    o_ref[0] = acc + b_ref[0]


def _ada_call(c_all, w_ada, b_ada):
    rows = c_all.shape[0]
    tn = D_MODEL
    return pl.pallas_call(
        _ada_kernel,
        grid=(DEPTH, ADA_DIM // tn),
        in_specs=[
            pl.BlockSpec((rows, D_MODEL), lambda l, j: (0, 0)),
            pl.BlockSpec((1, D_MODEL, tn), lambda l, j: (l, 0, j)),
            pl.BlockSpec((1, 1, tn), lambda l, j: (l, 0, j)),
        ],
        out_specs=pl.BlockSpec((1, rows, tn), lambda l, j: (l, 0, j)),
        out_shape=jax.ShapeDtypeStruct((DEPTH, rows, ADA_DIM), F32),
        compiler_params=pltpu.CompilerParams(
            dimension_semantics=("arbitrary", "arbitrary"), vmem_limit_bytes=VMEM_LIMIT),
        name="ada_mod",
    )(c_all, w_ada, b_ada.reshape(DEPTH, 1, ADA_DIM))


def _inproj_kernel(x_ref, sc_ref, sh_ref, nw_ref, wzs_ref, wxbc_ref, wzg_ref, wqkv_ref, wsh_ref, wsl_ref,
                   zs_ref, xbc_ref, zg_ref, qkv_ref, sm_ref):
    x = x_ref[...]
    ms = jnp.mean(x * x, axis=-1, keepdims=True)
    h = x * lax.rsqrt(ms + NORM_EPS) * nw_ref[...]
    h = h * (1.0 + sc_ref[0]) + sh_ref[0]
    h_hi, h_lo = _terms(h, 2)
    zs_ref[...] = jnp.dot(h_hi, wzs_ref[...], preferred_element_type=F32)
    xbc_ref[...] = jnp.dot(h_hi, wxbc_ref[...], preferred_element_type=F32)
    zg_ref[...] = jnp.dot(h_hi, wzg_ref[...], preferred_element_type=F32)
    qkv_ref[...] = jnp.dot(h_hi, wqkv_ref[...], preferred_element_type=F32)
    sm = jnp.dot(h_hi, wsh_ref[...], preferred_element_type=F32)
    sm = sm + (jnp.dot(h_hi, wsl_ref[...], preferred_element_type=F32)
               + jnp.dot(h_lo, wsh_ref[...], preferred_element_type=F32))
    sm_ref[...] = sm


def _row_mod_spec(per_row, tm, steps_per_group):
    if per_row:
        return pl.BlockSpec((1, tm, D_MODEL), lambda i: (0, i, 0))
    return pl.BlockSpec((1, 1, D_MODEL), lambda i: (i // steps_per_group, 0, 0))


def _resident(shape):
    nd = len(shape)
    return pl.BlockSpec(shape, lambda i: (0,) * nd, pipeline_mode=pl.Buffered(1))


def _inproj_call(x2d, scale, shift, norm_w, w, tm, per_row, steps_per_group):
    T = x2d.shape[0]
    assert T % tm == 0
    widths = (SSD_WIDTH, SSD_CONV_DIM, GDN_WIDTH, GDN_CONV_DIM, SMALL)
    row = lambda wd: pl.BlockSpec((tm, wd), lambda i: (i, 0))
    return pl.pallas_call(
        _inproj_kernel,
        grid=(T // tm,),
        in_specs=[
            row(D_MODEL),
            _row_mod_spec(per_row, tm, steps_per_group),
            _row_mod_spec(per_row, tm, steps_per_group),
            _resident((1, D_MODEL)),
            _resident((D_MODEL, SSD_WIDTH)),
            _resident((D_MODEL, SSD_CONV_DIM)),
            _resident((D_MODEL, GDN_WIDTH)),
            _resident((D_MODEL, GDN_CONV_DIM)),
            _resident((D_MODEL, SMALL)),
            _resident((D_MODEL, SMALL)),
        ],
        out_specs=[row(wd) for wd in widths],
        out_shape=[jax.ShapeDtypeStruct((T, wd), F32) for wd in widths],
        compiler_params=pltpu.CompilerParams(
            dimension_semantics=("arbitrary",), vmem_limit_bytes=VMEM_LIMIT),
        name="in_proj",
    )(x2d, scale, shift, norm_w, w["zs"], w["xbc"], w["zg"], w["qkv"], w["sm_hi"], w["sm_lo"])


def _outproj_kernel(x_ref, mix_ref, gate_ref, w_ref, fw_ref, o_ref, *, final_norm):
    y = jnp.dot(mix_ref[...], w_ref[...], preferred_element_type=F32)
    out = x_ref[...] + gate_ref[0] * y
    if final_norm:
        ms = jnp.mean(out * out, axis=-1, keepdims=True)
        out = out * lax.rsqrt(ms + NORM_EPS) * fw_ref[...]
    o_ref[...] = out


def _outproj_call(x2d, mix2d, gate, w_out, final_w, tm, per_row, steps_per_group, final_norm):
    T = x2d.shape[0]
    return pl.pallas_call(
        functools.partial(_outproj_kernel, final_norm=final_norm),
        grid=(T // tm,),
        in_specs=[
            pl.BlockSpec((tm, D_MODEL), lambda i: (i, 0)),
            pl.BlockSpec((tm, MIX_WIDTH), lambda i: (i, 0)),
            _row_mod_spec(per_row, tm, steps_per_group),
            _resident((MIX_WIDTH, D_MODEL)),
            _resident((1, D_MODEL)),
        ],
        out_specs=pl.BlockSpec((tm, D_MODEL), lambda i: (i, 0)),
        out_shape=jax.ShapeDtypeStruct((T, D_MODEL), F32),
        compiler_params=pltpu.CompilerParams(
            dimension_semantics=("arbitrary",), vmem_limit_bytes=VMEM_LIMIT),
        name="out_proj",
    )(x2d, mix2d, gate, w_out, final_w)


def _unit_lower_solve_many(m_list, rhs_list, q):
    nb = q // INV_BLOCK
    if nb > 1:
        r = lax.broadcasted_iota(jnp.int32, (q, q), 0) // INV_BLOCK
        c = lax.broadcasted_iota(jnp.int32, (q, q), 1) // INV_BLOCK
        same = r == c
        d_list = [jnp.where(same, m, 0.0) for m in m_list]
        o_list = [m - d for m, d in zip(m_list, d_list)]
    else:
        d_list = m_list
    n_list = [-d for d in d_list]
    e_list = d_list
    for _ in range(int(math.log2(INV_BLOCK)) - 1):
        e_list = [_bdot(e, e) for e in e_list]
        n_list = [n + e + _bdot(n, e) for n, e in zip(n_list, e_list)]
    y_list = [rhs + _bdot(n, rhs) for n, rhs in zip(n_list, rhs_list)]
    if nb > 1:
        p_list = [o + _bdot(n, o) for n, o in zip(n_list, o_list)]
        r_list = [-p for p in p_list]
        e_list = p_list
        for _ in range(int(math.ceil(math.log2(nb))) - 1):
            e_list = [_bdot(e, e) for e in e_list]
            r_list = [rr + e + _bdot(rr, e) for rr, e in zip(r_list, e_list)]
        y_list = [y + _bdot(rr, y) for rr, y in zip(r_list, y_list)]
    return y_list


def _mixer_kernel(*refs, q, nq, lb, lv, nc, zero_init, n_alias, chains):
    refs = list(refs)
    zs_ref, xbc_ref, zg_ref, qkv_ref, sm_ref = refs[:5]
    pos = 5
    if not zero_init:
        cs_ref, hs_ref, cg_ref, sg_ref = refs[pos:pos + 4]
        pos += 4
    scw_ref, scb_ref, gcw_ref, bias_ref, coef_ref, dexp_ref, snw_ref, gnw_ref = refs[pos:pos + 8]
    pos += 8 + n_alias
    mix_ref, ncs_ref, nhs_ref, ncg_ref, nsg_ref = refs[pos:pos + 5]
    pos += 5
    (xps, xpg, xbc_c, qkv_c, st_s, st_g, y_s, pe_s, xcd_s, bgt_s, wq_s, u_s, at_s, kdt_s,
     pzs, pzg, psm) = refs[pos:]

    c_idx = pl.program_id(1)
    rr_ = nq * q
    padded = lb < rr_
    pair = 2 * SSD_HEAD_DIM
    heads_per_group = SSD_HEADS // SSD_GROUPS
    gw = SSD_WIDTH // SSD_GROUPS

    @pl.when(c_idx == 0)
    def _load_state():
        xps[0:SUBLANES, :] = jnp.zeros((SUBLANES, SSD_CONV_DIM), F32)
        xpg[0:SUBLANES, :] = jnp.zeros((SUBLANES, GDN_CONV_DIM), F32)
        if zero_init:
            st_s[...] = jnp.zeros(st_s.shape, F32)
            st_g[...] = jnp.zeros(st_g.shape, F32)
        else:
            xps[SUBLANES - (CONV_K - 1):SUBLANES, :] = cs_ref[0, 0]
            xpg[SUBLANES - (CONV_K - 1):SUBLANES, :] = cg_ref[0, 0]
            for j in range(SSD_HEADS // 2):
                blk = hs_ref[0, 0, 2 * j:2 * j + 2].reshape(pair, SSD_STATE)
                st_s[:, pair * j:pair * (j + 1)] = blk.T
            st_g[...] = sg_ref[0, 0]

    def conv(x_ref, xp, w_ref, b_ref, out, width):
        if padded:
            xp[SUBLANES:SUBLANES + lb, :] = x_ref[0]
            xp[SUBLANES + lb:SUBLANES + rr_, :] = jnp.zeros((rr_ - lb, width), F32)
        else:
            xp[SUBLANES:SUBLANES + rr_, :] = x_ref[0]
        base = SUBLANES - (CONV_K - 1)
        cw = 512
        rows_per = min(rr_, 64)
        for i in range(rr_ // rows_per):
            r0 = i * rows_per
            for j in range(width // cw):
                cols = slice(j * cw, (j + 1) * cw)
                acc = xp[base + r0:base + r0 + rows_per, cols] * w_ref[0:1, cols]
                for k in range(1, CONV_K):
                    acc = acc + xp[base + k + r0:base + k + r0 + rows_per, cols] * w_ref[k:k + 1, cols]
                if b_ref is not None:
                    acc = acc + b_ref[:, cols]
                out[r0:r0 + rows_per, cols] = _silu(acc)

    conv(xbc_ref, xps, scw_ref, scb_ref, xbc_c, SSD_CONV_DIM)
    conv(qkv_ref, xpg, gcw_ref, None, qkv_c, GDN_CONV_DIM)

    @pl.when(c_idx == nc - 1)
    def _store_conv_state():
        ncs_ref[0, 0] = xps[SUBLANES + lv - (CONV_K - 1):SUBLANES + lv, :]
        ncg_ref[0, 0] = xpg[SUBLANES + lv - (CONV_K - 1):SUBLANES + lv, :]

    if nc > 1:
        xps[0:SUBLANES, :] = xps[rr_:rr_ + SUBLANES, :]
        xpg[0:SUBLANES, :] = xpg[rr_:rr_ + SUBLANES, :]

    if padded:
        for src, dst, wd in ((zs_ref, pzs, SSD_WIDTH), (zg_ref, pzg, GDN_WIDTH), (sm_ref, psm, SMALL)):
            dst[0:lb, :] = src[0]
            dst[lb:rr_, :] = jnp.zeros((rr_ - lb, wd), F32)
        zs_v, zg_v, sm = pzs, pzg, psm[...]
    else:
        zs_v, zg_v, sm = zs_ref.at[0], zg_ref.at[0], sm_ref[0]

    sp = _softplus(sm + bias_ref[...])
    beta_all = jax.nn.sigmoid(sm)
    if padded:
        valid = lax.broadcasted_iota(jnp.int32, (rr_, SMALL), 0) < lv
        sp = jnp.where(valid, sp, 0.0)
        beta_all = jnp.where(valid, beta_all, 0.0)
    ag = sp * coef_ref[...]
    r_all = lax.broadcasted_iota(jnp.int32, (rr_, rr_), 0)
    c_all = lax.broadcasted_iota(jnp.int32, (rr_, rr_), 1)
    chunk_start = (r_all // q) * q
    tril_bd = jnp.where(r_all >= c_all, jnp.where(c_all >= chunk_start, 1.0, 0.0), 0.0).astype(BF16)
    rq = lax.broadcasted_iota(jnp.int32, (q, q), 0)
    cq = lax.broadcasted_iota(jnp.int32, (q, q), 1)
    incl = rq >= cq
    strict = rq > cq
    r128 = lax.broadcasted_iota(jnp.int32, (LANES, LANES), 0)
    c128 = lax.broadcasted_iota(jnp.int32, (LANES, LANES), 1)
    eye = jnp.where(r128 == c128, 1.0, 0.0).astype(BF16)
    cum = _const_dot(tril_bd, ag, 3)
    cum_t = _transpose_via_identity(eye, cum, 3)
    sp_t = _transpose_via_identity(eye, sp, 3)
    ecum = jnp.exp(cum)
    if nq > 1:
        cum_last = jnp.concatenate(
            [jnp.broadcast_to(cum[(c + 1) * q - 1:(c + 1) * q, :], (q, SMALL)) for c in range(nq)], axis=0)
    else:
        cum_last = cum[q - 1:q, :]
    to_end = jnp.exp(cum_last - cum)
    er = lax.broadcasted_iota(jnp.int32, (SMALL, SSD_WIDTH), 0)
    ec = lax.broadcasted_iota(jnp.int32, (SMALL, SSD_WIDTH), 1) // SSD_HEAD_DIM
    expand = jnp.where(er == ec, 1.0, 0.0).astype(BF16)
    pe_s[...] = _dot_const(ecum, expand, 2)
    p_dtend = _dot_const(sp * to_end, expand, 2)
    xcd_s[...] = (xbc_c[:, 0:SSD_WIDTH] * p_dtend).astype(BF16)

    lane = lax.broadcasted_iota(jnp.int32, (q, pair), 1)
    lo_half = lane < SSD_HEAD_DIM
    scores = {}
    for c in range(nq):
        rows = slice(c * q, (c + 1) * q)
        for g in range(SSD_GROUPS):
            b_g = xbc_c[rows, SSD_WIDTH + g * SSD_STATE:SSD_WIDTH + (g + 1) * SSD_STATE]
            c_g = xbc_c[rows, SSD_WIDTH + (SSD_GROUPS + g) * SSD_STATE:
                        SSD_WIDTH + (SSD_GROUPS + g + 1) * SSD_STATE]
            scores[c, g] = _bdot_nt(c_g, b_g)
            bgt_s[c * SSD_GROUPS + g] = _transpose_via_identity(eye, b_g, 1).astype(BF16)
    for c in range(nq):
        rows = slice(c * q, (c + 1) * q)
        for j in range(SSD_HEADS // 2):
            g = j // (heads_per_group // 2)
            pcols = slice(pair * j, pair * (j + 1))
            x_pair = xbc_c[rows, pcols]
            y_pair = x_pair * dexp_ref[:, pcols]
            for half in range(2):
                h = 2 * j + half
                seg = cum[rows, h:h + 1] - cum_t[h:h + 1, rows]
                lmat = jnp.exp(jnp.where(incl, seg, -jnp.inf))
                m_h = scores[c, g] * lmat * sp_t[h:h + 1, rows]
                x_h = jnp.where(lo_half if half == 0 else jnp.logical_not(lo_half), x_pair, 0.0)
                y_pair = y_pair + _bdot(m_h, x_h)
            y_s[rows, pcols] = y_pair

    all_ch = [(c, h) for c in range(nq) for h in range(GDN_HEADS)]
    for start in range(0, len(all_ch), chains):
        group = all_ch[start:start + chains]
        m_list, rhs_list = [], []
        for (c, h) in group:
            rows = slice(c * q, (c + 1) * q)
            i = c * GDN_HEADS + h
            qh = qkv_c[rows, h * GDN_HEAD_DIM:(h + 1) * GDN_HEAD_DIM]
            kh = qkv_c[rows, GDN_WIDTH + h * GDN_HEAD_DIM:GDN_WIDTH + (h + 1) * GDN_HEAD_DIM]
            vh = qkv_c[rows, 2 * GDN_WIDTH + h * GDN_HEAD_DIM:2 * GDN_WIDTH + (h + 1) * GDN_HEAD_DIM]
            qh = qh * lax.rsqrt(jnp.sum(qh * qh, axis=-1, keepdims=True) + NORM_EPS) * (GDN_HEAD_DIM ** -0.5)
            kh = kh * lax.rsqrt(jnp.sum(kh * kh, axis=-1, keepdims=True) + NORM_EPS)
            la = _SM_A + h
            bcol = beta_all[rows, _SM_B + h:_SM_B + h + 1]
            eg = ecum[rows, la:la + 1]
            kb = kh * bcol
            dec = jnp.exp(jnp.where(incl, cum[rows, la:la + 1] - cum_t[la:la + 1, rows], -jnp.inf))
            kq = _bdot_nt(jnp.concatenate([kb, qh], axis=0), kh)
            m_list.append(jnp.where(strict, kq[0:q] * dec, 0.0))
            at_s[i] = (kq[q:2 * q] * dec).astype(BF16)
            rhs_list.append(jnp.concatenate([vh * bcol, kb * eg], axis=1))
            wq_s[i, q:2 * q, :] = (qh * eg).astype(BF16)
            kdt_s[i] = _transpose_via_identity(eye, kh * to_end[rows, la:la + 1], 1).astype(BF16)
        sol_list = _unit_lower_solve_many(m_list, rhs_list, q)
        for (c, h), sol in zip(group, sol_list):
            i = c * GDN_HEADS + h
            u_s[i] = sol[:, 0:GDN_HEAD_DIM]
            wq_s[i, 0:q, :] = sol[:, GDN_HEAD_DIM:2 * GDN_HEAD_DIM].astype(BF16)

    for c in range(nq):
        rows = slice(c * q, (c + 1) * q)
        orow = slice(c * q, c * q + min(q, lb))
        last = (c + 1) * q - 1
        for g in range(SSD_GROUPS):
            gcols = slice(g * gw, (g + 1) * gw)
            c_g = xbc_c[rows, SSD_WIDTH + (SSD_GROUPS + g) * SSD_STATE:
                        SSD_WIDTH + (SSD_GROUPS + g + 1) * SSD_STATE]
            state = st_s[:, gcols]
            y_g = y_s[rows, gcols] + _bdot(c_g, state) * pe_s[rows, gcols]
            st_s[:, gcols] = state * pe_s[last:last + 1, gcols] + jnp.dot(
                bgt_s[c * SSD_GROUPS + g], xcd_s[rows, gcols], preferred_element_type=F32)
            y_g = y_g * _silu(zs_v[rows, gcols])
            ms = jnp.mean(y_g * y_g, axis=-1, keepdims=True)
            y_g = y_g * lax.rsqrt(ms + NORM_EPS) * snw_ref[:, gcols]
            mix_ref[0, orow, gcols] = y_g[0:min(q, lb)].astype(mix_ref.dtype)
        ws_list = []
        for h in range(GDN_HEADS):
            i = c * GDN_HEADS + h
            ws_list.append(jnp.dot(wq_s[i], st_g[h].astype(BF16), preferred_element_type=F32))
        vn_list = [u_s[c * GDN_HEADS + h] - ws_list[h][0:q] for h in range(GDN_HEADS)]
        for h in range(GDN_HEADS):
            i = c * GDN_HEADS + h
            la = _SM_A + h
            vn = vn_list[h].astype(BF16)
            o = ws_list[h][q:2 * q] + jnp.dot(at_s[i], vn, preferred_element_type=F32)
            st_g[h] = st_g[h] * ecum[last:last + 1, la:la + 1] + jnp.dot(kdt_s[i], vn,
                                                                         preferred_element_type=F32)
            hc = slice(h * GDN_HEAD_DIM, (h + 1) * GDN_HEAD_DIM)
            ms = jnp.mean(o * o, axis=-1, keepdims=True)
            o = o * lax.rsqrt(ms + NORM_EPS) * gnw_ref[...] * _silu(zg_v[rows, hc])
            mix_ref[0, orow, SSD_WIDTH + h * GDN_HEAD_DIM:SSD_WIDTH + (h + 1) * GDN_HEAD_DIM] = (
                o[0:min(q, lb)].astype(mix_ref.dtype))

    @pl.when(c_idx == nc - 1)
    def _store_state():
        for j in range(SSD_HEADS // 2):
            blk = st_s[:, pair * j:pair * (j + 1)].T
            nhs_ref[0, 0, 2 * j:2 * j + 2] = blk.reshape(2, SSD_HEAD_DIM, SSD_STATE)
        nsg_ref[0, 0] = st_g[...]


_STATE_TAILS = (
    (CONV_K - 1, SSD_CONV_DIM),
    (SSD_HEADS, SSD_HEAD_DIM, SSD_STATE),
    (CONV_K - 1, GDN_CONV_DIM),
    (GDN_HEADS, GDN_HEAD_DIM, GDN_HEAD_DIM),
)


def _mixer_call(zs, xbc, zg, qkv, sm, states_in, prev_out, p, q, nq, lb, layer, chains):
    B, L, _ = zs.shape
    rows = nq * q
    assert L % lb == 0 and lb <= rows and q % INV_BLOCK == 0
    assert lb == rows or nq == 1
    nc = L // lb
    lv = lb
    assert lv >= CONV_K - 1 and (nc == 1 or lb == rows)
    zero_init = states_in is None
    tok = lambda wd: pl.BlockSpec((1, lb, wd), lambda b, c: (b, c, 0))
    per_b = lambda shape: pl.BlockSpec((1, 1) + shape, lambda b, c: (layer, b) + (0,) * len(shape))
    const = lambda shape: pl.BlockSpec(shape, lambda b, c: (0,) * len(shape))
    pad_rows = rows if lb < rows else SUBLANES
    nch = nq * GDN_HEADS
    in_specs = [tok(SSD_WIDTH), tok(SSD_CONV_DIM), tok(GDN_WIDTH), tok(GDN_CONV_DIM), tok(SMALL)]
    args = [zs, xbc, zg, qkv, sm]
    if not zero_init:
        in_specs += [per_b(t) for t in _STATE_TAILS]
        args += list(states_in)
    in_specs += [
        const((CONV_K, SSD_CONV_DIM)), const((1, SSD_CONV_DIM)), const((CONV_K, GDN_CONV_DIM)),
        const((1, SMALL)), const((1, SMALL)), const((1, SSD_WIDTH)), const((1, SSD_WIDTH)),
        const((1, GDN_HEAD_DIM)),
    ]
    args += [p["scw"], p["scb"], p["gcw"], p["bias"], p["coef"], p["dexp"], p["snw"], p["gnw"]]
    aliases = {}
    n_alias = 0
    if prev_out is not None:
        n_alias = len(prev_out)
        for k, a in enumerate(prev_out):
            aliases[len(args)] = 1 + k
            in_specs.append(pl.BlockSpec(memory_space=pl.ANY))
            args.append(a)
    out_shape = [jax.ShapeDtypeStruct((B, L, MIX_WIDTH), BF16)] + [
        jax.ShapeDtypeStruct((DEPTH, B) + t, F32) for t in _STATE_TAILS]
    return pl.pallas_call(
        functools.partial(_mixer_kernel, q=q, nq=nq, lb=lb, lv=lv, nc=nc, zero_init=zero_init,
                          n_alias=n_alias, chains=chains),
        grid=(B, nc),
        in_specs=in_specs,
        out_specs=[tok(MIX_WIDTH)] + [per_b(t) for t in _STATE_TAILS],
        out_shape=out_shape,
        input_output_aliases=aliases,
        scratch_shapes=[
            pltpu.VMEM((rows + SUBLANES, SSD_CONV_DIM), F32),
            pltpu.VMEM((rows + SUBLANES, GDN_CONV_DIM), F32),
            pltpu.VMEM((rows, SSD_CONV_DIM), F32),
            pltpu.VMEM((rows, GDN_CONV_DIM), F32),
            pltpu.VMEM((SSD_STATE, SSD_WIDTH), F32),
            pltpu.VMEM((GDN_HEADS, GDN_HEAD_DIM, GDN_HEAD_DIM), F32),
            pltpu.VMEM((rows, SSD_WIDTH), F32),
            pltpu.VMEM((rows, SSD_WIDTH), F32),
            pltpu.VMEM((rows, SSD_WIDTH), BF16),
            pltpu.VMEM((nq * SSD_GROUPS, SSD_STATE, q), BF16),
            pltpu.VMEM((nch, 2 * q, GDN_HEAD_DIM), BF16),
            pltpu.VMEM((nch, q, GDN_HEAD_DIM), F32),
            pltpu.VMEM((nch, q, q), BF16),
            pltpu.VMEM((nch, GDN_HEAD_DIM, q), BF16),
            pltpu.VMEM((pad_rows, SSD_WIDTH), F32),
            pltpu.VMEM((pad_rows, GDN_WIDTH), F32),
            pltpu.VMEM((pad_rows, SMALL), F32),
        ],
        compiler_params=pltpu.CompilerParams(
            dimension_semantics=("arbitrary", "arbitrary"), vmem_limit_bytes=VMEM_LIMIT),
        name="mixer",
    )(*args)


def _prep_layer(l, norm_w, w_in, ssd_conv_w, ssd_conv_b, ssd_dt_bias, ssd_a_log, ssd_d, ssd_norm_w,
                gdn_conv_w, gdn_dt_bias, gdn_a_log, gdn_norm_w, w_out):
    wi = w_in[l]
    small = jnp.concatenate(
        [wi[:, _OFF_DT:_OFF_DT + SSD_HEADS], wi[:, _OFF_A:_OFF_A + GDN_HEADS], wi[:, _OFF_B:_OFF_B + GDN_HEADS],
         jnp.zeros((D_MODEL, SMALL - SSD_HEADS - 2 * GDN_HEADS), F32)], axis=1)
    sm_hi = small.astype(BF16)
    sm_lo = (small - sm_hi.astype(F32)).astype(BF16)
    pad = jnp.zeros((SMALL - SSD_HEADS - GDN_HEADS,), F32)
    return {
        "norm_w": norm_w[l].reshape(1, D_MODEL),
        "zs": wi[:, _OFF_ZS:_OFF_ZS + SSD_WIDTH].astype(BF16),
        "xbc": wi[:, _OFF_XBC:_OFF_XBC + SSD_CONV_DIM].astype(BF16),
        "zg": wi[:, _OFF_ZG:_OFF_ZG + GDN_WIDTH].astype(BF16),
        "qkv": wi[:, _OFF_QKV:_OFF_QKV + GDN_CONV_DIM].astype(BF16),
        "sm_hi": sm_hi,
        "sm_lo": sm_lo,
        "scw": ssd_conv_w[l],
        "scb": ssd_conv_b[l].reshape(1, SSD_CONV_DIM),
        "gcw": gdn_conv_w[l],
        "bias": jnp.concatenate([ssd_dt_bias[l], gdn_dt_bias[l], pad]).reshape(1, SMALL),
        "coef": jnp.concatenate([-jnp.exp(ssd_a_log[l]), -jnp.exp(gdn_a_log[l]), pad]).reshape(1, SMALL),
        "dexp": jnp.repeat(ssd_d[l], SSD_HEAD_DIM).reshape(1, SSD_WIDTH),
        "snw": ssd_norm_w[l].reshape(1, SSD_WIDTH),
        "gnw": gdn_norm_w[l].reshape(1, GDN_HEAD_DIM),
        "w_out": w_out[l].astype(BF16),
    }


def _trunk(x, mod, states, params, final_w, q, nq, lb, tm, chains):
    B, L, _ = x.shape
    T = B * L
    per_row = L < SUBLANES
    steps_per_group = max(L // tm, 1)
    x2d = x.reshape(T, D_MODEL)
    new_states = None
    for l in range(DEPTH):
        p = params[l]
        shift, scale, gate = (mod[l, :, i * D_MODEL:(i + 1) * D_MODEL] for i in range(3))
        if per_row:
            rep = lambda a: jnp.repeat(a, L, axis=0).reshape(1, T, D_MODEL)
        else:
            rep = lambda a: a.reshape(B, 1, D_MODEL)
        zs, xbc, zg, qkv, sm = _inproj_call(x2d, rep(scale), rep(shift), p["norm_w"], p, tm, per_row,
                                            steps_per_group)
        r3 = lambda a: a.reshape(B, L, a.shape[-1])
        mix, *new_states = _mixer_call(r3(zs), r3(xbc), r3(zg), r3(qkv), r3(sm), states, new_states, p,
                                       q, nq, lb, l, chains)
        x2d = _outproj_call(x2d, mix.reshape(T, MIX_WIDTH), rep(gate), p["w_out"], final_w, tm, per_row,
                            steps_per_group, final_norm=(l == DEPTH - 1))
    return x2d.reshape(B, L, D_MODEL), new_states


def kernel(x_prompt, x_sample, state_ssd_conv, state_ssm, state_gdn_conv, state_gdn, c_prompt, c_sample,
           norm_w, w_ada, b_ada, w_in, ssd_conv_w, ssd_conv_b, ssd_dt_bias, ssd_a_log, ssd_d, ssd_norm_w,
           gdn_conv_w, gdn_dt_bias, gdn_a_log, gdn_norm_w, w_out, final_norm_w):
    bp = x_prompt.shape[0]
    params = [_prep_layer(l, norm_w, w_in, ssd_conv_w, ssd_conv_b, ssd_dt_bias, ssd_a_log, ssd_d, ssd_norm_w,
                          gdn_conv_w, gdn_dt_bias, gdn_a_log, gdn_norm_w, w_out) for l in range(DEPTH)]
    final_w = final_norm_w.reshape(1, D_MODEL)
    mod = _ada_call(jnp.concatenate([c_prompt, c_sample], axis=0), w_ada, b_ada)
    y_p, sp = _trunk(x_prompt, mod[:, :bp], None, params, final_w, q=64, nq=4, lb=256, tm=512, chains=16)
    y_s, ss = _trunk(x_sample, mod[:, bp:], (state_ssd_conv, state_ssm, state_gdn_conv, state_gdn),
                     params, final_w, q=16, nq=1, lb=x_sample.shape[1],
                     tm=x_sample.shape[0] * x_sample.shape[1], chains=8)
    return (y_p, y_s, sp[0], sp[1], sp[2], sp[3], ss[0], ss[1], ss[2], ss[3])
```

```python
import functools
import math

import jax
import jax.numpy as jnp
from jax import lax
from jax.experimental import pallas as pl
from jax.experimental.pallas import tpu as pltpu

F32 = jnp.float32
BF16 = jnp.bfloat16

D_MODEL = 1024
DEPTH = 2
SSD_WIDTH = 1024
SSD_HEAD_DIM = 64
SSD_HEADS = 16
SSD_GROUPS = 2
SSD_STATE = 128
SSD_CONV_DIM = SSD_WIDTH + 2 * SSD_GROUPS * SSD_STATE
GDN_WIDTH = 1024
GDN_HEAD_DIM = 128
GDN_HEADS = 8
GDN_CONV_DIM = 3 * GDN_WIDTH
MIX_WIDTH = SSD_WIDTH + GDN_WIDTH
CONV_K = 4
NORM_EPS = 1e-6
ADA_DIM = 3 * D_MODEL
SMALL = 128
LANES = 128
SUBLANES = 8
INV_BLOCK = 16
VMEM_LIMIT = 56 * 1024 * 1024

_OFF_ZS = 0
_OFF_XBC = _OFF_ZS + SSD_WIDTH
_OFF_DT = _OFF_XBC + SSD_CONV_DIM
_OFF_ZG = _OFF_DT + SSD_HEADS
_OFF_QKV = _OFF_ZG + GDN_WIDTH
_OFF_A = _OFF_QKV + GDN_CONV_DIM
_OFF_B = _OFF_A + GDN_HEADS
_SM_DT = 0
_SM_A = SSD_HEADS
_SM_B = SSD_HEADS + GDN_HEADS


def _bdot(a, b):
    return jnp.dot(a.astype(BF16), b.astype(BF16), preferred_element_type=F32)


def _bdot_nt(a, b):
    return lax.dot_general(a.astype(BF16), b.astype(BF16), (((1,), (1,)), ((), ())),
                           preferred_element_type=F32)


def _terms(x, n):
    out = []
    r = x
    for i in range(n):
        t = r.astype(BF16)
        out.append(t)
        if i + 1 < n:
            r = r - t.astype(F32)
    return out


def _const_dot(c, x, n):
    acc = None
    for t in _terms(x, n):
        p = jnp.dot(c, t, preferred_element_type=F32)
        acc = p if acc is None else acc + p
    return acc


def _dot_const(x, c, n):
    acc = None
    for t in _terms(x, n):
        p = jnp.dot(t, c, preferred_element_type=F32)
        acc = p if acc is None else acc + p
    return acc


def _transpose_via_identity(eye, x, n):
    acc = None
    for t in _terms(x, n):
        p = lax.dot_general(eye, t, (((1,), (1,)), ((), ())), preferred_element_type=F32)
        acc = p if acc is None else acc + p
    return acc


def _softplus(x):
    return jnp.maximum(x, 0.0) + jnp.log1p(jnp.exp(-jnp.abs(x)))


def _silu(x):
    return x * jax.nn.sigmoid(x)


def _ada_kernel(c_ref, w_ref, b_ref, o_ref):
    s = _silu(c_ref[...])
    w = w_ref[0]
    s_hi, s_lo = _terms(s, 2)
    w_hi, w_lo = _terms(w, 2)
    acc = jnp.dot(s_hi, w_hi, preferred_element_type=F32)
    acc = acc + (jnp.dot(s_hi, w_lo, preferred_element_type=F32)
                 + jnp.dot(s_lo, w_hi, preferred_element_type=F32))
    o_ref[0] = acc + b_ref[0]


def _ada_call(c_all, w_ada, b_ada):
    rows = c_all.shape[0]
    tn = D_MODEL
    return pl.pallas_call(
        _ada_kernel,
        grid=(DEPTH, ADA_DIM // tn),
        in_specs=[
            pl.BlockSpec((rows, D_MODEL), lambda l, j: (0, 0)),
            pl.BlockSpec((1, D_MODEL, tn), lambda l, j: (l, 0, j)),
            pl.BlockSpec((1, 1, tn), lambda l, j: (l, 0, j)),
        ],
        out_specs=pl.BlockSpec((1, rows, tn), lambda l, j: (l, 0, j)),
        out_shape=jax.ShapeDtypeStruct((DEPTH, rows, ADA_DIM), F32),
        compiler_params=pltpu.CompilerParams(
            dimension_semantics=("arbitrary", "arbitrary"), vmem_limit_bytes=VMEM_LIMIT),
        name="ada_mod",
    )(c_all, w_ada, b_ada.reshape(DEPTH, 1, ADA_DIM))


CONV_ROWS = 64
CONV_COLS = 256
PROJ_COLS = 512


def _conv_silu_rows(xp, w_ref, b_ref, out, rows, c0, c1):
    rows_per = min(rows, CONV_ROWS)
    for i in range(rows // rows_per):
        r0 = i * rows_per
        for j in range((c1 - c0) // CONV_COLS):
            cols = slice(c0 + j * CONV_COLS, c0 + (j + 1) * CONV_COLS)
            x = xp[r0:r0 + rows_per + SUBLANES, cols]
            acc = x * w_ref[0:1, cols]
            for k in range(1, CONV_K):
                acc = pltpu.roll(acc, 1, 0) + x * w_ref[k:k + 1, cols]
            if b_ref is not None:
                acc = acc + b_ref[:, cols]
            out[r0:r0 + rows_per, cols] = _silu(acc[SUBLANES:, :]).astype(out.dtype)


def _inproj_kernel(*refs, tm, steps_per_seq, do_conv):
    x_ref, sc_ref, sh_ref, nw_ref, wzs_ref, wxbc_ref, wzg_ref, wqkv_ref, wsh_ref, wsl_ref = refs[:10]
    if do_conv:
        scw_ref, scb_ref, gcw_ref = refs[10:13]
        zs_ref, xbc_ref, zg_ref, qkv_ref, sm_ref, ncs_ref, ncg_ref, xps, xpg = refs[13:]
    else:
        zs_ref, xbc_ref, zg_ref, qkv_ref, sm_ref = refs[10:]
    x = x_ref[...]
    ms = jnp.mean(x * x, axis=-1, keepdims=True)
    h = x * lax.rsqrt(ms + NORM_EPS) * nw_ref[...]
    h = h * (1.0 + sc_ref[0]) + sh_ref[0]
    h_hi, h_lo = _terms(h, 2)
    sm = jnp.dot(h_hi, wsh_ref[...], preferred_element_type=F32)
    sm = sm + (jnp.dot(h_hi, wsl_ref[...], preferred_element_type=F32)
               + jnp.dot(h_lo, wsh_ref[...], preferred_element_type=F32))
    sm_ref[...] = sm
    if not do_conv:
        zs_ref[...] = jnp.dot(h_hi, wzs_ref[...], preferred_element_type=F32)
        xbc_ref[...] = jnp.dot(h_hi, wxbc_ref[...], preferred_element_type=F32)
        zg_ref[...] = jnp.dot(h_hi, wzg_ref[...], preferred_element_type=F32)
        qkv_ref[...] = jnp.dot(h_hi, wqkv_ref[...], preferred_element_type=F32)
        return

    @pl.when(pl.program_id(0) % steps_per_seq == 0)
    def _zero_history():
        xps[0:SUBLANES, :] = jnp.zeros((SUBLANES, SSD_CONV_DIM), F32)
        xpg[0:SUBLANES, :] = jnp.zeros((SUBLANES, GDN_CONV_DIM), F32)

    zs_ref[...] = _silu(jnp.dot(h_hi, wzs_ref[...], preferred_element_type=F32)).astype(zs_ref.dtype)
    zg_ref[...] = _silu(jnp.dot(h_hi, wzg_ref[...], preferred_element_type=F32)).astype(zg_ref.dtype)
    for xp, w_ref, cw_ref, cb_ref, out, nst, width in (
            (xps, wxbc_ref, scw_ref, scb_ref, xbc_ref, ncs_ref, SSD_CONV_DIM),
            (xpg, wqkv_ref, gcw_ref, None, qkv_ref, ncg_ref, GDN_CONV_DIM)):
        for c0 in range(0, width, PROJ_COLS):
            cols = slice(c0, c0 + PROJ_COLS)
            xp[SUBLANES:SUBLANES + tm, cols] = jnp.dot(h_hi, w_ref[:, cols], preferred_element_type=F32)
            _conv_silu_rows(xp, cw_ref, cb_ref, out, tm, c0, c0 + PROJ_COLS)
        nst[0] = xp[SUBLANES + tm - (CONV_K - 1):SUBLANES + tm, :]
        xp[0:SUBLANES, :] = xp[tm:tm + SUBLANES, :]


def _row_mod_spec(per_row, tm, steps_per_group):
    if per_row:
        return pl.BlockSpec((1, tm, D_MODEL), lambda i: (0, i, 0))
    return pl.BlockSpec((1, 1, D_MODEL), lambda i: (i // steps_per_group, 0, 0))


def _resident(shape):
    nd = len(shape)
    return pl.BlockSpec(shape, lambda i: (0,) * nd, pipeline_mode=pl.Buffered(1))


def _inproj_call(x2d, scale, shift, norm_w, w, tm, per_row, steps_per_group, do_conv, act_dtype):
    T = x2d.shape[0]
    assert T % tm == 0
    widths = (SSD_WIDTH, SSD_CONV_DIM, GDN_WIDTH, GDN_CONV_DIM, SMALL)
    dtypes = (act_dtype, act_dtype, act_dtype, act_dtype, F32)
    row = lambda wd: pl.BlockSpec((tm, wd), lambda i: (i, 0))
    in_specs = [
        row(D_MODEL),
        _row_mod_spec(per_row, tm, steps_per_group),
        _row_mod_spec(per_row, tm, steps_per_group),
        _resident((1, D_MODEL)),
        _resident((D_MODEL, SSD_WIDTH)),
        _resident((D_MODEL, SSD_CONV_DIM)),
        _resident((D_MODEL, GDN_WIDTH)),
        _resident((D_MODEL, GDN_CONV_DIM)),
        _resident((D_MODEL, SMALL)),
        _resident((D_MODEL, SMALL)),
    ]
    args = [x2d, scale, shift, norm_w, w["zs"], w["xbc"], w["zg"], w["qkv"], w["sm_hi"], w["sm_lo"]]
    out_specs = [row(wd) for wd in widths]
    out_shape = [jax.ShapeDtypeStruct((T, wd), dt) for wd, dt in zip(widths, dtypes)]
    scratch = []
    if do_conv:
        assert not per_row
        nseq = T // (tm * steps_per_group)
        in_specs += [_resident((CONV_K, SSD_CONV_DIM)), _resident((1, SSD_CONV_DIM)),
                     _resident((CONV_K, GDN_CONV_DIM))]
        args += [w["scw"], w["scb"], w["gcw"]]
        for wd in (SSD_CONV_DIM, GDN_CONV_DIM):
            out_specs.append(pl.BlockSpec((1, CONV_K - 1, wd), lambda i: (i // steps_per_group, 0, 0)))
            out_shape.append(jax.ShapeDtypeStruct((nseq, CONV_K - 1, wd), F32))
            scratch.append(pltpu.VMEM((tm + SUBLANES, wd), F32))
    return pl.pallas_call(
        functools.partial(_inproj_kernel, tm=tm, steps_per_seq=steps_per_group, do_conv=do_conv),
        grid=(T // tm,),
        in_specs=in_specs,
        out_specs=out_specs,
        out_shape=out_shape,
        scratch_shapes=scratch,
        compiler_params=pltpu.CompilerParams(
            dimension_semantics=("arbitrary",), vmem_limit_bytes=VMEM_LIMIT),
        name="in_proj",
    )(*args)


def _outproj_kernel(x_ref, mix_ref, gate_ref, w_ref, fw_ref, o_ref, *, final_norm):
    y = jnp.dot(mix_ref[...], w_ref[...], preferred_element_type=F32)
    out = x_ref[...] + gate_ref[0] * y
    if final_norm:
        ms = jnp.mean(out * out, axis=-1, keepdims=True)
        out = out * lax.rsqrt(ms + NORM_EPS) * fw_ref[...]
    o_ref[...] = out


def _outproj_call(x2d, mix2d, gate, w_out, final_w, tm, per_row, steps_per_group, final_norm):
    T = x2d.shape[0]
    return pl.pallas_call(
        functools.partial(_outproj_kernel, final_norm=final_norm),
        grid=(T // tm,),
        in_specs=[
            pl.BlockSpec((tm, D_MODEL), lambda i: (i, 0)),
            pl.BlockSpec((tm, MIX_WIDTH), lambda i: (i, 0)),
            _row_mod_spec(per_row, tm, steps_per_group),
            _resident((MIX_WIDTH, D_MODEL)),
            _resident((1, D_MODEL)),
        ],
        out_specs=pl.BlockSpec((tm, D_MODEL), lambda i: (i, 0)),
        out_shape=jax.ShapeDtypeStruct((T, D_MODEL), F32),
        compiler_params=pltpu.CompilerParams(
            dimension_semantics=("arbitrary",), vmem_limit_bytes=VMEM_LIMIT),
        name="out_proj",
    )(x2d, mix2d, gate, w_out, final_w)


def _unit_lower_solve_many(m_list, rhs_list, q):
    nb = q // INV_BLOCK
    if nb > 1:
        r = lax.broadcasted_iota(jnp.int32, (q, q), 0) // INV_BLOCK
        c = lax.broadcasted_iota(jnp.int32, (q, q), 1) // INV_BLOCK
        same = r == c
        d_list = [jnp.where(same, m, 0.0) for m in m_list]
        o_list = [m - d for m, d in zip(m_list, d_list)]
    else:
        d_list = m_list
    n_list = [-d for d in d_list]
    e_list = d_list
    for _ in range(int(math.log2(INV_BLOCK)) - 1):
        e_list = [_bdot(e, e) for e in e_list]
        n_list = [n + e + _bdot(n, e) for n, e in zip(n_list, e_list)]
    y_list = [rhs + _bdot(n, rhs) for n, rhs in zip(n_list, rhs_list)]
    if nb > 1:
        p_list = [o + _bdot(n, o) for n, o in zip(n_list, o_list)]
        r_list = [-p for p in p_list]
        e_list = p_list
        for _ in range(int(math.ceil(math.log2(nb))) - 1):
            e_list = [_bdot(e, e) for e in e_list]
            r_list = [rr + e + _bdot(rr, e) for rr, e in zip(r_list, e_list)]
        y_list = [y + _bdot(rr, y) for rr, y in zip(r_list, y_list)]
    return y_list


def _mixer_kernel(*refs, names, q, nq, lb, lv, nc, zero_init, do_conv, chains):
    r = dict(zip(names, refs))
    zs_ref, xbc_ref, zg_ref, qkv_ref, sm_ref = r["zs"], r["xbc"], r["zg"], r["qkv"], r["sm"]
    bias_ref, coef_ref, dexp_ref, snw_ref, gnw_ref = r["bias"], r["coef"], r["dexp"], r["snw"], r["gnw"]
    mix_ref, nhs_ref, nsg_ref = r["mix"], r["nhs"], r["nsg"]
    st_s, st_g, y_s, pe_s, xcd_s, bgt_s = r["st_s"], r["st_g"], r["y_s"], r["pe_s"], r["xcd_s"], r["bgt_s"]
    wq_s, u_s, at_s, kdt_s = r["wq_s"], r["u_s"], r["at_s"], r["kdt_s"]

    c_idx = pl.program_id(1)
    rr_ = nq * q
    padded = lb < rr_
    pair = 2 * SSD_HEAD_DIM
    heads_per_group = SSD_HEADS // SSD_GROUPS
    gw = SSD_WIDTH // SSD_GROUPS

    @pl.when(c_idx == 0)
    def _load_state():
        if do_conv:
            r["xps"][0:SUBLANES, :] = jnp.zeros((SUBLANES, SSD_CONV_DIM), F32)
            r["xpg"][0:SUBLANES, :] = jnp.zeros((SUBLANES, GDN_CONV_DIM), F32)
        if zero_init:
            st_s[...] = jnp.zeros(st_s.shape, F32)
            st_g[...] = jnp.zeros(st_g.shape, F32)
        else:
            if do_conv:
                r["xps"][SUBLANES - (CONV_K - 1):SUBLANES, :] = r["cs"][0, 0]
                r["xpg"][SUBLANES - (CONV_K - 1):SUBLANES, :] = r["cg"][0, 0]
            for j in range(SSD_HEADS // 2):
                blk = r["hs"][0, 0, 2 * j:2 * j + 2].reshape(pair, SSD_STATE)
                st_s[:, pair * j:pair * (j + 1)] = blk.T
            st_g[...] = r["sg"][0, 0]

    if do_conv:
        for x_ref, xp, w_ref, b_ref, out, nst, width in (
                (xbc_ref, r["xps"], r["scw"], r["scb"], r["xbc_c"], r["ncs"], SSD_CONV_DIM),
                (qkv_ref, r["xpg"], r["gcw"], None, r["qkv_c"], r["ncg"], GDN_CONV_DIM)):
            if padded:
                xp[SUBLANES:SUBLANES + lb, :] = x_ref[0]
                xp[SUBLANES + lb:SUBLANES + rr_, :] = jnp.zeros((rr_ - lb, width), F32)
            else:
                xp[SUBLANES:SUBLANES + rr_, :] = x_ref[0]
            _conv_silu_rows(xp, w_ref, b_ref, out, rr_, 0, width)

            @pl.when(c_idx == nc - 1)
            def _store_conv_state():
                nst[0, 0] = xp[SUBLANES + lv - (CONV_K - 1):SUBLANES + lv, :]

            if nc > 1:
                xp[0:SUBLANES, :] = xp[rr_:rr_ + SUBLANES, :]
        xbc_c, qkv_c = r["xbc_c"], r["qkv_c"]
        gate = _silu
    else:
        xbc_c, qkv_c = xbc_ref.at[0], qkv_ref.at[0]
        gate = lambda z: z.astype(F32)

    if padded:
        for src, dst, wd in ((zs_ref, r["pzs"], SSD_WIDTH), (zg_ref, r["pzg"], GDN_WIDTH),
                             (sm_ref, r["psm"], SMALL)):
            dst[0:lb, :] = src[0]
            dst[lb:rr_, :] = jnp.zeros((rr_ - lb, wd), F32)
        zs_v, zg_v, sm = r["pzs"], r["pzg"], r["psm"][...]
    else:
        zs_v, zg_v, sm = zs_ref.at[0], zg_ref.at[0], sm_ref[0]

    sp = _softplus(sm + bias_ref[...])
    beta_all = jax.nn.sigmoid(sm)
    if padded:
        valid = lax.broadcasted_iota(jnp.int32, (rr_, SMALL), 0) < lv
        sp = jnp.where(valid, sp, 0.0)
        beta_all = jnp.where(valid, beta_all, 0.0)
    ag = sp * coef_ref[...]
    r_all = lax.broadcasted_iota(jnp.int32, (rr_, rr_), 0)
    c_all = lax.broadcasted_iota(jnp.int32, (rr_, rr_), 1)
    chunk_start = (r_all // q) * q
    tril_bd = jnp.where(r_all >= c_all, jnp.where(c_all >= chunk_start, 1.0, 0.0), 0.0).astype(BF16)
    rq = lax.broadcasted_iota(jnp.int32, (q, q), 0)
    cq = lax.broadcasted_iota(jnp.int32, (q, q), 1)
    incl = rq >= cq
    strict = rq > cq
    r128 = lax.broadcasted_iota(jnp.int32, (LANES, LANES), 0)
    c128 = lax.broadcasted_iota(jnp.int32, (LANES, LANES), 1)
    eye = jnp.where(r128 == c128, 1.0, 0.0).astype(BF16)
    cum = _const_dot(tril_bd, ag, 3)
    cum_t = _transpose_via_identity(eye, cum, 3)
    sp_t = _transpose_via_identity(eye, sp, 3)
    ecum = jnp.exp(cum)
    if nq > 1:
        cum_last = jnp.concatenate(
            [jnp.broadcast_to(cum[(c + 1) * q - 1:(c + 1) * q, :], (q, SMALL)) for c in range(nq)], axis=0)
    else:
        cum_last = cum[q - 1:q, :]
    to_end = jnp.exp(cum_last - cum)
    er = lax.broadcasted_iota(jnp.int32, (SMALL, SSD_WIDTH), 0)
    ec = lax.broadcasted_iota(jnp.int32, (SMALL, SSD_WIDTH), 1) // SSD_HEAD_DIM
    expand = jnp.where(er == ec, 1.0, 0.0).astype(BF16)
    pe_s[...] = _dot_const(ecum, expand, 2)
    p_dtend = _dot_const(sp * to_end, expand, 2)
    xcd_s[...] = (xbc_c[:, 0:SSD_WIDTH].astype(F32) * p_dtend).astype(BF16)

    lane = lax.broadcasted_iota(jnp.int32, (q, pair), 1)
    lo_half = lane < SSD_HEAD_DIM
    scores = {}
    for c in range(nq):
        rows = slice(c * q, (c + 1) * q)
        for g in range(SSD_GROUPS):
            b_g = xbc_c[rows, SSD_WIDTH + g * SSD_STATE:SSD_WIDTH + (g + 1) * SSD_STATE]
            c_g = xbc_c[rows, SSD_WIDTH + (SSD_GROUPS + g) * SSD_STATE:
                        SSD_WIDTH + (SSD_GROUPS + g + 1) * SSD_STATE]
            scores[c, g] = _bdot_nt(c_g, b_g)
            bgt_s[c * SSD_GROUPS + g] = _transpose_via_identity(eye, b_g, 1).astype(BF16)
    for c in range(nq):
        rows = slice(c * q, (c + 1) * q)
        for j in range(SSD_HEADS // 2):
            g = j // (heads_per_group // 2)
            pcols = slice(pair * j, pair * (j + 1))
            x_pair = xbc_c[rows, pcols].astype(F32)
            y_pair = x_pair * dexp_ref[:, pcols]
            for half in range(2):
                h = 2 * j + half
                seg = cum[rows, h:h + 1] - cum_t[h:h + 1, rows]
                lmat = jnp.exp(jnp.where(incl, seg, -jnp.inf))
                m_h = scores[c, g] * lmat * sp_t[h:h + 1, rows]
                x_h = jnp.where(lo_half if half == 0 else jnp.logical_not(lo_half), x_pair, 0.0)
                y_pair = y_pair + _bdot(m_h, x_h)
            y_s[rows, pcols] = y_pair

    all_ch = [(c, h) for c in range(nq) for h in range(GDN_HEADS)]
    for start in range(0, len(all_ch), chains):
        group = all_ch[start:start + chains]
        m_list, rhs_list = [], []
        for (c, h) in group:
            rows = slice(c * q, (c + 1) * q)
            i = c * GDN_HEADS + h
            qh = qkv_c[rows, h * GDN_HEAD_DIM:(h + 1) * GDN_HEAD_DIM].astype(F32)
            kh = qkv_c[rows, GDN_WIDTH + h * GDN_HEAD_DIM:GDN_WIDTH + (h + 1) * GDN_HEAD_DIM].astype(F32)
            vh = qkv_c[rows, 2 * GDN_WIDTH + h * GDN_HEAD_DIM:
                       2 * GDN_WIDTH + (h + 1) * GDN_HEAD_DIM].astype(F32)
            qh = qh * lax.rsqrt(jnp.sum(qh * qh, axis=-1, keepdims=True) + NORM_EPS) * (GDN_HEAD_DIM ** -0.5)
            kh = kh * lax.rsqrt(jnp.sum(kh * kh, axis=-1, keepdims=True) + NORM_EPS)
            la = _SM_A + h
            bcol = beta_all[rows, _SM_B + h:_SM_B + h + 1]
            eg = ecum[rows, la:la + 1]
            kb = kh * bcol
            dec = jnp.exp(jnp.where(incl, cum[rows, la:la + 1] - cum_t[la:la + 1, rows], -jnp.inf))
            kq = _bdot_nt(jnp.concatenate([kb, qh], axis=0), kh)
            m_list.append(jnp.where(strict, kq[0:q] * dec, 0.0))
            at_s[i] = (kq[q:2 * q] * dec).astype(BF16)
            rhs_list.append(jnp.concatenate([vh * bcol, kb * eg], axis=1))
            wq_s[i, q:2 * q, :] = (qh * eg).astype(BF16)
            kdt_s[i] = _transpose_via_identity(eye, kh * to_end[rows, la:la + 1], 1).astype(BF16)
        sol_list = _unit_lower_solve_many(m_list, rhs_list, q)
        for (c, h), sol in zip(group, sol_list):
            i = c * GDN_HEADS + h
            u_s[i] = sol[:, 0:GDN_HEAD_DIM]
            wq_s[i, 0:q, :] = sol[:, GDN_HEAD_DIM:2 * GDN_HEAD_DIM].astype(BF16)

    for c in range(nq):
        rows = slice(c * q, (c + 1) * q)
        orow = slice(c * q, c * q + min(q, lb))
        last = (c + 1) * q - 1
        for g in range(SSD_GROUPS):
            gcols = slice(g * gw, (g + 1) * gw)
            c_g = xbc_c[rows, SSD_WIDTH + (SSD_GROUPS + g) * SSD_STATE:
                        SSD_WIDTH + (SSD_GROUPS + g + 1) * SSD_STATE]
            state = st_s[:, gcols]
            y_g = y_s[rows, gcols] + _bdot(c_g, state) * pe_s[rows, gcols]
            st_s[:, gcols] = state * pe_s[last:last + 1, gcols] + jnp.dot(
                bgt_s[c * SSD_GROUPS + g], xcd_s[rows, gcols], preferred_element_type=F32)
            y_g = y_g * gate(zs_v[rows, gcols])
            ms = jnp.mean(y_g * y_g, axis=-1, keepdims=True)
            y_g = y_g * lax.rsqrt(ms + NORM_EPS) * snw_ref[:, gcols]
            mix_ref[0, orow, gcols] = y_g[0:min(q, lb)].astype(mix_ref.dtype)
        ws_list = []
        for h in range(GDN_HEADS):
            i = c * GDN_HEADS + h
            ws_list.append(jnp.dot(wq_s[i], st_g[h].astype(BF16), preferred_element_type=F32))
        vn_list = [u_s[c * GDN_HEADS + h] - ws_list[h][0:q] for h in range(GDN_HEADS)]
        for h in range(GDN_HEADS):
            i = c * GDN_HEADS + h
            la = _SM_A + h
            vn = vn_list[h].astype(BF16)
            o = ws_list[h][q:2 * q] + jnp.dot(at_s[i], vn, preferred_element_type=F32)
            st_g[h] = st_g[h] * ecum[last:last + 1, la:la + 1] + jnp.dot(kdt_s[i], vn,
                                                                         preferred_element_type=F32)
            hc = slice(h * GDN_HEAD_DIM, (h + 1) * GDN_HEAD_DIM)
            ms = jnp.mean(o * o, axis=-1, keepdims=True)
            o = o * lax.rsqrt(ms + NORM_EPS) * gnw_ref[...] * gate(zg_v[rows, hc])
            mix_ref[0, orow, SSD_WIDTH + h * GDN_HEAD_DIM:SSD_WIDTH + (h + 1) * GDN_HEAD_DIM] = (
                o[0:min(q, lb)].astype(mix_ref.dtype))

    @pl.when(c_idx == nc - 1)
    def _store_state():
        for j in range(SSD_HEADS // 2):
            blk = st_s[:, pair * j:pair * (j + 1)].T
            nhs_ref[0, 0, 2 * j:2 * j + 2] = blk.reshape(2, SSD_HEAD_DIM, SSD_STATE)
        nsg_ref[0, 0] = st_g[...]


_STATE_TAILS = (
    (CONV_K - 1, SSD_CONV_DIM),
    (SSD_HEADS, SSD_HEAD_DIM, SSD_STATE),
    (CONV_K - 1, GDN_CONV_DIM),
    (GDN_HEADS, GDN_HEAD_DIM, GDN_HEAD_DIM),
)


_STATE_NAMES = ("cs", "hs", "cg", "sg")


def _mixer_call(zs, xbc, zg, qkv, sm, states_in, prev_out, p, q, nq, lb, layer, chains, do_conv):
    B, L, _ = zs.shape
    rows = nq * q
    assert L % lb == 0 and lb <= rows and q % INV_BLOCK == 0
    assert lb == rows or nq == 1
    nc = L // lb
    lv = lb
    assert lv >= CONV_K - 1 and (nc == 1 or lb == rows)
    zero_init = states_in is None
    tok = lambda wd: pl.BlockSpec((1, lb, wd), lambda b, c: (b, c, 0))
    per_b = lambda shape: pl.BlockSpec((1, 1) + shape, lambda b, c: (layer, b) + (0,) * len(shape))
    const = lambda shape: pl.BlockSpec(shape, lambda b, c: (0,) * len(shape))
    nch = nq * GDN_HEADS
    kept = [k for k in range(4) if do_conv or k in (1, 3)]

    names = ["zs", "xbc", "zg", "qkv", "sm"]
    in_specs = [tok(SSD_WIDTH), tok(SSD_CONV_DIM), tok(GDN_WIDTH), tok(GDN_CONV_DIM), tok(SMALL)]
    args = [zs, xbc, zg, qkv, sm]
    if not zero_init:
        for k in kept:
            names.append(_STATE_NAMES[k])
            in_specs.append(per_b(_STATE_TAILS[k]))
            args.append(states_in[k])
    consts = [("bias", (1, SMALL)), ("coef", (1, SMALL)), ("dexp", (1, SSD_WIDTH)), ("snw", (1, SSD_WIDTH)),
              ("gnw", (1, GDN_HEAD_DIM))]
    if do_conv:
        consts += [("scw", (CONV_K, SSD_CONV_DIM)), ("scb", (1, SSD_CONV_DIM)), ("gcw", (CONV_K, GDN_CONV_DIM))]
    for nm, shape in consts:
        names.append(nm)
        in_specs.append(const(shape))
        args.append(p[nm])
    aliases = {}
    if prev_out is not None:
        for k, a in enumerate(prev_out):
            names.append("alias%d" % k)
            aliases[len(args)] = 1 + k
            in_specs.append(pl.BlockSpec(memory_space=pl.ANY))
            args.append(a)
    names += ["mix"] + ["n" + _STATE_NAMES[k] for k in kept]
    out_specs = [tok(MIX_WIDTH)] + [per_b(_STATE_TAILS[k]) for k in kept]
    out_shape = [jax.ShapeDtypeStruct((B, L, MIX_WIDTH), BF16)] + [
        jax.ShapeDtypeStruct((DEPTH, B) + _STATE_TAILS[k], F32) for k in kept]
    scratch = [
        ("st_s", pltpu.VMEM((SSD_STATE, SSD_WIDTH), F32)),
        ("st_g", pltpu.VMEM((GDN_HEADS, GDN_HEAD_DIM, GDN_HEAD_DIM), F32)),
        ("y_s", pltpu.VMEM((rows, SSD_WIDTH), F32)),
        ("pe_s", pltpu.VMEM((rows, SSD_WIDTH), F32)),
        ("xcd_s", pltpu.VMEM((rows, SSD_WIDTH), BF16)),
        ("bgt_s", pltpu.VMEM((nq * SSD_GROUPS, SSD_STATE, q), BF16)),
        ("wq_s", pltpu.VMEM((nch, 2 * q, GDN_HEAD_DIM), BF16)),
        ("u_s", pltpu.VMEM((nch, q, GDN_HEAD_DIM), F32)),
        ("at_s", pltpu.VMEM((nch, q, q), BF16)),
        ("kdt_s", pltpu.VMEM((nch, GDN_HEAD_DIM, q), BF16)),
    ]
    if do_conv:
        scratch += [
            ("xps", pltpu.VMEM((rows + SUBLANES, SSD_CONV_DIM), F32)),
            ("xpg", pltpu.VMEM((rows + SUBLANES, GDN_CONV_DIM), F32)),
            ("xbc_c", pltpu.VMEM((rows, SSD_CONV_DIM), F32)),
            ("qkv_c", pltpu.VMEM((rows, GDN_CONV_DIM), F32)),
        ]
    if lb < rows:
        scratch += [
            ("pzs", pltpu.VMEM((rows, SSD_WIDTH), F32)),
            ("pzg", pltpu.VMEM((rows, GDN_WIDTH), F32)),
            ("psm", pltpu.VMEM((rows, SMALL), F32)),
        ]
    names += [nm for nm, _ in scratch]
    return pl.pallas_call(
        functools.partial(_mixer_kernel, names=tuple(names), q=q, nq=nq, lb=lb, lv=lv, nc=nc,
                          zero_init=zero_init, do_conv=do_conv, chains=chains),
        grid=(B, nc),
        in_specs=in_specs,
        out_specs=out_specs,
        out_shape=out_shape,
        input_output_aliases=aliases,
        scratch_shapes=[s for _, s in scratch],
        compiler_params=pltpu.CompilerParams(
            dimension_semantics=("arbitrary", "arbitrary"), vmem_limit_bytes=VMEM_LIMIT),
        name="mixer",
    )(*args)


def _prep_layer(l, norm_w, w_in, ssd_conv_w, ssd_conv_b, ssd_dt_bias, ssd_a_log, ssd_d, ssd_norm_w,
                gdn_conv_w, gdn_dt_bias, gdn_a_log, gdn_norm_w, w_out):
    wi = w_in[l]
    small = jnp.concatenate(
        [wi[:, _OFF_DT:_OFF_DT + SSD_HEADS], wi[:, _OFF_A:_OFF_A + GDN_HEADS], wi[:, _OFF_B:_OFF_B + GDN_HEADS],
         jnp.zeros((D_MODEL, SMALL - SSD_HEADS - 2 * GDN_HEADS), F32)], axis=1)
    sm_hi = small.astype(BF16)
    sm_lo = (small - sm_hi.astype(F32)).astype(BF16)
    pad = jnp.zeros((SMALL - SSD_HEADS - GDN_HEADS,), F32)
    return {
        "norm_w": norm_w[l].reshape(1, D_MODEL),
        "zs": wi[:, _OFF_ZS:_OFF_ZS + SSD_WIDTH].astype(BF16),
        "xbc": wi[:, _OFF_XBC:_OFF_XBC + SSD_CONV_DIM].astype(BF16),
        "zg": wi[:, _OFF_ZG:_OFF_ZG + GDN_WIDTH].astype(BF16),
        "qkv": wi[:, _OFF_QKV:_OFF_QKV + GDN_CONV_DIM].astype(BF16),
        "sm_hi": sm_hi,
        "sm_lo": sm_lo,
        "scw": ssd_conv_w[l],
        "scb": ssd_conv_b[l].reshape(1, SSD_CONV_DIM),
        "gcw": gdn_conv_w[l],
        "bias": jnp.concatenate([ssd_dt_bias[l], gdn_dt_bias[l], pad]).reshape(1, SMALL),
        "coef": jnp.concatenate([-jnp.exp(ssd_a_log[l]), -jnp.exp(gdn_a_log[l]), pad]).reshape(1, SMALL),
        "dexp": jnp.repeat(ssd_d[l], SSD_HEAD_DIM).reshape(1, SSD_WIDTH),
        "snw": ssd_norm_w[l].reshape(1, SSD_WIDTH),
        "gnw": gdn_norm_w[l].reshape(1, GDN_HEAD_DIM),
        "w_out": w_out[l].astype(BF16),
    }


def _trunk(x, mod, states, params, final_w, q, nq, lb, tm, chains, act_dtype):
    B, L, _ = x.shape
    T = B * L
    per_row = L < SUBLANES
    steps_per_group = max(L // tm, 1)
    x2d = x.reshape(T, D_MODEL)
    conv_in_proj = states is None and not per_row
    new_states = None
    conv_states = ([], [])
    for l in range(DEPTH):
        p = params[l]
        shift, scale, gate = (mod[l, :, i * D_MODEL:(i + 1) * D_MODEL] for i in range(3))
        if per_row:
            rep = lambda a: jnp.repeat(a, L, axis=0).reshape(1, T, D_MODEL)
        else:
            rep = lambda a: a.reshape(B, 1, D_MODEL)
        zs, xbc, zg, qkv, sm, *cst = _inproj_call(x2d, rep(scale), rep(shift), p["norm_w"], p, tm, per_row,
                                                  steps_per_group, conv_in_proj, act_dtype)
        r3 = lambda a: a.reshape(B, L, a.shape[-1])
        mix, *new_states = _mixer_call(r3(zs), r3(xbc), r3(zg), r3(qkv), r3(sm), states, new_states, p,
                                       q, nq, lb, l, chains, not conv_in_proj)
        x2d = _outproj_call(x2d, mix.reshape(T, MIX_WIDTH), rep(gate), p["w_out"], final_w, tm, per_row,
                            steps_per_group, final_norm=(l == DEPTH - 1))
        for lst, s in zip(conv_states, cst):
            lst.append(s)
    if conv_in_proj:
        new_states = [jnp.stack(conv_states[0]), new_states[0], jnp.stack(conv_states[1]), new_states[1]]
    return x2d.reshape(B, L, D_MODEL), new_states


def kernel(x_prompt, x_sample, state_ssd_conv, state_ssm, state_gdn_conv, state_gdn, c_prompt, c_sample,
           norm_w, w_ada, b_ada, w_in, ssd_conv_w, ssd_conv_b, ssd_dt_bias, ssd_a_log, ssd_d, ssd_norm_w,
           gdn_conv_w, gdn_dt_bias, gdn_a_log, gdn_norm_w, w_out, final_norm_w):
    bp = x_prompt.shape[0]
    params = [_prep_layer(l, norm_w, w_in, ssd_conv_w, ssd_conv_b, ssd_dt_bias, ssd_a_log, ssd_d, ssd_norm_w,
                          gdn_conv_w, gdn_dt_bias, gdn_a_log, gdn_norm_w, w_out) for l in range(DEPTH)]
    final_w = final_norm_w.reshape(1, D_MODEL)
    mod = _ada_call(jnp.concatenate([c_prompt, c_sample], axis=0), w_ada, b_ada)
    y_p, sp = _trunk(x_prompt, mod[:, :bp], None, params, final_w, q=64, nq=4, lb=256, tm=512, chains=16,
                     act_dtype=BF16)
    y_s, ss = _trunk(x_sample, mod[:, bp:], (state_ssd_conv, state_ssm, state_gdn_conv, state_gdn),
                     params, final_w, q=16, nq=1, lb=x_sample.shape[1],
                     tm=x_sample.shape[0] * x_sample.shape[1], chains=8, act_dtype=F32)
    return (y_p, y_s, sp[0], sp[1], sp[2], sp[3], ss[0], ss[1], ss[2], ss[3])
```

```python
import functools
import math

import jax
import jax.numpy as jnp
from jax import lax
from jax.experimental import pallas as pl
from jax.experimental.pallas import tpu as pltpu

F32 = jnp.float32
BF16 = jnp.bfloat16

D_MODEL = 1024
DEPTH = 2
SSD_WIDTH = 1024
SSD_HEAD_DIM = 64
SSD_HEADS = 16
SSD_GROUPS = 2
SSD_STATE = 128
SSD_CONV_DIM = SSD_WIDTH + 2 * SSD_GROUPS * SSD_STATE
GDN_WIDTH = 1024
GDN_HEAD_DIM = 128
GDN_HEADS = 8
GDN_CONV_DIM = 3 * GDN_WIDTH
MIX_WIDTH = SSD_WIDTH + GDN_WIDTH
CONV_K = 4
NORM_EPS = 1e-6
ADA_DIM = 3 * D_MODEL
SMALL = 128
LANES = 128
SUBLANES = 8
INV_BLOCK = 16
VMEM_LIMIT = 56 * 1024 * 1024

_OFF_ZS = 0
_OFF_XBC = _OFF_ZS + SSD_WIDTH
_OFF_DT = _OFF_XBC + SSD_CONV_DIM
_OFF_ZG = _OFF_DT + SSD_HEADS
_OFF_QKV = _OFF_ZG + GDN_WIDTH
_OFF_A = _OFF_QKV + GDN_CONV_DIM
_OFF_B = _OFF_A + GDN_HEADS
_SM_DT = 0
_SM_A = SSD_HEADS
_SM_B = SSD_HEADS + GDN_HEADS


def _bdot(a, b):
    return jnp.dot(a.astype(BF16), b.astype(BF16), preferred_element_type=F32)


def _bdot_nt(a, b):
    return lax.dot_general(a.astype(BF16), b.astype(BF16), (((1,), (1,)), ((), ())),
                           preferred_element_type=F32)


def _terms(x, n):
    out = []
    r = x
    for i in range(n):
        t = r.astype(BF16)
        out.append(t)
        if i + 1 < n:
            r = r - t.astype(F32)
    return out


def _const_dot(c, x, n):
    acc = None
    for t in _terms(x, n):
        p = jnp.dot(c, t, preferred_element_type=F32)
        acc = p if acc is None else acc + p
    return acc


def _dot_const(x, c, n):
    acc = None
    for t in _terms(x, n):
        p = jnp.dot(t, c, preferred_element_type=F32)
        acc = p if acc is None else acc + p
    return acc


def _transpose_via_identity(eye, x, n):
    acc = None
    for t in _terms(x, n):
        p = lax.dot_general(eye, t, (((1,), (1,)), ((), ())), preferred_element_type=F32)
        acc = p if acc is None else acc + p
    return acc


def _softplus(x):
    return jnp.maximum(x, 0.0) + jnp.log1p(jnp.exp(-jnp.abs(x)))


def _silu(x):
    return x * jax.nn.sigmoid(x)


def _ada_kernel(c_ref, w_ref, b_ref, o_ref):
    s = _silu(c_ref[...])
    w = w_ref[0]
    s_hi, s_lo = _terms(s, 2)
    w_hi, w_lo = _terms(w, 2)
    acc = jnp.dot(s_hi, w_hi, preferred_element_type=F32)
    acc = acc + (jnp.dot(s_hi, w_lo, preferred_element_type=F32)
                 + jnp.dot(s_lo, w_hi, preferred_element_type=F32))
    o_ref[0] = acc + b_ref[0]


def _ada_call(c_all, w_ada, b_ada):
    rows = c_all.shape[0]
    tn = D_MODEL
    return pl.pallas_call(
        _ada_kernel,
        grid=(DEPTH, ADA_DIM // tn),
        in_specs=[
            pl.BlockSpec((rows, D_MODEL), lambda l, j: (0, 0)),
            pl.BlockSpec((1, D_MODEL, tn), lambda l, j: (l, 0, j)),
            pl.BlockSpec((1, 1, tn), lambda l, j: (l, 0, j)),
        ],
        out_specs=pl.BlockSpec((1, rows, tn), lambda l, j: (l, 0, j)),
        out_shape=jax.ShapeDtypeStruct((DEPTH, rows, ADA_DIM), F32),
        compiler_params=pltpu.CompilerParams(
            dimension_semantics=("arbitrary", "arbitrary"), vmem_limit_bytes=VMEM_LIMIT),
        name="ada_mod",
    )(c_all, w_ada, b_ada.reshape(DEPTH, 1, ADA_DIM))


CONV_ROWS = 64
CONV_COLS = 128
PROJ_COLS = 512


def _conv_silu_rows(xp, w_ref, b_ref, out, rows, c0, c1, xp_row0=0, out_row0=0):
    rows_per = min(rows, CONV_ROWS)
    for i in range(rows // rows_per):
        r0 = i * rows_per
        for j in range((c1 - c0) // CONV_COLS):
            cols = slice(c0 + j * CONV_COLS, c0 + (j + 1) * CONV_COLS)
            x = xp[xp_row0 + r0:xp_row0 + r0 + rows_per + SUBLANES, cols]
            acc = x * w_ref[0:1, cols]
            for k in range(1, CONV_K):
                acc = pltpu.roll(acc, 1, 0) + x * w_ref[k:k + 1, cols]
            if b_ref is not None:
                acc = acc + b_ref[:, cols]
            out[out_row0 + r0:out_row0 + r0 + rows_per, cols] = _silu(acc[SUBLANES:, :]).astype(out.dtype)


def _inproj_kernel(*refs, tm, steps_per_seq, do_conv):
    x_ref, sc_ref, sh_ref, nw_ref, wzs_ref, wxbc_ref, wzg_ref, wqkv_ref, wsh_ref, wsl_ref = refs[:10]
    if do_conv:
        scw_ref, scb_ref, gcw_ref = refs[10:13]
        zs_ref, xbc_ref, zg_ref, qkv_ref, sm_ref, ncs_ref, ncg_ref, xps, xpg = refs[13:]
    else:
        zs_ref, xbc_ref, zg_ref, qkv_ref, sm_ref = refs[10:]
    x = x_ref[...]
    ms = jnp.mean(x * x, axis=-1, keepdims=True)
    h = x * lax.rsqrt(ms + NORM_EPS) * nw_ref[...]
    h = h * (1.0 + sc_ref[0]) + sh_ref[0]
    h_hi, h_lo = _terms(h, 2)
    sm = jnp.dot(h_hi, wsh_ref[...], preferred_element_type=F32)
    sm = sm + (jnp.dot(h_hi, wsl_ref[...], preferred_element_type=F32)
               + jnp.dot(h_lo, wsh_ref[...], preferred_element_type=F32))
    sm_ref[...] = sm
    if not do_conv:
        zs_ref[...] = jnp.dot(h_hi, wzs_ref[...], preferred_element_type=F32)
        xbc_ref[...] = jnp.dot(h_hi, wxbc_ref[...], preferred_element_type=F32)
        zg_ref[...] = jnp.dot(h_hi, wzg_ref[...], preferred_element_type=F32)
        qkv_ref[...] = jnp.dot(h_hi, wqkv_ref[...], preferred_element_type=F32)
        return

    @pl.when(pl.program_id(0) % steps_per_seq == 0)
    def _zero_history():
        xps[0:SUBLANES, :] = jnp.zeros((SUBLANES, SSD_CONV_DIM), F32)
        xpg[0:SUBLANES, :] = jnp.zeros((SUBLANES, GDN_CONV_DIM), F32)

    conv_blocks = [(xp, w_ref, cw_ref, cb_ref, out, c0)
                   for xp, w_ref, cw_ref, cb_ref, out, width in (
                       (xps, wxbc_ref, scw_ref, scb_ref, xbc_ref, SSD_CONV_DIM),
                       (xpg, wqkv_ref, gcw_ref, None, qkv_ref, GDN_CONV_DIM))
                   for c0 in range(0, width, PROJ_COLS)]
    gate_blocks = [(w_ref, out, c0) for w_ref, out, width in ((wzs_ref, zs_ref, SSD_WIDTH),
                                                              (wzg_ref, zg_ref, GDN_WIDTH))
                   for c0 in range(0, width, PROJ_COLS)]

    def project(blk):
        xp, w_ref, _, _, _, c0 = blk
        cols = slice(c0, c0 + PROJ_COLS)
        xp[SUBLANES:SUBLANES + tm, cols] = jnp.dot(h_hi, w_ref[:, cols], preferred_element_type=F32)

    project(conv_blocks[0])
    for j, blk in enumerate(conv_blocks):
        if j + 1 < len(conv_blocks):
            project(conv_blocks[j + 1])
        if j % 2 == 0 and j // 2 < len(gate_blocks):
            w_ref, out, c0 = gate_blocks[j // 2]
            cols = slice(c0, c0 + PROJ_COLS)
            out[:, cols] = _silu(jnp.dot(h_hi, w_ref[:, cols], preferred_element_type=F32)).astype(out.dtype)
        xp, _, cw_ref, cb_ref, out, c0 = blk
        _conv_silu_rows(xp, cw_ref, cb_ref, out, tm, c0, c0 + PROJ_COLS)
    assert len(gate_blocks) <= (len(conv_blocks) + 1) // 2
    for xp, nst in ((xps, ncs_ref), (xpg, ncg_ref)):
        nst[0] = xp[SUBLANES + tm - (CONV_K - 1):SUBLANES + tm, :]
        xp[0:SUBLANES, :] = xp[tm:tm + SUBLANES, :]


def _row_mod_spec(per_row, tm, steps_per_group):
    if per_row:
        return pl.BlockSpec((1, tm, D_MODEL), lambda i: (0, i, 0))
    return pl.BlockSpec((1, 1, D_MODEL), lambda i: (i // steps_per_group, 0, 0))


def _resident(shape):
    nd = len(shape)
    return pl.BlockSpec(shape, lambda i: (0,) * nd, pipeline_mode=pl.Buffered(1))


def _inproj_call(x2d, scale, shift, norm_w, w, tm, per_row, steps_per_group, do_conv, act_dtype):
    T = x2d.shape[0]
    assert T % tm == 0
    widths = (SSD_WIDTH, SSD_CONV_DIM, GDN_WIDTH, GDN_CONV_DIM, SMALL)
    dtypes = (act_dtype, act_dtype, act_dtype, act_dtype, F32)
    row = lambda wd: pl.BlockSpec((tm, wd), lambda i: (i, 0))
    in_specs = [
        row(D_MODEL),
        _row_mod_spec(per_row, tm, steps_per_group),
        _row_mod_spec(per_row, tm, steps_per_group),
        _resident((1, D_MODEL)),
        _resident((D_MODEL, SSD_WIDTH)),
        _resident((D_MODEL, SSD_CONV_DIM)),
        _resident((D_MODEL, GDN_WIDTH)),
        _resident((D_MODEL, GDN_CONV_DIM)),
        _resident((D_MODEL, SMALL)),
        _resident((D_MODEL, SMALL)),
    ]
    args = [x2d, scale, shift, norm_w, w["zs"], w["xbc"], w["zg"], w["qkv"], w["sm_hi"], w["sm_lo"]]
    out_specs = [row(wd) for wd in widths]
    out_shape = [jax.ShapeDtypeStruct((T, wd), dt) for wd, dt in zip(widths, dtypes)]
    scratch = []
    if do_conv:
        assert not per_row
        nseq = T // (tm * steps_per_group)
        in_specs += [_resident((CONV_K, SSD_CONV_DIM)), _resident((1, SSD_CONV_DIM)),
                     _resident((CONV_K, GDN_CONV_DIM))]
        args += [w["scw"], w["scb"], w["gcw"]]
        for wd in (SSD_CONV_DIM, GDN_CONV_DIM):
            out_specs.append(pl.BlockSpec((1, CONV_K - 1, wd), lambda i: (i // steps_per_group, 0, 0)))
            out_shape.append(jax.ShapeDtypeStruct((nseq, CONV_K - 1, wd), F32))
            scratch.append(pltpu.VMEM((tm + SUBLANES, wd), F32))
    return pl.pallas_call(
        functools.partial(_inproj_kernel, tm=tm, steps_per_seq=steps_per_group, do_conv=do_conv),
        grid=(T // tm,),
        in_specs=in_specs,
        out_specs=out_specs,
        out_shape=out_shape,
        scratch_shapes=scratch,
        compiler_params=pltpu.CompilerParams(
            dimension_semantics=("arbitrary",), vmem_limit_bytes=VMEM_LIMIT),
        name="in_proj",
    )(*args)


def _outproj_kernel(x_ref, mix_ref, gate_ref, w_ref, fw_ref, o_ref, *, final_norm):
    y = jnp.dot(mix_ref[...], w_ref[...], preferred_element_type=F32)
    out = x_ref[...] + gate_ref[0] * y
    if final_norm:
        ms = jnp.mean(out * out, axis=-1, keepdims=True)
        out = out * lax.rsqrt(ms + NORM_EPS) * fw_ref[...]
    o_ref[...] = out


def _outproj_call(x2d, mix2d, gate, w_out, final_w, tm, per_row, steps_per_group, final_norm):
    T = x2d.shape[0]
    return pl.pallas_call(
        functools.partial(_outproj_kernel, final_norm=final_norm),
        grid=(T // tm,),
        in_specs=[
            pl.BlockSpec((tm, D_MODEL), lambda i: (i, 0)),
            pl.BlockSpec((tm, MIX_WIDTH), lambda i: (i, 0)),
            _row_mod_spec(per_row, tm, steps_per_group),
            _resident((MIX_WIDTH, D_MODEL)),
            _resident((1, D_MODEL)),
        ],
        out_specs=pl.BlockSpec((tm, D_MODEL), lambda i: (i, 0)),
        out_shape=jax.ShapeDtypeStruct((T, D_MODEL), F32),
        compiler_params=pltpu.CompilerParams(
            dimension_semantics=("arbitrary",), vmem_limit_bytes=VMEM_LIMIT),
        name="out_proj",
    )(x2d, mix2d, gate, w_out, final_w)


def _unit_lower_solve_many(m_list, rhs_list, q):
    nb = q // INV_BLOCK
    if nb > 1:
        r = lax.broadcasted_iota(jnp.int32, (q, q), 0) // INV_BLOCK
        c = lax.broadcasted_iota(jnp.int32, (q, q), 1) // INV_BLOCK
        same = r == c
        d_list = [jnp.where(same, m, 0.0) for m in m_list]
        o_list = [m - d for m, d in zip(m_list, d_list)]
    else:
        d_list = m_list
    n_list = [-d for d in d_list]
    e_list = d_list
    for _ in range(int(math.log2(INV_BLOCK)) - 1):
        e_list = [_bdot(e, e) for e in e_list]
        n_list = [n + e + _bdot(n, e) for n, e in zip(n_list, e_list)]
    y_list = [rhs + _bdot(n, rhs) for n, rhs in zip(n_list, rhs_list)]
    if nb > 1:
        p_list = [o + _bdot(n, o) for n, o in zip(n_list, o_list)]
        r_list = [-p for p in p_list]
        e_list = p_list
        for _ in range(int(math.ceil(math.log2(nb))) - 1):
            e_list = [_bdot(e, e) for e in e_list]
            r_list = [rr + e + _bdot(rr, e) for rr, e in zip(r_list, e_list)]
        y_list = [y + _bdot(rr, y) for rr, y in zip(r_list, y_list)]
    return y_list


def _mixer_kernel(*refs, names, q, nq, lv, nc, multi_seq, zero_init, do_conv, chains):
    r = dict(zip(names, refs))
    zs_ref, xbc_ref, zg_ref, qkv_ref, sm_ref = r["zs"], r["xbc"], r["zg"], r["qkv"], r["sm"]
    bias_ref, coef_ref, dexp_ref, snw_ref, gnw_ref = r["bias"], r["coef"], r["dexp"], r["snw"], r["gnw"]
    mix_ref, nhs_ref, nsg_ref = r["mix"], r["nhs"], r["nsg"]
    st_s, st_g, y_s, pe_s, xcd_s, bgt_s = r["st_s"], r["st_g"], r["y_s"], r["pe_s"], r["xcd_s"], r["bgt_s"]
    wq_s, u_s, at_s, kdt_s = r["wq_s"], r["u_s"], r["at_s"], r["kdt_s"]

    c_idx = pl.program_id(1)
    rr_ = nq * q
    n_slots = nq if multi_seq else 1
    slot = (lambda c: c) if multi_seq else (lambda c: 0)
    hist = q + SUBLANES
    pair = 2 * SSD_HEAD_DIM
    heads_per_group = SSD_HEADS // SSD_GROUPS
    gw = SSD_WIDTH // SSD_GROUPS
    assert not (do_conv and not multi_seq and nc > 1 and lv != rr_)

    @pl.when(c_idx == 0)
    def _load_state():
        if zero_init:
            st_s[...] = jnp.zeros(st_s.shape, F32)
            st_g[...] = jnp.zeros(st_g.shape, F32)
        else:
            for s in range(n_slots):
                for j in range(SSD_HEADS // 2):
                    blk = r["hs"][0, s, 2 * j:2 * j + 2].reshape(pair, SSD_STATE)
                    st_s[s, :, pair * j:pair * (j + 1)] = blk.T
            st_g[...] = r["sg"][0]
        if do_conv:
            for xp, key, width in ((r["xps"], "cs", SSD_CONV_DIM), (r["xpg"], "cg", GDN_CONV_DIM)):
                for s in range(n_slots):
                    xp[s * hist:s * hist + SUBLANES, :] = jnp.zeros((SUBLANES, width), F32)
                    if not zero_init:
                        xp[s * hist + SUBLANES - (CONV_K - 1):s * hist + SUBLANES, :] = r[key][0, s]

    if do_conv:
        for x_ref, xp, w_ref, b_ref, out, nst, width in (
                (xbc_ref, r["xps"], r["scw"], r["scb"], r["xbc_c"], r["ncs"], SSD_CONV_DIM),
                (qkv_ref, r["xpg"], r["gcw"], None, r["qkv_c"], r["ncg"], GDN_CONV_DIM)):
            if multi_seq:
                for c in range(nq):
                    base = c * hist + SUBLANES
                    xp[base:base + lv, :] = x_ref[c]
                    xp[base + lv:base + q, :] = jnp.zeros((q - lv, width), F32)
                    _conv_silu_rows(xp, w_ref, b_ref, out, q, 0, width, xp_row0=c * hist, out_row0=c * q)
                    nst[0, c] = xp[base + lv - (CONV_K - 1):base + lv, :]
            else:
                xp[SUBLANES:SUBLANES + rr_, :] = x_ref[0]
                _conv_silu_rows(xp, w_ref, b_ref, out, rr_, 0, width)

                @pl.when(c_idx == nc - 1)
                def _store_conv_state():
                    nst[0, 0] = xp[SUBLANES + rr_ - (CONV_K - 1):SUBLANES + rr_, :]

                if nc > 1:
                    xp[0:SUBLANES, :] = xp[rr_:rr_ + SUBLANES, :]
        xbc_c, qkv_c = r["xbc_c"], r["qkv_c"]
        gate = _silu
    else:
        xbc_c, qkv_c = xbc_ref.at[0], qkv_ref.at[0]
        gate = lambda z: z.astype(F32)

    if multi_seq:
        for src, dst, wd in ((zs_ref, r["pzs"], SSD_WIDTH), (zg_ref, r["pzg"], GDN_WIDTH),
                             (sm_ref, r["psm"], SMALL)):
            for c in range(nq):
                dst[c * q:c * q + lv, :] = src[c].astype(F32)
                dst[c * q + lv:(c + 1) * q, :] = jnp.zeros((q - lv, wd), F32)
        zs_v, zg_v, sm = r["pzs"], r["pzg"], r["psm"][...]
    else:
        zs_v, zg_v, sm = zs_ref.at[0], zg_ref.at[0], sm_ref[0]

    sp = _softplus(sm + bias_ref[...])
    beta_all = jax.nn.sigmoid(sm)
    if multi_seq and lv < q:
        assert q & (q - 1) == 0
        valid = jnp.bitwise_and(lax.broadcasted_iota(jnp.int32, (rr_, SMALL), 0), q - 1) < lv
        sp = jnp.where(valid, sp, 0.0)
        beta_all = jnp.where(valid, beta_all, 0.0)
    ag = sp * coef_ref[...]
    r_all = lax.broadcasted_iota(jnp.int32, (rr_, rr_), 0)
    c_all = lax.broadcasted_iota(jnp.int32, (rr_, rr_), 1)
    chunk_start = (r_all // q) * q
    tril_bd = jnp.where(r_all >= c_all, jnp.where(c_all >= chunk_start, 1.0, 0.0), 0.0).astype(BF16)
    rq = lax.broadcasted_iota(jnp.int32, (q, q), 0)
    cq = lax.broadcasted_iota(jnp.int32, (q, q), 1)
    incl = rq >= cq
    strict = rq > cq
    r128 = lax.broadcasted_iota(jnp.int32, (LANES, LANES), 0)
    c128 = lax.broadcasted_iota(jnp.int32, (LANES, LANES), 1)
    eye = jnp.where(r128 == c128, 1.0, 0.0).astype(BF16)
    cum = _const_dot(tril_bd, ag, 3)
    cum_t = _transpose_via_identity(eye, cum, 3)
    sp_t = _transpose_via_identity(eye, sp, 3)
    ecum = jnp.exp(cum)
    if nq > 1:
        cum_last = jnp.concatenate(
            [jnp.broadcast_to(cum[(c + 1) * q - 1:(c + 1) * q, :], (q, SMALL)) for c in range(nq)], axis=0)
    else:
        cum_last = cum[q - 1:q, :]
    to_end = jnp.exp(cum_last - cum)
    er = lax.broadcasted_iota(jnp.int32, (SMALL, SSD_WIDTH), 0)
    ec = lax.broadcasted_iota(jnp.int32, (SMALL, SSD_WIDTH), 1) // SSD_HEAD_DIM
    expand = jnp.where(er == ec, 1.0, 0.0).astype(BF16)
    pe_s[...] = _dot_const(ecum, expand, 2)
    p_dtend = _dot_const(sp * to_end, expand, 2)
    xcd_s[...] = (xbc_c[:, 0:SSD_WIDTH].astype(F32) * p_dtend).astype(BF16)

    lane = lax.broadcasted_iota(jnp.int32, (q, pair), 1)
    lo_half = lane < SSD_HEAD_DIM
    scores = {}
    for c in range(nq):
        rows = slice(c * q, (c + 1) * q)
        for g in range(SSD_GROUPS):
            b_g = xbc_c[rows, SSD_WIDTH + g * SSD_STATE:SSD_WIDTH + (g + 1) * SSD_STATE]
            c_g = xbc_c[rows, SSD_WIDTH + (SSD_GROUPS + g) * SSD_STATE:
                        SSD_WIDTH + (SSD_GROUPS + g + 1) * SSD_STATE]
            scores[c, g] = _bdot_nt(c_g, b_g)
            bgt_s[c * SSD_GROUPS + g] = _transpose_via_identity(eye, b_g, 1).astype(BF16)
    for c in range(nq):
        rows = slice(c * q, (c + 1) * q)
        for j in range(SSD_HEADS // 2):
            g = j // (heads_per_group // 2)
            pcols = slice(pair * j, pair * (j + 1))
            x_pair = xbc_c[rows, pcols].astype(F32)
            y_pair = x_pair * dexp_ref[:, pcols]
            for half in range(2):
                h = 2 * j + half
                seg = cum[rows, h:h + 1] - cum_t[h:h + 1, rows]
                lmat = jnp.exp(jnp.where(incl, seg, -jnp.inf))
                m_h = scores[c, g] * lmat * sp_t[h:h + 1, rows]
                x_h = jnp.where(lo_half if half == 0 else jnp.logical_not(lo_half), x_pair, 0.0)
                y_pair = y_pair + _bdot(m_h, x_h)
            y_s[rows, pcols] = y_pair

    all_ch = [(c, h) for c in range(nq) for h in range(GDN_HEADS)]
    for start in range(0, len(all_ch), chains):
        group = all_ch[start:start + chains]
        m_list, rhs_list = [], []
        for (c, h) in group:
            rows = slice(c * q, (c + 1) * q)
            i = c * GDN_HEADS + h
            qh = qkv_c[rows, h * GDN_HEAD_DIM:(h + 1) * GDN_HEAD_DIM].astype(F32)
            kh = qkv_c[rows, GDN_WIDTH + h * GDN_HEAD_DIM:GDN_WIDTH + (h + 1) * GDN_HEAD_DIM].astype(F32)
            vh = qkv_c[rows, 2 * GDN_WIDTH + h * GDN_HEAD_DIM:
                       2 * GDN_WIDTH + (h + 1) * GDN_HEAD_DIM].astype(F32)
            qh = qh * lax.rsqrt(jnp.sum(qh * qh, axis=-1, keepdims=True) + NORM_EPS) * (GDN_HEAD_DIM ** -0.5)
            kh = kh * lax.rsqrt(jnp.sum(kh * kh, axis=-1, keepdims=True) + NORM_EPS)
            la = _SM_A + h
            bcol = beta_all[rows, _SM_B + h:_SM_B + h + 1]
            eg = ecum[rows, la:la + 1]
            kb = kh * bcol
            dec = jnp.exp(jnp.where(incl, cum[rows, la:la + 1] - cum_t[la:la + 1, rows], -jnp.inf))
            kq = _bdot_nt(jnp.concatenate([kb, qh], axis=0), kh)
            m_list.append(jnp.where(strict, kq[0:q] * dec, 0.0))
            at_s[i] = (kq[q:2 * q] * dec).astype(BF16)
            rhs_list.append(jnp.concatenate([vh * bcol, kb * eg], axis=1))
            wq_s[i, q:2 * q, :] = (qh * eg).astype(BF16)
            kdt_s[i] = _transpose_via_identity(eye, kh * to_end[rows, la:la + 1], 1).astype(BF16)
        sol_list = _unit_lower_solve_many(m_list, rhs_list, q)
        for (c, h), sol in zip(group, sol_list):
            i = c * GDN_HEADS + h
            u_s[i] = sol[:, 0:GDN_HEAD_DIM]
            wq_s[i, 0:q, :] = sol[:, GDN_HEAD_DIM:2 * GDN_HEAD_DIM].astype(BF16)

    def store_mix(c, cols, val):
        if multi_seq:
            mix_ref[c, :, cols] = val[0:lv].astype(mix_ref.dtype)
        else:
            mix_ref[0, c * q:(c + 1) * q, cols] = val.astype(mix_ref.dtype)

    for items in ([list(range(nq))] if multi_seq else [[c] for c in range(nq)]):
        for c in items:
            rows = slice(c * q, (c + 1) * q)
            last = (c + 1) * q - 1
            s = slot(c)
            for g in range(SSD_GROUPS):
                gcols = slice(g * gw, (g + 1) * gw)
                c_g = xbc_c[rows, SSD_WIDTH + (SSD_GROUPS + g) * SSD_STATE:
                            SSD_WIDTH + (SSD_GROUPS + g + 1) * SSD_STATE]
                state = st_s[s, :, gcols]
                y_g = y_s[rows, gcols] + _bdot(c_g, state) * pe_s[rows, gcols]
                st_s[s, :, gcols] = state * pe_s[last:last + 1, gcols] + jnp.dot(
                    bgt_s[c * SSD_GROUPS + g], xcd_s[rows, gcols], preferred_element_type=F32)
                y_g = y_g * gate(zs_v[rows, gcols])
                ms = jnp.mean(y_g * y_g, axis=-1, keepdims=True)
                y_g = y_g * lax.rsqrt(ms + NORM_EPS) * snw_ref[:, gcols]
                store_mix(c, gcols, y_g)
        pairs = [(c, h) for c in items for h in range(GDN_HEADS)]
        ws = {}
        for (c, h) in pairs:
            ws[c, h] = jnp.dot(wq_s[c * GDN_HEADS + h], st_g[slot(c), h].astype(BF16),
                               preferred_element_type=F32)
        vn = {(c, h): (u_s[c * GDN_HEADS + h] - ws[c, h][0:q]).astype(BF16) for (c, h) in pairs}
        for (c, h) in pairs:
            i = c * GDN_HEADS + h
            la = _SM_A + h
            last = (c + 1) * q - 1
            s = slot(c)
            o = ws[c, h][q:2 * q] + jnp.dot(at_s[i], vn[c, h], preferred_element_type=F32)
            st_g[s, h] = st_g[s, h] * ecum[last:last + 1, la:la + 1] + jnp.dot(
                kdt_s[i], vn[c, h], preferred_element_type=F32)
            hc = slice(h * GDN_HEAD_DIM, (h + 1) * GDN_HEAD_DIM)
            ms = jnp.mean(o * o, axis=-1, keepdims=True)
            o = o * lax.rsqrt(ms + NORM_EPS) * gnw_ref[...] * gate(zg_v[c * q:(c + 1) * q, hc])
            store_mix(c, slice(SSD_WIDTH + h * GDN_HEAD_DIM, SSD_WIDTH + (h + 1) * GDN_HEAD_DIM), o)

    @pl.when(c_idx == nc - 1)
    def _store_state():
        for s in range(n_slots):
            for j in range(SSD_HEADS // 2):
                blk = st_s[s, :, pair * j:pair * (j + 1)].T
                nhs_ref[0, s, 2 * j:2 * j + 2] = blk.reshape(2, SSD_HEAD_DIM, SSD_STATE)
        nsg_ref[0] = st_g[...]


_STATE_TAILS = (
    (CONV_K - 1, SSD_CONV_DIM),
    (SSD_HEADS, SSD_HEAD_DIM, SSD_STATE),
    (CONV_K - 1, GDN_CONV_DIM),
    (GDN_HEADS, GDN_HEAD_DIM, GDN_HEAD_DIM),
)


_STATE_NAMES = ("cs", "hs", "cg", "sg")


def _mixer_call(zs, xbc, zg, qkv, sm, states_in, prev_out, p, q, nq, multi_seq, layer, chains, do_conv):
    B, L, _ = zs.shape
    rows = nq * q
    assert q % INV_BLOCK == 0
    zero_init = states_in is None
    if multi_seq:
        assert CONV_K - 1 <= L <= q and B % nq == 0 and do_conv
        nc, lv, bb = 1, L, nq
        tok = lambda wd: pl.BlockSpec((nq, L, wd), lambda b, c: (b, 0, 0))
    else:
        assert L % rows == 0
        nc, lv, bb = L // rows, rows, 1
        tok = lambda wd: pl.BlockSpec((1, rows, wd), lambda b, c: (b, c, 0))
    n_slots = bb
    per_b = lambda shape: pl.BlockSpec((1, bb) + shape, lambda b, c: (layer, b) + (0,) * len(shape))
    const = lambda shape: pl.BlockSpec(shape, lambda b, c: (0,) * len(shape))
    nch = nq * GDN_HEADS
    kept = [k for k in range(4) if do_conv or k in (1, 3)]

    names = ["zs", "xbc", "zg", "qkv", "sm"]
    in_specs = [tok(SSD_WIDTH), tok(SSD_CONV_DIM), tok(GDN_WIDTH), tok(GDN_CONV_DIM), tok(SMALL)]
    args = [zs, xbc, zg, qkv, sm]
    if not zero_init:
        for k in kept:
            names.append(_STATE_NAMES[k])
            in_specs.append(per_b(_STATE_TAILS[k]))
            args.append(states_in[k])
    consts = [("bias", (1, SMALL)), ("coef", (1, SMALL)), ("dexp", (1, SSD_WIDTH)), ("snw", (1, SSD_WIDTH)),
              ("gnw", (1, GDN_HEAD_DIM))]
    if do_conv:
        consts += [("scw", (CONV_K, SSD_CONV_DIM)), ("scb", (1, SSD_CONV_DIM)), ("gcw", (CONV_K, GDN_CONV_DIM))]
    for nm, shape in consts:
        names.append(nm)
        in_specs.append(const(shape))
        args.append(p[nm])
    aliases = {}
    if prev_out is not None:
        for k, a in enumerate(prev_out):
            names.append("alias%d" % k)
            aliases[len(args)] = 1 + k
            in_specs.append(pl.BlockSpec(memory_space=pl.ANY))
            args.append(a)
    names += ["mix"] + ["n" + _STATE_NAMES[k] for k in kept]
    out_specs = [tok(MIX_WIDTH)] + [per_b(_STATE_TAILS[k]) for k in kept]
    out_shape = [jax.ShapeDtypeStruct((B, L, MIX_WIDTH), BF16)] + [
        jax.ShapeDtypeStruct((DEPTH, B) + _STATE_TAILS[k], F32) for k in kept]
    scratch = [
        ("st_s", pltpu.VMEM((n_slots, SSD_STATE, SSD_WIDTH), F32)),
        ("st_g", pltpu.VMEM((n_slots, GDN_HEADS, GDN_HEAD_DIM, GDN_HEAD_DIM), F32)),
        ("y_s", pltpu.VMEM((rows, SSD_WIDTH), F32)),
        ("pe_s", pltpu.VMEM((rows, SSD_WIDTH), F32)),
        ("xcd_s", pltpu.VMEM((rows, SSD_WIDTH), BF16)),
        ("bgt_s", pltpu.VMEM((nq * SSD_GROUPS, SSD_STATE, q), BF16)),
        ("wq_s", pltpu.VMEM((nch, 2 * q, GDN_HEAD_DIM), BF16)),
        ("u_s", pltpu.VMEM((nch, q, GDN_HEAD_DIM), F32)),
        ("at_s", pltpu.VMEM((nch, q, q), BF16)),
        ("kdt_s", pltpu.VMEM((nch, GDN_HEAD_DIM, q), BF16)),
    ]
    if do_conv:
        xp_rows = nq * (q + SUBLANES) if multi_seq else rows + SUBLANES
        scratch += [
            ("xps", pltpu.VMEM((xp_rows, SSD_CONV_DIM), F32)),
            ("xpg", pltpu.VMEM((xp_rows, GDN_CONV_DIM), F32)),
            ("xbc_c", pltpu.VMEM((rows, SSD_CONV_DIM), F32)),
            ("qkv_c", pltpu.VMEM((rows, GDN_CONV_DIM), F32)),
        ]
    if multi_seq:
        scratch += [
            ("pzs", pltpu.VMEM((rows, SSD_WIDTH), F32)),
            ("pzg", pltpu.VMEM((rows, GDN_WIDTH), F32)),
            ("psm", pltpu.VMEM((rows, SMALL), F32)),
        ]
    names += [nm for nm, _ in scratch]
    return pl.pallas_call(
        functools.partial(_mixer_kernel, names=tuple(names), q=q, nq=nq, lv=lv, nc=nc, multi_seq=multi_seq,
                          zero_init=zero_init, do_conv=do_conv, chains=chains),
        grid=(B // bb, nc),
        in_specs=in_specs,
        out_specs=out_specs,
        out_shape=out_shape,
        input_output_aliases=aliases,
        scratch_shapes=[s for _, s in scratch],
        compiler_params=pltpu.CompilerParams(
            dimension_semantics=("arbitrary", "arbitrary"), vmem_limit_bytes=VMEM_LIMIT),
        name="mixer",
    )(*args)


def _prep_layer(l, norm_w, w_in, ssd_conv_w, ssd_conv_b, ssd_dt_bias, ssd_a_log, ssd_d, ssd_norm_w,
                gdn_conv_w, gdn_dt_bias, gdn_a_log, gdn_norm_w, w_out):
    wi = w_in[l]
    small = jnp.concatenate(
        [wi[:, _OFF_DT:_OFF_DT + SSD_HEADS], wi[:, _OFF_A:_OFF_A + GDN_HEADS], wi[:, _OFF_B:_OFF_B + GDN_HEADS],
         jnp.zeros((D_MODEL, SMALL - SSD_HEADS - 2 * GDN_HEADS), F32)], axis=1)
    sm_hi = small.astype(BF16)
    sm_lo = (small - sm_hi.astype(F32)).astype(BF16)
    pad = jnp.zeros((SMALL - SSD_HEADS - GDN_HEADS,), F32)
    return {
        "norm_w": norm_w[l].reshape(1, D_MODEL),
        "zs": wi[:, _OFF_ZS:_OFF_ZS + SSD_WIDTH].astype(BF16),
        "xbc": wi[:, _OFF_XBC:_OFF_XBC + SSD_CONV_DIM].astype(BF16),
        "zg": wi[:, _OFF_ZG:_OFF_ZG + GDN_WIDTH].astype(BF16),
        "qkv": wi[:, _OFF_QKV:_OFF_QKV + GDN_CONV_DIM].astype(BF16),
        "sm_hi": sm_hi,
        "sm_lo": sm_lo,
        "scw": ssd_conv_w[l],
        "scb": ssd_conv_b[l].reshape(1, SSD_CONV_DIM),
        "gcw": gdn_conv_w[l],
        "bias": jnp.concatenate([ssd_dt_bias[l], gdn_dt_bias[l], pad]).reshape(1, SMALL),
        "coef": jnp.concatenate([-jnp.exp(ssd_a_log[l]), -jnp.exp(gdn_a_log[l]), pad]).reshape(1, SMALL),
        "dexp": jnp.repeat(ssd_d[l], SSD_HEAD_DIM).reshape(1, SSD_WIDTH),
        "snw": ssd_norm_w[l].reshape(1, SSD_WIDTH),
        "gnw": gdn_norm_w[l].reshape(1, GDN_HEAD_DIM),
        "w_out": w_out[l].astype(BF16),
    }


def _trunk(x, mod, states, params, final_w, q, nq, tm, chains, act_dtype):
    B, L, _ = x.shape
    T = B * L
    per_row = L < SUBLANES
    steps_per_group = max(L // tm, 1)
    x2d = x.reshape(T, D_MODEL)
    conv_in_proj = states is None and not per_row
    new_states = None
    conv_states = ([], [])
    for l in range(DEPTH):
        p = params[l]
        shift, scale, gate = (mod[l, :, i * D_MODEL:(i + 1) * D_MODEL] for i in range(3))
        if per_row:
            rep = lambda a: jnp.repeat(a, L, axis=0).reshape(1, T, D_MODEL)
        else:
            rep = lambda a: a.reshape(B, 1, D_MODEL)
        zs, xbc, zg, qkv, sm, *cst = _inproj_call(x2d, rep(scale), rep(shift), p["norm_w"], p, tm, per_row,
                                                  steps_per_group, conv_in_proj, act_dtype)
        r3 = lambda a: a.reshape(B, L, a.shape[-1])
        mix, *new_states = _mixer_call(r3(zs), r3(xbc), r3(zg), r3(qkv), r3(sm), states, new_states, p,
                                       q, nq, per_row, l, chains, not conv_in_proj)
        x2d = _outproj_call(x2d, mix.reshape(T, MIX_WIDTH), rep(gate), p["w_out"], final_w, tm, per_row,
                            steps_per_group, final_norm=(l == DEPTH - 1))
        for lst, s in zip(conv_states, cst):
            lst.append(s)
    if conv_in_proj:
        new_states = [jnp.stack(conv_states[0]), new_states[0], jnp.stack(conv_states[1]), new_states[1]]
    return x2d.reshape(B, L, D_MODEL), new_states


def kernel(x_prompt, x_sample, state_ssd_conv, state_ssm, state_gdn_conv, state_gdn, c_prompt, c_sample,
           norm_w, w_ada, b_ada, w_in, ssd_conv_w, ssd_conv_b, ssd_dt_bias, ssd_a_log, ssd_d, ssd_norm_w,
           gdn_conv_w, gdn_dt_bias, gdn_a_log, gdn_norm_w, w_out, final_norm_w):
    bp = x_prompt.shape[0]
    params = [_prep_layer(l, norm_w, w_in, ssd_conv_w, ssd_conv_b, ssd_dt_bias, ssd_a_log, ssd_d, ssd_norm_w,
                          gdn_conv_w, gdn_dt_bias, gdn_a_log, gdn_norm_w, w_out) for l in range(DEPTH)]
    final_w = final_norm_w.reshape(1, D_MODEL)
    mod = _ada_call(jnp.concatenate([c_prompt, c_sample], axis=0), w_ada, b_ada)
    y_p, sp = _trunk(x_prompt, mod[:, :bp], None, params, final_w, q=64, nq=4, tm=512, chains=16,
                     act_dtype=BF16)
    y_s, ss = _trunk(x_sample, mod[:, bp:], (state_ssd_conv, state_ssm, state_gdn_conv, state_gdn),
                     params, final_w, q=16, nq=4, tm=x_sample.shape[0] * x_sample.shape[1], chains=32,
                     act_dtype=F32)
    return (y_p, y_s, sp[0], sp[1], sp[2], sp[3], ss[0], ss[1], ss[2], ss[3])
```

```python
import functools
import math

import jax
import jax.numpy as jnp
from jax import lax
from jax.experimental import pallas as pl
from jax.experimental.pallas import tpu as pltpu

F32 = jnp.float32
BF16 = jnp.bfloat16

D_MODEL = 1024
DEPTH = 2
SSD_WIDTH = 1024
SSD_HEAD_DIM = 64
SSD_HEADS = 16
SSD_GROUPS = 2
SSD_STATE = 128
SSD_CONV_DIM = SSD_WIDTH + 2 * SSD_GROUPS * SSD_STATE
GDN_WIDTH = 1024
GDN_HEAD_DIM = 128
GDN_HEADS = 8
GDN_CONV_DIM = 3 * GDN_WIDTH
MIX_WIDTH = SSD_WIDTH + GDN_WIDTH
CONV_K = 4
NORM_EPS = 1e-6
ADA_DIM = 3 * D_MODEL
SMALL = 128
LANES = 128
SUBLANES = 8
INV_BLOCK = 16
VMEM_LIMIT = 56 * 1024 * 1024

_OFF_ZS = 0
_OFF_XBC = _OFF_ZS + SSD_WIDTH
_OFF_DT = _OFF_XBC + SSD_CONV_DIM
_OFF_ZG = _OFF_DT + SSD_HEADS
_OFF_QKV = _OFF_ZG + GDN_WIDTH
_OFF_A = _OFF_QKV + GDN_CONV_DIM
_OFF_B = _OFF_A + GDN_HEADS
_SM_DT = 0
_SM_A = SSD_HEADS
_SM_B = SSD_HEADS + GDN_HEADS


def _bdot(a, b):
    return jnp.dot(a.astype(BF16), b.astype(BF16), preferred_element_type=F32)


def _bdot_nt(a, b):
    return lax.dot_general(a.astype(BF16), b.astype(BF16), (((1,), (1,)), ((), ())),
                           preferred_element_type=F32)


def _terms(x, n):
    out = []
    r = x
    for i in range(n):
        t = r.astype(BF16)
        out.append(t)
        if i + 1 < n:
            r = r - t.astype(F32)
    return out


def _const_dot(c, x, n):
    acc = None
    for t in _terms(x, n):
        p = jnp.dot(c, t, preferred_element_type=F32)
        acc = p if acc is None else acc + p
    return acc


def _dot_const(x, c, n):
    acc = None
    for t in _terms(x, n):
        p = jnp.dot(t, c, preferred_element_type=F32)
        acc = p if acc is None else acc + p
    return acc


def _transpose_via_identity(eye, x, n):
    acc = None
    for t in _terms(x, n):
        p = lax.dot_general(eye, t, (((1,), (1,)), ((), ())), preferred_element_type=F32)
        acc = p if acc is None else acc + p
    return acc


def _softplus(x):
    return jnp.maximum(x, 0.0) + jnp.log1p(jnp.exp(-jnp.abs(x)))


def _silu(x):
    return x * jax.nn.sigmoid(x)


def _ada_kernel(c_ref, w_ref, b_ref, o_ref):
    s = _silu(c_ref[...])
    w = w_ref[0]
    s_hi, s_lo = _terms(s, 2)
    w_hi, w_lo = _terms(w, 2)
    acc = jnp.dot(s_hi, w_hi, preferred_element_type=F32)
    acc = acc + (jnp.dot(s_hi, w_lo, preferred_element_type=F32)
                 + jnp.dot(s_lo, w_hi, preferred_element_type=F32))
    o_ref[0] = acc + b_ref[0]


def _ada_call(c_all, w_ada, b_ada):
    rows = c_all.shape[0]
    tn = D_MODEL
    return pl.pallas_call(
        _ada_kernel,
        grid=(DEPTH, ADA_DIM // tn),
        in_specs=[
            pl.BlockSpec((rows, D_MODEL), lambda l, j: (0, 0)),
            pl.BlockSpec((1, D_MODEL, tn), lambda l, j: (l, 0, j)),
            pl.BlockSpec((1, 1, tn), lambda l, j: (l, 0, j)),
        ],
        out_specs=pl.BlockSpec((1, rows, tn), lambda l, j: (l, 0, j)),
        out_shape=jax.ShapeDtypeStruct((DEPTH, rows, ADA_DIM), F32),
        compiler_params=pltpu.CompilerParams(
            dimension_semantics=("arbitrary", "arbitrary"), vmem_limit_bytes=VMEM_LIMIT),
        name="ada_mod",
    )(c_all, w_ada, b_ada.reshape(DEPTH, 1, ADA_DIM))


CONV_ROWS = 64
CONV_COLS = 128
PROJ_COLS = 512


def _conv_silu_rows(xp, w_ref, b_ref, out, rows, c0, c1, xp_row0=0, out_row0=0):
    rows_per = min(rows, CONV_ROWS)
    for i in range(rows // rows_per):
        r0 = i * rows_per
        for j in range((c1 - c0) // CONV_COLS):
            cols = slice(c0 + j * CONV_COLS, c0 + (j + 1) * CONV_COLS)
            x = xp[xp_row0 + r0:xp_row0 + r0 + rows_per + SUBLANES, cols]
            acc = x * w_ref[0:1, cols]
            for k in range(1, CONV_K):
                acc = pltpu.roll(acc, 1, 0) + x * w_ref[k:k + 1, cols]
            if b_ref is not None:
                acc = acc + b_ref[:, cols]
            out[out_row0 + r0:out_row0 + r0 + rows_per, cols] = _silu(acc[SUBLANES:, :]).astype(out.dtype)


PERM_CHUNK = SUBLANES * SUBLANES


def _conv_silu_permuted(xp, w_ref, b_ref, hist, up, out, rows, c0):
    sub0 = lax.broadcasted_iota(jnp.int32, (SUBLANES, LANES), 0) == 0
    taps = CONV_K - 1
    for j in range(PROJ_COLS // LANES):
        cols = slice(c0 + j * LANES, c0 + (j + 1) * LANES)
        lcols = slice(j * LANES, (j + 1) * LANES)
        w = [jnp.broadcast_to(w_ref[k:k + 1, cols], (SUBLANES, LANES)) for k in range(CONV_K)]
        b = None if b_ref is None else jnp.broadcast_to(b_ref[:, cols], (SUBLANES, LANES))
        prev = [hist[i * SUBLANES:(i + 1) * SUBLANES, cols] for i in range(taps)]
        for k in range(rows // PERM_CHUNK):
            base = k * PERM_CHUNK
            xs = [xp[base + a * SUBLANES:base + (a + 1) * SUBLANES, cols] for a in range(SUBLANES)]
            rolled = [pltpu.roll(xs[SUBLANES - taps + i], 1, 0) for i in range(taps)]
            ext = [jnp.where(sub0, prev[i], rolled[i]) for i in range(taps)] + xs
            for a in range(SUBLANES):
                y = ext[a] * w[0]
                for t in range(1, CONV_K):
                    y = y + ext[a + t] * w[t]
                if b is not None:
                    y = y + b
                up[j, pl.ds(base + a, SUBLANES, stride=SUBLANES), :] = _silu(y)
            prev = rolled
        for i in range(taps):
            hist[i * SUBLANES:(i + 1) * SUBLANES, cols] = prev[i]
        out[:, cols] = up[j].astype(out.dtype)


def _inproj_kernel(*refs, tm, steps_per_seq, do_conv):
    x_ref, sc_ref, sh_ref, nw_ref, wzs_ref, wxbc_ref, wzg_ref, wqkv_ref, wsh_ref, wsl_ref = refs[:10]
    if do_conv:
        scw_ref, scb_ref, gcw_ref = refs[10:13]
        (zs_ref, xbc_ref, zg_ref, qkv_ref, sm_ref, ncs_ref, ncg_ref,
         xps, xpg, hist_s, hist_g, hs, hp, up) = refs[13:]
    else:
        zs_ref, xbc_ref, zg_ref, qkv_ref, sm_ref = refs[10:]
    x = x_ref[...]
    ms = jnp.mean(x * x, axis=-1, keepdims=True)
    h = x * lax.rsqrt(ms + NORM_EPS) * nw_ref[...]
    h = h * (1.0 + sc_ref[0]) + sh_ref[0]
    h_hi, h_lo = _terms(h, 2)
    sm = jnp.dot(h_hi, wsh_ref[...], preferred_element_type=F32)
    sm = sm + (jnp.dot(h_hi, wsl_ref[...], preferred_element_type=F32)
               + jnp.dot(h_lo, wsh_ref[...], preferred_element_type=F32))
    sm_ref[...] = sm
    if not do_conv:
        zs_ref[...] = jnp.dot(h_hi, wzs_ref[...], preferred_element_type=F32)
        xbc_ref[...] = jnp.dot(h_hi, wxbc_ref[...], preferred_element_type=F32)
        zg_ref[...] = jnp.dot(h_hi, wzg_ref[...], preferred_element_type=F32)
        qkv_ref[...] = jnp.dot(h_hi, wqkv_ref[...], preferred_element_type=F32)
        return

    @pl.when(pl.program_id(0) % steps_per_seq == 0)
    def _zero_history():
        hist_s[...] = jnp.zeros(hist_s.shape, F32)
        hist_g[...] = jnp.zeros(hist_g.shape, F32)

    assert tm % PERM_CHUNK == 0
    ncb = D_MODEL // LANES
    for cb in range(ncb):
        hs[cb] = h[:, cb * LANES:(cb + 1) * LANES]
    for g in range(0, tm // SUBLANES, 2):
        parts = [jnp.concatenate(
            [hs[cb, pl.ds(PERM_CHUNK * (gg // SUBLANES) + gg % SUBLANES, SUBLANES, stride=SUBLANES), :]
             for cb in range(ncb)], axis=1) for gg in (g, g + 1)]
        hp[g * SUBLANES:(g + 2) * SUBLANES, :] = jnp.concatenate(parts, axis=0).astype(BF16)

    conv_blocks = [(xp, w_ref, cw_ref, cb_ref, hist, out, c0)
                   for xp, w_ref, cw_ref, cb_ref, hist, out, width in (
                       (xps, wxbc_ref, scw_ref, scb_ref, hist_s, xbc_ref, SSD_CONV_DIM),
                       (xpg, wqkv_ref, gcw_ref, None, hist_g, qkv_ref, GDN_CONV_DIM))
                   for c0 in range(0, width, PROJ_COLS)]
    gate_blocks = [(w_ref, out, c0) for w_ref, out, width in ((wzs_ref, zs_ref, SSD_WIDTH),
                                                              (wzg_ref, zg_ref, GDN_WIDTH))
                   for c0 in range(0, width, PROJ_COLS)]

    def project(blk):
        xp, w_ref, _, _, _, _, c0 = blk
        cols = slice(c0, c0 + PROJ_COLS)
        xp[:, cols] = jnp.dot(hp[...], w_ref[:, cols], preferred_element_type=F32)

    def gate_block(i):
        w_ref, out, c0 = gate_blocks[i]
        cols = slice(c0, c0 + PROJ_COLS)
        out[:, cols] = _silu(jnp.dot(h_hi, w_ref[:, cols], preferred_element_type=F32)).astype(out.dtype)

    early_gates = len(gate_blocks) // 2
    for i in range(early_gates):
        gate_block(i)
    project(conv_blocks[0])
    for j, blk in enumerate(conv_blocks):
        if j + 1 < len(conv_blocks):
            project(conv_blocks[j + 1])
        if j % 2 == 0 and early_gates + j // 2 < len(gate_blocks):
            gate_block(early_gates + j // 2)
        xp, _, cw_ref, cb_ref, hist, out, c0 = blk
        _conv_silu_permuted(xp, cw_ref, cb_ref, hist, up, out, tm, c0)
    assert len(gate_blocks) - early_gates <= (len(conv_blocks) + 1) // 2
    for xp, nst in ((xps, ncs_ref), (xpg, ncg_ref)):
        for i in range(CONV_K - 1):
            row = tm - PERM_CHUNK + SUBLANES * (SUBLANES - (CONV_K - 1) + i) + SUBLANES - 1
            nst[0, i:i + 1, :] = xp[row:row + 1, :]


def _row_mod_spec(per_row, tm, steps_per_group):
    if per_row:
        return pl.BlockSpec((1, tm, D_MODEL), lambda i: (0, i, 0))
    return pl.BlockSpec((1, 1, D_MODEL), lambda i: (i // steps_per_group, 0, 0))


def _resident(shape):
    nd = len(shape)
    return pl.BlockSpec(shape, lambda i: (0,) * nd, pipeline_mode=pl.Buffered(1))


def _inproj_call(x2d, scale, shift, norm_w, w, tm, per_row, steps_per_group, do_conv, act_dtype):
    T = x2d.shape[0]
    assert T % tm == 0
    widths = (SSD_WIDTH, SSD_CONV_DIM, GDN_WIDTH, GDN_CONV_DIM, SMALL)
    dtypes = (act_dtype, act_dtype, act_dtype, act_dtype, F32)
    row = lambda wd: pl.BlockSpec((tm, wd), lambda i: (i, 0))
    in_specs = [
        row(D_MODEL),
        _row_mod_spec(per_row, tm, steps_per_group),
        _row_mod_spec(per_row, tm, steps_per_group),
        _resident((1, D_MODEL)),
        _resident((D_MODEL, SSD_WIDTH)),
        _resident((D_MODEL, SSD_CONV_DIM)),
        _resident((D_MODEL, GDN_WIDTH)),
        _resident((D_MODEL, GDN_CONV_DIM)),
        _resident((D_MODEL, SMALL)),
        _resident((D_MODEL, SMALL)),
    ]
    args = [x2d, scale, shift, norm_w, w["zs"], w["xbc"], w["zg"], w["qkv"], w["sm_hi"], w["sm_lo"]]
    out_specs = [row(wd) for wd in widths]
    out_shape = [jax.ShapeDtypeStruct((T, wd), dt) for wd, dt in zip(widths, dtypes)]
    scratch = []
    if do_conv:
        assert not per_row
        nseq = T // (tm * steps_per_group)
        in_specs += [_resident((CONV_K, SSD_CONV_DIM)), _resident((1, SSD_CONV_DIM)),
                     _resident((CONV_K, GDN_CONV_DIM))]
        args += [w["scw"], w["scb"], w["gcw"]]
        for wd in (SSD_CONV_DIM, GDN_CONV_DIM):
            out_specs.append(pl.BlockSpec((1, CONV_K - 1, wd), lambda i: (i // steps_per_group, 0, 0)))
            out_shape.append(jax.ShapeDtypeStruct((nseq, CONV_K - 1, wd), F32))
            scratch.append(pltpu.VMEM((tm, wd), F32))
        scratch += [pltpu.VMEM(((CONV_K - 1) * SUBLANES, wd), F32) for wd in (SSD_CONV_DIM, GDN_CONV_DIM)]
        scratch += [pltpu.VMEM((D_MODEL // LANES, tm, LANES), F32), pltpu.VMEM((tm, D_MODEL), BF16),
                    pltpu.VMEM((PROJ_COLS // LANES, tm, LANES), F32)]
    return pl.pallas_call(
        functools.partial(_inproj_kernel, tm=tm, steps_per_seq=steps_per_group, do_conv=do_conv),
        grid=(T // tm,),
        in_specs=in_specs,
        out_specs=out_specs,
        out_shape=out_shape,
        scratch_shapes=scratch,
        compiler_params=pltpu.CompilerParams(
            dimension_semantics=("arbitrary",), vmem_limit_bytes=VMEM_LIMIT),
        name="in_proj",
    )(*args)


def _outproj_kernel(x_ref, mix_ref, gate_ref, w_ref, fw_ref, o_ref, *, final_norm):
    y = jnp.dot(mix_ref[...], w_ref[...], preferred_element_type=F32)
    out = x_ref[...] + gate_ref[0] * y
    if final_norm:
        ms = jnp.mean(out * out, axis=-1, keepdims=True)
        out = out * lax.rsqrt(ms + NORM_EPS) * fw_ref[...]
    o_ref[...] = out


def _outproj_call(x2d, mix2d, gate, w_out, final_w, tm, per_row, steps_per_group, final_norm):
    T = x2d.shape[0]
    return pl.pallas_call(
        functools.partial(_outproj_kernel, final_norm=final_norm),
        grid=(T // tm,),
        in_specs=[
            pl.BlockSpec((tm, D_MODEL), lambda i: (i, 0)),
            pl.BlockSpec((tm, MIX_WIDTH), lambda i: (i, 0)),
            _row_mod_spec(per_row, tm, steps_per_group),
            _resident((MIX_WIDTH, D_MODEL)),
            _resident((1, D_MODEL)),
        ],
        out_specs=pl.BlockSpec((tm, D_MODEL), lambda i: (i, 0)),
        out_shape=jax.ShapeDtypeStruct((T, D_MODEL), F32),
        compiler_params=pltpu.CompilerParams(
            dimension_semantics=("arbitrary",), vmem_limit_bytes=VMEM_LIMIT),
        name="out_proj",
    )(x2d, mix2d, gate, w_out, final_w)


def _unit_lower_solve_many(m_list, rhs_list, q):
    nb = q // INV_BLOCK
    if nb > 1:
        r = lax.broadcasted_iota(jnp.int32, (q, q), 0) // INV_BLOCK
        c = lax.broadcasted_iota(jnp.int32, (q, q), 1) // INV_BLOCK
        same = r == c
        d_list = [jnp.where(same, m, 0.0) for m in m_list]
        o_list = [m - d for m, d in zip(m_list, d_list)]
    else:
        d_list = m_list
    n_list = [-d for d in d_list]
    e_list = d_list
    for _ in range(int(math.log2(INV_BLOCK)) - 1):
        e_list = [_bdot(e, e) for e in e_list]
        n_list = [n + e + _bdot(n, e) for n, e in zip(n_list, e_list)]
    y_list = [rhs + _bdot(n, rhs) for n, rhs in zip(n_list, rhs_list)]
    if nb > 1:
        p_list = [o + _bdot(n, o) for n, o in zip(n_list, o_list)]
        r_list = [-p for p in p_list]
        e_list = p_list
        for _ in range(int(math.ceil(math.log2(nb))) - 1):
            e_list = [_bdot(e, e) for e in e_list]
            r_list = [rr + e + _bdot(rr, e) for rr, e in zip(r_list, e_list)]
        y_list = [y + _bdot(rr, y) for rr, y in zip(r_list, y_list)]
    return y_list


def _mixer_kernel(*refs, names, q, nq, lv, nc, multi_seq, zero_init, do_conv, chains):
    r = dict(zip(names, refs))
    zs_ref, xbc_ref, zg_ref, qkv_ref, sm_ref = r["zs"], r["xbc"], r["zg"], r["qkv"], r["sm"]
    bias_ref, coef_ref, dexp_ref, snw_ref, gnw_ref = r["bias"], r["coef"], r["dexp"], r["snw"], r["gnw"]
    mix_ref, nhs_ref, nsg_ref = r["mix"], r["nhs"], r["nsg"]
    st_s, st_g, y_s, pe_s, xcd_s, bgt_s = r["st_s"], r["st_g"], r["y_s"], r["pe_s"], r["xcd_s"], r["bgt_s"]
    wq_s, u_s, at_s, kdt_s = r["wq_s"], r["u_s"], r["at_s"], r["kdt_s"]

    c_idx = pl.program_id(1)
    rr_ = nq * q
    n_slots = nq if multi_seq else 1
    slot = (lambda c: c) if multi_seq else (lambda c: 0)
    hist = q + SUBLANES
    pair = 2 * SSD_HEAD_DIM
    heads_per_group = SSD_HEADS // SSD_GROUPS
    gw = SSD_WIDTH // SSD_GROUPS
    assert not (do_conv and not multi_seq and nc > 1 and lv != rr_)

    @pl.when(c_idx == 0)
    def _load_state():
        if zero_init:
            st_s[...] = jnp.zeros(st_s.shape, F32)
            st_g[...] = jnp.zeros(st_g.shape, F32)
        else:
            for s in range(n_slots):
                for j in range(SSD_HEADS // 2):
                    blk = r["hs"][0, s, 2 * j:2 * j + 2].reshape(pair, SSD_STATE)
                    st_s[s, :, pair * j:pair * (j + 1)] = blk.T
            st_g[...] = r["sg"][0]
        if do_conv:
            for xp, key, width in ((r["xps"], "cs", SSD_CONV_DIM), (r["xpg"], "cg", GDN_CONV_DIM)):
                for s in range(n_slots):
                    xp[s * hist:s * hist + SUBLANES, :] = jnp.zeros((SUBLANES, width), F32)
                    if not zero_init:
                        xp[s * hist + SUBLANES - (CONV_K - 1):s * hist + SUBLANES, :] = r[key][0, s]

    if do_conv:
        for x_ref, xp, w_ref, b_ref, out, nst, width in (
                (xbc_ref, r["xps"], r["scw"], r["scb"], r["xbc_c"], r["ncs"], SSD_CONV_DIM),
                (qkv_ref, r["xpg"], r["gcw"], None, r["qkv_c"], r["ncg"], GDN_CONV_DIM)):
            if multi_seq:
                for c in range(nq):
                    base = c * hist + SUBLANES
                    xp[base:base + lv, :] = x_ref[c]
                    xp[base + lv:base + q, :] = jnp.zeros((q - lv, width), F32)
                    _conv_silu_rows(xp, w_ref, b_ref, out, q, 0, width, xp_row0=c * hist, out_row0=c * q)
                    nst[0, c] = xp[base + lv - (CONV_K - 1):base + lv, :]
            else:
                xp[SUBLANES:SUBLANES + rr_, :] = x_ref[0]
                _conv_silu_rows(xp, w_ref, b_ref, out, rr_, 0, width)

                @pl.when(c_idx == nc - 1)
                def _store_conv_state():
                    nst[0, 0] = xp[SUBLANES + rr_ - (CONV_K - 1):SUBLANES + rr_, :]

                if nc > 1:
                    xp[0:SUBLANES, :] = xp[rr_:rr_ + SUBLANES, :]
        xbc_c, qkv_c = r["xbc_c"], r["qkv_c"]
        gate = _silu
    else:
        xbc_c, qkv_c = xbc_ref.at[0], qkv_ref.at[0]
        gate = lambda z: z.astype(F32)

    if multi_seq:
        for src, dst, wd in ((zs_ref, r["pzs"], SSD_WIDTH), (zg_ref, r["pzg"], GDN_WIDTH),
                             (sm_ref, r["psm"], SMALL)):
            for c in range(nq):
                dst[c * q:c * q + lv, :] = src[c].astype(F32)
                dst[c * q + lv:(c + 1) * q, :] = jnp.zeros((q - lv, wd), F32)
        zs_v, zg_v, sm = r["pzs"], r["pzg"], r["psm"][...]
    else:
        zs_v, zg_v, sm = zs_ref.at[0], zg_ref.at[0], sm_ref[0]

    sp = _softplus(sm + bias_ref[...])
    beta_all = jax.nn.sigmoid(sm)
    if multi_seq and lv < q:
        assert q & (q - 1) == 0
        valid = jnp.bitwise_and(lax.broadcasted_iota(jnp.int32, (rr_, SMALL), 0), q - 1) < lv
        sp = jnp.where(valid, sp, 0.0)
        beta_all = jnp.where(valid, beta_all, 0.0)
    ag = sp * coef_ref[...]
    r_all = lax.broadcasted_iota(jnp.int32, (rr_, rr_), 0)
    c_all = lax.broadcasted_iota(jnp.int32, (rr_, rr_), 1)
    chunk_start = (r_all // q) * q
    tril_bd = jnp.where(r_all >= c_all, jnp.where(c_all >= chunk_start, 1.0, 0.0), 0.0).astype(BF16)
    rq = lax.broadcasted_iota(jnp.int32, (q, q), 0)
    cq = lax.broadcasted_iota(jnp.int32, (q, q), 1)
    incl = rq >= cq
    strict = rq > cq
    r128 = lax.broadcasted_iota(jnp.int32, (LANES, LANES), 0)
    c128 = lax.broadcasted_iota(jnp.int32, (LANES, LANES), 1)
    eye = jnp.where(r128 == c128, 1.0, 0.0).astype(BF16)
    cum = _const_dot(tril_bd, ag, 3)
    cum_t = _transpose_via_identity(eye, cum, 3)
    sp_t = _transpose_via_identity(eye, sp, 3)
    ecum = jnp.exp(cum)
    if nq > 1:
        cum_last = jnp.concatenate(
            [jnp.broadcast_to(cum[(c + 1) * q - 1:(c + 1) * q, :], (q, SMALL)) for c in range(nq)], axis=0)
    else:
        cum_last = cum[q - 1:q, :]
    to_end = jnp.exp(cum_last - cum)
    er = lax.broadcasted_iota(jnp.int32, (SMALL, SSD_WIDTH), 0)
    ec = lax.broadcasted_iota(jnp.int32, (SMALL, SSD_WIDTH), 1) // SSD_HEAD_DIM
    expand = jnp.where(er == ec, 1.0, 0.0).astype(BF16)
    pe_s[...] = _dot_const(ecum, expand, 2)
    p_dtend = _dot_const(sp * to_end, expand, 2)
    xcd_s[...] = (xbc_c[:, 0:SSD_WIDTH].astype(F32) * p_dtend).astype(BF16)

    lane = lax.broadcasted_iota(jnp.int32, (q, pair), 1)
    lo_half = lane < SSD_HEAD_DIM
    scores = {}
    for c in range(nq):
        rows = slice(c * q, (c + 1) * q)
        for g in range(SSD_GROUPS):
            b_g = xbc_c[rows, SSD_WIDTH + g * SSD_STATE:SSD_WIDTH + (g + 1) * SSD_STATE]
            c_g = xbc_c[rows, SSD_WIDTH + (SSD_GROUPS + g) * SSD_STATE:
                        SSD_WIDTH + (SSD_GROUPS + g + 1) * SSD_STATE]
            scores[c, g] = _bdot_nt(c_g, b_g)
            bgt_s[c * SSD_GROUPS + g] = _transpose_via_identity(eye, b_g, 1).astype(BF16)
    for c in range(nq):
        rows = slice(c * q, (c + 1) * q)
        for j in range(SSD_HEADS // 2):
            g = j // (heads_per_group // 2)
            pcols = slice(pair * j, pair * (j + 1))
            x_pair = xbc_c[rows, pcols].astype(F32)
            y_pair = x_pair * dexp_ref[:, pcols]
            for half in range(2):
                h = 2 * j + half
                seg = cum[rows, h:h + 1] - cum_t[h:h + 1, rows]
                lmat = jnp.exp(jnp.where(incl, seg, -jnp.inf))
                m_h = scores[c, g] * lmat * sp_t[h:h + 1, rows]
                x_h = jnp.where(lo_half if half == 0 else jnp.logical_not(lo_half), x_pair, 0.0)
                y_pair = y_pair + _bdot(m_h, x_h)
            y_s[rows, pcols] = y_pair

    all_ch = [(c, h) for c in range(nq) for h in range(GDN_HEADS)]
    for start in range(0, len(all_ch), chains):
        group = all_ch[start:start + chains]
        m_list, rhs_list = [], []
        for (c, h) in group:
            rows = slice(c * q, (c + 1) * q)
            i = c * GDN_HEADS + h
            qh = qkv_c[rows, h * GDN_HEAD_DIM:(h + 1) * GDN_HEAD_DIM].astype(F32)
            kh = qkv_c[rows, GDN_WIDTH + h * GDN_HEAD_DIM:GDN_WIDTH + (h + 1) * GDN_HEAD_DIM].astype(F32)
            vh = qkv_c[rows, 2 * GDN_WIDTH + h * GDN_HEAD_DIM:
                       2 * GDN_WIDTH + (h + 1) * GDN_HEAD_DIM].astype(F32)
            qh = qh * lax.rsqrt(jnp.sum(qh * qh, axis=-1, keepdims=True) + NORM_EPS) * (GDN_HEAD_DIM ** -0.5)
            kh = kh * lax.rsqrt(jnp.sum(kh * kh, axis=-1, keepdims=True) + NORM_EPS)
            la = _SM_A + h
            bcol = beta_all[rows, _SM_B + h:_SM_B + h + 1]
            eg = ecum[rows, la:la + 1]
            kb = kh * bcol
            dec = jnp.exp(jnp.where(incl, cum[rows, la:la + 1] - cum_t[la:la + 1, rows], -jnp.inf))
            kq = _bdot_nt(jnp.concatenate([kb, qh], axis=0), kh)
            m_list.append(jnp.where(strict, kq[0:q] * dec, 0.0))
            at_s[i] = (kq[q:2 * q] * dec).astype(BF16)
            rhs_list.append(jnp.concatenate([vh * bcol, kb * eg], axis=1))
            wq_s[i, q:2 * q, :] = (qh * eg).astype(BF16)
            kdt_s[i] = _transpose_via_identity(eye, kh * to_end[rows, la:la + 1], 1).astype(BF16)
        sol_list = _unit_lower_solve_many(m_list, rhs_list, q)
        for (c, h), sol in zip(group, sol_list):
            i = c * GDN_HEADS + h
            u_s[i] = sol[:, 0:GDN_HEAD_DIM]
            wq_s[i, 0:q, :] = sol[:, GDN_HEAD_DIM:2 * GDN_HEAD_DIM].astype(BF16)

    def store_mix(c, cols, val):
        if multi_seq:
            mix_ref[c, :, cols] = val[0:lv].astype(mix_ref.dtype)
        else:
            mix_ref[0, c * q:(c + 1) * q, cols] = val.astype(mix_ref.dtype)

    for items in ([list(range(nq))] if multi_seq else [[c] for c in range(nq)]):
        for c in items:
            rows = slice(c * q, (c + 1) * q)
            last = (c + 1) * q - 1
            s = slot(c)
            for g in range(SSD_GROUPS):
                gcols = slice(g * gw, (g + 1) * gw)
                c_g = xbc_c[rows, SSD_WIDTH + (SSD_GROUPS + g) * SSD_STATE:
                            SSD_WIDTH + (SSD_GROUPS + g + 1) * SSD_STATE]
                state = st_s[s, :, gcols]
                y_g = y_s[rows, gcols] + _bdot(c_g, state) * pe_s[rows, gcols]
                st_s[s, :, gcols] = state * pe_s[last:last + 1, gcols] + jnp.dot(
                    bgt_s[c * SSD_GROUPS + g], xcd_s[rows, gcols], preferred_element_type=F32)
                y_g = y_g * gate(zs_v[rows, gcols])
                ms = jnp.mean(y_g * y_g, axis=-1, keepdims=True)
                y_g = y_g * lax.rsqrt(ms + NORM_EPS) * snw_ref[:, gcols]
                store_mix(c, gcols, y_g)
        pairs = [(c, h) for c in items for h in range(GDN_HEADS)]
        ws = {}
        for (c, h) in pairs:
            ws[c, h] = jnp.dot(wq_s[c * GDN_HEADS + h], st_g[slot(c), h].astype(BF16),
                               preferred_element_type=F32)
        vn = {(c, h): (u_s[c * GDN_HEADS + h] - ws[c, h][0:q]).astype(BF16) for (c, h) in pairs}
        for (c, h) in pairs:
            i = c * GDN_HEADS + h
            la = _SM_A + h
            last = (c + 1) * q - 1
            s = slot(c)
            o = ws[c, h][q:2 * q] + jnp.dot(at_s[i], vn[c, h], preferred_element_type=F32)
            st_g[s, h] = st_g[s, h] * ecum[last:last + 1, la:la + 1] + jnp.dot(
                kdt_s[i], vn[c, h], preferred_element_type=F32)
            hc = slice(h * GDN_HEAD_DIM, (h + 1) * GDN_HEAD_DIM)
            ms = jnp.mean(o * o, axis=-1, keepdims=True)
            o = o * lax.rsqrt(ms + NORM_EPS) * gnw_ref[...] * gate(zg_v[c * q:(c + 1) * q, hc])
            store_mix(c, slice(SSD_WIDTH + h * GDN_HEAD_DIM, SSD_WIDTH + (h + 1) * GDN_HEAD_DIM), o)

    @pl.when(c_idx == nc - 1)
    def _store_state():
        for s in range(n_slots):
            for j in range(SSD_HEADS // 2):
                blk = st_s[s, :, pair * j:pair * (j + 1)].T
                nhs_ref[0, s, 2 * j:2 * j + 2] = blk.reshape(2, SSD_HEAD_DIM, SSD_STATE)
        nsg_ref[0] = st_g[...]


_STATE_TAILS = (
    (CONV_K - 1, SSD_CONV_DIM),
    (SSD_HEADS, SSD_HEAD_DIM, SSD_STATE),
    (CONV_K - 1, GDN_CONV_DIM),
    (GDN_HEADS, GDN_HEAD_DIM, GDN_HEAD_DIM),
)


_STATE_NAMES = ("cs", "hs", "cg", "sg")


def _mixer_call(zs, xbc, zg, qkv, sm, states_in, prev_out, p, q, nq, multi_seq, layer, chains, do_conv):
    B, L, _ = zs.shape
    rows = nq * q
    assert q % INV_BLOCK == 0
    zero_init = states_in is None
    if multi_seq:
        assert CONV_K - 1 <= L <= q and B % nq == 0 and do_conv
        nc, lv, bb = 1, L, nq
        tok = lambda wd: pl.BlockSpec((nq, L, wd), lambda b, c: (b, 0, 0))
    else:
        assert L % rows == 0
        nc, lv, bb = L // rows, rows, 1
        tok = lambda wd: pl.BlockSpec((1, rows, wd), lambda b, c: (b, c, 0))
    n_slots = bb
    per_b = lambda shape: pl.BlockSpec((1, bb) + shape, lambda b, c: (layer, b) + (0,) * len(shape))
    const = lambda shape: pl.BlockSpec(shape, lambda b, c: (0,) * len(shape))
    nch = nq * GDN_HEADS
    kept = [k for k in range(4) if do_conv or k in (1, 3)]

    names = ["zs", "xbc", "zg", "qkv", "sm"]
    in_specs = [tok(SSD_WIDTH), tok(SSD_CONV_DIM), tok(GDN_WIDTH), tok(GDN_CONV_DIM), tok(SMALL)]
    args = [zs, xbc, zg, qkv, sm]
    if not zero_init:
        for k in kept:
            names.append(_STATE_NAMES[k])
            in_specs.append(per_b(_STATE_TAILS[k]))
            args.append(states_in[k])
    consts = [("bias", (1, SMALL)), ("coef", (1, SMALL)), ("dexp", (1, SSD_WIDTH)), ("snw", (1, SSD_WIDTH)),
              ("gnw", (1, GDN_HEAD_DIM))]
    if do_conv:
        consts += [("scw", (CONV_K, SSD_CONV_DIM)), ("scb", (1, SSD_CONV_DIM)), ("gcw", (CONV_K, GDN_CONV_DIM))]
    for nm, shape in consts:
        names.append(nm)
        in_specs.append(const(shape))
        args.append(p[nm])
    aliases = {}
    if prev_out is not None:
        for k, a in enumerate(prev_out):
            names.append("alias%d" % k)
            aliases[len(args)] = 1 + k
            in_specs.append(pl.BlockSpec(memory_space=pl.ANY))
            args.append(a)
    names += ["mix"] + ["n" + _STATE_NAMES[k] for k in kept]
    out_specs = [tok(MIX_WIDTH)] + [per_b(_STATE_TAILS[k]) for k in kept]
    out_shape = [jax.ShapeDtypeStruct((B, L, MIX_WIDTH), BF16)] + [
        jax.ShapeDtypeStruct((DEPTH, B) + _STATE_TAILS[k], F32) for k in kept]
    scratch = [
        ("st_s", pltpu.VMEM((n_slots, SSD_STATE, SSD_WIDTH), F32)),
        ("st_g", pltpu.VMEM((n_slots, GDN_HEADS, GDN_HEAD_DIM, GDN_HEAD_DIM), F32)),
        ("y_s", pltpu.VMEM((rows, SSD_WIDTH), F32)),
        ("pe_s", pltpu.VMEM((rows, SSD_WIDTH), F32)),
        ("xcd_s", pltpu.VMEM((rows, SSD_WIDTH), BF16)),
        ("bgt_s", pltpu.VMEM((nq * SSD_GROUPS, SSD_STATE, q), BF16)),
        ("wq_s", pltpu.VMEM((nch, 2 * q, GDN_HEAD_DIM), BF16)),
        ("u_s", pltpu.VMEM((nch, q, GDN_HEAD_DIM), F32)),
        ("at_s", pltpu.VMEM((nch, q, q), BF16)),
        ("kdt_s", pltpu.VMEM((nch, GDN_HEAD_DIM, q), BF16)),
    ]
    if do_conv:
        xp_rows = nq * (q + SUBLANES) if multi_seq else rows + SUBLANES
        scratch += [
            ("xps", pltpu.VMEM((xp_rows, SSD_CONV_DIM), F32)),
            ("xpg", pltpu.VMEM((xp_rows, GDN_CONV_DIM), F32)),
            ("xbc_c", pltpu.VMEM((rows, SSD_CONV_DIM), F32)),
            ("qkv_c", pltpu.VMEM((rows, GDN_CONV_DIM), F32)),
        ]
    if multi_seq:
        scratch += [
            ("pzs", pltpu.VMEM((rows, SSD_WIDTH), F32)),
            ("pzg", pltpu.VMEM((rows, GDN_WIDTH), F32)),
            ("psm", pltpu.VMEM((rows, SMALL), F32)),
        ]
    names += [nm for nm, _ in scratch]
    return pl.pallas_call(
        functools.partial(_mixer_kernel, names=tuple(names), q=q, nq=nq, lv=lv, nc=nc, multi_seq=multi_seq,
                          zero_init=zero_init, do_conv=do_conv, chains=chains),
        grid=(B // bb, nc),
        in_specs=in_specs,
        out_specs=out_specs,
        out_shape=out_shape,
        input_output_aliases=aliases,
        scratch_shapes=[s for _, s in scratch],
        compiler_params=pltpu.CompilerParams(
            dimension_semantics=("arbitrary", "arbitrary"), vmem_limit_bytes=VMEM_LIMIT),
        name="mixer",
    )(*args)


def _prep_layer(l, norm_w, w_in, ssd_conv_w, ssd_conv_b, ssd_dt_bias, ssd_a_log, ssd_d, ssd_norm_w,
                gdn_conv_w, gdn_dt_bias, gdn_a_log, gdn_norm_w, w_out):
    wi = w_in[l]
    small = jnp.concatenate(
        [wi[:, _OFF_DT:_OFF_DT + SSD_HEADS], wi[:, _OFF_A:_OFF_A + GDN_HEADS], wi[:, _OFF_B:_OFF_B + GDN_HEADS],
         jnp.zeros((D_MODEL, SMALL - SSD_HEADS - 2 * GDN_HEADS), F32)], axis=1)
    sm_hi = small.astype(BF16)
    sm_lo = (small - sm_hi.astype(F32)).astype(BF16)
    pad = jnp.zeros((SMALL - SSD_HEADS - GDN_HEADS,), F32)
    return {
        "norm_w": norm_w[l].reshape(1, D_MODEL),
        "zs": wi[:, _OFF_ZS:_OFF_ZS + SSD_WIDTH].astype(BF16),
        "xbc": wi[:, _OFF_XBC:_OFF_XBC + SSD_CONV_DIM].astype(BF16),
        "zg": wi[:, _OFF_ZG:_OFF_ZG + GDN_WIDTH].astype(BF16),
        "qkv": wi[:, _OFF_QKV:_OFF_QKV + GDN_CONV_DIM].astype(BF16),
        "sm_hi": sm_hi,
        "sm_lo": sm_lo,
        "scw": ssd_conv_w[l],
        "scb": ssd_conv_b[l].reshape(1, SSD_CONV_DIM),
        "gcw": gdn_conv_w[l],
        "bias": jnp.concatenate([ssd_dt_bias[l], gdn_dt_bias[l], pad]).reshape(1, SMALL),
        "coef": jnp.concatenate([-jnp.exp(ssd_a_log[l]), -jnp.exp(gdn_a_log[l]), pad]).reshape(1, SMALL),
        "dexp": jnp.repeat(ssd_d[l], SSD_HEAD_DIM).reshape(1, SSD_WIDTH),
        "snw": ssd_norm_w[l].reshape(1, SSD_WIDTH),
        "gnw": gdn_norm_w[l].reshape(1, GDN_HEAD_DIM),
        "w_out": w_out[l].astype(BF16),
    }


def _trunk(x, mod, states, params, final_w, q, nq, tm, chains, act_dtype):
    B, L, _ = x.shape
    T = B * L
    per_row = L < SUBLANES
    steps_per_group = max(L // tm, 1)
    x2d = x.reshape(T, D_MODEL)
    conv_in_proj = states is None and not per_row
    new_states = None
    conv_states = ([], [])
    for l in range(DEPTH):
        p = params[l]
        shift, scale, gate = (mod[l, :, i * D_MODEL:(i + 1) * D_MODEL] for i in range(3))
        if per_row:
            rep = lambda a: jnp.repeat(a, L, axis=0).reshape(1, T, D_MODEL)
        else:
            rep = lambda a: a.reshape(B, 1, D_MODEL)
        zs, xbc, zg, qkv, sm, *cst = _inproj_call(x2d, rep(scale), rep(shift), p["norm_w"], p, tm, per_row,
                                                  steps_per_group, conv_in_proj, act_dtype)
        r3 = lambda a: a.reshape(B, L, a.shape[-1])
        mix, *new_states = _mixer_call(r3(zs), r3(xbc), r3(zg), r3(qkv), r3(sm), states, new_states, p,
                                       q, nq, per_row, l, chains, not conv_in_proj)
        x2d = _outproj_call(x2d, mix.reshape(T, MIX_WIDTH), rep(gate), p["w_out"], final_w, tm, per_row,
                            steps_per_group, final_norm=(l == DEPTH - 1))
        for lst, s in zip(conv_states, cst):
            lst.append(s)
    if conv_in_proj:
        new_states = [jnp.stack(conv_states[0]), new_states[0], jnp.stack(conv_states[1]), new_states[1]]
    return x2d.reshape(B, L, D_MODEL), new_states


def kernel(x_prompt, x_sample, state_ssd_conv, state_ssm, state_gdn_conv, state_gdn, c_prompt, c_sample,
           norm_w, w_ada, b_ada, w_in, ssd_conv_w, ssd_conv_b, ssd_dt_bias, ssd_a_log, ssd_d, ssd_norm_w,
           gdn_conv_w, gdn_dt_bias, gdn_a_log, gdn_norm_w, w_out, final_norm_w):
    bp = x_prompt.shape[0]
    params = [_prep_layer(l, norm_w, w_in, ssd_conv_w, ssd_conv_b, ssd_dt_bias, ssd_a_log, ssd_d, ssd_norm_w,
                          gdn_conv_w, gdn_dt_bias, gdn_a_log, gdn_norm_w, w_out) for l in range(DEPTH)]
    final_w = final_norm_w.reshape(1, D_MODEL)
    mod = _ada_call(jnp.concatenate([c_prompt, c_sample], axis=0), w_ada, b_ada)
    y_p, sp = _trunk(x_prompt, mod[:, :bp], None, params, final_w, q=64, nq=4, tm=512, chains=16,
                     act_dtype=BF16)
    y_s, ss = _trunk(x_sample, mod[:, bp:], (state_ssd_conv, state_ssm, state_gdn_conv, state_gdn),
                     params, final_w, q=16, nq=4, tm=x_sample.shape[0] * x_sample.shape[1], chains=16,
                     act_dtype=F32)
    return (y_p, y_s, sp[0], sp[1], sp[2], sp[3], ss[0], ss[1], ss[2], ss[3])
```

```python
import functools
import math

import jax
import jax.numpy as jnp
from jax import lax
from jax.experimental import pallas as pl
from jax.experimental.pallas import tpu as pltpu

F32 = jnp.float32
BF16 = jnp.bfloat16

D_MODEL = 1024
DEPTH = 2
SSD_WIDTH = 1024
SSD_HEAD_DIM = 64
SSD_HEADS = 16
SSD_GROUPS = 2
SSD_STATE = 128
SSD_CONV_DIM = SSD_WIDTH + 2 * SSD_GROUPS * SSD_STATE
GDN_WIDTH = 1024
GDN_HEAD_DIM = 128
GDN_HEADS = 8
GDN_CONV_DIM = 3 * GDN_WIDTH
MIX_WIDTH = SSD_WIDTH + GDN_WIDTH
CONV_K = 4
NORM_EPS = 1e-6
ADA_DIM = 3 * D_MODEL
SMALL = 128
LANES = 128
SUBLANES = 8
INV_BLOCK = 16
VMEM_LIMIT = 56 * 1024 * 1024

_OFF_ZS = 0
_OFF_XBC = _OFF_ZS + SSD_WIDTH
_OFF_DT = _OFF_XBC + SSD_CONV_DIM
_OFF_ZG = _OFF_DT + SSD_HEADS
_OFF_QKV = _OFF_ZG + GDN_WIDTH
_OFF_A = _OFF_QKV + GDN_CONV_DIM
_OFF_B = _OFF_A + GDN_HEADS
_SM_DT = 0
_SM_A = SSD_HEADS
_SM_B = SSD_HEADS + GDN_HEADS


def _bdot(a, b):
    return jnp.dot(a.astype(BF16), b.astype(BF16), preferred_element_type=F32)


def _bdot_nt(a, b):
    return lax.dot_general(a.astype(BF16), b.astype(BF16), (((1,), (1,)), ((), ())),
                           preferred_element_type=F32)


def _terms(x, n):
    out = []
    r = x
    for i in range(n):
        t = r.astype(BF16)
        out.append(t)
        if i + 1 < n:
            r = r - t.astype(F32)
    return out


def _const_dot(c, x, n):
    acc = None
    for t in _terms(x, n):
        p = jnp.dot(c, t, preferred_element_type=F32)
        acc = p if acc is None else acc + p
    return acc


def _dot_const(x, c, n):
    acc = None
    for t in _terms(x, n):
        p = jnp.dot(t, c, preferred_element_type=F32)
        acc = p if acc is None else acc + p
    return acc


def _transpose_via_identity(eye, x, n):
    acc = None
    for t in _terms(x, n):
        p = lax.dot_general(eye, t, (((1,), (1,)), ((), ())), preferred_element_type=F32)
        acc = p if acc is None else acc + p
    return acc


def _softplus(x):
    return jnp.maximum(x, 0.0) + jnp.log1p(jnp.exp(-jnp.abs(x)))


def _silu(x):
    return x * jax.nn.sigmoid(x)


def _ada_kernel(c_ref, w_ref, b_ref, o_ref):
    s = _silu(c_ref[...])
    w = w_ref[0]
    s_hi, s_lo = _terms(s, 2)
    w_hi, w_lo = _terms(w, 2)
    acc = jnp.dot(s_hi, w_hi, preferred_element_type=F32)
    acc = acc + (jnp.dot(s_hi, w_lo, preferred_element_type=F32)
                 + jnp.dot(s_lo, w_hi, preferred_element_type=F32))
    o_ref[0] = acc + b_ref[0]


def _ada_call(c_all, w_ada, b_ada):
    rows = c_all.shape[0]
    tn = D_MODEL
    return pl.pallas_call(
        _ada_kernel,
        grid=(DEPTH, ADA_DIM // tn),
        in_specs=[
            pl.BlockSpec((rows, D_MODEL), lambda l, j: (0, 0)),
            pl.BlockSpec((1, D_MODEL, tn), lambda l, j: (l, 0, j)),
            pl.BlockSpec((1, 1, tn), lambda l, j: (l, 0, j)),
        ],
        out_specs=pl.BlockSpec((1, rows, tn), lambda l, j: (l, 0, j)),
        out_shape=jax.ShapeDtypeStruct((DEPTH, rows, ADA_DIM), F32),
        compiler_params=pltpu.CompilerParams(
            dimension_semantics=("arbitrary", "arbitrary"), vmem_limit_bytes=VMEM_LIMIT),
        name="ada_mod",
    )(c_all, w_ada, b_ada.reshape(DEPTH, 1, ADA_DIM))


CONV_ROWS = 64
CONV_COLS = 128
PROJ_COLS = 512


def _conv_silu_rows(xp, w_ref, b_ref, out, rows, c0, c1, xp_row0=0, out_row0=0):
    rows_per = min(rows, CONV_ROWS)
    for i in range(rows // rows_per):
        r0 = i * rows_per
        for j in range((c1 - c0) // CONV_COLS):
            cols = slice(c0 + j * CONV_COLS, c0 + (j + 1) * CONV_COLS)
            x = xp[xp_row0 + r0:xp_row0 + r0 + rows_per + SUBLANES, cols]
            acc = x * w_ref[0:1, cols]
            for k in range(1, CONV_K):
                acc = pltpu.roll(acc, 1, 0) + x * w_ref[k:k + 1, cols]
            if b_ref is not None:
                acc = acc + b_ref[:, cols]
            out[out_row0 + r0:out_row0 + r0 + rows_per, cols] = _silu(acc[SUBLANES:, :]).astype(out.dtype)


PERM_CHUNK = SUBLANES * SUBLANES


def _qk_l2_scale(col):
    assert GDN_HEAD_DIM == LANES
    if col < GDN_WIDTH:
        return GDN_HEAD_DIM ** -0.5
    if col < 2 * GDN_WIDTH:
        return 1.0
    return None


def _conv_silu_permuted(xp, w_ref, b_ref, hist, up, out, rows, c0, l2_scale=None):
    sub0 = lax.broadcasted_iota(jnp.int32, (SUBLANES, LANES), 0) == 0
    taps = CONV_K - 1
    for j in range(PROJ_COLS // LANES):
        cols = slice(c0 + j * LANES, c0 + (j + 1) * LANES)
        lcols = slice(j * LANES, (j + 1) * LANES)
        w = [jnp.broadcast_to(w_ref[k:k + 1, cols], (SUBLANES, LANES)) for k in range(CONV_K)]
        b = None if b_ref is None else jnp.broadcast_to(b_ref[:, cols], (SUBLANES, LANES))
        prev = [hist[i * SUBLANES:(i + 1) * SUBLANES, cols] for i in range(taps)]
        for k in range(rows // PERM_CHUNK):
            base = k * PERM_CHUNK
            xs = [xp[base + a * SUBLANES:base + (a + 1) * SUBLANES, cols] for a in range(SUBLANES)]
            rolled = [pltpu.roll(xs[SUBLANES - taps + i], 1, 0) for i in range(taps)]
            ext = [jnp.where(sub0, prev[i], rolled[i]) for i in range(taps)] + xs
            for a in range(SUBLANES):
                y = ext[a] * w[0]
                for t in range(1, CONV_K):
                    y = y + ext[a + t] * w[t]
                if b is not None:
                    y = y + b
                y = _silu(y)
                scale = None if l2_scale is None else l2_scale(c0 + j * LANES)
                if scale is not None:
                    y = y * (lax.rsqrt(jnp.sum(y * y, axis=-1, keepdims=True) + NORM_EPS) * scale)
                up[j, pl.ds(base + a, SUBLANES, stride=SUBLANES), :] = y
            prev = rolled
        for i in range(taps):
            hist[i * SUBLANES:(i + 1) * SUBLANES, cols] = prev[i]
        out[:, cols] = up[j].astype(out.dtype)


def _inproj_kernel(*refs, tm, steps_per_seq, do_conv):
    x_ref, sc_ref, sh_ref, nw_ref, wzs_ref, wxbc_ref, wzg_ref, wqkv_ref, wsh_ref, wsl_ref = refs[:10]
    if do_conv:
        scw_ref, scb_ref, gcw_ref = refs[10:13]
        (zs_ref, xbc_ref, zg_ref, qkv_ref, sm_ref, ncs_ref, ncg_ref,
         xps, xpg, hist_s, hist_g, hs, hp, up) = refs[13:]
    else:
        zs_ref, xbc_ref, zg_ref, qkv_ref, sm_ref = refs[10:]
    x = x_ref[...]
    ms = jnp.mean(x * x, axis=-1, keepdims=True)
    h = x * lax.rsqrt(ms + NORM_EPS) * nw_ref[...]
    h = h * (1.0 + sc_ref[0]) + sh_ref[0]
    h_hi, h_lo = _terms(h, 2)
    sm = jnp.dot(h_hi, wsh_ref[...], preferred_element_type=F32)
    sm = sm + (jnp.dot(h_hi, wsl_ref[...], preferred_element_type=F32)
               + jnp.dot(h_lo, wsh_ref[...], preferred_element_type=F32))
    sm_ref[...] = sm
    if not do_conv:
        zs_ref[...] = jnp.dot(h_hi, wzs_ref[...], preferred_element_type=F32)
        xbc_ref[...] = jnp.dot(h_hi, wxbc_ref[...], preferred_element_type=F32)
        zg_ref[...] = jnp.dot(h_hi, wzg_ref[...], preferred_element_type=F32)
        qkv_ref[...] = jnp.dot(h_hi, wqkv_ref[...], preferred_element_type=F32)
        return

    @pl.when(pl.program_id(0) % steps_per_seq == 0)
    def _zero_history():
        hist_s[...] = jnp.zeros(hist_s.shape, F32)
        hist_g[...] = jnp.zeros(hist_g.shape, F32)

    assert tm % PERM_CHUNK == 0
    ncb = D_MODEL // LANES
    for cb in range(ncb):
        hs[cb] = h[:, cb * LANES:(cb + 1) * LANES]
    for g in range(0, tm // SUBLANES, 2):
        parts = [jnp.concatenate(
            [hs[cb, pl.ds(PERM_CHUNK * (gg // SUBLANES) + gg % SUBLANES, SUBLANES, stride=SUBLANES), :]
             for cb in range(ncb)], axis=1) for gg in (g, g + 1)]
        hp[g * SUBLANES:(g + 2) * SUBLANES, :] = jnp.concatenate(parts, axis=0).astype(BF16)

    conv_blocks = [(xp, w_ref, cw_ref, cb_ref, hist, out, c0)
                   for xp, w_ref, cw_ref, cb_ref, hist, out, width in (
                       (xps, wxbc_ref, scw_ref, scb_ref, hist_s, xbc_ref, SSD_CONV_DIM),
                       (xpg, wqkv_ref, gcw_ref, None, hist_g, qkv_ref, GDN_CONV_DIM))
                   for c0 in range(0, width, PROJ_COLS)]
    gate_blocks = [(w_ref, out, c0) for w_ref, out, width in ((wzs_ref, zs_ref, SSD_WIDTH),
                                                              (wzg_ref, zg_ref, GDN_WIDTH))
                   for c0 in range(0, width, PROJ_COLS)]

    def project(blk):
        xp, w_ref, _, _, _, _, c0 = blk
        cols = slice(c0, c0 + PROJ_COLS)
        xp[:, cols] = jnp.dot(hp[...], w_ref[:, cols], preferred_element_type=F32)

    def gate_block(i):
        w_ref, out, c0 = gate_blocks[i]
        cols = slice(c0, c0 + PROJ_COLS)
        out[:, cols] = _silu(jnp.dot(h_hi, w_ref[:, cols], preferred_element_type=F32)).astype(out.dtype)

    early_gates = len(gate_blocks) // 2
    for i in range(early_gates):
        gate_block(i)
    project(conv_blocks[0])
    for j, blk in enumerate(conv_blocks):
        if j + 1 < len(conv_blocks):
            project(conv_blocks[j + 1])
        if j % 2 == 0 and early_gates + j // 2 < len(gate_blocks):
            gate_block(early_gates + j // 2)
        xp, _, cw_ref, cb_ref, hist, out, c0 = blk
        _conv_silu_permuted(xp, cw_ref, cb_ref, hist, up, out, tm, c0,
                            l2_scale=_qk_l2_scale if xp is xpg else None)
    assert len(gate_blocks) - early_gates <= (len(conv_blocks) + 1) // 2
    for xp, nst in ((xps, ncs_ref), (xpg, ncg_ref)):
        for i in range(CONV_K - 1):
            row = tm - PERM_CHUNK + SUBLANES * (SUBLANES - (CONV_K - 1) + i) + SUBLANES - 1
            nst[0, i:i + 1, :] = xp[row:row + 1, :]


def _row_mod_spec(per_row, tm, steps_per_group):
    if per_row:
        return pl.BlockSpec((1, tm, D_MODEL), lambda i: (0, i, 0))
    return pl.BlockSpec((1, 1, D_MODEL), lambda i: (i // steps_per_group, 0, 0))


def _resident(shape):
    nd = len(shape)
    return pl.BlockSpec(shape, lambda i: (0,) * nd, pipeline_mode=pl.Buffered(1))


def _inproj_call(x2d, scale, shift, norm_w, w, tm, per_row, steps_per_group, do_conv, act_dtype):
    T = x2d.shape[0]
    assert T % tm == 0
    widths = (SSD_WIDTH, SSD_CONV_DIM, GDN_WIDTH, GDN_CONV_DIM, SMALL)
    dtypes = (act_dtype, act_dtype, act_dtype, act_dtype, F32)
    row = lambda wd: pl.BlockSpec((tm, wd), lambda i: (i, 0))
    in_specs = [
        row(D_MODEL),
        _row_mod_spec(per_row, tm, steps_per_group),
        _row_mod_spec(per_row, tm, steps_per_group),
        _resident((1, D_MODEL)),
        _resident((D_MODEL, SSD_WIDTH)),
        _resident((D_MODEL, SSD_CONV_DIM)),
        _resident((D_MODEL, GDN_WIDTH)),
        _resident((D_MODEL, GDN_CONV_DIM)),
        _resident((D_MODEL, SMALL)),
        _resident((D_MODEL, SMALL)),
    ]
    args = [x2d, scale, shift, norm_w, w["zs"], w["xbc"], w["zg"], w["qkv"], w["sm_hi"], w["sm_lo"]]
    out_specs = [row(wd) for wd in widths]
    out_shape = [jax.ShapeDtypeStruct((T, wd), dt) for wd, dt in zip(widths, dtypes)]
    scratch = []
    if do_conv:
        assert not per_row
        nseq = T // (tm * steps_per_group)
        in_specs += [_resident((CONV_K, SSD_CONV_DIM)), _resident((1, SSD_CONV_DIM)),
                     _resident((CONV_K, GDN_CONV_DIM))]
        args += [w["scw"], w["scb"], w["gcw"]]
        for wd in (SSD_CONV_DIM, GDN_CONV_DIM):
            out_specs.append(pl.BlockSpec((1, CONV_K - 1, wd), lambda i: (i // steps_per_group, 0, 0)))
            out_shape.append(jax.ShapeDtypeStruct((nseq, CONV_K - 1, wd), F32))
            scratch.append(pltpu.VMEM((tm, wd), F32))
        scratch += [pltpu.VMEM(((CONV_K - 1) * SUBLANES, wd), F32) for wd in (SSD_CONV_DIM, GDN_CONV_DIM)]
        scratch += [pltpu.VMEM((D_MODEL // LANES, tm, LANES), F32), pltpu.VMEM((tm, D_MODEL), BF16),
                    pltpu.VMEM((PROJ_COLS // LANES, tm, LANES), F32)]
    return pl.pallas_call(
        functools.partial(_inproj_kernel, tm=tm, steps_per_seq=steps_per_group, do_conv=do_conv),
        grid=(T // tm,),
        in_specs=in_specs,
        out_specs=out_specs,
        out_shape=out_shape,
        scratch_shapes=scratch,
        compiler_params=pltpu.CompilerParams(
            dimension_semantics=("arbitrary",), vmem_limit_bytes=VMEM_LIMIT),
        name="in_proj",
    )(*args)


def _outproj_kernel(x_ref, mix_ref, gate_ref, w_ref, fw_ref, o_ref, *, final_norm):
    y = jnp.dot(mix_ref[...], w_ref[...], preferred_element_type=F32)
    out = x_ref[...] + gate_ref[0] * y
    if final_norm:
        ms = jnp.mean(out * out, axis=-1, keepdims=True)
        out = out * lax.rsqrt(ms + NORM_EPS) * fw_ref[...]
    o_ref[...] = out


def _outproj_call(x2d, mix2d, gate, w_out, final_w, tm, per_row, steps_per_group, final_norm):
    T = x2d.shape[0]
    return pl.pallas_call(
        functools.partial(_outproj_kernel, final_norm=final_norm),
        grid=(T // tm,),
        in_specs=[
            pl.BlockSpec((tm, D_MODEL), lambda i: (i, 0)),
            pl.BlockSpec((tm, MIX_WIDTH), lambda i: (i, 0)),
            _row_mod_spec(per_row, tm, steps_per_group),
            _resident((MIX_WIDTH, D_MODEL)),
            _resident((1, D_MODEL)),
        ],
        out_specs=pl.BlockSpec((tm, D_MODEL), lambda i: (i, 0)),
        out_shape=jax.ShapeDtypeStruct((T, D_MODEL), F32),
        compiler_params=pltpu.CompilerParams(
            dimension_semantics=("arbitrary",), vmem_limit_bytes=VMEM_LIMIT),
        name="out_proj",
    )(x2d, mix2d, gate, w_out, final_w)


def _run_interleaved(*gens):
    live = [g for g in gens if g is not None]
    while live:
        for g in list(live):
            try:
                next(g)
            except StopIteration:
                live.remove(g)


def _chain_gens(gens):
    for g in gens:
        yield from g


def _unit_lower_solve_many(m_list, rhs_list, q, out):
    nb = q // INV_BLOCK
    if nb > 1:
        r = lax.broadcasted_iota(jnp.int32, (q, q), 0) // INV_BLOCK
        c = lax.broadcasted_iota(jnp.int32, (q, q), 1) // INV_BLOCK
        same = r == c
        d_list = [jnp.where(same, m, 0.0) for m in m_list]
        o_list = [m - d for m, d in zip(m_list, d_list)]
    else:
        d_list = m_list
    n_list = [-d for d in d_list]
    e_list = d_list
    for _ in range(int(math.log2(INV_BLOCK)) - 1):
        e_list = [_bdot(e, e) for e in e_list]
        yield
        n_list = [n + e + _bdot(n, e) for n, e in zip(n_list, e_list)]
        yield
    y_list = [rhs + _bdot(n, rhs) for n, rhs in zip(n_list, rhs_list)]
    if nb > 1:
        p_list = [o + _bdot(n, o) for n, o in zip(n_list, o_list)]
        yield
        r_list = [-p for p in p_list]
        e_list = p_list
        for _ in range(int(math.ceil(math.log2(nb))) - 1):
            e_list = [_bdot(e, e) for e in e_list]
            yield
            r_list = [rr + e + _bdot(rr, e) for rr, e in zip(r_list, e_list)]
            yield
        y_list = [y + _bdot(rr, y) for rr, y in zip(r_list, y_list)]
    yield
    out.extend(y_list)


def _mixer_kernel(*refs, names, q, nq, lv, nc, multi_seq, zero_init, do_conv, chains):
    r = dict(zip(names, refs))
    zs_ref, xbc_ref, zg_ref, qkv_ref, sm_ref = r["zs"], r["xbc"], r["zg"], r["qkv"], r["sm"]
    bias_ref, coef_ref, dexp_ref, snw_ref, gnw_ref = r["bias"], r["coef"], r["dexp"], r["snw"], r["gnw"]
    mix_ref, nhs_ref, nsg_ref = r["mix"], r["nhs"], r["nsg"]
    st_s, st_g, y_s, pe_s, xcd_s, bgt_s = r["st_s"], r["st_g"], r["y_s"], r["pe_s"], r["xcd_s"], r["bgt_s"]
    wq_s, u_s, at_s, kdt_s = r["wq_s"], r["u_s"], r["at_s"], r["kdt_s"]

    c_idx = pl.program_id(1)
    rr_ = nq * q
    n_slots = nq if multi_seq else 1
    slot = (lambda c: c) if multi_seq else (lambda c: 0)
    hist = q + SUBLANES
    pair = 2 * SSD_HEAD_DIM
    heads_per_group = SSD_HEADS // SSD_GROUPS
    gw = SSD_WIDTH // SSD_GROUPS
    assert not (do_conv and not multi_seq and nc > 1 and lv != rr_)

    @pl.when(c_idx == 0)
    def _load_state():
        if zero_init:
            st_s[...] = jnp.zeros(st_s.shape, F32)
            st_g[...] = jnp.zeros(st_g.shape, F32)
        else:
            for s in range(n_slots):
                for j in range(SSD_HEADS // 2):
                    blk = r["hs"][0, s, 2 * j:2 * j + 2].reshape(pair, SSD_STATE)
                    st_s[s, :, pair * j:pair * (j + 1)] = blk.T
            st_g[...] = r["sg"][0]
        if do_conv:
            for xp, key, width in ((r["xps"], "cs", SSD_CONV_DIM), (r["xpg"], "cg", GDN_CONV_DIM)):
                for s in range(n_slots):
                    xp[s * hist:s * hist + SUBLANES, :] = jnp.zeros((SUBLANES, width), F32)
                    if not zero_init:
                        xp[s * hist + SUBLANES - (CONV_K - 1):s * hist + SUBLANES, :] = r[key][0, s]

    if do_conv:
        for x_ref, xp, w_ref, b_ref, out, nst, width in (
                (xbc_ref, r["xps"], r["scw"], r["scb"], r["xbc_c"], r["ncs"], SSD_CONV_DIM),
                (qkv_ref, r["xpg"], r["gcw"], None, r["qkv_c"], r["ncg"], GDN_CONV_DIM)):
            if multi_seq:
                for c in range(nq):
                    base = c * hist + SUBLANES
                    xp[base:base + lv, :] = x_ref[c]
                    xp[base + lv:base + q, :] = jnp.zeros((q - lv, width), F32)
                    _conv_silu_rows(xp, w_ref, b_ref, out, q, 0, width, xp_row0=c * hist, out_row0=c * q)
                    nst[0, c] = xp[base + lv - (CONV_K - 1):base + lv, :]
            else:
                xp[SUBLANES:SUBLANES + rr_, :] = x_ref[0]
                _conv_silu_rows(xp, w_ref, b_ref, out, rr_, 0, width)

                @pl.when(c_idx == nc - 1)
                def _store_conv_state():
                    nst[0, 0] = xp[SUBLANES + rr_ - (CONV_K - 1):SUBLANES + rr_, :]

                if nc > 1:
                    xp[0:SUBLANES, :] = xp[rr_:rr_ + SUBLANES, :]
        xbc_c, qkv_c = r["xbc_c"], r["qkv_c"]
        gate = _silu
    else:
        xbc_c, qkv_c = xbc_ref.at[0], qkv_ref.at[0]
        gate = lambda z: z.astype(F32)

    if multi_seq:
        for src, dst, wd in ((zs_ref, r["pzs"], SSD_WIDTH), (zg_ref, r["pzg"], GDN_WIDTH),
                             (sm_ref, r["psm"], SMALL)):
            for c in range(nq):
                dst[c * q:c * q + lv, :] = src[c].astype(F32)
                dst[c * q + lv:(c + 1) * q, :] = jnp.zeros((q - lv, wd), F32)
        zs_v, zg_v, sm = r["pzs"], r["pzg"], r["psm"][...]
    else:
        zs_v, zg_v, sm = zs_ref.at[0], zg_ref.at[0], sm_ref[0]

    sp = _softplus(sm + bias_ref[...])
    beta_all = jax.nn.sigmoid(sm)
    if multi_seq and lv < q:
        assert q & (q - 1) == 0
        valid = jnp.bitwise_and(lax.broadcasted_iota(jnp.int32, (rr_, SMALL), 0), q - 1) < lv
        sp = jnp.where(valid, sp, 0.0)
        beta_all = jnp.where(valid, beta_all, 0.0)
    ag = sp * coef_ref[...]
    r_all = lax.broadcasted_iota(jnp.int32, (rr_, rr_), 0)
    c_all = lax.broadcasted_iota(jnp.int32, (rr_, rr_), 1)
    chunk_start = (r_all // q) * q
    tril_bd = jnp.where(r_all >= c_all, jnp.where(c_all >= chunk_start, 1.0, 0.0), 0.0).astype(BF16)
    rq = lax.broadcasted_iota(jnp.int32, (q, q), 0)
    cq = lax.broadcasted_iota(jnp.int32, (q, q), 1)
    incl = rq >= cq
    strict = rq > cq
    r128 = lax.broadcasted_iota(jnp.int32, (LANES, LANES), 0)
    c128 = lax.broadcasted_iota(jnp.int32, (LANES, LANES), 1)
    eye = jnp.where(r128 == c128, 1.0, 0.0).astype(BF16)
    cum = _const_dot(tril_bd, ag, 3)
    cum_t = _transpose_via_identity(eye, cum, 3)
    sp_t = _transpose_via_identity(eye, sp, 3)
    ecum = jnp.exp(cum)
    if nq > 1:
        cum_last = jnp.concatenate(
            [jnp.broadcast_to(cum[(c + 1) * q - 1:(c + 1) * q, :], (q, SMALL)) for c in range(nq)], axis=0)
    else:
        cum_last = cum[q - 1:q, :]
    to_end = jnp.exp(cum_last - cum)
    er = lax.broadcasted_iota(jnp.int32, (SMALL, SSD_WIDTH), 0)
    ec = lax.broadcasted_iota(jnp.int32, (SMALL, SSD_WIDTH), 1) // SSD_HEAD_DIM
    expand = jnp.where(er == ec, 1.0, 0.0).astype(BF16)
    pe_s[...] = _dot_const(ecum, expand, 2)
    p_dtend = _dot_const(sp * to_end, expand, 2)
    xcd_s[...] = (xbc_c[:, 0:SSD_WIDTH].astype(F32) * p_dtend).astype(BF16)

    lane = lax.broadcasted_iota(jnp.int32, (q, pair), 1)
    lo_half = lane < SSD_HEAD_DIM
    scores = {}
    for c in range(nq):
        rows = slice(c * q, (c + 1) * q)
        for g in range(SSD_GROUPS):
            b_g = xbc_c[rows, SSD_WIDTH + g * SSD_STATE:SSD_WIDTH + (g + 1) * SSD_STATE]
            c_g = xbc_c[rows, SSD_WIDTH + (SSD_GROUPS + g) * SSD_STATE:
                        SSD_WIDTH + (SSD_GROUPS + g + 1) * SSD_STATE]
            scores[c, g] = _bdot_nt(c_g, b_g)
            bgt_s[c * SSD_GROUPS + g] = _transpose_via_identity(eye, b_g, 1).astype(BF16)
    def ssd_diag():
        for c in range(nq):
            rows = slice(c * q, (c + 1) * q)
            for j in range(SSD_HEADS // 2):
                g = j // (heads_per_group // 2)
                pcols = slice(pair * j, pair * (j + 1))
                x_pair = xbc_c[rows, pcols].astype(F32)
                y_pair = x_pair * dexp_ref[:, pcols]
                for half in range(2):
                    h = 2 * j + half
                    seg = cum[rows, h:h + 1] - cum_t[h:h + 1, rows]
                    lmat = jnp.exp(jnp.where(incl, seg, -jnp.inf))
                    m_h = scores[c, g] * lmat * sp_t[h:h + 1, rows]
                    x_h = jnp.where(lo_half if half == 0 else jnp.logical_not(lo_half), x_pair, 0.0)
                    y_pair = y_pair + _bdot(m_h, x_h)
                y_s[rows, pcols] = y_pair
                if j % 2 == 1:
                    yield

    def gdn_independent(group):
        m_list, rhs_list = [], []
        for n, (c, h) in enumerate(group):
            rows = slice(c * q, (c + 1) * q)
            i = c * GDN_HEADS + h
            qh = qkv_c[rows, h * GDN_HEAD_DIM:(h + 1) * GDN_HEAD_DIM].astype(F32)
            kh = qkv_c[rows, GDN_WIDTH + h * GDN_HEAD_DIM:GDN_WIDTH + (h + 1) * GDN_HEAD_DIM].astype(F32)
            vh = qkv_c[rows, 2 * GDN_WIDTH + h * GDN_HEAD_DIM:
                       2 * GDN_WIDTH + (h + 1) * GDN_HEAD_DIM].astype(F32)
            if do_conv:
                qh = qh * lax.rsqrt(jnp.sum(qh * qh, axis=-1, keepdims=True) + NORM_EPS) * (GDN_HEAD_DIM ** -0.5)
                kh = kh * lax.rsqrt(jnp.sum(kh * kh, axis=-1, keepdims=True) + NORM_EPS)
            la = _SM_A + h
            bcol = beta_all[rows, _SM_B + h:_SM_B + h + 1]
            eg = ecum[rows, la:la + 1]
            kb = kh * bcol
            dec = jnp.exp(jnp.where(incl, cum[rows, la:la + 1] - cum_t[la:la + 1, rows], -jnp.inf))
            kq = _bdot_nt(jnp.concatenate([kb, qh], axis=0), kh)
            m_list.append(jnp.where(strict, kq[0:q] * dec, 0.0))
            at_s[i] = (kq[q:2 * q] * dec).astype(BF16)
            rhs_list.append(jnp.concatenate([vh * bcol, kb * eg], axis=1))
            wq_s[i, q:2 * q, :] = (qh * eg).astype(BF16)
            kdt_s[i] = _transpose_via_identity(eye, kh * to_end[rows, la:la + 1], 1).astype(BF16)
            if n % 4 == 3:
                yield
        sol_list = []
        yield from _unit_lower_solve_many(m_list, rhs_list, q, sol_list)
        for (c, h), sol in zip(group, sol_list):
            i = c * GDN_HEADS + h
            u_s[i] = sol[:, 0:GDN_HEAD_DIM]
            wq_s[i, 0:q, :] = sol[:, GDN_HEAD_DIM:2 * GDN_HEAD_DIM].astype(BF16)
        yield

    def store_mix(c, cols, val):
        if multi_seq:
            mix_ref[c, :, cols] = val[0:lv].astype(mix_ref.dtype)
        else:
            mix_ref[0, c * q:(c + 1) * q, cols] = val.astype(mix_ref.dtype)

    def recurrences(items):
        for c in items:
            rows = slice(c * q, (c + 1) * q)
            last = (c + 1) * q - 1
            s = slot(c)
            for g in range(SSD_GROUPS):
                gcols = slice(g * gw, (g + 1) * gw)
                c_g = xbc_c[rows, SSD_WIDTH + (SSD_GROUPS + g) * SSD_STATE:
                            SSD_WIDTH + (SSD_GROUPS + g + 1) * SSD_STATE]
                state = st_s[s, :, gcols]
                y_g = y_s[rows, gcols] + _bdot(c_g, state) * pe_s[rows, gcols]
                st_s[s, :, gcols] = state * pe_s[last:last + 1, gcols] + jnp.dot(
                    bgt_s[c * SSD_GROUPS + g], xcd_s[rows, gcols], preferred_element_type=F32)
                y_g = y_g * gate(zs_v[rows, gcols])
                ms = jnp.mean(y_g * y_g, axis=-1, keepdims=True)
                y_g = y_g * lax.rsqrt(ms + NORM_EPS) * snw_ref[:, gcols]
                store_mix(c, gcols, y_g)
            yield
        pairs = [(c, h) for c in items for h in range(GDN_HEADS)]
        ws = {}
        for (c, h) in pairs:
            ws[c, h] = jnp.dot(wq_s[c * GDN_HEADS + h], st_g[slot(c), h].astype(BF16),
                               preferred_element_type=F32)
        yield
        vn = {(c, h): (u_s[c * GDN_HEADS + h] - ws[c, h][0:q]).astype(BF16) for (c, h) in pairs}
        for (c, h) in pairs:
            i = c * GDN_HEADS + h
            la = _SM_A + h
            last = (c + 1) * q - 1
            s = slot(c)
            o = ws[c, h][q:2 * q] + jnp.dot(at_s[i], vn[c, h], preferred_element_type=F32)
            st_g[s, h] = st_g[s, h] * ecum[last:last + 1, la:la + 1] + jnp.dot(
                kdt_s[i], vn[c, h], preferred_element_type=F32)
            hc = slice(h * GDN_HEAD_DIM, (h + 1) * GDN_HEAD_DIM)
            ms = jnp.mean(o * o, axis=-1, keepdims=True)
            o = o * lax.rsqrt(ms + NORM_EPS) * gnw_ref[...] * gate(zg_v[c * q:(c + 1) * q, hc])
            store_mix(c, slice(SSD_WIDTH + h * GDN_HEAD_DIM, SSD_WIDTH + (h + 1) * GDN_HEAD_DIM), o)
        yield

    all_ch = [(c, h) for c in range(nq) for h in range(GDN_HEADS)]
    groups = [all_ch[i:i + chains] for i in range(0, len(all_ch), chains)]
    batches = [list(range(nq))] if multi_seq else [[c] for c in range(nq)]
    assert chains % GDN_HEADS == 0
    pending = list(batches)
    done_items = set()
    for gi, grp in enumerate(groups):
        runnable = [b for b in pending if set(b) <= done_items]
        for b in runnable:
            pending.remove(b)
        others = [_chain_gens([recurrences(b) for b in runnable])] if runnable else []
        if gi == 0:
            others.append(ssd_diag())
        _run_interleaved(gdn_independent(grp), *others)
        done_items |= {c for c, _ in grp}
    for b in pending:
        for _ in recurrences(b):
            pass

    @pl.when(c_idx == nc - 1)
    def _store_state():
        for s in range(n_slots):
            for j in range(SSD_HEADS // 2):
                blk = st_s[s, :, pair * j:pair * (j + 1)].T
                nhs_ref[0, s, 2 * j:2 * j + 2] = blk.reshape(2, SSD_HEAD_DIM, SSD_STATE)
        nsg_ref[0] = st_g[...]


_STATE_TAILS = (
    (CONV_K - 1, SSD_CONV_DIM),
    (SSD_HEADS, SSD_HEAD_DIM, SSD_STATE),
    (CONV_K - 1, GDN_CONV_DIM),
    (GDN_HEADS, GDN_HEAD_DIM, GDN_HEAD_DIM),
)


_STATE_NAMES = ("cs", "hs", "cg", "sg")


def _mixer_call(zs, xbc, zg, qkv, sm, states_in, prev_out, p, q, nq, multi_seq, layer, chains, do_conv):
    B, L, _ = zs.shape
    rows = nq * q
    assert q % INV_BLOCK == 0
    zero_init = states_in is None
    if multi_seq:
        assert CONV_K - 1 <= L <= q and B % nq == 0 and do_conv
        nc, lv, bb = 1, L, nq
        tok = lambda wd: pl.BlockSpec((nq, L, wd), lambda b, c: (b, 0, 0))
    else:
        assert L % rows == 0
        nc, lv, bb = L // rows, rows, 1
        tok = lambda wd: pl.BlockSpec((1, rows, wd), lambda b, c: (b, c, 0))
    n_slots = bb
    per_b = lambda shape: pl.BlockSpec((1, bb) + shape, lambda b, c: (layer, b) + (0,) * len(shape))
    const = lambda shape: pl.BlockSpec(shape, lambda b, c: (0,) * len(shape))
    nch = nq * GDN_HEADS
    kept = [k for k in range(4) if do_conv or k in (1, 3)]

    names = ["zs", "xbc", "zg", "qkv", "sm"]
    in_specs = [tok(SSD_WIDTH), tok(SSD_CONV_DIM), tok(GDN_WIDTH), tok(GDN_CONV_DIM), tok(SMALL)]
    args = [zs, xbc, zg, qkv, sm]
    if not zero_init:
        for k in kept:
            names.append(_STATE_NAMES[k])
            in_specs.append(per_b(_STATE_TAILS[k]))
            args.append(states_in[k])
    consts = [("bias", (1, SMALL)), ("coef", (1, SMALL)), ("dexp", (1, SSD_WIDTH)), ("snw", (1, SSD_WIDTH)),
              ("gnw", (1, GDN_HEAD_DIM))]
    if do_conv:
        consts += [("scw", (CONV_K, SSD_CONV_DIM)), ("scb", (1, SSD_CONV_DIM)), ("gcw", (CONV_K, GDN_CONV_DIM))]
    for nm, shape in consts:
        names.append(nm)
        in_specs.append(const(shape))
        args.append(p[nm])
    aliases = {}
    if prev_out is not None:
        for k, a in enumerate(prev_out):
            names.append("alias%d" % k)
            aliases[len(args)] = 1 + k
            in_specs.append(pl.BlockSpec(memory_space=pl.ANY))
            args.append(a)
    names += ["mix"] + ["n" + _STATE_NAMES[k] for k in kept]
    out_specs = [tok(MIX_WIDTH)] + [per_b(_STATE_TAILS[k]) for k in kept]
    out_shape = [jax.ShapeDtypeStruct((B, L, MIX_WIDTH), BF16)] + [
        jax.ShapeDtypeStruct((DEPTH, B) + _STATE_TAILS[k], F32) for k in kept]
    scratch = [
        ("st_s", pltpu.VMEM((n_slots, SSD_STATE, SSD_WIDTH), F32)),
        ("st_g", pltpu.VMEM((n_slots, GDN_HEADS, GDN_HEAD_DIM, GDN_HEAD_DIM), F32)),
        ("y_s", pltpu.VMEM((rows, SSD_WIDTH), F32)),
        ("pe_s", pltpu.VMEM((rows, SSD_WIDTH), F32)),
        ("xcd_s", pltpu.VMEM((rows, SSD_WIDTH), BF16)),
        ("bgt_s", pltpu.VMEM((nq * SSD_GROUPS, SSD_STATE, q), BF16)),
        ("wq_s", pltpu.VMEM((nch, 2 * q, GDN_HEAD_DIM), BF16)),
        ("u_s", pltpu.VMEM((nch, q, GDN_HEAD_DIM), F32)),
        ("at_s", pltpu.VMEM((nch, q, q), BF16)),
        ("kdt_s", pltpu.VMEM((nch, GDN_HEAD_DIM, q), BF16)),
    ]
    if do_conv:
        xp_rows = nq * (q + SUBLANES) if multi_seq else rows + SUBLANES
        scratch += [
            ("xps", pltpu.VMEM((xp_rows, SSD_CONV_DIM), F32)),
            ("xpg", pltpu.VMEM((xp_rows, GDN_CONV_DIM), F32)),
            ("xbc_c", pltpu.VMEM((rows, SSD_CONV_DIM), F32)),
            ("qkv_c", pltpu.VMEM((rows, GDN_CONV_DIM), F32)),
        ]
    if multi_seq:
        scratch += [
            ("pzs", pltpu.VMEM((rows, SSD_WIDTH), F32)),
            ("pzg", pltpu.VMEM((rows, GDN_WIDTH), F32)),
            ("psm", pltpu.VMEM((rows, SMALL), F32)),
        ]
    names += [nm for nm, _ in scratch]
    return pl.pallas_call(
        functools.partial(_mixer_kernel, names=tuple(names), q=q, nq=nq, lv=lv, nc=nc, multi_seq=multi_seq,
                          zero_init=zero_init, do_conv=do_conv, chains=chains),
        grid=(B // bb, nc),
        in_specs=in_specs,
        out_specs=out_specs,
        out_shape=out_shape,
        input_output_aliases=aliases,
        scratch_shapes=[s for _, s in scratch],
        compiler_params=pltpu.CompilerParams(
            dimension_semantics=("arbitrary", "arbitrary"), vmem_limit_bytes=VMEM_LIMIT),
        name="mixer",
    )(*args)


def _prep_layer(l, norm_w, w_in, ssd_conv_w, ssd_conv_b, ssd_dt_bias, ssd_a_log, ssd_d, ssd_norm_w,
                gdn_conv_w, gdn_dt_bias, gdn_a_log, gdn_norm_w, w_out):
    wi = w_in[l]
    small = jnp.concatenate(
        [wi[:, _OFF_DT:_OFF_DT + SSD_HEADS], wi[:, _OFF_A:_OFF_A + GDN_HEADS], wi[:, _OFF_B:_OFF_B + GDN_HEADS],
         jnp.zeros((D_MODEL, SMALL - SSD_HEADS - 2 * GDN_HEADS), F32)], axis=1)
    sm_hi = small.astype(BF16)
    sm_lo = (small - sm_hi.astype(F32)).astype(BF16)
    pad = jnp.zeros((SMALL - SSD_HEADS - GDN_HEADS,), F32)
    return {
        "norm_w": norm_w[l].reshape(1, D_MODEL),
        "zs": wi[:, _OFF_ZS:_OFF_ZS + SSD_WIDTH].astype(BF16),
        "xbc": wi[:, _OFF_XBC:_OFF_XBC + SSD_CONV_DIM].astype(BF16),
        "zg": wi[:, _OFF_ZG:_OFF_ZG + GDN_WIDTH].astype(BF16),
        "qkv": wi[:, _OFF_QKV:_OFF_QKV + GDN_CONV_DIM].astype(BF16),
        "sm_hi": sm_hi,
        "sm_lo": sm_lo,
        "scw": ssd_conv_w[l],
        "scb": ssd_conv_b[l].reshape(1, SSD_CONV_DIM),
        "gcw": gdn_conv_w[l],
        "bias": jnp.concatenate([ssd_dt_bias[l], gdn_dt_bias[l], pad]).reshape(1, SMALL),
        "coef": jnp.concatenate([-jnp.exp(ssd_a_log[l]), -jnp.exp(gdn_a_log[l]), pad]).reshape(1, SMALL),
        "dexp": jnp.repeat(ssd_d[l], SSD_HEAD_DIM).reshape(1, SSD_WIDTH),
        "snw": ssd_norm_w[l].reshape(1, SSD_WIDTH),
        "gnw": gdn_norm_w[l].reshape(1, GDN_HEAD_DIM),
        "w_out": w_out[l].astype(BF16),
    }


def _trunk(x, mod, states, params, final_w, q, nq, tm, chains, act_dtype):
    B, L, _ = x.shape
    T = B * L
    per_row = L < SUBLANES
    steps_per_group = max(L // tm, 1)
    x2d = x.reshape(T, D_MODEL)
    conv_in_proj = states is None and not per_row
    new_states = None
    conv_states = ([], [])
    for l in range(DEPTH):
        p = params[l]
        shift, scale, gate = (mod[l, :, i * D_MODEL:(i + 1) * D_MODEL] for i in range(3))
        if per_row:
            rep = lambda a: jnp.repeat(a, L, axis=0).reshape(1, T, D_MODEL)
        else:
            rep = lambda a: a.reshape(B, 1, D_MODEL)
        zs, xbc, zg, qkv, sm, *cst = _inproj_call(x2d, rep(scale), rep(shift), p["norm_w"], p, tm, per_row,
                                                  steps_per_group, conv_in_proj, act_dtype)
        r3 = lambda a: a.reshape(B, L, a.shape[-1])
        mix, *new_states = _mixer_call(r3(zs), r3(xbc), r3(zg), r3(qkv), r3(sm), states, new_states, p,
                                       q, nq, per_row, l, chains, not conv_in_proj)
        x2d = _outproj_call(x2d, mix.reshape(T, MIX_WIDTH), rep(gate), p["w_out"], final_w, tm, per_row,
                            steps_per_group, final_norm=(l == DEPTH - 1))
        for lst, s in zip(conv_states, cst):
            lst.append(s)
    if conv_in_proj:
        new_states = [jnp.stack(conv_states[0]), new_states[0], jnp.stack(conv_states[1]), new_states[1]]
    return x2d.reshape(B, L, D_MODEL), new_states


def kernel(x_prompt, x_sample, state_ssd_conv, state_ssm, state_gdn_conv, state_gdn, c_prompt, c_sample,
           norm_w, w_ada, b_ada, w_in, ssd_conv_w, ssd_conv_b, ssd_dt_bias, ssd_a_log, ssd_d, ssd_norm_w,
           gdn_conv_w, gdn_dt_bias, gdn_a_log, gdn_norm_w, w_out, final_norm_w):
    bp = x_prompt.shape[0]
    params = [_prep_layer(l, norm_w, w_in, ssd_conv_w, ssd_conv_b, ssd_dt_bias, ssd_a_log, ssd_d, ssd_norm_w,
                          gdn_conv_w, gdn_dt_bias, gdn_a_log, gdn_norm_w, w_out) for l in range(DEPTH)]
    final_w = final_norm_w.reshape(1, D_MODEL)
    mod = _ada_call(jnp.concatenate([c_prompt, c_sample], axis=0), w_ada, b_ada)
    y_p, sp = _trunk(x_prompt, mod[:, :bp], None, params, final_w, q=64, nq=4, tm=512, chains=16,
                     act_dtype=BF16)
    y_s, ss = _trunk(x_sample, mod[:, bp:], (state_ssd_conv, state_ssm, state_gdn_conv, state_gdn),
                     params, final_w, q=16, nq=4, tm=x_sample.shape[0] * x_sample.shape[1], chains=16,
                     act_dtype=F32)
    return (y_p, y_s, sp[0], sp[1], sp[2], sp[3], ss[0], ss[1], ss[2], ss[3])
```

```python
import functools
import math

import jax
import jax.numpy as jnp
from jax import lax
from jax.experimental import pallas as pl
from jax.experimental.pallas import tpu as pltpu

F32 = jnp.float32
BF16 = jnp.bfloat16

D_MODEL = 1024
DEPTH = 2
SSD_WIDTH = 1024
SSD_HEAD_DIM = 64
SSD_HEADS = 16
SSD_GROUPS = 2
SSD_STATE = 128
SSD_CONV_DIM = SSD_WIDTH + 2 * SSD_GROUPS * SSD_STATE
GDN_WIDTH = 1024
GDN_HEAD_DIM = 128
GDN_HEADS = 8
GDN_CONV_DIM = 3 * GDN_WIDTH
MIX_WIDTH = SSD_WIDTH + GDN_WIDTH
CONV_K = 4
NORM_EPS = 1e-6
ADA_DIM = 3 * D_MODEL
SMALL = 128
LANES = 128
SUBLANES = 8
INV_BLOCK = 16
VMEM_LIMIT = 56 * 1024 * 1024

_OFF_ZS = 0
_OFF_XBC = _OFF_ZS + SSD_WIDTH
_OFF_DT = _OFF_XBC + SSD_CONV_DIM
_OFF_ZG = _OFF_DT + SSD_HEADS
_OFF_QKV = _OFF_ZG + GDN_WIDTH
_OFF_A = _OFF_QKV + GDN_CONV_DIM
_OFF_B = _OFF_A + GDN_HEADS
_SM_DT = 0
_SM_A = SSD_HEADS
_SM_B = SSD_HEADS + GDN_HEADS


def _bdot(a, b):
    return jnp.dot(a.astype(BF16), b.astype(BF16), preferred_element_type=F32)


def _bdot_nt(a, b):
    return lax.dot_general(a.astype(BF16), b.astype(BF16), (((1,), (1,)), ((), ())),
                           preferred_element_type=F32)


def _terms(x, n):
    out = []
    r = x
    for i in range(n):
        t = r.astype(BF16)
        out.append(t)
        if i + 1 < n:
            r = r - t.astype(F32)
    return out


def _const_dot(c, x, n):
    acc = None
    for t in _terms(x, n):
        p = jnp.dot(c, t, preferred_element_type=F32)
        acc = p if acc is None else acc + p
    return acc


def _dot_const(x, c, n):
    acc = None
    for t in _terms(x, n):
        p = jnp.dot(t, c, preferred_element_type=F32)
        acc = p if acc is None else acc + p
    return acc


def _transpose_via_identity(eye, x, n):
    acc = None
    for t in _terms(x, n):
        p = lax.dot_general(eye, t, (((1,), (1,)), ((), ())), preferred_element_type=F32)
        acc = p if acc is None else acc + p
    return acc


def _softplus(x):
    return jnp.maximum(x, 0.0) + jnp.log1p(jnp.exp(-jnp.abs(x)))


def _silu(x):
    return x * jax.nn.sigmoid(x)


def _ada_kernel(c_ref, w_ref, b_ref, o_ref):
    s = _silu(c_ref[...])
    w = w_ref[0]
    s_hi, s_lo = _terms(s, 2)
    w_hi, w_lo = _terms(w, 2)
    acc = jnp.dot(s_hi, w_hi, preferred_element_type=F32)
    acc = acc + (jnp.dot(s_hi, w_lo, preferred_element_type=F32)
                 + jnp.dot(s_lo, w_hi, preferred_element_type=F32))
    o_ref[0] = acc + b_ref[0]


def _ada_call(c_all, w_ada, b_ada):
    rows = c_all.shape[0]
    tn = D_MODEL
    return pl.pallas_call(
        _ada_kernel,
        grid=(DEPTH, ADA_DIM // tn),
        in_specs=[
            pl.BlockSpec((rows, D_MODEL), lambda l, j: (0, 0)),
            pl.BlockSpec((1, D_MODEL, tn), lambda l, j: (l, 0, j)),
            pl.BlockSpec((1, 1, tn), lambda l, j: (l, 0, j)),
        ],
        out_specs=pl.BlockSpec((1, rows, tn), lambda l, j: (l, 0, j)),
        out_shape=jax.ShapeDtypeStruct((DEPTH, rows, ADA_DIM), F32),
        compiler_params=pltpu.CompilerParams(
            dimension_semantics=("arbitrary", "arbitrary"), vmem_limit_bytes=VMEM_LIMIT),
        name="ada_mod",
    )(c_all, w_ada, b_ada.reshape(DEPTH, 1, ADA_DIM))


CONV_ROWS = 64
CONV_COLS = 128
PROJ_COLS = 512


def _conv_silu_rows(xp, w_ref, b_ref, out, rows, c0, c1, xp_row0=0, out_row0=0):
    rows_per = min(rows, CONV_ROWS)
    for i in range(rows // rows_per):
        r0 = i * rows_per
        for j in range((c1 - c0) // CONV_COLS):
            cols = slice(c0 + j * CONV_COLS, c0 + (j + 1) * CONV_COLS)
            x = xp[xp_row0 + r0:xp_row0 + r0 + rows_per + SUBLANES, cols]
            acc = x * w_ref[0:1, cols]
            for k in range(1, CONV_K):
                acc = pltpu.roll(acc, 1, 0) + x * w_ref[k:k + 1, cols]
            if b_ref is not None:
                acc = acc + b_ref[:, cols]
            out[out_row0 + r0:out_row0 + r0 + rows_per, cols] = _silu(acc[SUBLANES:, :]).astype(out.dtype)


PERM_CHUNK = SUBLANES * SUBLANES


def _qk_l2_scale(col):
    assert GDN_HEAD_DIM == LANES
    if col < GDN_WIDTH:
        return GDN_HEAD_DIM ** -0.5
    if col < 2 * GDN_WIDTH:
        return 1.0
    return None


def _conv_silu_permuted(xp, w_ref, b_ref, hist, up, out, rows, c0, l2_scale=None):
    sub0 = lax.broadcasted_iota(jnp.int32, (SUBLANES, LANES), 0) == 0
    taps = CONV_K - 1
    for j in range(PROJ_COLS // LANES):
        cols = slice(c0 + j * LANES, c0 + (j + 1) * LANES)
        lcols = slice(j * LANES, (j + 1) * LANES)
        w = [jnp.broadcast_to(w_ref[k:k + 1, cols], (SUBLANES, LANES)) for k in range(CONV_K)]
        b = None if b_ref is None else jnp.broadcast_to(b_ref[:, cols], (SUBLANES, LANES))
        prev = [hist[i * SUBLANES:(i + 1) * SUBLANES, cols] for i in range(taps)]
        for k in range(rows // PERM_CHUNK):
            base = k * PERM_CHUNK
            xs = [xp[base + a * SUBLANES:base + (a + 1) * SUBLANES, cols] for a in range(SUBLANES)]
            rolled = [pltpu.roll(xs[SUBLANES - taps + i], 1, 0) for i in range(taps)]
            ext = [jnp.where(sub0, prev[i], rolled[i]) for i in range(taps)] + xs
            for a in range(SUBLANES):
                y = ext[a] * w[0]
                for t in range(1, CONV_K):
                    y = y + ext[a + t] * w[t]
                if b is not None:
                    y = y + b
                y = _silu(y)
                scale = None if l2_scale is None else l2_scale(c0 + j * LANES)
                if scale is not None:
                    y = y * (lax.rsqrt(jnp.sum(y * y, axis=-1, keepdims=True) + NORM_EPS) * scale)
                up[j, pl.ds(base + a, SUBLANES, stride=SUBLANES), :] = y
            prev = rolled
        for i in range(taps):
            hist[i * SUBLANES:(i + 1) * SUBLANES, cols] = prev[i]
        out[:, cols] = up[j].astype(out.dtype)


def _inproj_kernel(*refs, tm, steps_per_seq, do_conv):
    x_ref, sc_ref, sh_ref, nw_ref, wzs_ref, wxbc_ref, wzg_ref, wqkv_ref, wsh_ref, wsl_ref = refs[:10]
    if do_conv:
        scw_ref, scb_ref, gcw_ref = refs[10:13]
        (zs_ref, xbc_ref, zg_ref, qkv_ref, sm_ref, ncs_ref, ncg_ref,
         xps, xpg, hist_s, hist_g, hs, hp, up) = refs[13:]
    else:
        zs_ref, xbc_ref, zg_ref, qkv_ref, sm_ref = refs[10:]
    x = x_ref[...]
    ms = jnp.mean(x * x, axis=-1, keepdims=True)
    h = x * lax.rsqrt(ms + NORM_EPS) * nw_ref[...]
    h = h * (1.0 + sc_ref[0]) + sh_ref[0]
    h_hi, h_lo = _terms(h, 2)
    sm = jnp.dot(h_hi, wsh_ref[...], preferred_element_type=F32)
    sm = sm + (jnp.dot(h_hi, wsl_ref[...], preferred_element_type=F32)
               + jnp.dot(h_lo, wsh_ref[...], preferred_element_type=F32))
    sm_ref[...] = sm
    if not do_conv:
        zs_ref[...] = jnp.dot(h_hi, wzs_ref[...], preferred_element_type=F32)
        xbc_ref[...] = jnp.dot(h_hi, wxbc_ref[...], preferred_element_type=F32)
        zg_ref[...] = jnp.dot(h_hi, wzg_ref[...], preferred_element_type=F32)
        qkv_ref[...] = jnp.dot(h_hi, wqkv_ref[...], preferred_element_type=F32)
        return

    @pl.when(pl.program_id(0) % steps_per_seq == 0)
    def _zero_history():
        hist_s[...] = jnp.zeros(hist_s.shape, F32)
        hist_g[...] = jnp.zeros(hist_g.shape, F32)

    assert tm % PERM_CHUNK == 0
    ncb = D_MODEL // LANES
    for cb in range(ncb):
        hs[cb] = h[:, cb * LANES:(cb + 1) * LANES]
    for g in range(0, tm // SUBLANES, 2):
        parts = [jnp.concatenate(
            [hs[cb, pl.ds(PERM_CHUNK * (gg // SUBLANES) + gg % SUBLANES, SUBLANES, stride=SUBLANES), :]
             for cb in range(ncb)], axis=1) for gg in (g, g + 1)]
        hp[g * SUBLANES:(g + 2) * SUBLANES, :] = jnp.concatenate(parts, axis=0).astype(BF16)

    conv_blocks = [(xp, w_ref, cw_ref, cb_ref, hist, out, c0)
                   for xp, w_ref, cw_ref, cb_ref, hist, out, width in (
                       (xps, wxbc_ref, scw_ref, scb_ref, hist_s, xbc_ref, SSD_CONV_DIM),
                       (xpg, wqkv_ref, gcw_ref, None, hist_g, qkv_ref, GDN_CONV_DIM))
                   for c0 in range(0, width, PROJ_COLS)]
    gate_blocks = [(w_ref, out, c0) for w_ref, out, width in ((wzs_ref, zs_ref, SSD_WIDTH),
                                                              (wzg_ref, zg_ref, GDN_WIDTH))
                   for c0 in range(0, width, PROJ_COLS)]

    def project(blk):
        xp, w_ref, _, _, _, _, c0 = blk
        cols = slice(c0, c0 + PROJ_COLS)
        xp[:, cols] = jnp.dot(hp[...], w_ref[:, cols], preferred_element_type=F32)

    def gate_block(i):
        w_ref, out, c0 = gate_blocks[i]
        cols = slice(c0, c0 + PROJ_COLS)
        out[:, cols] = _silu(jnp.dot(h_hi, w_ref[:, cols], preferred_element_type=F32)).astype(out.dtype)

    early_gates = len(gate_blocks) // 2
    for i in range(early_gates):
        gate_block(i)
    project(conv_blocks[0])
    for j, blk in enumerate(conv_blocks):
        if j + 1 < len(conv_blocks):
            project(conv_blocks[j + 1])
        if j % 2 == 0 and early_gates + j // 2 < len(gate_blocks):
            gate_block(early_gates + j // 2)
        xp, _, cw_ref, cb_ref, hist, out, c0 = blk
        _conv_silu_permuted(xp, cw_ref, cb_ref, hist, up, out, tm, c0,
                            l2_scale=_qk_l2_scale if xp is xpg else None)
    assert len(gate_blocks) - early_gates <= (len(conv_blocks) + 1) // 2
    for xp, nst in ((xps, ncs_ref), (xpg, ncg_ref)):
        for i in range(CONV_K - 1):
            row = tm - PERM_CHUNK + SUBLANES * (SUBLANES - (CONV_K - 1) + i) + SUBLANES - 1
            nst[0, i:i + 1, :] = xp[row:row + 1, :]


def _row_mod_spec(per_row, tm, steps_per_group):
    if per_row:
        return pl.BlockSpec((1, tm, D_MODEL), lambda i: (0, i, 0))
    return pl.BlockSpec((1, 1, D_MODEL), lambda i: (i // steps_per_group, 0, 0))


def _resident(shape):
    nd = len(shape)
    return pl.BlockSpec(shape, lambda i: (0,) * nd, pipeline_mode=pl.Buffered(1))


def _inproj_call(x2d, scale, shift, norm_w, w, tm, per_row, steps_per_group, do_conv, act_dtype):
    T = x2d.shape[0]
    assert T % tm == 0
    widths = (SSD_WIDTH, SSD_CONV_DIM, GDN_WIDTH, GDN_CONV_DIM, SMALL)
    dtypes = (act_dtype, act_dtype, act_dtype, act_dtype, F32)
    row = lambda wd: pl.BlockSpec((tm, wd), lambda i: (i, 0))
    in_specs = [
        row(D_MODEL),
        _row_mod_spec(per_row, tm, steps_per_group),
        _row_mod_spec(per_row, tm, steps_per_group),
        _resident((1, D_MODEL)),
        _resident((D_MODEL, SSD_WIDTH)),
        _resident((D_MODEL, SSD_CONV_DIM)),
        _resident((D_MODEL, GDN_WIDTH)),
        _resident((D_MODEL, GDN_CONV_DIM)),
        _resident((D_MODEL, SMALL)),
        _resident((D_MODEL, SMALL)),
    ]
    args = [x2d, scale, shift, norm_w, w["zs"], w["xbc"], w["zg"], w["qkv"], w["sm_hi"], w["sm_lo"]]
    out_specs = [row(wd) for wd in widths]
    out_shape = [jax.ShapeDtypeStruct((T, wd), dt) for wd, dt in zip(widths, dtypes)]
    scratch = []
    if do_conv:
        assert not per_row
        nseq = T // (tm * steps_per_group)
        in_specs += [_resident((CONV_K, SSD_CONV_DIM)), _resident((1, SSD_CONV_DIM)),
                     _resident((CONV_K, GDN_CONV_DIM))]
        args += [w["scw"], w["scb"], w["gcw"]]
        for wd in (SSD_CONV_DIM, GDN_CONV_DIM):
            out_specs.append(pl.BlockSpec((1, CONV_K - 1, wd), lambda i: (i // steps_per_group, 0, 0)))
            out_shape.append(jax.ShapeDtypeStruct((nseq, CONV_K - 1, wd), F32))
            scratch.append(pltpu.VMEM((tm, wd), F32))
        scratch += [pltpu.VMEM(((CONV_K - 1) * SUBLANES, wd), F32) for wd in (SSD_CONV_DIM, GDN_CONV_DIM)]
        scratch += [pltpu.VMEM((D_MODEL // LANES, tm, LANES), F32), pltpu.VMEM((tm, D_MODEL), BF16),
                    pltpu.VMEM((PROJ_COLS // LANES, tm, LANES), F32)]
    return pl.pallas_call(
        functools.partial(_inproj_kernel, tm=tm, steps_per_seq=steps_per_group, do_conv=do_conv),
        grid=(T // tm,),
        in_specs=in_specs,
        out_specs=out_specs,
        out_shape=out_shape,
        scratch_shapes=scratch,
        compiler_params=pltpu.CompilerParams(
            dimension_semantics=("arbitrary",), vmem_limit_bytes=VMEM_LIMIT),
        name="in_proj",
    )(*args)


def _outproj_kernel(x_ref, mix_ref, gate_ref, w_ref, fw_ref, o_ref, *, final_norm):
    y = jnp.dot(mix_ref[...], w_ref[...], preferred_element_type=F32)
    out = x_ref[...] + gate_ref[0] * y
    if final_norm:
        ms = jnp.mean(out * out, axis=-1, keepdims=True)
        out = out * lax.rsqrt(ms + NORM_EPS) * fw_ref[...]
    o_ref[...] = out


def _outproj_call(x2d, mix2d, gate, w_out, final_w, tm, per_row, steps_per_group, final_norm):
    T = x2d.shape[0]
    return pl.pallas_call(
        functools.partial(_outproj_kernel, final_norm=final_norm),
        grid=(T // tm,),
        in_specs=[
            pl.BlockSpec((tm, D_MODEL), lambda i: (i, 0)),
            pl.BlockSpec((tm, MIX_WIDTH), lambda i: (i, 0)),
            _row_mod_spec(per_row, tm, steps_per_group),
            _resident((MIX_WIDTH, D_MODEL)),
            _resident((1, D_MODEL)),
        ],
        out_specs=pl.BlockSpec((tm, D_MODEL), lambda i: (i, 0)),
        out_shape=jax.ShapeDtypeStruct((T, D_MODEL), F32),
        compiler_params=pltpu.CompilerParams(
            dimension_semantics=("arbitrary",), vmem_limit_bytes=VMEM_LIMIT),
        name="out_proj",
    )(x2d, mix2d, gate, w_out, final_w)


def _run_interleaved(*gens):
    live = [g for g in gens if g is not None]
    while live:
        for g in list(live):
            try:
                next(g)
            except StopIteration:
                live.remove(g)


def _chain_gens(gens):
    for g in gens:
        yield from g


def _unit_lower_solve_many(m_list, rhs_list, q, out):
    nb = q // INV_BLOCK
    if nb > 1:
        r = lax.broadcasted_iota(jnp.int32, (q, q), 0) // INV_BLOCK
        c = lax.broadcasted_iota(jnp.int32, (q, q), 1) // INV_BLOCK
        same = r == c
        d_list = [jnp.where(same, m, 0.0) for m in m_list]
        o_list = [m - d for m, d in zip(m_list, d_list)]
    else:
        d_list = m_list
    n_list = [-d for d in d_list]
    e_list = d_list
    for _ in range(int(math.log2(INV_BLOCK)) - 1):
        e_list = [_bdot(e, e) for e in e_list]
        yield
        n_list = [n + e + _bdot(n, e) for n, e in zip(n_list, e_list)]
        yield
    y_list = [rhs + _bdot(n, rhs) for n, rhs in zip(n_list, rhs_list)]
    if nb > 1:
        p_list = [o + _bdot(n, o) for n, o in zip(n_list, o_list)]
        yield
        r_list = [-p for p in p_list]
        e_list = p_list
        for _ in range(int(math.ceil(math.log2(nb))) - 1):
            e_list = [_bdot(e, e) for e in e_list]
            yield
            r_list = [rr + e + _bdot(rr, e) for rr, e in zip(r_list, e_list)]
            yield
        y_list = [y + _bdot(rr, y) for rr, y in zip(r_list, y_list)]
    yield
    out.extend(y_list)


def _mixer_kernel(*refs, names, q, nq, lv, nc, multi_seq, zero_init, do_conv, chains):
    r = dict(zip(names, refs))
    zs_ref, xbc_ref, zg_ref, qkv_ref, sm_ref = r["zs"], r["xbc"], r["zg"], r["qkv"], r["sm"]
    bias_ref, coef_ref, dexp_ref, snw_ref, gnw_ref = r["bias"], r["coef"], r["dexp"], r["snw"], r["gnw"]
    mix_ref, nhs_ref, nsg_ref = r["mix"], r["nhs"], r["nsg"]
    st_s, st_g, y_s, pe_s, xcd_s, bgt_s = r["st_s"], r["st_g"], r["y_s"], r["pe_s"], r["xcd_s"], r["bgt_s"]
    wq_s, u_s, at_s, kdt_s = r["wq_s"], r["u_s"], r["at_s"], r["kdt_s"]

    c_idx = pl.program_id(1)
    rr_ = nq * q
    n_slots = nq if multi_seq else 1
    slot = (lambda c: c) if multi_seq else (lambda c: 0)
    hist = q + SUBLANES
    pair = 2 * SSD_HEAD_DIM
    heads_per_group = SSD_HEADS // SSD_GROUPS
    gw = SSD_WIDTH // SSD_GROUPS
    assert not (do_conv and not multi_seq and nc > 1 and lv != rr_)

    @pl.when(c_idx == 0)
    def _load_state():
        if zero_init:
            st_s[...] = jnp.zeros(st_s.shape, F32)
            st_g[...] = jnp.zeros(st_g.shape, F32)
        else:
            for s in range(n_slots):
                for j in range(SSD_HEADS // 2):
                    blk = r["hs"][0, s, 2 * j:2 * j + 2].reshape(pair, SSD_STATE)
                    st_s[s, :, pair * j:pair * (j + 1)] = blk.T
            st_g[...] = r["sg"][0]
        if do_conv:
            for xp, key, width in ((r["xps"], "cs", SSD_CONV_DIM), (r["xpg"], "cg", GDN_CONV_DIM)):
                for s in range(n_slots):
                    xp[s * hist:s * hist + SUBLANES, :] = jnp.zeros((SUBLANES, width), F32)
                    if not zero_init:
                        xp[s * hist + SUBLANES - (CONV_K - 1):s * hist + SUBLANES, :] = r[key][0, s]

    if do_conv:
        for x_ref, xp, w_ref, b_ref, out, nst, width in (
                (xbc_ref, r["xps"], r["scw"], r["scb"], r["xbc_c"], r["ncs"], SSD_CONV_DIM),
                (qkv_ref, r["xpg"], r["gcw"], None, r["qkv_c"], r["ncg"], GDN_CONV_DIM)):
            if multi_seq:
                for c in range(nq):
                    base = c * hist + SUBLANES
                    xp[base:base + lv, :] = x_ref[c * lv:(c + 1) * lv, :]
                    xp[base + lv:base + q, :] = jnp.zeros((q - lv, width), F32)
                    _conv_silu_rows(xp, w_ref, b_ref, out, q, 0, width, xp_row0=c * hist, out_row0=c * q)
                    nst[0, c] = xp[base + lv - (CONV_K - 1):base + lv, :]
            else:
                xp[SUBLANES:SUBLANES + rr_, :] = x_ref[...]
                _conv_silu_rows(xp, w_ref, b_ref, out, rr_, 0, width)

                @pl.when(c_idx == nc - 1)
                def _store_conv_state():
                    nst[0, 0] = xp[SUBLANES + rr_ - (CONV_K - 1):SUBLANES + rr_, :]

                if nc > 1:
                    xp[0:SUBLANES, :] = xp[rr_:rr_ + SUBLANES, :]
        xbc_c, qkv_c = r["xbc_c"], r["qkv_c"]
        gate = _silu
    else:
        xbc_c, qkv_c = xbc_ref, qkv_ref
        gate = lambda z: z.astype(F32)

    if multi_seq:
        for src, dst, wd in ((zs_ref, r["pzs"], SSD_WIDTH), (zg_ref, r["pzg"], GDN_WIDTH),
                             (sm_ref, r["psm"], SMALL)):
            for c in range(nq):
                dst[c * q:c * q + lv, :] = src[c * lv:(c + 1) * lv, :].astype(F32)
                dst[c * q + lv:(c + 1) * q, :] = jnp.zeros((q - lv, wd), F32)
        zs_v, zg_v, sm = r["pzs"], r["pzg"], r["psm"][...]
    else:
        zs_v, zg_v, sm = zs_ref, zg_ref, sm_ref[...]

    sp = _softplus(sm + bias_ref[...])
    beta_all = jax.nn.sigmoid(sm)
    if multi_seq and lv < q:
        assert q & (q - 1) == 0
        valid = jnp.bitwise_and(lax.broadcasted_iota(jnp.int32, (rr_, SMALL), 0), q - 1) < lv
        sp = jnp.where(valid, sp, 0.0)
        beta_all = jnp.where(valid, beta_all, 0.0)
    ag = sp * coef_ref[...]
    r_all = lax.broadcasted_iota(jnp.int32, (rr_, rr_), 0)
    c_all = lax.broadcasted_iota(jnp.int32, (rr_, rr_), 1)
    chunk_start = (r_all // q) * q
    tril_bd = jnp.where(r_all >= c_all, jnp.where(c_all >= chunk_start, 1.0, 0.0), 0.0).astype(BF16)
    rq = lax.broadcasted_iota(jnp.int32, (q, q), 0)
    cq = lax.broadcasted_iota(jnp.int32, (q, q), 1)
    incl = rq >= cq
    strict = rq > cq
    r128 = lax.broadcasted_iota(jnp.int32, (LANES, LANES), 0)
    c128 = lax.broadcasted_iota(jnp.int32, (LANES, LANES), 1)
    eye = jnp.where(r128 == c128, 1.0, 0.0).astype(BF16)
    cum = _const_dot(tril_bd, ag, 3)
    cum_t = _transpose_via_identity(eye, cum, 3)
    sp_t = _transpose_via_identity(eye, sp, 3)
    ecum = jnp.exp(cum)
    if nq > 1:
        cum_last = jnp.concatenate(
            [jnp.broadcast_to(cum[(c + 1) * q - 1:(c + 1) * q, :], (q, SMALL)) for c in range(nq)], axis=0)
    else:
        cum_last = cum[q - 1:q, :]
    to_end = jnp.exp(cum_last - cum)
    er = lax.broadcasted_iota(jnp.int32, (SMALL, SSD_WIDTH), 0)
    ec = lax.broadcasted_iota(jnp.int32, (SMALL, SSD_WIDTH), 1) // SSD_HEAD_DIM
    expand = jnp.where(er == ec, 1.0, 0.0).astype(BF16)
    pe_s[...] = _dot_const(ecum, expand, 2)
    p_dtend = _dot_const(sp * to_end, expand, 2)
    xcd_s[...] = (xbc_c[:, 0:SSD_WIDTH].astype(F32) * p_dtend).astype(BF16)

    lane = lax.broadcasted_iota(jnp.int32, (q, pair), 1)
    lo_half = lane < SSD_HEAD_DIM
    scores = {}
    for c in range(nq):
        rows = slice(c * q, (c + 1) * q)
        for g in range(SSD_GROUPS):
            b_g = xbc_c[rows, SSD_WIDTH + g * SSD_STATE:SSD_WIDTH + (g + 1) * SSD_STATE]
            c_g = xbc_c[rows, SSD_WIDTH + (SSD_GROUPS + g) * SSD_STATE:
                        SSD_WIDTH + (SSD_GROUPS + g + 1) * SSD_STATE]
            scores[c, g] = _bdot_nt(c_g, b_g)
            bgt_s[c * SSD_GROUPS + g] = _transpose_via_identity(eye, b_g, 1).astype(BF16)
    def ssd_diag(items):
        for c in items:
            rows = slice(c * q, (c + 1) * q)
            for j in range(SSD_HEADS // 2):
                g = j // (heads_per_group // 2)
                pcols = slice(pair * j, pair * (j + 1))
                x_pair = xbc_c[rows, pcols].astype(F32)
                y_pair = x_pair * dexp_ref[:, pcols]
                for half in range(2):
                    h = 2 * j + half
                    seg = cum[rows, h:h + 1] - cum_t[h:h + 1, rows]
                    lmat = jnp.exp(jnp.where(incl, seg, -jnp.inf))
                    m_h = scores[c, g] * lmat * sp_t[h:h + 1, rows]
                    x_h = jnp.where(lo_half if half == 0 else jnp.logical_not(lo_half), x_pair, 0.0)
                    y_pair = y_pair + _bdot(m_h, x_h)
                y_s[rows, pcols] = y_pair
                if j % 2 == 1:
                    yield

    def gdn_independent(group):
        m_list, rhs_list = [], []
        for n, (c, h) in enumerate(group):
            rows = slice(c * q, (c + 1) * q)
            i = c * GDN_HEADS + h
            qh = qkv_c[rows, h * GDN_HEAD_DIM:(h + 1) * GDN_HEAD_DIM].astype(F32)
            kh = qkv_c[rows, GDN_WIDTH + h * GDN_HEAD_DIM:GDN_WIDTH + (h + 1) * GDN_HEAD_DIM].astype(F32)
            vh = qkv_c[rows, 2 * GDN_WIDTH + h * GDN_HEAD_DIM:
                       2 * GDN_WIDTH + (h + 1) * GDN_HEAD_DIM].astype(F32)
            if do_conv:
                qh = qh * lax.rsqrt(jnp.sum(qh * qh, axis=-1, keepdims=True) + NORM_EPS) * (GDN_HEAD_DIM ** -0.5)
                kh = kh * lax.rsqrt(jnp.sum(kh * kh, axis=-1, keepdims=True) + NORM_EPS)
            la = _SM_A + h
            bcol = beta_all[rows, _SM_B + h:_SM_B + h + 1]
            eg = ecum[rows, la:la + 1]
            kb = kh * bcol
            dec = jnp.exp(jnp.where(incl, cum[rows, la:la + 1] - cum_t[la:la + 1, rows], -jnp.inf))
            kq = _bdot_nt(jnp.concatenate([kb, qh], axis=0), kh)
            m_list.append(jnp.where(strict, kq[0:q] * dec, 0.0))
            at_s[i] = (kq[q:2 * q] * dec).astype(BF16)
            rhs_list.append(jnp.concatenate([vh * bcol, kb * eg], axis=1))
            wq_s[i, q:2 * q, :] = (qh * eg).astype(BF16)
            kdt_s[i] = _transpose_via_identity(eye, kh * to_end[rows, la:la + 1], 1).astype(BF16)
            if n % 4 == 3:
                yield
        sol_list = []
        yield from _unit_lower_solve_many(m_list, rhs_list, q, sol_list)
        for (c, h), sol in zip(group, sol_list):
            i = c * GDN_HEADS + h
            u_s[i] = sol[:, 0:GDN_HEAD_DIM]
            wq_s[i, 0:q, :] = sol[:, GDN_HEAD_DIM:2 * GDN_HEAD_DIM].astype(BF16)
        yield

    def store_mix(c, cols, val):
        if multi_seq:
            mix_ref[c * lv:(c + 1) * lv, cols] = val[0:lv].astype(mix_ref.dtype)
        else:
            mix_ref[c * q:(c + 1) * q, cols] = val.astype(mix_ref.dtype)

    def recurrences(items):
        for c in items:
            rows = slice(c * q, (c + 1) * q)
            last = (c + 1) * q - 1
            s = slot(c)
            for g in range(SSD_GROUPS):
                gcols = slice(g * gw, (g + 1) * gw)
                c_g = xbc_c[rows, SSD_WIDTH + (SSD_GROUPS + g) * SSD_STATE:
                            SSD_WIDTH + (SSD_GROUPS + g + 1) * SSD_STATE]
                state = st_s[s, :, gcols]
                y_g = y_s[rows, gcols] + _bdot(c_g, state) * pe_s[rows, gcols]
                st_s[s, :, gcols] = state * pe_s[last:last + 1, gcols] + jnp.dot(
                    bgt_s[c * SSD_GROUPS + g], xcd_s[rows, gcols], preferred_element_type=F32)
                y_g = y_g * gate(zs_v[rows, gcols])
                ms = jnp.mean(y_g * y_g, axis=-1, keepdims=True)
                y_g = y_g * lax.rsqrt(ms + NORM_EPS) * snw_ref[:, gcols]
                store_mix(c, gcols, y_g)
            yield
        pairs = [(c, h) for c in items for h in range(GDN_HEADS)]
        ws = {}
        for (c, h) in pairs:
            ws[c, h] = jnp.dot(wq_s[c * GDN_HEADS + h], st_g[slot(c), h].astype(BF16),
                               preferred_element_type=F32)
        yield
        vn = {(c, h): (u_s[c * GDN_HEADS + h] - ws[c, h][0:q]).astype(BF16) for (c, h) in pairs}
        for (c, h) in pairs:
            i = c * GDN_HEADS + h
            la = _SM_A + h
            last = (c + 1) * q - 1
            s = slot(c)
            o = ws[c, h][q:2 * q] + jnp.dot(at_s[i], vn[c, h], preferred_element_type=F32)
            st_g[s, h] = st_g[s, h] * ecum[last:last + 1, la:la + 1] + jnp.dot(
                kdt_s[i], vn[c, h], preferred_element_type=F32)
            hc = slice(h * GDN_HEAD_DIM, (h + 1) * GDN_HEAD_DIM)
            ms = jnp.mean(o * o, axis=-1, keepdims=True)
            o = o * lax.rsqrt(ms + NORM_EPS) * gnw_ref[...] * gate(zg_v[c * q:(c + 1) * q, hc])
            store_mix(c, slice(SSD_WIDTH + h * GDN_HEAD_DIM, SSD_WIDTH + (h + 1) * GDN_HEAD_DIM), o)
        yield

    all_ch = [(c, h) for c in range(nq) for h in range(GDN_HEADS)]
    groups = [all_ch[i:i + chains] for i in range(0, len(all_ch), chains)]
    batches = [list(range(nq))] if multi_seq else [[c] for c in range(nq)]
    assert chains % GDN_HEADS == 0
    pending = list(batches)
    done_items = set()
    if multi_seq:
        for _ in ssd_diag(range(nq)):
            pass
    for grp in groups:
        runnable = [b for b in pending if set(b) <= done_items]
        for b in runnable:
            pending.remove(b)
        others = [_chain_gens([recurrences(b) for b in runnable])] if runnable else []
        if not multi_seq:
            others.append(ssd_diag(sorted({c for c, _ in grp})))
        _run_interleaved(gdn_independent(grp), *others)
        done_items |= {c for c, _ in grp}
    for b in pending:
        for _ in recurrences(b):
            pass

    @pl.when(c_idx == nc - 1)
    def _store_state():
        for s in range(n_slots):
            for j in range(SSD_HEADS // 2):
                blk = st_s[s, :, pair * j:pair * (j + 1)].T
                nhs_ref[0, s, 2 * j:2 * j + 2] = blk.reshape(2, SSD_HEAD_DIM, SSD_STATE)
        nsg_ref[0] = st_g[...]


_STATE_TAILS = (
    (CONV_K - 1, SSD_CONV_DIM),
    (SSD_HEADS, SSD_HEAD_DIM, SSD_STATE),
    (CONV_K - 1, GDN_CONV_DIM),
    (GDN_HEADS, GDN_HEAD_DIM, GDN_HEAD_DIM),
)


_STATE_NAMES = ("cs", "hs", "cg", "sg")


def _mixer_call(zs, xbc, zg, qkv, sm, B, L, states_in, prev_out, p, q, nq, multi_seq, layer, chains, do_conv):
    assert zs.shape[0] == B * L
    rows = nq * q
    assert q % INV_BLOCK == 0
    zero_init = states_in is None
    if multi_seq:
        assert CONV_K - 1 <= L <= q and B % nq == 0 and do_conv
        nc, lv, bb = 1, L, nq
        tok = lambda wd: pl.BlockSpec((nq * L, wd), lambda b, c: (b, 0))
    else:
        assert L % rows == 0
        nc, lv, bb = L // rows, rows, 1
        tok = lambda wd: pl.BlockSpec((rows, wd), lambda b, c: (b * nc + c, 0))
    n_slots = bb
    per_b = lambda shape: pl.BlockSpec((1, bb) + shape, lambda b, c: (layer, b) + (0,) * len(shape))
    const = lambda shape: pl.BlockSpec(shape, lambda b, c: (0,) * len(shape))
    nch = nq * GDN_HEADS
    kept = [k for k in range(4) if do_conv or k in (1, 3)]

    names = ["zs", "xbc", "zg", "qkv", "sm"]
    in_specs = [tok(SSD_WIDTH), tok(SSD_CONV_DIM), tok(GDN_WIDTH), tok(GDN_CONV_DIM), tok(SMALL)]
    args = [zs, xbc, zg, qkv, sm]
    if not zero_init:
        for k in kept:
            names.append(_STATE_NAMES[k])
            in_specs.append(per_b(_STATE_TAILS[k]))
            args.append(states_in[k])
    consts = [("bias", (1, SMALL)), ("coef", (1, SMALL)), ("dexp", (1, SSD_WIDTH)), ("snw", (1, SSD_WIDTH)),
              ("gnw", (1, GDN_HEAD_DIM))]
    if do_conv:
        consts += [("scw", (CONV_K, SSD_CONV_DIM)), ("scb", (1, SSD_CONV_DIM)), ("gcw", (CONV_K, GDN_CONV_DIM))]
    for nm, shape in consts:
        names.append(nm)
        in_specs.append(const(shape))
        args.append(p[nm])
    aliases = {}
    if prev_out is not None:
        for k, a in enumerate(prev_out):
            names.append("alias%d" % k)
            aliases[len(args)] = 1 + k
            in_specs.append(pl.BlockSpec(memory_space=pl.ANY))
            args.append(a)
    names += ["mix"] + ["n" + _STATE_NAMES[k] for k in kept]
    out_specs = [tok(MIX_WIDTH)] + [per_b(_STATE_TAILS[k]) for k in kept]
    out_shape = [jax.ShapeDtypeStruct((B * L, MIX_WIDTH), BF16)] + [
        jax.ShapeDtypeStruct((DEPTH, B) + _STATE_TAILS[k], F32) for k in kept]
    scratch = [
        ("st_s", pltpu.VMEM((n_slots, SSD_STATE, SSD_WIDTH), F32)),
        ("st_g", pltpu.VMEM((n_slots, GDN_HEADS, GDN_HEAD_DIM, GDN_HEAD_DIM), F32)),
        ("y_s", pltpu.VMEM((rows, SSD_WIDTH), F32)),
        ("pe_s", pltpu.VMEM((rows, SSD_WIDTH), F32)),
        ("xcd_s", pltpu.VMEM((rows, SSD_WIDTH), BF16)),
        ("bgt_s", pltpu.VMEM((nq * SSD_GROUPS, SSD_STATE, q), BF16)),
        ("wq_s", pltpu.VMEM((nch, 2 * q, GDN_HEAD_DIM), BF16)),
        ("u_s", pltpu.VMEM((nch, q, GDN_HEAD_DIM), F32)),
        ("at_s", pltpu.VMEM((nch, q, q), BF16)),
        ("kdt_s", pltpu.VMEM((nch, GDN_HEAD_DIM, q), BF16)),
    ]
    if do_conv:
        xp_rows = nq * (q + SUBLANES) if multi_seq else rows + SUBLANES
        scratch += [
            ("xps", pltpu.VMEM((xp_rows, SSD_CONV_DIM), F32)),
            ("xpg", pltpu.VMEM((xp_rows, GDN_CONV_DIM), F32)),
            ("xbc_c", pltpu.VMEM((rows, SSD_CONV_DIM), F32)),
            ("qkv_c", pltpu.VMEM((rows, GDN_CONV_DIM), F32)),
        ]
    if multi_seq:
        scratch += [
            ("pzs", pltpu.VMEM((rows, SSD_WIDTH), F32)),
            ("pzg", pltpu.VMEM((rows, GDN_WIDTH), F32)),
            ("psm", pltpu.VMEM((rows, SMALL), F32)),
        ]
    names += [nm for nm, _ in scratch]
    return pl.pallas_call(
        functools.partial(_mixer_kernel, names=tuple(names), q=q, nq=nq, lv=lv, nc=nc, multi_seq=multi_seq,
                          zero_init=zero_init, do_conv=do_conv, chains=chains),
        grid=(B // bb, nc),
        in_specs=in_specs,
        out_specs=out_specs,
        out_shape=out_shape,
        input_output_aliases=aliases,
        scratch_shapes=[s for _, s in scratch],
        compiler_params=pltpu.CompilerParams(
            dimension_semantics=("arbitrary", "arbitrary"), vmem_limit_bytes=VMEM_LIMIT),
        name="mixer",
    )(*args)


def _prep_layer(l, norm_w, w_in, ssd_conv_w, ssd_conv_b, ssd_dt_bias, ssd_a_log, ssd_d, ssd_norm_w,
                gdn_conv_w, gdn_dt_bias, gdn_a_log, gdn_norm_w, w_out):
    wi = w_in[l]
    small = jnp.concatenate(
        [wi[:, _OFF_DT:_OFF_DT + SSD_HEADS], wi[:, _OFF_A:_OFF_A + GDN_HEADS], wi[:, _OFF_B:_OFF_B + GDN_HEADS],
         jnp.zeros((D_MODEL, SMALL - SSD_HEADS - 2 * GDN_HEADS), F32)], axis=1)
    sm_hi = small.astype(BF16)
    sm_lo = (small - sm_hi.astype(F32)).astype(BF16)
    pad = jnp.zeros((SMALL - SSD_HEADS - GDN_HEADS,), F32)
    return {
        "norm_w": norm_w[l].reshape(1, D_MODEL),
        "zs": wi[:, _OFF_ZS:_OFF_ZS + SSD_WIDTH].astype(BF16),
        "xbc": wi[:, _OFF_XBC:_OFF_XBC + SSD_CONV_DIM].astype(BF16),
        "zg": wi[:, _OFF_ZG:_OFF_ZG + GDN_WIDTH].astype(BF16),
        "qkv": wi[:, _OFF_QKV:_OFF_QKV + GDN_CONV_DIM].astype(BF16),
        "sm_hi": sm_hi,
        "sm_lo": sm_lo,
        "scw": ssd_conv_w[l],
        "scb": ssd_conv_b[l].reshape(1, SSD_CONV_DIM),
        "gcw": gdn_conv_w[l],
        "bias": jnp.concatenate([ssd_dt_bias[l], gdn_dt_bias[l], pad]).reshape(1, SMALL),
        "coef": jnp.concatenate([-jnp.exp(ssd_a_log[l]), -jnp.exp(gdn_a_log[l]), pad]).reshape(1, SMALL),
        "dexp": jnp.repeat(ssd_d[l], SSD_HEAD_DIM).reshape(1, SSD_WIDTH),
        "snw": ssd_norm_w[l].reshape(1, SSD_WIDTH),
        "gnw": gdn_norm_w[l].reshape(1, GDN_HEAD_DIM),
        "w_out": w_out[l].astype(BF16),
    }


def _trunk(x, mod, states, params, final_w, q, nq, tm, chains, act_dtype):
    B, L, _ = x.shape
    T = B * L
    per_row = L < SUBLANES
    steps_per_group = max(L // tm, 1)
    x2d = x.reshape(T, D_MODEL)
    conv_in_proj = states is None and not per_row
    new_states = None
    conv_states = ([], [])
    for l in range(DEPTH):
        p = params[l]
        shift, scale, gate = (mod[l, :, i * D_MODEL:(i + 1) * D_MODEL] for i in range(3))
        if per_row:
            rep = lambda a: jnp.repeat(a, L, axis=0).reshape(1, T, D_MODEL)
        else:
            rep = lambda a: a.reshape(B, 1, D_MODEL)
        zs, xbc, zg, qkv, sm, *cst = _inproj_call(x2d, rep(scale), rep(shift), p["norm_w"], p, tm, per_row,
                                                  steps_per_group, conv_in_proj, act_dtype)
        mix, *new_states = _mixer_call(zs, xbc, zg, qkv, sm, B, L, states, new_states, p,
                                       q, nq, per_row, l, chains, not conv_in_proj)
        x2d = _outproj_call(x2d, mix, rep(gate), p["w_out"], final_w, tm, per_row,
                            steps_per_group, final_norm=(l == DEPTH - 1))
        for lst, s in zip(conv_states, cst):
            lst.append(s)
    if conv_in_proj:
        new_states = [jnp.stack(conv_states[0]), new_states[0], jnp.stack(conv_states[1]), new_states[1]]
    return x2d.reshape(B, L, D_MODEL), new_states


def kernel(x_prompt, x_sample, state_ssd_conv, state_ssm, state_gdn_conv, state_gdn, c_prompt, c_sample,
           norm_w, w_ada, b_ada, w_in, ssd_conv_w, ssd_conv_b, ssd_dt_bias, ssd_a_log, ssd_d, ssd_norm_w,
           gdn_conv_w, gdn_dt_bias, gdn_a_log, gdn_norm_w, w_out, final_norm_w):
    bp = x_prompt.shape[0]
    params = [_prep_layer(l, norm_w, w_in, ssd_conv_w, ssd_conv_b, ssd_dt_bias, ssd_a_log, ssd_d, ssd_norm_w,
                          gdn_conv_w, gdn_dt_bias, gdn_a_log, gdn_norm_w, w_out) for l in range(DEPTH)]
    final_w = final_norm_w.reshape(1, D_MODEL)
    mod = _ada_call(jnp.concatenate([c_prompt, c_sample], axis=0), w_ada, b_ada)
    y_p, sp = _trunk(x_prompt, mod[:, :bp], None, params, final_w, q=64, nq=4, tm=512, chains=16,
                     act_dtype=BF16)
    y_s, ss = _trunk(x_sample, mod[:, bp:], (state_ssd_conv, state_ssm, state_gdn_conv, state_gdn),
                     params, final_w, q=16, nq=4, tm=x_sample.shape[0] * x_sample.shape[1], chains=16,
                     act_dtype=F32)
    return (y_p, y_s, sp[0], sp[1], sp[2], sp[3], ss[0], ss[1], ss[2], ss[3])
```

```python
import functools
import math

import jax
import jax.numpy as jnp
from jax import lax
from jax.experimental import pallas as pl
from jax.experimental.pallas import tpu as pltpu

F32 = jnp.float32
BF16 = jnp.bfloat16

D_MODEL = 1024
DEPTH = 2
SSD_WIDTH = 1024
SSD_HEAD_DIM = 64
SSD_HEADS = 16
SSD_GROUPS = 2
SSD_STATE = 128
SSD_CONV_DIM = SSD_WIDTH + 2 * SSD_GROUPS * SSD_STATE
GDN_WIDTH = 1024
GDN_HEAD_DIM = 128
GDN_HEADS = 8
GDN_CONV_DIM = 3 * GDN_WIDTH
MIX_WIDTH = SSD_WIDTH + GDN_WIDTH
CONV_K = 4
NORM_EPS = 1e-6
ADA_DIM = 3 * D_MODEL
SMALL = 128
LANES = 128
SUBLANES = 8
INV_BLOCK = 16
VMEM_LIMIT = 56 * 1024 * 1024

_OFF_ZS = 0
_OFF_XBC = _OFF_ZS + SSD_WIDTH
_OFF_DT = _OFF_XBC + SSD_CONV_DIM
_OFF_ZG = _OFF_DT + SSD_HEADS
_OFF_QKV = _OFF_ZG + GDN_WIDTH
_OFF_A = _OFF_QKV + GDN_CONV_DIM
_OFF_B = _OFF_A + GDN_HEADS
_SM_DT = 0
_SM_A = SSD_HEADS
_SM_B = SSD_HEADS + GDN_HEADS


def _bdot(a, b):
    return jnp.dot(a.astype(BF16), b.astype(BF16), preferred_element_type=F32)


def _bdot_nt(a, b):
    return lax.dot_general(a.astype(BF16), b.astype(BF16), (((1,), (1,)), ((), ())),
                           preferred_element_type=F32)


def _terms(x, n):
    out = []
    r = x
    for i in range(n):
        t = r.astype(BF16)
        out.append(t)
        if i + 1 < n:
            r = r - t.astype(F32)
    return out


def _const_dot(c, x, n):
    acc = None
    for t in _terms(x, n):
        p = jnp.dot(c, t, preferred_element_type=F32)
        acc = p if acc is None else acc + p
    return acc


def _dot_const(x, c, n):
    acc = None
    for t in _terms(x, n):
        p = jnp.dot(t, c, preferred_element_type=F32)
        acc = p if acc is None else acc + p
    return acc


def _transpose_via_identity(eye, x, n):
    acc = None
    for t in _terms(x, n):
        p = lax.dot_general(eye, t, (((1,), (1,)), ((), ())), preferred_element_type=F32)
        acc = p if acc is None else acc + p
    return acc


def _softplus(x):
    return jnp.maximum(x, 0.0) + jnp.log1p(jnp.exp(-jnp.abs(x)))


def _silu(x):
    return x * jax.nn.sigmoid(x)


def _ada_kernel(c_ref, w_ref, b_ref, o_ref):
    s = _silu(c_ref[...])
    w = w_ref[0]
    s_hi, s_lo = _terms(s, 2)
    w_hi, w_lo = _terms(w, 2)
    acc = jnp.dot(s_hi, w_hi, preferred_element_type=F32)
    acc = acc + (jnp.dot(s_hi, w_lo, preferred_element_type=F32)
                 + jnp.dot(s_lo, w_hi, preferred_element_type=F32))
    o_ref[0] = acc + b_ref[0]


def _ada_call(c_all, w_ada, b_ada):
    rows = c_all.shape[0]
    tn = D_MODEL
    return pl.pallas_call(
        _ada_kernel,
        grid=(DEPTH, ADA_DIM // tn),
        in_specs=[
            pl.BlockSpec((rows, D_MODEL), lambda l, j: (0, 0)),
            pl.BlockSpec((1, D_MODEL, tn), lambda l, j: (l, 0, j)),
            pl.BlockSpec((1, 1, tn), lambda l, j: (l, 0, j)),
        ],
        out_specs=pl.BlockSpec((1, rows, tn), lambda l, j: (l, 0, j)),
        out_shape=jax.ShapeDtypeStruct((DEPTH, rows, ADA_DIM), F32),
        compiler_params=pltpu.CompilerParams(
            dimension_semantics=("arbitrary", "arbitrary"), vmem_limit_bytes=VMEM_LIMIT),
        name="ada_mod",
    )(c_all, w_ada, b_ada.reshape(DEPTH, 1, ADA_DIM))


CONV_ROWS = 64
CONV_COLS = 128
PROJ_COLS = 512


def _conv_silu_rows(xp, w_ref, b_ref, out, rows, c0, c1, xp_row0=0, out_row0=0):
    rows_per = min(rows, CONV_ROWS)
    for i in range(rows // rows_per):
        r0 = i * rows_per
        for j in range((c1 - c0) // CONV_COLS):
            cols = slice(c0 + j * CONV_COLS, c0 + (j + 1) * CONV_COLS)
            x = xp[xp_row0 + r0:xp_row0 + r0 + rows_per + SUBLANES, cols]
            acc = x * w_ref[0:1, cols]
            for k in range(1, CONV_K):
                acc = pltpu.roll(acc, 1, 0) + x * w_ref[k:k + 1, cols]
            if b_ref is not None:
                acc = acc + b_ref[:, cols]
            out[out_row0 + r0:out_row0 + r0 + rows_per, cols] = _silu(acc[SUBLANES:, :]).astype(out.dtype)


PERM_CHUNK = SUBLANES * SUBLANES


def _qk_l2_scale(col):
    assert GDN_HEAD_DIM == LANES
    if col < GDN_WIDTH:
        return GDN_HEAD_DIM ** -0.5
    if col < 2 * GDN_WIDTH:
        return 1.0
    return None


def _conv_silu_permuted(xp, w_ref, b_ref, hist, up, out, rows, c0, l2_scale=None):
    sub0 = lax.broadcasted_iota(jnp.int32, (SUBLANES, LANES), 0) == 0
    taps = CONV_K - 1
    for j in range(PROJ_COLS // LANES):
        cols = slice(c0 + j * LANES, c0 + (j + 1) * LANES)
        w = [jnp.broadcast_to(w_ref[k:k + 1, cols], (SUBLANES, LANES)) for k in range(CONV_K)]
        b = None if b_ref is None else jnp.broadcast_to(b_ref[:, cols], (SUBLANES, LANES))
        prev = [hist[i * SUBLANES:(i + 1) * SUBLANES, cols] for i in range(taps)]
        for k in range(rows // PERM_CHUNK):
            base = k * PERM_CHUNK
            xs = [xp[base + a * SUBLANES:base + (a + 1) * SUBLANES, cols] for a in range(SUBLANES)]
            rolled = [pltpu.roll(xs[SUBLANES - taps + i], 1, 0) for i in range(taps)]
            ext = [jnp.where(sub0, prev[i], rolled[i]) for i in range(taps)] + xs
            for a in range(SUBLANES):
                y = ext[a] * w[0]
                for t in range(1, CONV_K):
                    y = y + ext[a + t] * w[t]
                if b is not None:
                    y = y + b
                y = _silu(y)
                scale = None if l2_scale is None else l2_scale(c0 + j * LANES)
                if scale is not None:
                    y = y * (lax.rsqrt(jnp.sum(y * y, axis=-1, keepdims=True) + NORM_EPS) * scale)
                up[j, pl.ds(base + a, SUBLANES, stride=SUBLANES), :] = y
            prev = rolled
        for i in range(taps):
            hist[i * SUBLANES:(i + 1) * SUBLANES, cols] = prev[i]
        out[:, cols] = up[j].astype(out.dtype)


def _inproj_kernel(*refs, tm, steps_per_seq, do_conv):
    x_ref, sc_ref, sh_ref, nw_ref, wzs_ref, wxbc_ref, wzg_ref, wqkv_ref, wsh_ref, wsl_ref = refs[:10]
    if do_conv:
        scw_ref, scb_ref, gcw_ref = refs[10:13]
        (zs_ref, xbc_ref, zg_ref, qkv_ref, sm_ref, ncs_ref, ncg_ref,
         xps, xpg, hist_s, hist_g, hs, hp, up) = refs[13:]
    else:
        zs_ref, xbc_ref, zg_ref, qkv_ref, sm_ref = refs[10:]
    x = x_ref[...]
    ms = jnp.mean(x * x, axis=-1, keepdims=True)
    h = x * lax.rsqrt(ms + NORM_EPS) * nw_ref[...]
    h = h * (1.0 + sc_ref[0]) + sh_ref[0]
    h_hi, h_lo = _terms(h, 2)
    sm = jnp.dot(h_hi, wsh_ref[...], preferred_element_type=F32)
    sm = sm + (jnp.dot(h_hi, wsl_ref[...], preferred_element_type=F32)
               + jnp.dot(h_lo, wsh_ref[...], preferred_element_type=F32))
    sm_ref[...] = sm
    if not do_conv:
        zs_ref[...] = jnp.dot(h_hi, wzs_ref[...], preferred_element_type=F32)
        xbc_ref[...] = jnp.dot(h_hi, wxbc_ref[...], preferred_element_type=F32)
        zg_ref[...] = jnp.dot(h_hi, wzg_ref[...], preferred_element_type=F32)
        qkv_ref[...] = jnp.dot(h_hi, wqkv_ref[...], preferred_element_type=F32)
        return

    @pl.when(pl.program_id(0) % steps_per_seq == 0)
    def _zero_history():
        hist_s[...] = jnp.zeros(hist_s.shape, F32)
        hist_g[...] = jnp.zeros(hist_g.shape, F32)

    assert tm % PERM_CHUNK == 0
    ncb = D_MODEL // LANES
    for cb in range(ncb):
        hs[cb] = h[:, cb * LANES:(cb + 1) * LANES]
    for g in range(0, tm // SUBLANES, 2):
        parts = [jnp.concatenate(
            [hs[cb, pl.ds(PERM_CHUNK * (gg // SUBLANES) + gg % SUBLANES, SUBLANES, stride=SUBLANES), :]
             for cb in range(ncb)], axis=1) for gg in (g, g + 1)]
        hp[g * SUBLANES:(g + 2) * SUBLANES, :] = jnp.concatenate(parts, axis=0).astype(BF16)

    conv_blocks = [(xp, w_ref, cw_ref, cb_ref, hist, out, c0)
                   for xp, w_ref, cw_ref, cb_ref, hist, out, width in (
                       (xps, wxbc_ref, scw_ref, scb_ref, hist_s, xbc_ref, SSD_CONV_DIM),
                       (xpg, wqkv_ref, gcw_ref, None, hist_g, qkv_ref, GDN_CONV_DIM))
                   for c0 in range(0, width, PROJ_COLS)]
    gate_blocks = [(w_ref, out, c0) for w_ref, out, width in ((wzs_ref, zs_ref, SSD_WIDTH),
                                                              (wzg_ref, zg_ref, GDN_WIDTH))
                   for c0 in range(0, width, PROJ_COLS)]

    def project(blk):
        xp, w_ref, _, _, _, _, c0 = blk
        cols = slice(c0, c0 + PROJ_COLS)
        xp[:, cols] = jnp.dot(hp[...], w_ref[:, cols], preferred_element_type=F32)

    def gate_block(i):
        w_ref, out, c0 = gate_blocks[i]
        cols = slice(c0, c0 + PROJ_COLS)
        out[:, cols] = _silu(jnp.dot(h_hi, w_ref[:, cols], preferred_element_type=F32)).astype(out.dtype)

    early_gates = len(gate_blocks) // 2
    for i in range(early_gates):
        gate_block(i)
    project(conv_blocks[0])
    for j, blk in enumerate(conv_blocks):
        if j + 1 < len(conv_blocks):
            project(conv_blocks[j + 1])
        if j % 2 == 0 and early_gates + j // 2 < len(gate_blocks):
            gate_block(early_gates + j // 2)
        xp, _, cw_ref, cb_ref, hist, out, c0 = blk
        _conv_silu_permuted(xp, cw_ref, cb_ref, hist, up, out, tm, c0,
                            l2_scale=_qk_l2_scale if xp is xpg else None)
    assert len(gate_blocks) - early_gates <= (len(conv_blocks) + 1) // 2
    for xp, nst in ((xps, ncs_ref), (xpg, ncg_ref)):
        for i in range(CONV_K - 1):
            row = tm - PERM_CHUNK + SUBLANES * (SUBLANES - (CONV_K - 1) + i) + SUBLANES - 1
            nst[0, i:i + 1, :] = xp[row:row + 1, :]


def _row_mod_spec(per_row, tm, steps_per_group):
    if per_row:
        return pl.BlockSpec((1, tm, D_MODEL), lambda i: (0, i, 0))
    return pl.BlockSpec((1, 1, D_MODEL), lambda i: (i // steps_per_group, 0, 0))


def _resident(shape):
    nd = len(shape)
    return pl.BlockSpec(shape, lambda i: (0,) * nd, pipeline_mode=pl.Buffered(1))


def _inproj_call(x2d, scale, shift, norm_w, w, tm, per_row, steps_per_group, do_conv, act_dtype):
    T = x2d.shape[0]
    assert T % tm == 0
    widths = (SSD_WIDTH, SSD_CONV_DIM, GDN_WIDTH, GDN_CONV_DIM, SMALL)
    dtypes = (act_dtype, act_dtype, act_dtype, act_dtype, F32)
    row = lambda wd: pl.BlockSpec((tm, wd), lambda i: (i, 0))
    in_specs = [
        row(D_MODEL),
        _row_mod_spec(per_row, tm, steps_per_group),
        _row_mod_spec(per_row, tm, steps_per_group),
        _resident((1, D_MODEL)),
        _resident((D_MODEL, SSD_WIDTH)),
        _resident((D_MODEL, SSD_CONV_DIM)),
        _resident((D_MODEL, GDN_WIDTH)),
        _resident((D_MODEL, GDN_CONV_DIM)),
        _resident((D_MODEL, SMALL)),
        _resident((D_MODEL, SMALL)),
    ]
    args = [x2d, scale, shift, norm_w, w["zs"], w["xbc"], w["zg"], w["qkv"], w["sm_hi"], w["sm_lo"]]
    out_specs = [row(wd) for wd in widths]
    out_shape = [jax.ShapeDtypeStruct((T, wd), dt) for wd, dt in zip(widths, dtypes)]
    scratch = []
    if do_conv:
        assert not per_row
        nseq = T // (tm * steps_per_group)
        in_specs += [_resident((CONV_K, SSD_CONV_DIM)), _resident((1, SSD_CONV_DIM)),
                     _resident((CONV_K, GDN_CONV_DIM))]
        args += [w["scw"], w["scb"], w["gcw"]]
        for wd in (SSD_CONV_DIM, GDN_CONV_DIM):
            out_specs.append(pl.BlockSpec((1, CONV_K - 1, wd), lambda i: (i // steps_per_group, 0, 0)))
            out_shape.append(jax.ShapeDtypeStruct((nseq, CONV_K - 1, wd), F32))
            scratch.append(pltpu.VMEM((tm, wd), F32))
        scratch += [pltpu.VMEM(((CONV_K - 1) * SUBLANES, wd), F32) for wd in (SSD_CONV_DIM, GDN_CONV_DIM)]
        scratch += [pltpu.VMEM((D_MODEL // LANES, tm, LANES), F32), pltpu.VMEM((tm, D_MODEL), BF16),
                    pltpu.VMEM((PROJ_COLS // LANES, tm, LANES), F32)]
    return pl.pallas_call(
        functools.partial(_inproj_kernel, tm=tm, steps_per_seq=steps_per_group, do_conv=do_conv),
        grid=(T // tm,),
        in_specs=in_specs,
        out_specs=out_specs,
        out_shape=out_shape,
        scratch_shapes=scratch,
        compiler_params=pltpu.CompilerParams(
            dimension_semantics=("arbitrary",), vmem_limit_bytes=VMEM_LIMIT),
        name="in_proj",
    )(*args)


def _outproj_kernel(x_ref, mix_ref, gate_ref, w_ref, fw_ref, o_ref, *, final_norm):
    y = jnp.dot(mix_ref[...], w_ref[...], preferred_element_type=F32)
    out = x_ref[...] + gate_ref[0] * y
    if final_norm:
        ms = jnp.mean(out * out, axis=-1, keepdims=True)
        out = out * lax.rsqrt(ms + NORM_EPS) * fw_ref[...]
    o_ref[...] = out


def _outproj_call(x2d, mix2d, gate, w_out, final_w, tm, per_row, steps_per_group, final_norm):
    T = x2d.shape[0]
    return pl.pallas_call(
        functools.partial(_outproj_kernel, final_norm=final_norm),
        grid=(T // tm,),
        in_specs=[
            pl.BlockSpec((tm, D_MODEL), lambda i: (i, 0)),
            pl.BlockSpec((tm, MIX_WIDTH), lambda i: (i, 0)),
            _row_mod_spec(per_row, tm, steps_per_group),
            _resident((MIX_WIDTH, D_MODEL)),
            _resident((1, D_MODEL)),
        ],
        out_specs=pl.BlockSpec((tm, D_MODEL), lambda i: (i, 0)),
        out_shape=jax.ShapeDtypeStruct((T, D_MODEL), F32),
        compiler_params=pltpu.CompilerParams(
            dimension_semantics=("arbitrary",), vmem_limit_bytes=VMEM_LIMIT),
        name="out_proj",
    )(x2d, mix2d, gate, w_out, final_w)


def _run_interleaved(*gens):
    live = [g for g in gens if g is not None]
    while live:
        for g in list(live):
            try:
                next(g)
            except StopIteration:
                live.remove(g)


def _chain_gens(gens):
    for g in gens:
        yield from g


def _unit_lower_solve_many(m_list, rhs_list, q, out):
    nb = q // INV_BLOCK
    if nb > 1:
        r = lax.broadcasted_iota(jnp.int32, (q, q), 0) // INV_BLOCK
        c = lax.broadcasted_iota(jnp.int32, (q, q), 1) // INV_BLOCK
        same = r == c
        d_list = [jnp.where(same, m, 0.0) for m in m_list]
        o_list = [m - d for m, d in zip(m_list, d_list)]
    else:
        d_list = m_list
    n_list = [-d for d in d_list]
    e_list = d_list
    for _ in range(int(math.log2(INV_BLOCK)) - 1):
        e_list = [_bdot(e, e) for e in e_list]
        yield
        n_list = [n + e + _bdot(n, e) for n, e in zip(n_list, e_list)]
        yield
    y_list = [rhs + _bdot(n, rhs) for n, rhs in zip(n_list, rhs_list)]
    if nb > 1:
        p_list = [o + _bdot(n, o) for n, o in zip(n_list, o_list)]
        yield
        r_list = [-p for p in p_list]
        e_list = p_list
        for _ in range(int(math.ceil(math.log2(nb))) - 1):
            e_list = [_bdot(e, e) for e in e_list]
            yield
            r_list = [rr + e + _bdot(rr, e) for rr, e in zip(r_list, e_list)]
            yield
        y_list = [y + _bdot(rr, y) for rr, y in zip(r_list, y_list)]
    yield
    out.extend(y_list)


def _mixer_kernel(*refs, names, q, nq, lv, nc, multi_seq, zero_init, do_conv, chains):
    r = dict(zip(names, refs))
    zs_ref, xbc_ref, zg_ref, qkv_ref, sm_ref = r["zs"], r["xbc"], r["zg"], r["qkv"], r["sm"]
    bias_ref, coef_ref, dexp_ref, snw_ref, gnw_ref = r["bias"], r["coef"], r["dexp"], r["snw"], r["gnw"]
    mix_ref, nhs_ref, nsg_ref = r["mix"], r["nhs"], r["nsg"]
    st_s, st_g, y_s, pe_s, xcd_s, bgt_s = r["st_s"], r["st_g"], r["y_s"], r["pe_s"], r["xcd_s"], r["bgt_s"]
    wq_s, u_s, at_s, kdt_s = r["wq_s"], r["u_s"], r["at_s"], r["kdt_s"]

    c_idx = pl.program_id(1)
    rr_ = nq * q
    n_slots = nq if multi_seq else 1
    slot = (lambda c: c) if multi_seq else (lambda c: 0)
    hist = q + SUBLANES
    pair = 2 * SSD_HEAD_DIM
    heads_per_group = SSD_HEADS // SSD_GROUPS
    gw = SSD_WIDTH // SSD_GROUPS
    assert not (do_conv and not multi_seq and nc > 1 and lv != rr_)

    @pl.when(c_idx == 0)
    def _load_state():
        if zero_init:
            st_s[...] = jnp.zeros(st_s.shape, F32)
            st_g[...] = jnp.zeros(st_g.shape, F32)
        else:
            for s in range(n_slots):
                for j in range(SSD_HEADS // 2):
                    blk = r["hs"][0, s, 2 * j:2 * j + 2].reshape(pair, SSD_STATE)
                    st_s[s, :, pair * j:pair * (j + 1)] = blk.T
            st_g[...] = r["sg"][0]
        if do_conv:
            for xp, key, width in ((r["xps"], "cs", SSD_CONV_DIM), (r["xpg"], "cg", GDN_CONV_DIM)):
                for s in range(n_slots):
                    xp[s * hist:s * hist + SUBLANES, :] = jnp.zeros((SUBLANES, width), F32)
                    if not zero_init:
                        xp[s * hist + SUBLANES - (CONV_K - 1):s * hist + SUBLANES, :] = r[key][0, s]

    if do_conv:
        for x_ref, xp, w_ref, b_ref, out, nst, width in (
                (xbc_ref, r["xps"], r["scw"], r["scb"], r["xbc_c"], r["ncs"], SSD_CONV_DIM),
                (qkv_ref, r["xpg"], r["gcw"], None, r["qkv_c"], r["ncg"], GDN_CONV_DIM)):
            if multi_seq:
                for c in range(nq):
                    base = c * hist + SUBLANES
                    xp[base:base + lv, :] = x_ref[c * lv:(c + 1) * lv, :]
                    xp[base + lv:base + q, :] = jnp.zeros((q - lv, width), F32)
                    _conv_silu_rows(xp, w_ref, b_ref, out, q, 0, width, xp_row0=c * hist, out_row0=c * q)
                    nst[0, c] = xp[base + lv - (CONV_K - 1):base + lv, :]
            else:
                xp[SUBLANES:SUBLANES + rr_, :] = x_ref[...]
                _conv_silu_rows(xp, w_ref, b_ref, out, rr_, 0, width)

                @pl.when(c_idx == nc - 1)
                def _store_conv_state():
                    nst[0, 0] = xp[SUBLANES + rr_ - (CONV_K - 1):SUBLANES + rr_, :]

                if nc > 1:
                    xp[0:SUBLANES, :] = xp[rr_:rr_ + SUBLANES, :]
        xbc_c, qkv_c = r["xbc_c"], r["qkv_c"]
        gate = _silu
    else:
        xbc_c, qkv_c = xbc_ref, qkv_ref
        gate = lambda z: z.astype(F32)

    if multi_seq:
        for src, dst, wd in ((zs_ref, r["pzs"], SSD_WIDTH), (zg_ref, r["pzg"], GDN_WIDTH),
                             (sm_ref, r["psm"], SMALL)):
            for c in range(nq):
                dst[c * q:c * q + lv, :] = src[c * lv:(c + 1) * lv, :].astype(F32)
                dst[c * q + lv:(c + 1) * q, :] = jnp.zeros((q - lv, wd), F32)
        zs_v, zg_v, sm = r["pzs"], r["pzg"], r["psm"][...]
    else:
        zs_v, zg_v, sm = zs_ref, zg_ref, sm_ref[...]

    sp = _softplus(sm + bias_ref[...])
    beta_all = jax.nn.sigmoid(sm)
    if multi_seq and lv < q:
        assert q & (q - 1) == 0
        valid = jnp.bitwise_and(lax.broadcasted_iota(jnp.int32, (rr_, SMALL), 0), q - 1) < lv
        sp = jnp.where(valid, sp, 0.0)
        beta_all = jnp.where(valid, beta_all, 0.0)
    ag = sp * coef_ref[...]
    r_all = lax.broadcasted_iota(jnp.int32, (rr_, rr_), 0)
    c_all = lax.broadcasted_iota(jnp.int32, (rr_, rr_), 1)
    chunk_start = (r_all // q) * q
    tril_bd = jnp.where(r_all >= c_all, jnp.where(c_all >= chunk_start, 1.0, 0.0), 0.0).astype(BF16)
    rq = lax.broadcasted_iota(jnp.int32, (q, q), 0)
    cq = lax.broadcasted_iota(jnp.int32, (q, q), 1)
    incl = rq >= cq
    strict = rq > cq
    r128 = lax.broadcasted_iota(jnp.int32, (LANES, LANES), 0)
    c128 = lax.broadcasted_iota(jnp.int32, (LANES, LANES), 1)
    eye = jnp.where(r128 == c128, 1.0, 0.0).astype(BF16)
    cum = _const_dot(tril_bd, ag, 3)
    cum_t = _transpose_via_identity(eye, cum, 3)
    sp_t = _transpose_via_identity(eye, sp, 3)
    ecum = jnp.exp(cum)
    if nq > 1:
        cum_last = jnp.concatenate(
            [jnp.broadcast_to(cum[(c + 1) * q - 1:(c + 1) * q, :], (q, SMALL)) for c in range(nq)], axis=0)
    else:
        cum_last = cum[q - 1:q, :]
    to_end = jnp.exp(cum_last - cum)
    er = lax.broadcasted_iota(jnp.int32, (SMALL, SSD_WIDTH), 0)
    ec = lax.broadcasted_iota(jnp.int32, (SMALL, SSD_WIDTH), 1) // SSD_HEAD_DIM
    expand = jnp.where(er == ec, 1.0, 0.0).astype(BF16)
    pe_s[...] = _dot_const(ecum, expand, 2)
    p_dtend = _dot_const(sp * to_end, expand, 2)
    xcd_s[...] = (xbc_c[:, 0:SSD_WIDTH].astype(F32) * p_dtend).astype(BF16)

    lane = lax.broadcasted_iota(jnp.int32, (q, pair), 1)
    lo_half = lane < SSD_HEAD_DIM
    scores = {}
    for c in range(nq):
        rows = slice(c * q, (c + 1) * q)
        for g in range(SSD_GROUPS):
            b_g = xbc_c[rows, SSD_WIDTH + g * SSD_STATE:SSD_WIDTH + (g + 1) * SSD_STATE]
            c_g = xbc_c[rows, SSD_WIDTH + (SSD_GROUPS + g) * SSD_STATE:
                        SSD_WIDTH + (SSD_GROUPS + g + 1) * SSD_STATE]
            scores[c, g] = _bdot_nt(c_g, b_g)
            bgt_s[c * SSD_GROUPS + g] = _transpose_via_identity(eye, b_g, 1).astype(BF16)
    def ssd_diag(items):
        for c in items:
            rows = slice(c * q, (c + 1) * q)
            for j in range(SSD_HEADS // 2):
                g = j // (heads_per_group // 2)
                pcols = slice(pair * j, pair * (j + 1))
                x_pair = xbc_c[rows, pcols].astype(F32)
                y_pair = x_pair * dexp_ref[:, pcols]
                for half in range(2):
                    h = 2 * j + half
                    seg = cum[rows, h:h + 1] - cum_t[h:h + 1, rows]
                    lmat = jnp.exp(jnp.where(incl, seg, -jnp.inf))
                    m_h = scores[c, g] * lmat * sp_t[h:h + 1, rows]
                    x_h = jnp.where(lo_half if half == 0 else jnp.logical_not(lo_half), x_pair, 0.0)
                    y_pair = y_pair + _bdot(m_h, x_h)
                y_s[rows, pcols] = y_pair
                if j % 2 == 1:
                    yield

    def gdn_independent(group):
        m_list, rhs_list = [], []
        for n, (c, h) in enumerate(group):
            rows = slice(c * q, (c + 1) * q)
            i = c * GDN_HEADS + h
            qh = qkv_c[rows, h * GDN_HEAD_DIM:(h + 1) * GDN_HEAD_DIM].astype(F32)
            kh = qkv_c[rows, GDN_WIDTH + h * GDN_HEAD_DIM:GDN_WIDTH + (h + 1) * GDN_HEAD_DIM].astype(F32)
            vh = qkv_c[rows, 2 * GDN_WIDTH + h * GDN_HEAD_DIM:
                       2 * GDN_WIDTH + (h + 1) * GDN_HEAD_DIM].astype(F32)
            if do_conv:
                qh = qh * lax.rsqrt(jnp.sum(qh * qh, axis=-1, keepdims=True) + NORM_EPS) * (GDN_HEAD_DIM ** -0.5)
                kh = kh * lax.rsqrt(jnp.sum(kh * kh, axis=-1, keepdims=True) + NORM_EPS)
            la = _SM_A + h
            bcol = beta_all[rows, _SM_B + h:_SM_B + h + 1]
            eg = ecum[rows, la:la + 1]
            kb = kh * bcol
            dec = jnp.exp(jnp.where(incl, cum[rows, la:la + 1] - cum_t[la:la + 1, rows], -jnp.inf))
            kq = _bdot_nt(jnp.concatenate([kb, qh], axis=0), kh)
            m_list.append(jnp.where(strict, kq[0:q] * dec, 0.0))
            at_s[i] = (kq[q:2 * q] * dec).astype(BF16)
            rhs_list.append(jnp.concatenate([vh * bcol, kb * eg], axis=1))
            wq_s[i, q:2 * q, :] = (qh * eg).astype(BF16)
            kdt_s[i] = _transpose_via_identity(eye, kh * to_end[rows, la:la + 1], 1).astype(BF16)
            if n % 4 == 3:
                yield
        sol_list = []
        yield from _unit_lower_solve_many(m_list, rhs_list, q, sol_list)
        for (c, h), sol in zip(group, sol_list):
            i = c * GDN_HEADS + h
            u_s[i] = sol[:, 0:GDN_HEAD_DIM]
            wq_s[i, 0:q, :] = sol[:, GDN_HEAD_DIM:2 * GDN_HEAD_DIM].astype(BF16)
        yield

    def store_mix(c, cols, val):
        if multi_seq:
            mix_ref[c * lv:(c + 1) * lv, cols] = val[0:lv].astype(mix_ref.dtype)
        else:
            mix_ref[c * q:(c + 1) * q, cols] = val.astype(mix_ref.dtype)

    def recurrences(items):
        for c in items:
            rows = slice(c * q, (c + 1) * q)
            last = (c + 1) * q - 1
            s = slot(c)
            for g in range(SSD_GROUPS):
                gcols = slice(g * gw, (g + 1) * gw)
                c_g = xbc_c[rows, SSD_WIDTH + (SSD_GROUPS + g) * SSD_STATE:
                            SSD_WIDTH + (SSD_GROUPS + g + 1) * SSD_STATE]
                state = st_s[s, :, gcols]
                y_g = y_s[rows, gcols] + _bdot(c_g, state) * pe_s[rows, gcols]
                st_s[s, :, gcols] = state * pe_s[last:last + 1, gcols] + jnp.dot(
                    bgt_s[c * SSD_GROUPS + g], xcd_s[rows, gcols], preferred_element_type=F32)
                y_g = y_g * gate(zs_v[rows, gcols])
                ms = jnp.mean(y_g * y_g, axis=-1, keepdims=True)
                y_g = y_g * lax.rsqrt(ms + NORM_EPS) * snw_ref[:, gcols]
                store_mix(c, gcols, y_g)
            yield
        pairs = [(c, h) for c in items for h in range(GDN_HEADS)]
        ws = {}
        for (c, h) in pairs:
            ws[c, h] = jnp.dot(wq_s[c * GDN_HEADS + h], st_g[slot(c), h].astype(BF16),
                               preferred_element_type=F32)
        yield
        vn = {(c, h): (u_s[c * GDN_HEADS + h] - ws[c, h][0:q]).astype(BF16) for (c, h) in pairs}
        for (c, h) in pairs:
            i = c * GDN_HEADS + h
            la = _SM_A + h
            last = (c + 1) * q - 1
            s = slot(c)
            o = ws[c, h][q:2 * q] + jnp.dot(at_s[i], vn[c, h], preferred_element_type=F32)
            st_g[s, h] = st_g[s, h] * ecum[last:last + 1, la:la + 1] + jnp.dot(
                kdt_s[i], vn[c, h], preferred_element_type=F32)
            hc = slice(h * GDN_HEAD_DIM, (h + 1) * GDN_HEAD_DIM)
            ms = jnp.mean(o * o, axis=-1, keepdims=True)
            o = o * lax.rsqrt(ms + NORM_EPS) * gnw_ref[...] * gate(zg_v[c * q:(c + 1) * q, hc])
            store_mix(c, slice(SSD_WIDTH + h * GDN_HEAD_DIM, SSD_WIDTH + (h + 1) * GDN_HEAD_DIM), o)
        yield

    all_ch = [(c, h) for c in range(nq) for h in range(GDN_HEADS)]
    groups = [all_ch[i:i + chains] for i in range(0, len(all_ch), chains)]
    batches = [list(range(nq))] if multi_seq else [[c] for c in range(nq)]
    assert chains % GDN_HEADS == 0
    pending = list(batches)
    done_items = set()
    if multi_seq:
        for _ in ssd_diag(range(nq)):
            pass
    for grp in groups:
        runnable = [b for b in pending if set(b) <= done_items]
        for b in runnable:
            pending.remove(b)
        others = [_chain_gens([recurrences(b) for b in runnable])] if runnable else []
        if not multi_seq:
            others.append(ssd_diag(sorted({c for c, _ in grp})))
        _run_interleaved(gdn_independent(grp), *others)
        done_items |= {c for c, _ in grp}
    for b in pending:
        for _ in recurrences(b):
            pass

    @pl.when(c_idx == nc - 1)
    def _store_state():
        for s in range(n_slots):
            for j in range(SSD_HEADS // 2):
                blk = st_s[s, :, pair * j:pair * (j + 1)].T
                nhs_ref[0, s, 2 * j:2 * j + 2] = blk.reshape(2, SSD_HEAD_DIM, SSD_STATE)
        nsg_ref[0] = st_g[...]


_STATE_TAILS = (
    (CONV_K - 1, SSD_CONV_DIM),
    (SSD_HEADS, SSD_HEAD_DIM, SSD_STATE),
    (CONV_K - 1, GDN_CONV_DIM),
    (GDN_HEADS, GDN_HEAD_DIM, GDN_HEAD_DIM),
)


_STATE_NAMES = ("cs", "hs", "cg", "sg")


def _mixer_call(zs, xbc, zg, qkv, sm, B, L, states_in, prev_out, p, q, nq, multi_seq, layer, chains, do_conv):
    assert zs.shape[0] == B * L
    rows = nq * q
    assert q % INV_BLOCK == 0
    zero_init = states_in is None
    if multi_seq:
        assert CONV_K - 1 <= L <= q and B % nq == 0 and do_conv
        nc, lv, bb = 1, L, nq
        tok = lambda wd: pl.BlockSpec((nq * L, wd), lambda b, c: (b, 0))
    else:
        assert L % rows == 0
        nc, lv, bb = L // rows, rows, 1
        tok = lambda wd: pl.BlockSpec((rows, wd), lambda b, c: (b * nc + c, 0))
    n_slots = bb
    per_b = lambda shape: pl.BlockSpec((1, bb) + shape, lambda b, c: (layer, b) + (0,) * len(shape))
    const = lambda shape: pl.BlockSpec(shape, lambda b, c: (0,) * len(shape))
    nch = nq * GDN_HEADS
    kept = [k for k in range(4) if do_conv or k in (1, 3)]

    names = ["zs", "xbc", "zg", "qkv", "sm"]
    in_specs = [tok(SSD_WIDTH), tok(SSD_CONV_DIM), tok(GDN_WIDTH), tok(GDN_CONV_DIM), tok(SMALL)]
    args = [zs, xbc, zg, qkv, sm]
    if not zero_init:
        for k in kept:
            names.append(_STATE_NAMES[k])
            in_specs.append(per_b(_STATE_TAILS[k]))
            args.append(states_in[k])
    consts = [("bias", (1, SMALL)), ("coef", (1, SMALL)), ("dexp", (1, SSD_WIDTH)), ("snw", (1, SSD_WIDTH)),
              ("gnw", (1, GDN_HEAD_DIM))]
    if do_conv:
        consts += [("scw", (CONV_K, SSD_CONV_DIM)), ("scb", (1, SSD_CONV_DIM)), ("gcw", (CONV_K, GDN_CONV_DIM))]
    for nm, shape in consts:
        names.append(nm)
        in_specs.append(const(shape))
        args.append(p[nm])
    aliases = {}
    if prev_out is not None:
        for k, a in enumerate(prev_out):
            names.append("alias%d" % k)
            aliases[len(args)] = 1 + k
            in_specs.append(pl.BlockSpec(memory_space=pl.ANY))
            args.append(a)
    names += ["mix"] + ["n" + _STATE_NAMES[k] for k in kept]
    out_specs = [tok(MIX_WIDTH)] + [per_b(_STATE_TAILS[k]) for k in kept]
    out_shape = [jax.ShapeDtypeStruct((B * L, MIX_WIDTH), BF16)] + [
        jax.ShapeDtypeStruct((DEPTH, B) + _STATE_TAILS[k], F32) for k in kept]
    scratch = [
        ("st_s", pltpu.VMEM((n_slots, SSD_STATE, SSD_WIDTH), F32)),
        ("st_g", pltpu.VMEM((n_slots, GDN_HEADS, GDN_HEAD_DIM, GDN_HEAD_DIM), F32)),
        ("y_s", pltpu.VMEM((rows, SSD_WIDTH), F32)),
        ("pe_s", pltpu.VMEM((rows, SSD_WIDTH), F32)),
        ("xcd_s", pltpu.VMEM((rows, SSD_WIDTH), BF16)),
        ("bgt_s", pltpu.VMEM((nq * SSD_GROUPS, SSD_STATE, q), BF16)),
        ("wq_s", pltpu.VMEM((nch, 2 * q, GDN_HEAD_DIM), BF16)),
        ("u_s", pltpu.VMEM((nch, q, GDN_HEAD_DIM), F32)),
        ("at_s", pltpu.VMEM((nch, q, q), BF16)),
        ("kdt_s", pltpu.VMEM((nch, GDN_HEAD_DIM, q), BF16)),
    ]
    if do_conv:
        xp_rows = nq * (q + SUBLANES) if multi_seq else rows + SUBLANES
        scratch += [
            ("xps", pltpu.VMEM((xp_rows, SSD_CONV_DIM), F32)),
            ("xpg", pltpu.VMEM((xp_rows, GDN_CONV_DIM), F32)),
            ("xbc_c", pltpu.VMEM((rows, SSD_CONV_DIM), F32)),
            ("qkv_c", pltpu.VMEM((rows, GDN_CONV_DIM), F32)),
        ]
    if multi_seq:
        scratch += [
            ("pzs", pltpu.VMEM((rows, SSD_WIDTH), F32)),
            ("pzg", pltpu.VMEM((rows, GDN_WIDTH), F32)),
            ("psm", pltpu.VMEM((rows, SMALL), F32)),
        ]
    names += [nm for nm, _ in scratch]
    return pl.pallas_call(
        functools.partial(_mixer_kernel, names=tuple(names), q=q, nq=nq, lv=lv, nc=nc, multi_seq=multi_seq,
                          zero_init=zero_init, do_conv=do_conv, chains=chains),
        grid=(B // bb, nc),
        in_specs=in_specs,
        out_specs=out_specs,
        out_shape=out_shape,
        input_output_aliases=aliases,
        scratch_shapes=[s for _, s in scratch],
        compiler_params=pltpu.CompilerParams(
            dimension_semantics=("arbitrary", "arbitrary"), vmem_limit_bytes=VMEM_LIMIT),
        name="mixer",
    )(*args)


def _prep_layer(l, norm_w, w_in, w_in_bf, ssd_conv_w, ssd_conv_b, ssd_dt_bias, ssd_a_log, ssd_d, ssd_norm_w,
                gdn_conv_w, gdn_dt_bias, gdn_a_log, gdn_norm_w, w_out):
    wi = w_in[l]
    wb = w_in_bf[l]
    small = jnp.concatenate(
        [wi[:, _OFF_DT:_OFF_DT + SSD_HEADS], wi[:, _OFF_A:_OFF_A + GDN_HEADS], wi[:, _OFF_B:_OFF_B + GDN_HEADS],
         jnp.zeros((D_MODEL, SMALL - SSD_HEADS - 2 * GDN_HEADS), F32)], axis=1)
    sm_hi = small.astype(BF16)
    sm_lo = (small - sm_hi.astype(F32)).astype(BF16)
    pad = jnp.zeros((SMALL - SSD_HEADS - GDN_HEADS,), F32)
    return {
        "norm_w": norm_w[l].reshape(1, D_MODEL),
        "zs": wb[:, _OFF_ZS:_OFF_ZS + SSD_WIDTH],
        "xbc": wb[:, _OFF_XBC:_OFF_XBC + SSD_CONV_DIM],
        "zg": wb[:, _OFF_ZG:_OFF_ZG + GDN_WIDTH],
        "qkv": wb[:, _OFF_QKV:_OFF_QKV + GDN_CONV_DIM],
        "sm_hi": sm_hi,
        "sm_lo": sm_lo,
        "scw": ssd_conv_w[l],
        "scb": ssd_conv_b[l].reshape(1, SSD_CONV_DIM),
        "gcw": gdn_conv_w[l],
        "bias": jnp.concatenate([ssd_dt_bias[l], gdn_dt_bias[l], pad]).reshape(1, SMALL),
        "coef": jnp.concatenate([-jnp.exp(ssd_a_log[l]), -jnp.exp(gdn_a_log[l]), pad]).reshape(1, SMALL),
        "dexp": jnp.repeat(ssd_d[l], SSD_HEAD_DIM).reshape(1, SSD_WIDTH),
        "snw": ssd_norm_w[l].reshape(1, SSD_WIDTH),
        "gnw": gdn_norm_w[l].reshape(1, GDN_HEAD_DIM),
        "w_out": w_out[l].astype(BF16),
    }


def _trunk(x, mod, states, params, final_w, q, nq, tm, chains, act_dtype):
    B, L, _ = x.shape
    T = B * L
    per_row = L < SUBLANES
    steps_per_group = max(L // tm, 1)
    x2d = x.reshape(T, D_MODEL)
    conv_in_proj = states is None and not per_row
    new_states = None
    conv_states = ([], [])
    for l in range(DEPTH):
        p = params[l]
        shift, scale, gate = (mod[l, :, i * D_MODEL:(i + 1) * D_MODEL] for i in range(3))
        if per_row:
            rep = lambda a: jnp.repeat(a, L, axis=0).reshape(1, T, D_MODEL)
        else:
            rep = lambda a: a.reshape(B, 1, D_MODEL)
        zs, xbc, zg, qkv, sm, *cst = _inproj_call(x2d, rep(scale), rep(shift), p["norm_w"], p, tm, per_row,
                                                  steps_per_group, conv_in_proj, act_dtype)
        mix, *new_states = _mixer_call(zs, xbc, zg, qkv, sm, B, L, states, new_states, p,
                                       q, nq, per_row, l, chains, not conv_in_proj)
        x2d = _outproj_call(x2d, mix, rep(gate), p["w_out"], final_w, tm, per_row,
                            steps_per_group, final_norm=(l == DEPTH - 1))
        for lst, s in zip(conv_states, cst):
            lst.append(s)
    if conv_in_proj:
        new_states = [jnp.stack(conv_states[0]), new_states[0], jnp.stack(conv_states[1]), new_states[1]]
    return x2d.reshape(B, L, D_MODEL), new_states


def kernel(x_prompt, x_sample, state_ssd_conv, state_ssm, state_gdn_conv, state_gdn, c_prompt, c_sample,
           norm_w, w_ada, b_ada, w_in, ssd_conv_w, ssd_conv_b, ssd_dt_bias, ssd_a_log, ssd_d, ssd_norm_w,
           gdn_conv_w, gdn_dt_bias, gdn_a_log, gdn_norm_w, w_out, final_norm_w):
    bp = x_prompt.shape[0]
    w_in_bf = lax.optimization_barrier(w_in.astype(BF16))
    params = [_prep_layer(l, norm_w, w_in, w_in_bf, ssd_conv_w, ssd_conv_b, ssd_dt_bias, ssd_a_log, ssd_d,
                          ssd_norm_w, gdn_conv_w, gdn_dt_bias, gdn_a_log, gdn_norm_w, w_out)
              for l in range(DEPTH)]
    final_w = final_norm_w.reshape(1, D_MODEL)
    mod = _ada_call(jnp.concatenate([c_prompt, c_sample], axis=0), w_ada, b_ada)
    y_p, sp = _trunk(x_prompt, mod[:, :bp], None, params, final_w, q=64, nq=4, tm=512, chains=16,
                     act_dtype=BF16)
    y_s, ss = _trunk(x_sample, mod[:, bp:], (state_ssd_conv, state_ssm, state_gdn_conv, state_gdn),
                     params, final_w, q=16, nq=4, tm=x_sample.shape[0] * x_sample.shape[1], chains=16,
                     act_dtype=F32)
    return (y_p, y_s, sp[0], sp[1], sp[2], sp[3], ss[0], ss[1], ss[2], ss[3])
```

```python
import functools
import math

import jax
import jax.numpy as jnp
from jax import lax
from jax.experimental import pallas as pl
from jax.experimental.pallas import tpu as pltpu

F32 = jnp.float32
BF16 = jnp.bfloat16

D_MODEL = 1024
DEPTH = 2
SSD_WIDTH = 1024
SSD_HEAD_DIM = 64
SSD_HEADS = 16
SSD_GROUPS = 2
SSD_STATE = 128
SSD_CONV_DIM = SSD_WIDTH + 2 * SSD_GROUPS * SSD_STATE
GDN_WIDTH = 1024
GDN_HEAD_DIM = 128
GDN_HEADS = 8
GDN_CONV_DIM = 3 * GDN_WIDTH
MIX_WIDTH = SSD_WIDTH + GDN_WIDTH
CONV_K = 4
NORM_EPS = 1e-6
ADA_DIM = 3 * D_MODEL
SMALL = 128
LANES = 128
SUBLANES = 8
INV_BLOCK = 16
VMEM_LIMIT = 56 * 1024 * 1024

_OFF_ZS = 0
_OFF_XBC = _OFF_ZS + SSD_WIDTH
_OFF_DT = _OFF_XBC + SSD_CONV_DIM
_OFF_ZG = _OFF_DT + SSD_HEADS
_OFF_QKV = _OFF_ZG + GDN_WIDTH
_OFF_A = _OFF_QKV + GDN_CONV_DIM
_OFF_B = _OFF_A + GDN_HEADS
_SM_DT = 0
_SM_A = SSD_HEADS
_SM_B = SSD_HEADS + GDN_HEADS


def _bdot(a, b):
    return jnp.dot(a.astype(BF16), b.astype(BF16), preferred_element_type=F32)


def _bdot_nt(a, b):
    return lax.dot_general(a.astype(BF16), b.astype(BF16), (((1,), (1,)), ((), ())),
                           preferred_element_type=F32)


def _terms(x, n):
    out = []
    r = x
    for i in range(n):
        t = r.astype(BF16)
        out.append(t)
        if i + 1 < n:
            r = r - t.astype(F32)
    return out


def _const_dot(c, x, n):
    acc = None
    for t in _terms(x, n):
        p = jnp.dot(c, t, preferred_element_type=F32)
        acc = p if acc is None else acc + p
    return acc


def _dot_const(x, c, n):
    acc = None
    for t in _terms(x, n):
        p = jnp.dot(t, c, preferred_element_type=F32)
        acc = p if acc is None else acc + p
    return acc


def _transpose_via_identity(eye, x, n):
    acc = None
    for t in _terms(x, n):
        p = lax.dot_general(eye, t, (((1,), (1,)), ((), ())), preferred_element_type=F32)
        acc = p if acc is None else acc + p
    return acc


def _softplus(x):
    return jnp.maximum(x, 0.0) + jnp.log1p(jnp.exp(-jnp.abs(x)))


def _silu(x):
    return x * jax.nn.sigmoid(x)


def _ada_kernel(c_ref, w_ref, b_ref, o_ref):
    s = _silu(c_ref[...])
    w = w_ref[0]
    s_hi, s_lo = _terms(s, 2)
    w_hi, w_lo = _terms(w, 2)
    acc = jnp.dot(s_hi, w_hi, preferred_element_type=F32)
    acc = acc + (jnp.dot(s_hi, w_lo, preferred_element_type=F32)
                 + jnp.dot(s_lo, w_hi, preferred_element_type=F32))
    o_ref[0] = acc + b_ref[0]


def _ada_call(c_all, w_ada, b_ada):
    rows = c_all.shape[0]
    tn = D_MODEL
    return pl.pallas_call(
        _ada_kernel,
        grid=(DEPTH, ADA_DIM // tn),
        in_specs=[
            pl.BlockSpec((rows, D_MODEL), lambda l, j: (0, 0)),
            pl.BlockSpec((1, D_MODEL, tn), lambda l, j: (l, 0, j)),
            pl.BlockSpec((1, 1, tn), lambda l, j: (l, 0, j)),
        ],
        out_specs=pl.BlockSpec((1, rows, tn), lambda l, j: (l, 0, j)),
        out_shape=jax.ShapeDtypeStruct((DEPTH, rows, ADA_DIM), F32),
        compiler_params=pltpu.CompilerParams(
            dimension_semantics=("arbitrary", "arbitrary"), vmem_limit_bytes=VMEM_LIMIT),
        name="ada_mod",
    )(c_all, w_ada, b_ada.reshape(DEPTH, 1, ADA_DIM))


CONV_ROWS = 64
CONV_COLS = 128
PROJ_COLS = 512


def _conv_silu_rows(xp, w_ref, b_ref, out, rows, c0, c1, xp_row0=0, out_row0=0):
    rows_per = min(rows, CONV_ROWS)
    for i in range(rows // rows_per):
        r0 = i * rows_per
        for j in range((c1 - c0) // CONV_COLS):
            cols = slice(c0 + j * CONV_COLS, c0 + (j + 1) * CONV_COLS)
            x = xp[xp_row0 + r0:xp_row0 + r0 + rows_per + SUBLANES, cols]
            acc = x * w_ref[0:1, cols]
            for k in range(1, CONV_K):
                acc = pltpu.roll(acc, 1, 0) + x * w_ref[k:k + 1, cols]
            if b_ref is not None:
                acc = acc + b_ref[:, cols]
            out[out_row0 + r0:out_row0 + r0 + rows_per, cols] = _silu(acc[SUBLANES:, :]).astype(out.dtype)


PERM_CHUNK = SUBLANES * SUBLANES


def _qk_l2_scale(col):
    assert GDN_HEAD_DIM == LANES
    if col < GDN_WIDTH:
        return GDN_HEAD_DIM ** -0.5
    if col < 2 * GDN_WIDTH:
        return 1.0
    return None


def _conv_silu_permuted(xp, w_ref, b_ref, hist, up, out, rows, c0, l2_scale=None):
    sub0 = lax.broadcasted_iota(jnp.int32, (SUBLANES, LANES), 0) == 0
    taps = CONV_K - 1
    for j in range(PROJ_COLS // LANES):
        cols = slice(c0 + j * LANES, c0 + (j + 1) * LANES)
        w = [jnp.broadcast_to(w_ref[k:k + 1, cols], (SUBLANES, LANES)) for k in range(CONV_K)]
        b = None if b_ref is None else jnp.broadcast_to(b_ref[:, cols], (SUBLANES, LANES))
        prev = [hist[i * SUBLANES:(i + 1) * SUBLANES, cols] for i in range(taps)]
        for k in range(rows // PERM_CHUNK):
            base = k * PERM_CHUNK
            xs = [xp[base + a * SUBLANES:base + (a + 1) * SUBLANES, cols] for a in range(SUBLANES)]
            rolled = [pltpu.roll(xs[SUBLANES - taps + i], 1, 0) for i in range(taps)]
            ext = [jnp.where(sub0, prev[i], rolled[i]) for i in range(taps)] + xs
            for a in range(SUBLANES):
                y = ext[a] * w[0]
                for t in range(1, CONV_K):
                    y = y + ext[a + t] * w[t]
                if b is not None:
                    y = y + b
                y = _silu(y)
                scale = None if l2_scale is None else l2_scale(c0 + j * LANES)
                if scale is not None:
                    y = y * (lax.rsqrt(jnp.sum(y * y, axis=-1, keepdims=True) + NORM_EPS) * scale)
                up[j, pl.ds(base + a, SUBLANES, stride=SUBLANES), :] = y
            prev = rolled
        for i in range(taps):
            hist[i * SUBLANES:(i + 1) * SUBLANES, cols] = prev[i]
        out[:, cols] = up[j].astype(out.dtype)


def _inproj_kernel(*refs, tm, steps_per_seq, do_conv):
    x_ref, sc_ref, sh_ref, nw_ref, wzs_ref, wxbc_ref, wzg_ref, wqkv_ref, wsh_ref, wsl_ref = refs[:10]
    if do_conv:
        scw_ref, scb_ref, gcw_ref = refs[10:13]
        (zs_ref, xbc_ref, zg_ref, qkv_ref, sm_ref, ncs_ref, ncg_ref,
         xps, xpg, hist_s, hist_g, hs, hp, up) = refs[13:]
    else:
        zs_ref, xbc_ref, zg_ref, qkv_ref, sm_ref = refs[10:]
    x = x_ref[...]
    ms = jnp.mean(x * x, axis=-1, keepdims=True)
    h = x * lax.rsqrt(ms + NORM_EPS) * nw_ref[...]
    h = h * (1.0 + sc_ref[0]) + sh_ref[0]
    h_hi, h_lo = _terms(h, 2)
    sm = jnp.dot(h_hi, wsh_ref[...], preferred_element_type=F32)
    sm = sm + (jnp.dot(h_hi, wsl_ref[...], preferred_element_type=F32)
               + jnp.dot(h_lo, wsh_ref[...], preferred_element_type=F32))
    sm_ref[...] = sm
    if not do_conv:
        zs_ref[...] = jnp.dot(h_hi, wzs_ref[...], preferred_element_type=F32)
        xbc_ref[...] = jnp.dot(h_hi, wxbc_ref[...], preferred_element_type=F32)
        zg_ref[...] = jnp.dot(h_hi, wzg_ref[...], preferred_element_type=F32)
        qkv_ref[...] = jnp.dot(h_hi, wqkv_ref[...], preferred_element_type=F32)
        return

    @pl.when(pl.program_id(0) % steps_per_seq == 0)
    def _zero_history():
        hist_s[...] = jnp.zeros(hist_s.shape, F32)
        hist_g[...] = jnp.zeros(hist_g.shape, F32)

    assert tm % PERM_CHUNK == 0
    ncb = D_MODEL // LANES
    for cb in range(ncb):
        hs[cb] = h[:, cb * LANES:(cb + 1) * LANES]
    for g in range(0, tm // SUBLANES, 2):
        parts = [jnp.concatenate(
            [hs[cb, pl.ds(PERM_CHUNK * (gg // SUBLANES) + gg % SUBLANES, SUBLANES, stride=SUBLANES), :]
             for cb in range(ncb)], axis=1) for gg in (g, g + 1)]
        hp[g * SUBLANES:(g + 2) * SUBLANES, :] = jnp.concatenate(parts, axis=0).astype(BF16)

    conv_blocks = [(xp, w_ref, cw_ref, cb_ref, hist, out, c0)
                   for xp, w_ref, cw_ref, cb_ref, hist, out, width in (
                       (xps, wxbc_ref, scw_ref, scb_ref, hist_s, xbc_ref, SSD_CONV_DIM),
                       (xpg, wqkv_ref, gcw_ref, None, hist_g, qkv_ref, GDN_CONV_DIM))
                   for c0 in range(0, width, PROJ_COLS)]
    gate_blocks = [(w_ref, out, c0) for w_ref, out, width in ((wzs_ref, zs_ref, SSD_WIDTH),
                                                              (wzg_ref, zg_ref, GDN_WIDTH))
                   for c0 in range(0, width, PROJ_COLS)]

    def project(blk):
        xp, w_ref, _, _, _, _, c0 = blk
        cols = slice(c0, c0 + PROJ_COLS)
        xp[:, cols] = jnp.dot(hp[...], w_ref[:, cols], preferred_element_type=F32)

    def gate_block(i):
        w_ref, out, c0 = gate_blocks[i]
        cols = slice(c0, c0 + PROJ_COLS)
        out[:, cols] = _silu(jnp.dot(h_hi, w_ref[:, cols], preferred_element_type=F32)).astype(out.dtype)

    early_gates = len(gate_blocks) // 2
    for i in range(early_gates):
        gate_block(i)
    project(conv_blocks[0])
    for j, blk in enumerate(conv_blocks):
        if j + 1 < len(conv_blocks):
            project(conv_blocks[j + 1])
        if j % 2 == 0 and early_gates + j // 2 < len(gate_blocks):
            gate_block(early_gates + j // 2)
        xp, _, cw_ref, cb_ref, hist, out, c0 = blk
        _conv_silu_permuted(xp, cw_ref, cb_ref, hist, up, out, tm, c0,
                            l2_scale=_qk_l2_scale if xp is xpg else None)
    assert len(gate_blocks) - early_gates <= (len(conv_blocks) + 1) // 2
    for xp, nst in ((xps, ncs_ref), (xpg, ncg_ref)):
        for i in range(CONV_K - 1):
            row = tm - PERM_CHUNK + SUBLANES * (SUBLANES - (CONV_K - 1) + i) + SUBLANES - 1
            nst[0, i:i + 1, :] = xp[row:row + 1, :]


def _row_mod_spec(per_row, tm, steps_per_group):
    if per_row:
        return pl.BlockSpec((1, tm, D_MODEL), lambda i: (0, i, 0))
    return pl.BlockSpec((1, 1, D_MODEL), lambda i: (i // steps_per_group, 0, 0))


def _resident(shape):
    nd = len(shape)
    return pl.BlockSpec(shape, lambda i: (0,) * nd, pipeline_mode=pl.Buffered(1))


def _inproj_call(x2d, scale, shift, norm_w, w, tm, per_row, steps_per_group, do_conv, act_dtype):
    T = x2d.shape[0]
    assert T % tm == 0
    widths = (SSD_WIDTH, SSD_CONV_DIM, GDN_WIDTH, GDN_CONV_DIM, SMALL)
    dtypes = (act_dtype, act_dtype, act_dtype, act_dtype, F32)
    row = lambda wd: pl.BlockSpec((tm, wd), lambda i: (i, 0))
    in_specs = [
        row(D_MODEL),
        _row_mod_spec(per_row, tm, steps_per_group),
        _row_mod_spec(per_row, tm, steps_per_group),
        _resident((1, D_MODEL)),
        _resident((D_MODEL, SSD_WIDTH)),
        _resident((D_MODEL, SSD_CONV_DIM)),
        _resident((D_MODEL, GDN_WIDTH)),
        _resident((D_MODEL, GDN_CONV_DIM)),
        _resident((D_MODEL, SMALL)),
        _resident((D_MODEL, SMALL)),
    ]
    args = [x2d, scale, shift, norm_w, w["zs"], w["xbc"], w["zg"], w["qkv"], w["sm_hi"], w["sm_lo"]]
    out_specs = [row(wd) for wd in widths]
    out_shape = [jax.ShapeDtypeStruct((T, wd), dt) for wd, dt in zip(widths, dtypes)]
    scratch = []
    if do_conv:
        assert not per_row
        nseq = T // (tm * steps_per_group)
        in_specs += [_resident((CONV_K, SSD_CONV_DIM)), _resident((1, SSD_CONV_DIM)),
                     _resident((CONV_K, GDN_CONV_DIM))]
        args += [w["scw"], w["scb"], w["gcw"]]
        for wd in (SSD_CONV_DIM, GDN_CONV_DIM):
            out_specs.append(pl.BlockSpec((1, CONV_K - 1, wd), lambda i: (i // steps_per_group, 0, 0)))
            out_shape.append(jax.ShapeDtypeStruct((nseq, CONV_K - 1, wd), F32))
            scratch.append(pltpu.VMEM((tm, wd), F32))
        scratch += [pltpu.VMEM(((CONV_K - 1) * SUBLANES, wd), F32) for wd in (SSD_CONV_DIM, GDN_CONV_DIM)]
        scratch += [pltpu.VMEM((D_MODEL // LANES, tm, LANES), F32), pltpu.VMEM((tm, D_MODEL), BF16),
                    pltpu.VMEM((PROJ_COLS // LANES, tm, LANES), F32)]
    return pl.pallas_call(
        functools.partial(_inproj_kernel, tm=tm, steps_per_seq=steps_per_group, do_conv=do_conv),
        grid=(T // tm,),
        in_specs=in_specs,
        out_specs=out_specs,
        out_shape=out_shape,
        scratch_shapes=scratch,
        compiler_params=pltpu.CompilerParams(
            dimension_semantics=("arbitrary",), vmem_limit_bytes=VMEM_LIMIT),
        name="in_proj",
    )(*args)


def _outproj_kernel(x_ref, mix_ref, gate_ref, w_ref, fw_ref, o_ref, *, final_norm):
    y = jnp.dot(mix_ref[...], w_ref[...], preferred_element_type=F32)
    out = x_ref[...] + gate_ref[0] * y
    if final_norm:
        ms = jnp.mean(out * out, axis=-1, keepdims=True)
        out = out * lax.rsqrt(ms + NORM_EPS) * fw_ref[...]
    o_ref[...] = out


def _outproj_call(x2d, mix2d, gate, w_out, final_w, tm, per_row, steps_per_group, final_norm):
    T = x2d.shape[0]
    return pl.pallas_call(
        functools.partial(_outproj_kernel, final_norm=final_norm),
        grid=(T // tm,),
        in_specs=[
            pl.BlockSpec((tm, D_MODEL), lambda i: (i, 0)),
            pl.BlockSpec((tm, MIX_WIDTH), lambda i: (i, 0)),
            _row_mod_spec(per_row, tm, steps_per_group),
            _resident((MIX_WIDTH, D_MODEL)),
            _resident((1, D_MODEL)),
        ],
        out_specs=pl.BlockSpec((tm, D_MODEL), lambda i: (i, 0)),
        out_shape=jax.ShapeDtypeStruct((T, D_MODEL), F32),
        compiler_params=pltpu.CompilerParams(
            dimension_semantics=("arbitrary",), vmem_limit_bytes=VMEM_LIMIT),
        name="out_proj",
    )(x2d, mix2d, gate, w_out, final_w)


def _run_interleaved(*gens):
    live = [g for g in gens if g is not None]
    while live:
        for g in list(live):
            try:
                next(g)
            except StopIteration:
                live.remove(g)


def _chain_gens(gens):
    for g in gens:
        yield from g


def _unit_lower_solve_many(m_list, rhs_list, q, out):
    blk = min(INV_BLOCK, q)
    nb = q // blk
    if nb > 1:
        r = lax.broadcasted_iota(jnp.int32, (q, q), 0) // blk
        c = lax.broadcasted_iota(jnp.int32, (q, q), 1) // blk
        same = r == c
        d_list = [jnp.where(same, m, 0.0) for m in m_list]
        o_list = [m - d for m, d in zip(m_list, d_list)]
    else:
        d_list = m_list
    n_list = [-d for d in d_list]
    e_list = d_list
    for _ in range(int(math.log2(blk)) - 1):
        e_list = [_bdot(e, e) for e in e_list]
        yield
        n_list = [n + e + _bdot(n, e) for n, e in zip(n_list, e_list)]
        yield
    y_list = [rhs + _bdot(n, rhs) for n, rhs in zip(n_list, rhs_list)]
    if nb > 1:
        p_list = [o + _bdot(n, o) for n, o in zip(n_list, o_list)]
        yield
        r_list = [-p for p in p_list]
        e_list = p_list
        for _ in range(int(math.ceil(math.log2(nb))) - 1):
            e_list = [_bdot(e, e) for e in e_list]
            yield
            r_list = [rr + e + _bdot(rr, e) for rr, e in zip(r_list, e_list)]
            yield
        y_list = [y + _bdot(rr, y) for rr, y in zip(r_list, y_list)]
    yield
    out.extend(y_list)


def _mixer_kernel(*refs, names, q, nq, lv, nc, multi_seq, zero_init, do_conv, chains):
    r = dict(zip(names, refs))
    zs_ref, xbc_ref, zg_ref, qkv_ref, sm_ref = r["zs"], r["xbc"], r["zg"], r["qkv"], r["sm"]
    bias_ref, coef_ref, dexp_ref, snw_ref, gnw_ref = r["bias"], r["coef"], r["dexp"], r["snw"], r["gnw"]
    mix_ref, nhs_ref, nsg_ref = r["mix"], r["nhs"], r["nsg"]
    st_s, st_g, y_s, pe_s, xcd_s, bgt_s = r["st_s"], r["st_g"], r["y_s"], r["pe_s"], r["xcd_s"], r["bgt_s"]
    wq_s, u_s, at_s, kdt_s = r["wq_s"], r["u_s"], r["at_s"], r["kdt_s"]

    c_idx = pl.program_id(1)
    rr_ = nq * q
    n_slots = nq if multi_seq else 1
    slot = (lambda c: c) if multi_seq else (lambda c: 0)
    hist = q + SUBLANES
    pair = 2 * SSD_HEAD_DIM
    heads_per_group = SSD_HEADS // SSD_GROUPS
    gw = SSD_WIDTH // SSD_GROUPS
    assert not (do_conv and not multi_seq and nc > 1 and lv != rr_)

    @pl.when(c_idx == 0)
    def _load_state():
        if zero_init:
            st_s[...] = jnp.zeros(st_s.shape, F32)
            st_g[...] = jnp.zeros(st_g.shape, F32)
        else:
            for s in range(n_slots):
                for j in range(SSD_HEADS // 2):
                    blk = r["hs"][0, s, 2 * j:2 * j + 2].reshape(pair, SSD_STATE)
                    st_s[s, :, pair * j:pair * (j + 1)] = blk.T
            st_g[...] = r["sg"][0]
        if do_conv:
            for xp, key, width in ((r["xps"], "cs", SSD_CONV_DIM), (r["xpg"], "cg", GDN_CONV_DIM)):
                for s in range(n_slots):
                    xp[s * hist:s * hist + SUBLANES, :] = jnp.zeros((SUBLANES, width), F32)
                    if not zero_init:
                        xp[s * hist + SUBLANES - (CONV_K - 1):s * hist + SUBLANES, :] = r[key][0, s]

    if do_conv:
        for x_ref, xp, w_ref, b_ref, out, nst, width in (
                (xbc_ref, r["xps"], r["scw"], r["scb"], r["xbc_c"], r["ncs"], SSD_CONV_DIM),
                (qkv_ref, r["xpg"], r["gcw"], None, r["qkv_c"], r["ncg"], GDN_CONV_DIM)):
            if multi_seq:
                for c in range(nq):
                    base = c * hist + SUBLANES
                    xp[base:base + lv, :] = x_ref[c * lv:(c + 1) * lv, :]
                    xp[base + lv:base + q, :] = jnp.zeros((q - lv, width), F32)
                    _conv_silu_rows(xp, w_ref, b_ref, out, q, 0, width, xp_row0=c * hist, out_row0=c * q)
                    nst[0, c] = xp[base + lv - (CONV_K - 1):base + lv, :]
            else:
                xp[SUBLANES:SUBLANES + rr_, :] = x_ref[...]
                _conv_silu_rows(xp, w_ref, b_ref, out, rr_, 0, width)

                @pl.when(c_idx == nc - 1)
                def _store_conv_state():
                    nst[0, 0] = xp[SUBLANES + rr_ - (CONV_K - 1):SUBLANES + rr_, :]

                if nc > 1:
                    xp[0:SUBLANES, :] = xp[rr_:rr_ + SUBLANES, :]
        xbc_c, qkv_c = r["xbc_c"], r["qkv_c"]
        gate = _silu
    else:
        xbc_c, qkv_c = xbc_ref, qkv_ref
        gate = lambda z: z.astype(F32)

    if multi_seq:
        for src, dst, wd in ((zs_ref, r["pzs"], SSD_WIDTH), (zg_ref, r["pzg"], GDN_WIDTH),
                             (sm_ref, r["psm"], SMALL)):
            for c in range(nq):
                dst[c * q:c * q + lv, :] = src[c * lv:(c + 1) * lv, :].astype(F32)
                dst[c * q + lv:(c + 1) * q, :] = jnp.zeros((q - lv, wd), F32)
        zs_v, zg_v, sm = r["pzs"], r["pzg"], r["psm"][...]
    else:
        zs_v, zg_v, sm = zs_ref, zg_ref, sm_ref[...]

    sp = _softplus(sm + bias_ref[...])
    beta_all = jax.nn.sigmoid(sm)
    if multi_seq and lv < q:
        assert q & (q - 1) == 0
        valid = jnp.bitwise_and(lax.broadcasted_iota(jnp.int32, (rr_, SMALL), 0), q - 1) < lv
        sp = jnp.where(valid, sp, 0.0)
        beta_all = jnp.where(valid, beta_all, 0.0)
    ag = sp * coef_ref[...]
    r_all = lax.broadcasted_iota(jnp.int32, (rr_, rr_), 0)
    c_all = lax.broadcasted_iota(jnp.int32, (rr_, rr_), 1)
    chunk_start = (r_all // q) * q
    tril_bd = jnp.where(r_all >= c_all, jnp.where(c_all >= chunk_start, 1.0, 0.0), 0.0).astype(BF16)
    rq = lax.broadcasted_iota(jnp.int32, (q, q), 0)
    cq = lax.broadcasted_iota(jnp.int32, (q, q), 1)
    incl = rq >= cq
    strict = rq > cq
    r128 = lax.broadcasted_iota(jnp.int32, (LANES, LANES), 0)
    c128 = lax.broadcasted_iota(jnp.int32, (LANES, LANES), 1)
    eye = jnp.where(r128 == c128, 1.0, 0.0).astype(BF16)
    cum = _const_dot(tril_bd, ag, 3)
    cum_t = _transpose_via_identity(eye, cum, 3)
    sp_t = _transpose_via_identity(eye, sp, 3)
    ecum = jnp.exp(cum)
    if nq > 1:
        cum_last = jnp.concatenate(
            [jnp.broadcast_to(cum[(c + 1) * q - 1:(c + 1) * q, :], (q, SMALL)) for c in range(nq)], axis=0)
    else:
        cum_last = cum[q - 1:q, :]
    to_end = jnp.exp(cum_last - cum)
    er = lax.broadcasted_iota(jnp.int32, (SMALL, SSD_WIDTH), 0)
    ec = lax.broadcasted_iota(jnp.int32, (SMALL, SSD_WIDTH), 1) // SSD_HEAD_DIM
    expand = jnp.where(er == ec, 1.0, 0.0).astype(BF16)
    pe_s[...] = _dot_const(ecum, expand, 2)
    p_dtend = _dot_const(sp * to_end, expand, 2)
    xcd_s[...] = (xbc_c[:, 0:SSD_WIDTH].astype(F32) * p_dtend).astype(BF16)

    lane = lax.broadcasted_iota(jnp.int32, (q, pair), 1)
    lo_half = lane < SSD_HEAD_DIM
    scores = {}
    for c in range(nq):
        rows = slice(c * q, (c + 1) * q)
        for g in range(SSD_GROUPS):
            b_g = xbc_c[rows, SSD_WIDTH + g * SSD_STATE:SSD_WIDTH + (g + 1) * SSD_STATE]
            c_g = xbc_c[rows, SSD_WIDTH + (SSD_GROUPS + g) * SSD_STATE:
                        SSD_WIDTH + (SSD_GROUPS + g + 1) * SSD_STATE]
            scores[c, g] = _bdot_nt(c_g, b_g)
            bgt_s[c * SSD_GROUPS + g] = _transpose_via_identity(eye, b_g, 1).astype(BF16)
    def ssd_diag(items):
        for c in items:
            rows = slice(c * q, (c + 1) * q)
            for j in range(SSD_HEADS // 2):
                g = j // (heads_per_group // 2)
                pcols = slice(pair * j, pair * (j + 1))
                x_pair = xbc_c[rows, pcols].astype(F32)
                y_pair = x_pair * dexp_ref[:, pcols]
                for half in range(2):
                    h = 2 * j + half
                    seg = cum[rows, h:h + 1] - cum_t[h:h + 1, rows]
                    lmat = jnp.exp(jnp.where(incl, seg, -jnp.inf))
                    m_h = scores[c, g] * lmat * sp_t[h:h + 1, rows]
                    x_h = jnp.where(lo_half if half == 0 else jnp.logical_not(lo_half), x_pair, 0.0)
                    y_pair = y_pair + _bdot(m_h, x_h)
                y_s[rows, pcols] = y_pair
                if j % 2 == 1:
                    yield

    def gdn_independent(group):
        m_list, rhs_list = [], []
        for n, (c, h) in enumerate(group):
            rows = slice(c * q, (c + 1) * q)
            i = c * GDN_HEADS + h
            qh = qkv_c[rows, h * GDN_HEAD_DIM:(h + 1) * GDN_HEAD_DIM].astype(F32)
            kh = qkv_c[rows, GDN_WIDTH + h * GDN_HEAD_DIM:GDN_WIDTH + (h + 1) * GDN_HEAD_DIM].astype(F32)
            vh = qkv_c[rows, 2 * GDN_WIDTH + h * GDN_HEAD_DIM:
                       2 * GDN_WIDTH + (h + 1) * GDN_HEAD_DIM].astype(F32)
            if do_conv:
                qh = qh * lax.rsqrt(jnp.sum(qh * qh, axis=-1, keepdims=True) + NORM_EPS) * (GDN_HEAD_DIM ** -0.5)
                kh = kh * lax.rsqrt(jnp.sum(kh * kh, axis=-1, keepdims=True) + NORM_EPS)
            la = _SM_A + h
            bcol = beta_all[rows, _SM_B + h:_SM_B + h + 1]
            eg = ecum[rows, la:la + 1]
            kb = kh * bcol
            dec = jnp.exp(jnp.where(incl, cum[rows, la:la + 1] - cum_t[la:la + 1, rows], -jnp.inf))
            kq = _bdot_nt(jnp.concatenate([kb, qh], axis=0), kh)
            m_list.append(jnp.where(strict, kq[0:q] * dec, 0.0))
            at_s[i] = (kq[q:2 * q] * dec).astype(BF16)
            rhs_list.append(jnp.concatenate([vh * bcol, kb * eg], axis=1))
            wq_s[i, q:2 * q, :] = (qh * eg).astype(BF16)
            kdt_s[i] = _transpose_via_identity(eye, kh * to_end[rows, la:la + 1], 1).astype(BF16)
            if n % 4 == 3:
                yield
        sol_list = []
        yield from _unit_lower_solve_many(m_list, rhs_list, q, sol_list)
        for (c, h), sol in zip(group, sol_list):
            i = c * GDN_HEADS + h
            u_s[i] = sol[:, 0:GDN_HEAD_DIM]
            wq_s[i, 0:q, :] = sol[:, GDN_HEAD_DIM:2 * GDN_HEAD_DIM].astype(BF16)
        yield

    def store_mix(c, cols, val):
        if multi_seq:
            mix_ref[c * lv:(c + 1) * lv, cols] = val[0:lv].astype(mix_ref.dtype)
        else:
            mix_ref[c * q:(c + 1) * q, cols] = val.astype(mix_ref.dtype)

    def recurrences(items):
        for c in items:
            rows = slice(c * q, (c + 1) * q)
            last = (c + 1) * q - 1
            s = slot(c)
            for g in range(SSD_GROUPS):
                gcols = slice(g * gw, (g + 1) * gw)
                c_g = xbc_c[rows, SSD_WIDTH + (SSD_GROUPS + g) * SSD_STATE:
                            SSD_WIDTH + (SSD_GROUPS + g + 1) * SSD_STATE]
                state = st_s[s, :, gcols]
                y_g = y_s[rows, gcols] + _bdot(c_g, state) * pe_s[rows, gcols]
                st_s[s, :, gcols] = state * pe_s[last:last + 1, gcols] + jnp.dot(
                    bgt_s[c * SSD_GROUPS + g], xcd_s[rows, gcols], preferred_element_type=F32)
                y_g = y_g * gate(zs_v[rows, gcols])
                ms = jnp.mean(y_g * y_g, axis=-1, keepdims=True)
                y_g = y_g * lax.rsqrt(ms + NORM_EPS) * snw_ref[:, gcols]
                store_mix(c, gcols, y_g)
            yield
        pairs = [(c, h) for c in items for h in range(GDN_HEADS)]
        ws = {}
        for (c, h) in pairs:
            ws[c, h] = jnp.dot(wq_s[c * GDN_HEADS + h], st_g[slot(c), h].astype(BF16),
                               preferred_element_type=F32)
        yield
        vn = {(c, h): (u_s[c * GDN_HEADS + h] - ws[c, h][0:q]).astype(BF16) for (c, h) in pairs}
        for (c, h) in pairs:
            i = c * GDN_HEADS + h
            la = _SM_A + h
            last = (c + 1) * q - 1
            s = slot(c)
            o = ws[c, h][q:2 * q] + jnp.dot(at_s[i], vn[c, h], preferred_element_type=F32)
            st_g[s, h] = st_g[s, h] * ecum[last:last + 1, la:la + 1] + jnp.dot(
                kdt_s[i], vn[c, h], preferred_element_type=F32)
            hc = slice(h * GDN_HEAD_DIM, (h + 1) * GDN_HEAD_DIM)
            ms = jnp.mean(o * o, axis=-1, keepdims=True)
            o = o * lax.rsqrt(ms + NORM_EPS) * gnw_ref[...] * gate(zg_v[c * q:(c + 1) * q, hc])
            store_mix(c, slice(SSD_WIDTH + h * GDN_HEAD_DIM, SSD_WIDTH + (h + 1) * GDN_HEAD_DIM), o)
        yield

    all_ch = [(c, h) for c in range(nq) for h in range(GDN_HEADS)]
    groups = [all_ch[i:i + chains] for i in range(0, len(all_ch), chains)]
    batches = [list(range(nq))] if multi_seq else [[c] for c in range(nq)]
    assert chains % GDN_HEADS == 0
    pending = list(batches)
    done_items = set()
    if multi_seq:
        for _ in ssd_diag(range(nq)):
            pass
    for grp in groups:
        runnable = [b for b in pending if set(b) <= done_items]
        for b in runnable:
            pending.remove(b)
        others = [_chain_gens([recurrences(b) for b in runnable])] if runnable else []
        if not multi_seq:
            others.append(ssd_diag(sorted({c for c, _ in grp})))
        _run_interleaved(gdn_independent(grp), *others)
        done_items |= {c for c, _ in grp}
    for b in pending:
        for _ in recurrences(b):
            pass

    @pl.when(c_idx == nc - 1)
    def _store_state():
        for s in range(n_slots):
            for j in range(SSD_HEADS // 2):
                blk = st_s[s, :, pair * j:pair * (j + 1)].T
                nhs_ref[0, s, 2 * j:2 * j + 2] = blk.reshape(2, SSD_HEAD_DIM, SSD_STATE)
        nsg_ref[0] = st_g[...]


_STATE_TAILS = (
    (CONV_K - 1, SSD_CONV_DIM),
    (SSD_HEADS, SSD_HEAD_DIM, SSD_STATE),
    (CONV_K - 1, GDN_CONV_DIM),
    (GDN_HEADS, GDN_HEAD_DIM, GDN_HEAD_DIM),
)


_STATE_NAMES = ("cs", "hs", "cg", "sg")


def _mixer_call(zs, xbc, zg, qkv, sm, B, L, states_in, prev_out, p, q, nq, multi_seq, layer, chains, do_conv):
    assert zs.shape[0] == B * L
    rows = nq * q
    assert q % min(INV_BLOCK, q) == 0 and q & (q - 1) == 0
    zero_init = states_in is None
    if multi_seq:
        assert CONV_K - 1 <= L <= q and B % nq == 0 and do_conv
        nc, lv, bb = 1, L, nq
        tok = lambda wd: pl.BlockSpec((nq * L, wd), lambda b, c: (b, 0))
    else:
        assert L % rows == 0
        nc, lv, bb = L // rows, rows, 1
        tok = lambda wd: pl.BlockSpec((rows, wd), lambda b, c: (b * nc + c, 0))
    n_slots = bb
    per_b = lambda shape: pl.BlockSpec((1, bb) + shape, lambda b, c: (layer, b) + (0,) * len(shape))
    const = lambda shape: pl.BlockSpec(shape, lambda b, c: (0,) * len(shape))
    nch = nq * GDN_HEADS
    kept = [k for k in range(4) if do_conv or k in (1, 3)]

    names = ["zs", "xbc", "zg", "qkv", "sm"]
    in_specs = [tok(SSD_WIDTH), tok(SSD_CONV_DIM), tok(GDN_WIDTH), tok(GDN_CONV_DIM), tok(SMALL)]
    args = [zs, xbc, zg, qkv, sm]
    if not zero_init:
        for k in kept:
            names.append(_STATE_NAMES[k])
            in_specs.append(per_b(_STATE_TAILS[k]))
            args.append(states_in[k])
    consts = [("bias", (1, SMALL)), ("coef", (1, SMALL)), ("dexp", (1, SSD_WIDTH)), ("snw", (1, SSD_WIDTH)),
              ("gnw", (1, GDN_HEAD_DIM))]
    if do_conv:
        consts += [("scw", (CONV_K, SSD_CONV_DIM)), ("scb", (1, SSD_CONV_DIM)), ("gcw", (CONV_K, GDN_CONV_DIM))]
    for nm, shape in consts:
        names.append(nm)
        in_specs.append(const(shape))
        args.append(p[nm])
    aliases = {}
    if prev_out is not None:
        for k, a in enumerate(prev_out):
            names.append("alias%d" % k)
            aliases[len(args)] = 1 + k
            in_specs.append(pl.BlockSpec(memory_space=pl.ANY))
            args.append(a)
    names += ["mix"] + ["n" + _STATE_NAMES[k] for k in kept]
    out_specs = [tok(MIX_WIDTH)] + [per_b(_STATE_TAILS[k]) for k in kept]
    out_shape = [jax.ShapeDtypeStruct((B * L, MIX_WIDTH), BF16)] + [
        jax.ShapeDtypeStruct((DEPTH, B) + _STATE_TAILS[k], F32) for k in kept]
    scratch = [
        ("st_s", pltpu.VMEM((n_slots, SSD_STATE, SSD_WIDTH), F32)),
        ("st_g", pltpu.VMEM((n_slots, GDN_HEADS, GDN_HEAD_DIM, GDN_HEAD_DIM), F32)),
        ("y_s", pltpu.VMEM((rows, SSD_WIDTH), F32)),
        ("pe_s", pltpu.VMEM((rows, SSD_WIDTH), F32)),
        ("xcd_s", pltpu.VMEM((rows, SSD_WIDTH), BF16)),
        ("bgt_s", pltpu.VMEM((nq * SSD_GROUPS, SSD_STATE, q), BF16)),
        ("wq_s", pltpu.VMEM((nch, 2 * q, GDN_HEAD_DIM), BF16)),
        ("u_s", pltpu.VMEM((nch, q, GDN_HEAD_DIM), F32)),
        ("at_s", pltpu.VMEM((nch, q, q), BF16)),
        ("kdt_s", pltpu.VMEM((nch, GDN_HEAD_DIM, q), BF16)),
    ]
    if do_conv:
        xp_rows = nq * (q + SUBLANES) if multi_seq else rows + SUBLANES
        scratch += [
            ("xps", pltpu.VMEM((xp_rows, SSD_CONV_DIM), F32)),
            ("xpg", pltpu.VMEM((xp_rows, GDN_CONV_DIM), F32)),
            ("xbc_c", pltpu.VMEM((rows, SSD_CONV_DIM), F32)),
            ("qkv_c", pltpu.VMEM((rows, GDN_CONV_DIM), F32)),
        ]
    if multi_seq:
        scratch += [
            ("pzs", pltpu.VMEM((rows, SSD_WIDTH), F32)),
            ("pzg", pltpu.VMEM((rows, GDN_WIDTH), F32)),
            ("psm", pltpu.VMEM((rows, SMALL), F32)),
        ]
    names += [nm for nm, _ in scratch]
    return pl.pallas_call(
        functools.partial(_mixer_kernel, names=tuple(names), q=q, nq=nq, lv=lv, nc=nc, multi_seq=multi_seq,
                          zero_init=zero_init, do_conv=do_conv, chains=chains),
        grid=(B // bb, nc),
        in_specs=in_specs,
        out_specs=out_specs,
        out_shape=out_shape,
        input_output_aliases=aliases,
        scratch_shapes=[s for _, s in scratch],
        compiler_params=pltpu.CompilerParams(
            dimension_semantics=("arbitrary", "arbitrary"), vmem_limit_bytes=VMEM_LIMIT),
        name="mixer",
    )(*args)


def _prep_layer(l, norm_w, w_in, ssd_conv_w, ssd_conv_b, ssd_dt_bias, ssd_a_log, ssd_d, ssd_norm_w,
                gdn_conv_w, gdn_dt_bias, gdn_a_log, gdn_norm_w, w_out):
    wi = w_in[l]
    small = jnp.concatenate(
        [wi[:, _OFF_DT:_OFF_DT + SSD_HEADS], wi[:, _OFF_A:_OFF_A + GDN_HEADS], wi[:, _OFF_B:_OFF_B + GDN_HEADS],
         jnp.zeros((D_MODEL, SMALL - SSD_HEADS - 2 * GDN_HEADS), F32)], axis=1)
    sm_hi = small.astype(BF16)
    sm_lo = (small - sm_hi.astype(F32)).astype(BF16)
    pad = jnp.zeros((SMALL - SSD_HEADS - GDN_HEADS,), F32)
    return {
        "norm_w": norm_w[l].reshape(1, D_MODEL),
        "zs": wi[:, _OFF_ZS:_OFF_ZS + SSD_WIDTH].astype(BF16),
        "xbc": wi[:, _OFF_XBC:_OFF_XBC + SSD_CONV_DIM].astype(BF16),
        "zg": wi[:, _OFF_ZG:_OFF_ZG + GDN_WIDTH].astype(BF16),
        "qkv": wi[:, _OFF_QKV:_OFF_QKV + GDN_CONV_DIM].astype(BF16),
        "sm_hi": sm_hi,
        "sm_lo": sm_lo,
        "scw": ssd_conv_w[l],
        "scb": ssd_conv_b[l].reshape(1, SSD_CONV_DIM),
        "gcw": gdn_conv_w[l],
        "bias": jnp.concatenate([ssd_dt_bias[l], gdn_dt_bias[l], pad]).reshape(1, SMALL),
        "coef": jnp.concatenate([-jnp.exp(ssd_a_log[l]), -jnp.exp(gdn_a_log[l]), pad]).reshape(1, SMALL),
        "dexp": jnp.repeat(ssd_d[l], SSD_HEAD_DIM).reshape(1, SSD_WIDTH),
        "snw": ssd_norm_w[l].reshape(1, SSD_WIDTH),
        "gnw": gdn_norm_w[l].reshape(1, GDN_HEAD_DIM),
        "w_out": w_out[l].astype(BF16),
    }


def _trunk(x, mod, states, params, final_w, q, nq, tm, chains, act_dtype):
    B, L, _ = x.shape
    T = B * L
    per_row = L < SUBLANES
    steps_per_group = max(L // tm, 1)
    x2d = x.reshape(T, D_MODEL)
    conv_in_proj = states is None and not per_row
    new_states = None
    conv_states = ([], [])
    for l in range(DEPTH):
        p = params[l]
        shift, scale, gate = (mod[l, :, i * D_MODEL:(i + 1) * D_MODEL] for i in range(3))
        if per_row:
            rep = lambda a: jnp.repeat(a, L, axis=0).reshape(1, T, D_MODEL)
        else:
            rep = lambda a: a.reshape(B, 1, D_MODEL)
        zs, xbc, zg, qkv, sm, *cst = _inproj_call(x2d, rep(scale), rep(shift), p["norm_w"], p, tm, per_row,
                                                  steps_per_group, conv_in_proj, act_dtype)
        mix, *new_states = _mixer_call(zs, xbc, zg, qkv, sm, B, L, states, new_states, p,
                                       q, nq, per_row, l, chains, not conv_in_proj)
        x2d = _outproj_call(x2d, mix, rep(gate), p["w_out"], final_w, tm, per_row,
                            steps_per_group, final_norm=(l == DEPTH - 1))
        for lst, s in zip(conv_states, cst):
            lst.append(s)
    if conv_in_proj:
        new_states = [jnp.stack(conv_states[0]), new_states[0], jnp.stack(conv_states[1]), new_states[1]]
    return x2d.reshape(B, L, D_MODEL), new_states


def kernel(x_prompt, x_sample, state_ssd_conv, state_ssm, state_gdn_conv, state_gdn, c_prompt, c_sample,
           norm_w, w_ada, b_ada, w_in, ssd_conv_w, ssd_conv_b, ssd_dt_bias, ssd_a_log, ssd_d, ssd_norm_w,
           gdn_conv_w, gdn_dt_bias, gdn_a_log, gdn_norm_w, w_out, final_norm_w):
    bp = x_prompt.shape[0]
    params = [_prep_layer(l, norm_w, w_in, ssd_conv_w, ssd_conv_b, ssd_dt_bias, ssd_a_log, ssd_d, ssd_norm_w,
                          gdn_conv_w, gdn_dt_bias, gdn_a_log, gdn_norm_w, w_out) for l in range(DEPTH)]
    final_w = final_norm_w.reshape(1, D_MODEL)
    mod = _ada_call(jnp.concatenate([c_prompt, c_sample], axis=0), w_ada, b_ada)
    y_p, sp = _trunk(x_prompt, mod[:, :bp], None, params, final_w, q=64, nq=4, tm=512, chains=16,
                     act_dtype=BF16)
    y_s, ss = _trunk(x_sample, mod[:, bp:], (state_ssd_conv, state_ssm, state_gdn_conv, state_gdn),
                     params, final_w, q=8, nq=8, tm=x_sample.shape[0] * x_sample.shape[1], chains=16,
                     act_dtype=F32)
    return (y_p, y_s, sp[0], sp[1], sp[2], sp[3], ss[0], ss[1], ss[2], ss[3])
```

```python
import functools
import math

import jax
import jax.numpy as jnp
from jax import lax
from jax.experimental import pallas as pl
from jax.experimental.pallas import tpu as pltpu

F32 = jnp.float32
BF16 = jnp.bfloat16

D_MODEL = 1024
DEPTH = 2
SSD_WIDTH = 1024
SSD_HEAD_DIM = 64
SSD_HEADS = 16
SSD_GROUPS = 2
SSD_STATE = 128
SSD_CONV_DIM = SSD_WIDTH + 2 * SSD_GROUPS * SSD_STATE
GDN_WIDTH = 1024
GDN_HEAD_DIM = 128
GDN_HEADS = 8
GDN_CONV_DIM = 3 * GDN_WIDTH
MIX_WIDTH = SSD_WIDTH + GDN_WIDTH
CONV_K = 4
NORM_EPS = 1e-6
ADA_DIM = 3 * D_MODEL
SMALL = 128
LANES = 128
SUBLANES = 8
INV_BLOCK = 16
VMEM_LIMIT = 56 * 1024 * 1024

_OFF_ZS = 0
_OFF_XBC = _OFF_ZS + SSD_WIDTH
_OFF_DT = _OFF_XBC + SSD_CONV_DIM
_OFF_ZG = _OFF_DT + SSD_HEADS
_OFF_QKV = _OFF_ZG + GDN_WIDTH
_OFF_A = _OFF_QKV + GDN_CONV_DIM
_OFF_B = _OFF_A + GDN_HEADS
_SM_DT = 0
_SM_A = SSD_HEADS
_SM_B = SSD_HEADS + GDN_HEADS


def _bdot(a, b):
    return jnp.dot(a.astype(BF16), b.astype(BF16), preferred_element_type=F32)


def _bdot_nt(a, b):
    return lax.dot_general(a.astype(BF16), b.astype(BF16), (((1,), (1,)), ((), ())),
                           preferred_element_type=F32)


def _terms(x, n):
    out = []
    r = x
    for i in range(n):
        t = r.astype(BF16)
        out.append(t)
        if i + 1 < n:
            r = r - t.astype(F32)
    return out


def _const_dot(c, x, n):
    acc = None
    for t in _terms(x, n):
        p = jnp.dot(c, t, preferred_element_type=F32)
        acc = p if acc is None else acc + p
    return acc


def _dot_const(x, c, n):
    acc = None
    for t in _terms(x, n):
        p = jnp.dot(t, c, preferred_element_type=F32)
        acc = p if acc is None else acc + p
    return acc


def _transpose_via_identity(eye, x, n):
    acc = None
    for t in _terms(x, n):
        p = lax.dot_general(eye, t, (((1,), (1,)), ((), ())), preferred_element_type=F32)
        acc = p if acc is None else acc + p
    return acc


def _softplus(x):
    return jnp.maximum(x, 0.0) + jnp.log1p(jnp.exp(-jnp.abs(x)))


def _silu(x):
    return x * jax.nn.sigmoid(x)


def _ada_kernel(c_ref, w_ref, b_ref, o_ref):
    s = _silu(c_ref[...])
    w = w_ref[0]
    s_hi, s_lo = _terms(s, 2)
    w_hi, w_lo = _terms(w, 2)
    acc = jnp.dot(s_hi, w_hi, preferred_element_type=F32)
    acc = acc + (jnp.dot(s_hi, w_lo, preferred_element_type=F32)
                 + jnp.dot(s_lo, w_hi, preferred_element_type=F32))
    o_ref[0] = acc + b_ref[0]


def _ada_call(c_all, w_ada, b_ada):
    rows = c_all.shape[0]
    tn = ADA_DIM // 2
    return pl.pallas_call(
        _ada_kernel,
        grid=(DEPTH, ADA_DIM // tn),
        in_specs=[
            pl.BlockSpec((rows, D_MODEL), lambda l, j: (0, 0)),
            pl.BlockSpec((1, D_MODEL, tn), lambda l, j: (l, 0, j)),
            pl.BlockSpec((1, 1, tn), lambda l, j: (l, 0, j)),
        ],
        out_specs=pl.BlockSpec((1, rows, tn), lambda l, j: (l, 0, j)),
        out_shape=jax.ShapeDtypeStruct((DEPTH, rows, ADA_DIM), F32),
        compiler_params=pltpu.CompilerParams(
            dimension_semantics=("arbitrary", "arbitrary"), vmem_limit_bytes=VMEM_LIMIT),
        name="ada_mod",
    )(c_all, w_ada, b_ada.reshape(DEPTH, 1, ADA_DIM))


CONV_ROWS = 64
CONV_COLS = 128
PROJ_COLS = 512


def _conv_silu_rows(xp, w_ref, b_ref, out, rows, c0, c1, xp_row0=0, out_row0=0):
    rows_per = min(rows, CONV_ROWS)
    for i in range(rows // rows_per):
        r0 = i * rows_per
        for j in range((c1 - c0) // CONV_COLS):
            cols = slice(c0 + j * CONV_COLS, c0 + (j + 1) * CONV_COLS)
            x = xp[xp_row0 + r0:xp_row0 + r0 + rows_per + SUBLANES, cols]
            acc = x * w_ref[0:1, cols]
            for k in range(1, CONV_K):
                acc = pltpu.roll(acc, 1, 0) + x * w_ref[k:k + 1, cols]
            if b_ref is not None:
                acc = acc + b_ref[:, cols]
            out[out_row0 + r0:out_row0 + r0 + rows_per, cols] = _silu(acc[SUBLANES:, :]).astype(out.dtype)


PERM_CHUNK = SUBLANES * SUBLANES


def _qk_l2_scale(col):
    assert GDN_HEAD_DIM == LANES
    if col < GDN_WIDTH:
        return GDN_HEAD_DIM ** -0.5
    if col < 2 * GDN_WIDTH:
        return 1.0
    return None


def _conv_silu_permuted(xp, w_ref, b_ref, hist, up, out, rows, c0, l2_scale=None):
    sub0 = lax.broadcasted_iota(jnp.int32, (SUBLANES, LANES), 0) == 0
    taps = CONV_K - 1
    for j in range(PROJ_COLS // LANES):
        cols = slice(c0 + j * LANES, c0 + (j + 1) * LANES)
        w = [jnp.broadcast_to(w_ref[k:k + 1, cols], (SUBLANES, LANES)) for k in range(CONV_K)]
        b = None if b_ref is None else jnp.broadcast_to(b_ref[:, cols], (SUBLANES, LANES))
        prev = [hist[i * SUBLANES:(i + 1) * SUBLANES, cols] for i in range(taps)]
        for k in range(rows // PERM_CHUNK):
            base = k * PERM_CHUNK
            xs = [xp[base + a * SUBLANES:base + (a + 1) * SUBLANES, cols] for a in range(SUBLANES)]
            rolled = [pltpu.roll(xs[SUBLANES - taps + i], 1, 0) for i in range(taps)]
            ext = [jnp.where(sub0, prev[i], rolled[i]) for i in range(taps)] + xs
            for a in range(SUBLANES):
                y = ext[a] * w[0]
                for t in range(1, CONV_K):
                    y = y + ext[a + t] * w[t]
                if b is not None:
                    y = y + b
                y = _silu(y)
                scale = None if l2_scale is None else l2_scale(c0 + j * LANES)
                if scale is not None:
                    y = y * (lax.rsqrt(jnp.sum(y * y, axis=-1, keepdims=True) + NORM_EPS) * scale)
                up[j, pl.ds(base + a, SUBLANES, stride=SUBLANES), :] = y
            prev = rolled
        for i in range(taps):
            hist[i * SUBLANES:(i + 1) * SUBLANES, cols] = prev[i]
        out[:, cols] = up[j].astype(out.dtype)


def _inproj_kernel(*refs, tm, steps_per_seq, do_conv):
    x_ref, sc_ref, sh_ref, nw_ref, wzs_ref, wxbc_ref, wzg_ref, wqkv_ref, wsh_ref, wsl_ref = refs[:10]
    if do_conv:
        scw_ref, scb_ref, gcw_ref = refs[10:13]
        (zs_ref, xbc_ref, zg_ref, qkv_ref, sm_ref, ncs_ref, ncg_ref,
         xps, xpg, hist_s, hist_g, hs, hp, up) = refs[13:]
    else:
        zs_ref, xbc_ref, zg_ref, qkv_ref, sm_ref = refs[10:]
    x = x_ref[...]
    ms = jnp.mean(x * x, axis=-1, keepdims=True)
    h = x * lax.rsqrt(ms + NORM_EPS) * nw_ref[...]
    h = h * (1.0 + sc_ref[0]) + sh_ref[0]
    h_hi, h_lo = _terms(h, 2)
    sm = jnp.dot(h_hi, wsh_ref[...], preferred_element_type=F32)
    sm = sm + (jnp.dot(h_hi, wsl_ref[...], preferred_element_type=F32)
               + jnp.dot(h_lo, wsh_ref[...], preferred_element_type=F32))
    sm_ref[...] = sm
    if not do_conv:
        zs_ref[...] = jnp.dot(h_hi, wzs_ref[...], preferred_element_type=F32)
        xbc_ref[...] = jnp.dot(h_hi, wxbc_ref[...], preferred_element_type=F32)
        zg_ref[...] = jnp.dot(h_hi, wzg_ref[...], preferred_element_type=F32)
        qkv_ref[...] = jnp.dot(h_hi, wqkv_ref[...], preferred_element_type=F32)
        return

    @pl.when(pl.program_id(0) % steps_per_seq == 0)
    def _zero_history():
        hist_s[...] = jnp.zeros(hist_s.shape, F32)
        hist_g[...] = jnp.zeros(hist_g.shape, F32)

    assert tm % PERM_CHUNK == 0
    ncb = D_MODEL // LANES
    for cb in range(ncb):
        hs[cb] = h[:, cb * LANES:(cb + 1) * LANES]
    for g in range(0, tm // SUBLANES, 2):
        parts = [jnp.concatenate(
            [hs[cb, pl.ds(PERM_CHUNK * (gg // SUBLANES) + gg % SUBLANES, SUBLANES, stride=SUBLANES), :]
             for cb in range(ncb)], axis=1) for gg in (g, g + 1)]
        hp[g * SUBLANES:(g + 2) * SUBLANES, :] = jnp.concatenate(parts, axis=0).astype(BF16)

    conv_blocks = [(xp, w_ref, cw_ref, cb_ref, hist, out, c0)
                   for xp, w_ref, cw_ref, cb_ref, hist, out, width in (
                       (xps, wxbc_ref, scw_ref, scb_ref, hist_s, xbc_ref, SSD_CONV_DIM),
                       (xpg, wqkv_ref, gcw_ref, None, hist_g, qkv_ref, GDN_CONV_DIM))
                   for c0 in range(0, width, PROJ_COLS)]
    gate_blocks = [(w_ref, out, c0) for w_ref, out, width in ((wzs_ref, zs_ref, SSD_WIDTH),
                                                              (wzg_ref, zg_ref, GDN_WIDTH))
                   for c0 in range(0, width, PROJ_COLS)]

    def project(blk):
        xp, w_ref, _, _, _, _, c0 = blk
        cols = slice(c0, c0 + PROJ_COLS)
        xp[:, cols] = jnp.dot(hp[...], w_ref[:, cols], preferred_element_type=F32)

    def gate_block(i):
        w_ref, out, c0 = gate_blocks[i]
        cols = slice(c0, c0 + PROJ_COLS)
        out[:, cols] = _silu(jnp.dot(h_hi, w_ref[:, cols], preferred_element_type=F32)).astype(out.dtype)

    early_gates = len(gate_blocks) // 2
    for i in range(early_gates):
        gate_block(i)
    project(conv_blocks[0])
    for j, blk in enumerate(conv_blocks):
        if j + 1 < len(conv_blocks):
            project(conv_blocks[j + 1])
        if j % 2 == 0 and early_gates + j // 2 < len(gate_blocks):
            gate_block(early_gates + j // 2)
        xp, _, cw_ref, cb_ref, hist, out, c0 = blk
        _conv_silu_permuted(xp, cw_ref, cb_ref, hist, up, out, tm, c0,
                            l2_scale=_qk_l2_scale if xp is xpg else None)
    assert len(gate_blocks) - early_gates <= (len(conv_blocks) + 1) // 2
    for xp, nst in ((xps, ncs_ref), (xpg, ncg_ref)):
        for i in range(CONV_K - 1):
            row = tm - PERM_CHUNK + SUBLANES * (SUBLANES - (CONV_K - 1) + i) + SUBLANES - 1
            nst[0, i:i + 1, :] = xp[row:row + 1, :]


def _row_mod_spec(per_row, tm, steps_per_group):
    if per_row:
        return pl.BlockSpec((1, tm, D_MODEL), lambda i: (0, i, 0))
    return pl.BlockSpec((1, 1, D_MODEL), lambda i: (i // steps_per_group, 0, 0))


def _resident(shape):
    nd = len(shape)
    return pl.BlockSpec(shape, lambda i: (0,) * nd, pipeline_mode=pl.Buffered(1))


def _inproj_call(x2d, scale, shift, norm_w, w, tm, per_row, steps_per_group, do_conv, act_dtype):
    T = x2d.shape[0]
    assert T % tm == 0
    widths = (SSD_WIDTH, SSD_CONV_DIM, GDN_WIDTH, GDN_CONV_DIM, SMALL)
    dtypes = (act_dtype, act_dtype, act_dtype, act_dtype, F32)
    row = lambda wd: pl.BlockSpec((tm, wd), lambda i: (i, 0))
    in_specs = [
        row(D_MODEL),
        _row_mod_spec(per_row, tm, steps_per_group),
        _row_mod_spec(per_row, tm, steps_per_group),
        _resident((1, D_MODEL)),
        _resident((D_MODEL, SSD_WIDTH)),
        _resident((D_MODEL, SSD_CONV_DIM)),
        _resident((D_MODEL, GDN_WIDTH)),
        _resident((D_MODEL, GDN_CONV_DIM)),
        _resident((D_MODEL, SMALL)),
        _resident((D_MODEL, SMALL)),
    ]
    args = [x2d, scale, shift, norm_w, w["zs"], w["xbc"], w["zg"], w["qkv"], w["sm_hi"], w["sm_lo"]]
    out_specs = [row(wd) for wd in widths]
    out_shape = [jax.ShapeDtypeStruct((T, wd), dt) for wd, dt in zip(widths, dtypes)]
    scratch = []
    if do_conv:
        assert not per_row
        nseq = T // (tm * steps_per_group)
        in_specs += [_resident((CONV_K, SSD_CONV_DIM)), _resident((1, SSD_CONV_DIM)),
                     _resident((CONV_K, GDN_CONV_DIM))]
        args += [w["scw"], w["scb"], w["gcw"]]
        for wd in (SSD_CONV_DIM, GDN_CONV_DIM):
            out_specs.append(pl.BlockSpec((1, CONV_K - 1, wd), lambda i: (i // steps_per_group, 0, 0)))
            out_shape.append(jax.ShapeDtypeStruct((nseq, CONV_K - 1, wd), F32))
            scratch.append(pltpu.VMEM((tm, wd), F32))
        scratch += [pltpu.VMEM(((CONV_K - 1) * SUBLANES, wd), F32) for wd in (SSD_CONV_DIM, GDN_CONV_DIM)]
        scratch += [pltpu.VMEM((D_MODEL // LANES, tm, LANES), F32), pltpu.VMEM((tm, D_MODEL), BF16),
                    pltpu.VMEM((PROJ_COLS // LANES, tm, LANES), F32)]
    return pl.pallas_call(
        functools.partial(_inproj_kernel, tm=tm, steps_per_seq=steps_per_group, do_conv=do_conv),
        grid=(T // tm,),
        in_specs=in_specs,
        out_specs=out_specs,
        out_shape=out_shape,
        scratch_shapes=scratch,
        compiler_params=pltpu.CompilerParams(
            dimension_semantics=("arbitrary",), vmem_limit_bytes=VMEM_LIMIT),
        name="in_proj",
    )(*args)


def _outproj_kernel(x_ref, mix_ref, gate_ref, w_ref, fw_ref, o_ref, *, final_norm):
    y = jnp.dot(mix_ref[...], w_ref[...], preferred_element_type=F32)
    out = x_ref[...] + gate_ref[0] * y
    if final_norm:
        ms = jnp.mean(out * out, axis=-1, keepdims=True)
        out = out * lax.rsqrt(ms + NORM_EPS) * fw_ref[...]
    o_ref[...] = out


def _outproj_call(x2d, mix2d, gate, w_out, final_w, tm, per_row, steps_per_group, final_norm):
    T = x2d.shape[0]
    return pl.pallas_call(
        functools.partial(_outproj_kernel, final_norm=final_norm),
        grid=(T // tm,),
        in_specs=[
            pl.BlockSpec((tm, D_MODEL), lambda i: (i, 0)),
            pl.BlockSpec((tm, MIX_WIDTH), lambda i: (i, 0)),
            _row_mod_spec(per_row, tm, steps_per_group),
            _resident((MIX_WIDTH, D_MODEL)),
            _resident((1, D_MODEL)),
        ],
        out_specs=pl.BlockSpec((tm, D_MODEL), lambda i: (i, 0)),
        out_shape=jax.ShapeDtypeStruct((T, D_MODEL), F32),
        compiler_params=pltpu.CompilerParams(
            dimension_semantics=("arbitrary",), vmem_limit_bytes=VMEM_LIMIT),
        name="out_proj",
    )(x2d, mix2d, gate, w_out, final_w)


def _run_interleaved(*gens):
    live = [g for g in gens if g is not None]
    while live:
        for g in list(live):
            try:
                next(g)
            except StopIteration:
                live.remove(g)


def _chain_gens(gens):
    for g in gens:
        yield from g


def _unit_lower_solve_many(m_list, rhs_list, q, out):
    blk = min(INV_BLOCK, q)
    nb = q // blk
    if nb > 1:
        r = lax.broadcasted_iota(jnp.int32, (q, q), 0) // blk
        c = lax.broadcasted_iota(jnp.int32, (q, q), 1) // blk
        same = r == c
        d_list = [jnp.where(same, m, 0.0) for m in m_list]
        o_list = [m - d for m, d in zip(m_list, d_list)]
    else:
        d_list = m_list
    n_list = [-d for d in d_list]
    e_list = d_list
    for _ in range(int(math.log2(blk)) - 1):
        e_list = [_bdot(e, e) for e in e_list]
        yield
        n_list = [n + e + _bdot(n, e) for n, e in zip(n_list, e_list)]
        yield
    y_list = [rhs + _bdot(n, rhs) for n, rhs in zip(n_list, rhs_list)]
    if nb > 1:
        p_list = [o + _bdot(n, o) for n, o in zip(n_list, o_list)]
        yield
        r_list = [-p for p in p_list]
        e_list = p_list
        for _ in range(int(math.ceil(math.log2(nb))) - 1):
            e_list = [_bdot(e, e) for e in e_list]
            yield
            r_list = [rr + e + _bdot(rr, e) for rr, e in zip(r_list, e_list)]
            yield
        y_list = [y + _bdot(rr, y) for rr, y in zip(r_list, y_list)]
    yield
    out.extend(y_list)


def _mixer_kernel(*refs, names, q, nq, lv, nc, multi_seq, zero_init, do_conv, chains):
    r = dict(zip(names, refs))
    zs_ref, xbc_ref, zg_ref, qkv_ref, sm_ref = r["zs"], r["xbc"], r["zg"], r["qkv"], r["sm"]
    bias_ref, coef_ref, dexp_ref, snw_ref, gnw_ref = r["bias"], r["coef"], r["dexp"], r["snw"], r["gnw"]
    mix_ref, nhs_ref, nsg_ref = r["mix"], r["nhs"], r["nsg"]
    st_s, st_g, y_s, pe_s, xcd_s, bgt_s = r["st_s"], r["st_g"], r["y_s"], r["pe_s"], r["xcd_s"], r["bgt_s"]
    wq_s, u_s, at_s, kdt_s = r["wq_s"], r["u_s"], r["at_s"], r["kdt_s"]

    c_idx = pl.program_id(1)
    rr_ = nq * q
    n_slots = nq if multi_seq else 1
    slot = (lambda c: c) if multi_seq else (lambda c: 0)
    hist = q + SUBLANES
    pair = 2 * SSD_HEAD_DIM
    heads_per_group = SSD_HEADS // SSD_GROUPS
    gw = SSD_WIDTH // SSD_GROUPS
    assert not (do_conv and not multi_seq and nc > 1 and lv != rr_)

    @pl.when(c_idx == 0)
    def _load_state():
        if zero_init:
            st_s[...] = jnp.zeros(st_s.shape, F32)
            st_g[...] = jnp.zeros(st_g.shape, F32)
        else:
            for s in range(n_slots):
                for j in range(SSD_HEADS // 2):
                    blk = r["hs"][0, s, 2 * j:2 * j + 2].reshape(pair, SSD_STATE)
                    st_s[s, :, pair * j:pair * (j + 1)] = blk.T
            st_g[...] = r["sg"][0]
        if do_conv:
            for xp, key, width in ((r["xps"], "cs", SSD_CONV_DIM), (r["xpg"], "cg", GDN_CONV_DIM)):
                for s in range(n_slots):
                    xp[s * hist:s * hist + SUBLANES, :] = jnp.zeros((SUBLANES, width), F32)
                    if not zero_init:
                        xp[s * hist + SUBLANES - (CONV_K - 1):s * hist + SUBLANES, :] = r[key][0, s]

    if do_conv:
        for x_ref, xp, w_ref, b_ref, out, nst, width in (
                (xbc_ref, r["xps"], r["scw"], r["scb"], r["xbc_c"], r["ncs"], SSD_CONV_DIM),
                (qkv_ref, r["xpg"], r["gcw"], None, r["qkv_c"], r["ncg"], GDN_CONV_DIM)):
            if multi_seq:
                for c in range(nq):
                    base = c * hist + SUBLANES
                    xp[base:base + lv, :] = x_ref[c * lv:(c + 1) * lv, :]
                    xp[base + lv:base + q, :] = jnp.zeros((q - lv, width), F32)
                    _conv_silu_rows(xp, w_ref, b_ref, out, q, 0, width, xp_row0=c * hist, out_row0=c * q)
                    nst[0, c] = xp[base + lv - (CONV_K - 1):base + lv, :]
            else:
                xp[SUBLANES:SUBLANES + rr_, :] = x_ref[...]
                _conv_silu_rows(xp, w_ref, b_ref, out, rr_, 0, width)

                @pl.when(c_idx == nc - 1)
                def _store_conv_state():
                    nst[0, 0] = xp[SUBLANES + rr_ - (CONV_K - 1):SUBLANES + rr_, :]

                if nc > 1:
                    xp[0:SUBLANES, :] = xp[rr_:rr_ + SUBLANES, :]
        xbc_c, qkv_c = r["xbc_c"], r["qkv_c"]
        gate = _silu
    else:
        xbc_c, qkv_c = xbc_ref, qkv_ref
        gate = lambda z: z.astype(F32)

    if multi_seq:
        for src, dst, wd in ((zs_ref, r["pzs"], SSD_WIDTH), (zg_ref, r["pzg"], GDN_WIDTH),
                             (sm_ref, r["psm"], SMALL)):
            for c in range(nq):
                dst[c * q:c * q + lv, :] = src[c * lv:(c + 1) * lv, :].astype(F32)
                dst[c * q + lv:(c + 1) * q, :] = jnp.zeros((q - lv, wd), F32)
        zs_v, zg_v, sm = r["pzs"], r["pzg"], r["psm"][...]
    else:
        zs_v, zg_v, sm = zs_ref, zg_ref, sm_ref[...]

    sp = _softplus(sm + bias_ref[...])
    beta_all = jax.nn.sigmoid(sm)
    if multi_seq and lv < q:
        assert q & (q - 1) == 0
        valid = jnp.bitwise_and(lax.broadcasted_iota(jnp.int32, (rr_, SMALL), 0), q - 1) < lv
        sp = jnp.where(valid, sp, 0.0)
        beta_all = jnp.where(valid, beta_all, 0.0)
    ag = sp * coef_ref[...]
    r_all = lax.broadcasted_iota(jnp.int32, (rr_, rr_), 0)
    c_all = lax.broadcasted_iota(jnp.int32, (rr_, rr_), 1)
    chunk_start = (r_all // q) * q
    tril_bd = jnp.where(r_all >= c_all, jnp.where(c_all >= chunk_start, 1.0, 0.0), 0.0).astype(BF16)
    rq = lax.broadcasted_iota(jnp.int32, (q, q), 0)
    cq = lax.broadcasted_iota(jnp.int32, (q, q), 1)
    incl = rq >= cq
    strict = rq > cq
    r128 = lax.broadcasted_iota(jnp.int32, (LANES, LANES), 0)
    c128 = lax.broadcasted_iota(jnp.int32, (LANES, LANES), 1)
    eye = jnp.where(r128 == c128, 1.0, 0.0).astype(BF16)
    cum = _const_dot(tril_bd, ag, 3)
    cum_t = _transpose_via_identity(eye, cum, 3)
    sp_t = _transpose_via_identity(eye, sp, 3)
    ecum = jnp.exp(cum)
    if nq > 1:
        cum_last = jnp.concatenate(
            [jnp.broadcast_to(cum[(c + 1) * q - 1:(c + 1) * q, :], (q, SMALL)) for c in range(nq)], axis=0)
    else:
        cum_last = cum[q - 1:q, :]
    to_end = jnp.exp(cum_last - cum)
    er = lax.broadcasted_iota(jnp.int32, (SMALL, SSD_WIDTH), 0)
    ec = lax.broadcasted_iota(jnp.int32, (SMALL, SSD_WIDTH), 1) // SSD_HEAD_DIM
    expand = jnp.where(er == ec, 1.0, 0.0).astype(BF16)
    pe_s[...] = _dot_const(ecum, expand, 2)
    p_dtend = _dot_const(sp * to_end, expand, 2)
    xcd_s[...] = (xbc_c[:, 0:SSD_WIDTH].astype(F32) * p_dtend).astype(BF16)

    lane = lax.broadcasted_iota(jnp.int32, (q, pair), 1)
    lo_half = lane < SSD_HEAD_DIM
    scores = {}
    for c in range(nq):
        rows = slice(c * q, (c + 1) * q)
        for g in range(SSD_GROUPS):
            b_g = xbc_c[rows, SSD_WIDTH + g * SSD_STATE:SSD_WIDTH + (g + 1) * SSD_STATE]
            c_g = xbc_c[rows, SSD_WIDTH + (SSD_GROUPS + g) * SSD_STATE:
                        SSD_WIDTH + (SSD_GROUPS + g + 1) * SSD_STATE]
            scores[c, g] = _bdot_nt(c_g, b_g)
            bgt_s[c * SSD_GROUPS + g] = _transpose_via_identity(eye, b_g, 1).astype(BF16)
    def ssd_diag(items):
        for c in items:
            rows = slice(c * q, (c + 1) * q)
            for j in range(SSD_HEADS // 2):
                g = j // (heads_per_group // 2)
                pcols = slice(pair * j, pair * (j + 1))
                x_pair = xbc_c[rows, pcols].astype(F32)
                y_pair = x_pair * dexp_ref[:, pcols]
                for half in range(2):
                    h = 2 * j + half
                    seg = cum[rows, h:h + 1] - cum_t[h:h + 1, rows]
                    lmat = jnp.exp(jnp.where(incl, seg, -jnp.inf))
                    m_h = scores[c, g] * lmat * sp_t[h:h + 1, rows]
                    x_h = jnp.where(lo_half if half == 0 else jnp.logical_not(lo_half), x_pair, 0.0)
                    y_pair = y_pair + _bdot(m_h, x_h)
                y_s[rows, pcols] = y_pair
                if j % 2 == 1:
                    yield

    def gdn_independent(group):
        m_list, rhs_list = [], []
        for n, (c, h) in enumerate(group):
            rows = slice(c * q, (c + 1) * q)
            i = c * GDN_HEADS + h
            qh = qkv_c[rows, h * GDN_HEAD_DIM:(h + 1) * GDN_HEAD_DIM].astype(F32)
            kh = qkv_c[rows, GDN_WIDTH + h * GDN_HEAD_DIM:GDN_WIDTH + (h + 1) * GDN_HEAD_DIM].astype(F32)
            vh = qkv_c[rows, 2 * GDN_WIDTH + h * GDN_HEAD_DIM:
                       2 * GDN_WIDTH + (h + 1) * GDN_HEAD_DIM].astype(F32)
            if do_conv:
                qh = qh * lax.rsqrt(jnp.sum(qh * qh, axis=-1, keepdims=True) + NORM_EPS) * (GDN_HEAD_DIM ** -0.5)
                kh = kh * lax.rsqrt(jnp.sum(kh * kh, axis=-1, keepdims=True) + NORM_EPS)
            la = _SM_A + h
            bcol = beta_all[rows, _SM_B + h:_SM_B + h + 1]
            eg = ecum[rows, la:la + 1]
            kb = kh * bcol
            dec = jnp.exp(jnp.where(incl, cum[rows, la:la + 1] - cum_t[la:la + 1, rows], -jnp.inf))
            kq = _bdot_nt(jnp.concatenate([kb, qh], axis=0), kh)
            m_list.append(jnp.where(strict, kq[0:q] * dec, 0.0))
            at_s[i] = (kq[q:2 * q] * dec).astype(BF16)
            rhs_list.append(jnp.concatenate([vh * bcol, kb * eg], axis=1))
            wq_s[i, q:2 * q, :] = (qh * eg).astype(BF16)
            kdt_s[i] = _transpose_via_identity(eye, kh * to_end[rows, la:la + 1], 1).astype(BF16)
            if n % 4 == 3:
                yield
        sol_list = []
        yield from _unit_lower_solve_many(m_list, rhs_list, q, sol_list)
        for (c, h), sol in zip(group, sol_list):
            i = c * GDN_HEADS + h
            u_s[i] = sol[:, 0:GDN_HEAD_DIM]
            wq_s[i, 0:q, :] = sol[:, GDN_HEAD_DIM:2 * GDN_HEAD_DIM].astype(BF16)
        yield

    def store_mix(c, cols, val):
        if multi_seq:
            mix_ref[c * lv:(c + 1) * lv, cols] = val[0:lv].astype(mix_ref.dtype)
        else:
            mix_ref[c * q:(c + 1) * q, cols] = val.astype(mix_ref.dtype)

    def recurrences(items):
        for c in items:
            rows = slice(c * q, (c + 1) * q)
            last = (c + 1) * q - 1
            s = slot(c)
            for g in range(SSD_GROUPS):
                gcols = slice(g * gw, (g + 1) * gw)
                c_g = xbc_c[rows, SSD_WIDTH + (SSD_GROUPS + g) * SSD_STATE:
                            SSD_WIDTH + (SSD_GROUPS + g + 1) * SSD_STATE]
                state = st_s[s, :, gcols]
                y_g = y_s[rows, gcols] + _bdot(c_g, state) * pe_s[rows, gcols]
                st_s[s, :, gcols] = state * pe_s[last:last + 1, gcols] + jnp.dot(
                    bgt_s[c * SSD_GROUPS + g], xcd_s[rows, gcols], preferred_element_type=F32)
                y_g = y_g * gate(zs_v[rows, gcols])
                ms = jnp.mean(y_g * y_g, axis=-1, keepdims=True)
                y_g = y_g * lax.rsqrt(ms + NORM_EPS) * snw_ref[:, gcols]
                store_mix(c, gcols, y_g)
            yield
        pairs = [(c, h) for c in items for h in range(GDN_HEADS)]
        ws = {}
        for (c, h) in pairs:
            ws[c, h] = jnp.dot(wq_s[c * GDN_HEADS + h], st_g[slot(c), h].astype(BF16),
                               preferred_element_type=F32)
        yield
        vn = {(c, h): (u_s[c * GDN_HEADS + h] - ws[c, h][0:q]).astype(BF16) for (c, h) in pairs}
        for (c, h) in pairs:
            i = c * GDN_HEADS + h
            la = _SM_A + h
            last = (c + 1) * q - 1
            s = slot(c)
            o = ws[c, h][q:2 * q] + jnp.dot(at_s[i], vn[c, h], preferred_element_type=F32)
            st_g[s, h] = st_g[s, h] * ecum[last:last + 1, la:la + 1] + jnp.dot(
                kdt_s[i], vn[c, h], preferred_element_type=F32)
            hc = slice(h * GDN_HEAD_DIM, (h + 1) * GDN_HEAD_DIM)
            ms = jnp.mean(o * o, axis=-1, keepdims=True)
            o = o * lax.rsqrt(ms + NORM_EPS) * gnw_ref[...] * gate(zg_v[c * q:(c + 1) * q, hc])
            store_mix(c, slice(SSD_WIDTH + h * GDN_HEAD_DIM, SSD_WIDTH + (h + 1) * GDN_HEAD_DIM), o)
        yield

    all_ch = [(c, h) for c in range(nq) for h in range(GDN_HEADS)]
    groups = [all_ch[i:i + chains] for i in range(0, len(all_ch), chains)]
    batches = [list(range(nq))] if multi_seq else [[c] for c in range(nq)]
    assert chains % GDN_HEADS == 0
    pending = list(batches)
    done_items = set()
    if multi_seq:
        for _ in ssd_diag(range(nq)):
            pass
    for grp in groups:
        runnable = [b for b in pending if set(b) <= done_items]
        for b in runnable:
            pending.remove(b)
        others = [_chain_gens([recurrences(b) for b in runnable])] if runnable else []
        if not multi_seq:
            others.append(ssd_diag(sorted({c for c, _ in grp})))
        _run_interleaved(gdn_independent(grp), *others)
        done_items |= {c for c, _ in grp}
    for b in pending:
        for _ in recurrences(b):
            pass

    @pl.when(c_idx == nc - 1)
    def _store_state():
        for s in range(n_slots):
            for j in range(SSD_HEADS // 2):
                blk = st_s[s, :, pair * j:pair * (j + 1)].T
                nhs_ref[0, s, 2 * j:2 * j + 2] = blk.reshape(2, SSD_HEAD_DIM, SSD_STATE)
        nsg_ref[0] = st_g[...]


_STATE_TAILS = (
    (CONV_K - 1, SSD_CONV_DIM),
    (SSD_HEADS, SSD_HEAD_DIM, SSD_STATE),
    (CONV_K - 1, GDN_CONV_DIM),
    (GDN_HEADS, GDN_HEAD_DIM, GDN_HEAD_DIM),
)


_STATE_NAMES = ("cs", "hs", "cg", "sg")


def _mixer_call(zs, xbc, zg, qkv, sm, B, L, states_in, prev_out, p, q, nq, multi_seq, layer, chains, do_conv):
    assert zs.shape[0] == B * L
    rows = nq * q
    assert q % min(INV_BLOCK, q) == 0 and q & (q - 1) == 0
    zero_init = states_in is None
    if multi_seq:
        assert CONV_K - 1 <= L <= q and B % nq == 0 and do_conv
        nc, lv, bb = 1, L, nq
        tok = lambda wd: pl.BlockSpec((nq * L, wd), lambda b, c: (b, 0))
    else:
        assert L % rows == 0
        nc, lv, bb = L // rows, rows, 1
        tok = lambda wd: pl.BlockSpec((rows, wd), lambda b, c: (b * nc + c, 0))
    n_slots = bb
    per_b = lambda shape: pl.BlockSpec((1, bb) + shape, lambda b, c: (layer, b) + (0,) * len(shape))
    const = lambda shape: pl.BlockSpec(shape, lambda b, c: (0,) * len(shape))
    nch = nq * GDN_HEADS
    kept = [k for k in range(4) if do_conv or k in (1, 3)]

    names = ["zs", "xbc", "zg", "qkv", "sm"]
    in_specs = [tok(SSD_WIDTH), tok(SSD_CONV_DIM), tok(GDN_WIDTH), tok(GDN_CONV_DIM), tok(SMALL)]
    args = [zs, xbc, zg, qkv, sm]
    if not zero_init:
        for k in kept:
            names.append(_STATE_NAMES[k])
            in_specs.append(per_b(_STATE_TAILS[k]))
            args.append(states_in[k])
    consts = [("bias", (1, SMALL)), ("coef", (1, SMALL)), ("dexp", (1, SSD_WIDTH)), ("snw", (1, SSD_WIDTH)),
              ("gnw", (1, GDN_HEAD_DIM))]
    if do_conv:
        consts += [("scw", (CONV_K, SSD_CONV_DIM)), ("scb", (1, SSD_CONV_DIM)), ("gcw", (CONV_K, GDN_CONV_DIM))]
    for nm, shape in consts:
        names.append(nm)
        in_specs.append(const(shape))
        args.append(p[nm])
    aliases = {}
    if prev_out is not None:
        for k, a in enumerate(prev_out):
            names.append("alias%d" % k)
            aliases[len(args)] = 1 + k
            in_specs.append(pl.BlockSpec(memory_space=pl.ANY))
            args.append(a)
    names += ["mix"] + ["n" + _STATE_NAMES[k] for k in kept]
    out_specs = [tok(MIX_WIDTH)] + [per_b(_STATE_TAILS[k]) for k in kept]
    out_shape = [jax.ShapeDtypeStruct((B * L, MIX_WIDTH), BF16)] + [
        jax.ShapeDtypeStruct((DEPTH, B) + _STATE_TAILS[k], F32) for k in kept]
    scratch = [
        ("st_s", pltpu.VMEM((n_slots, SSD_STATE, SSD_WIDTH), F32)),
        ("st_g", pltpu.VMEM((n_slots, GDN_HEADS, GDN_HEAD_DIM, GDN_HEAD_DIM), F32)),
        ("y_s", pltpu.VMEM((rows, SSD_WIDTH), F32)),
        ("pe_s", pltpu.VMEM((rows, SSD_WIDTH), F32)),
        ("xcd_s", pltpu.VMEM((rows, SSD_WIDTH), BF16)),
        ("bgt_s", pltpu.VMEM((nq * SSD_GROUPS, SSD_STATE, q), BF16)),
        ("wq_s", pltpu.VMEM((nch, 2 * q, GDN_HEAD_DIM), BF16)),
        ("u_s", pltpu.VMEM((nch, q, GDN_HEAD_DIM), F32)),
        ("at_s", pltpu.VMEM((nch, q, q), BF16)),
        ("kdt_s", pltpu.VMEM((nch, GDN_HEAD_DIM, q), BF16)),
    ]
    if do_conv:
        xp_rows = nq * (q + SUBLANES) if multi_seq else rows + SUBLANES
        scratch += [
            ("xps", pltpu.VMEM((xp_rows, SSD_CONV_DIM), F32)),
            ("xpg", pltpu.VMEM((xp_rows, GDN_CONV_DIM), F32)),
            ("xbc_c", pltpu.VMEM((rows, SSD_CONV_DIM), F32)),
            ("qkv_c", pltpu.VMEM((rows, GDN_CONV_DIM), F32)),
        ]
    if multi_seq:
        scratch += [
            ("pzs", pltpu.VMEM((rows, SSD_WIDTH), F32)),
            ("pzg", pltpu.VMEM((rows, GDN_WIDTH), F32)),
            ("psm", pltpu.VMEM((rows, SMALL), F32)),
        ]
    names += [nm for nm, _ in scratch]
    return pl.pallas_call(
        functools.partial(_mixer_kernel, names=tuple(names), q=q, nq=nq, lv=lv, nc=nc, multi_seq=multi_seq,
                          zero_init=zero_init, do_conv=do_conv, chains=chains),
        grid=(B // bb, nc),
        in_specs=in_specs,
        out_specs=out_specs,
        out_shape=out_shape,
        input_output_aliases=aliases,
        scratch_shapes=[s for _, s in scratch],
        compiler_params=pltpu.CompilerParams(
            dimension_semantics=("arbitrary", "arbitrary"), vmem_limit_bytes=VMEM_LIMIT),
        name="mixer",
    )(*args)


def _prep_layer(l, norm_w, w_in, ssd_conv_w, ssd_conv_b, ssd_dt_bias, ssd_a_log, ssd_d, ssd_norm_w,
                gdn_conv_w, gdn_dt_bias, gdn_a_log, gdn_norm_w, w_out):
    wi = w_in[l]
    small = jnp.concatenate(
        [wi[:, _OFF_DT:_OFF_DT + SSD_HEADS], wi[:, _OFF_A:_OFF_A + GDN_HEADS], wi[:, _OFF_B:_OFF_B + GDN_HEADS],
         jnp.zeros((D_MODEL, SMALL - SSD_HEADS - 2 * GDN_HEADS), F32)], axis=1)
    sm_hi = small.astype(BF16)
    sm_lo = (small - sm_hi.astype(F32)).astype(BF16)
    pad = jnp.zeros((SMALL - SSD_HEADS - GDN_HEADS,), F32)
    return {
        "norm_w": norm_w[l].reshape(1, D_MODEL),
        "zs": wi[:, _OFF_ZS:_OFF_ZS + SSD_WIDTH].astype(BF16),
        "xbc": wi[:, _OFF_XBC:_OFF_XBC + SSD_CONV_DIM].astype(BF16),
        "zg": wi[:, _OFF_ZG:_OFF_ZG + GDN_WIDTH].astype(BF16),
        "qkv": wi[:, _OFF_QKV:_OFF_QKV + GDN_CONV_DIM].astype(BF16),
        "sm_hi": sm_hi,
        "sm_lo": sm_lo,
        "scw": ssd_conv_w[l],
        "scb": ssd_conv_b[l].reshape(1, SSD_CONV_DIM),
        "gcw": gdn_conv_w[l],
        "bias": jnp.concatenate([ssd_dt_bias[l], gdn_dt_bias[l], pad]).reshape(1, SMALL),
        "coef": jnp.concatenate([-jnp.exp(ssd_a_log[l]), -jnp.exp(gdn_a_log[l]), pad]).reshape(1, SMALL),
        "dexp": jnp.repeat(ssd_d[l], SSD_HEAD_DIM).reshape(1, SSD_WIDTH),
        "snw": ssd_norm_w[l].reshape(1, SSD_WIDTH),
        "gnw": gdn_norm_w[l].reshape(1, GDN_HEAD_DIM),
        "w_out": w_out[l].astype(BF16),
    }


def _trunk(x, mod, states, params, final_w, q, nq, tm, tm_out, chains, act_dtype):
    B, L, _ = x.shape
    T = B * L
    per_row = L < SUBLANES
    steps_per_group = max(L // tm, 1)
    x2d = x.reshape(T, D_MODEL)
    conv_in_proj = states is None and not per_row
    new_states = None
    conv_states = ([], [])
    for l in range(DEPTH):
        p = params[l]
        shift, scale, gate = (mod[l, :, i * D_MODEL:(i + 1) * D_MODEL] for i in range(3))
        if per_row:
            rep = lambda a: jnp.repeat(a, L, axis=0).reshape(1, T, D_MODEL)
        else:
            rep = lambda a: a.reshape(B, 1, D_MODEL)
        zs, xbc, zg, qkv, sm, *cst = _inproj_call(x2d, rep(scale), rep(shift), p["norm_w"], p, tm, per_row,
                                                  steps_per_group, conv_in_proj, act_dtype)
        mix, *new_states = _mixer_call(zs, xbc, zg, qkv, sm, B, L, states, new_states, p,
                                       q, nq, per_row, l, chains, not conv_in_proj)
        x2d = _outproj_call(x2d, mix, rep(gate), p["w_out"], final_w, tm_out, per_row,
                            max(L // tm_out, 1), final_norm=(l == DEPTH - 1))
        for lst, s in zip(conv_states, cst):
            lst.append(s)
    if conv_in_proj:
        new_states = [jnp.stack(conv_states[0]), new_states[0], jnp.stack(conv_states[1]), new_states[1]]
    return x2d.reshape(B, L, D_MODEL), new_states


def kernel(x_prompt, x_sample, state_ssd_conv, state_ssm, state_gdn_conv, state_gdn, c_prompt, c_sample,
           norm_w, w_ada, b_ada, w_in, ssd_conv_w, ssd_conv_b, ssd_dt_bias, ssd_a_log, ssd_d, ssd_norm_w,
           gdn_conv_w, gdn_dt_bias, gdn_a_log, gdn_norm_w, w_out, final_norm_w):
    bp = x_prompt.shape[0]
    params = [_prep_layer(l, norm_w, w_in, ssd_conv_w, ssd_conv_b, ssd_dt_bias, ssd_a_log, ssd_d, ssd_norm_w,
                          gdn_conv_w, gdn_dt_bias, gdn_a_log, gdn_norm_w, w_out) for l in range(DEPTH)]
    final_w = final_norm_w.reshape(1, D_MODEL)
    mod = _ada_call(jnp.concatenate([c_prompt, c_sample], axis=0), w_ada, b_ada)
    y_p, sp = _trunk(x_prompt, mod[:, :bp], None, params, final_w, q=64, nq=4, tm=512, tm_out=1024, chains=16,
                     act_dtype=BF16)
    ts = x_sample.shape[0] * x_sample.shape[1]
    y_s, ss = _trunk(x_sample, mod[:, bp:], (state_ssd_conv, state_ssm, state_gdn_conv, state_gdn),
                     params, final_w, q=8, nq=8, tm=ts, tm_out=ts, chains=16, act_dtype=F32)
    return (y_p, y_s, sp[0], sp[1], sp[2], sp[3], ss[0], ss[1], ss[2], ss[3])
```

```python
import functools
import math

import jax
import jax.numpy as jnp
from jax import lax
from jax.experimental import pallas as pl
from jax.experimental.pallas import tpu as pltpu

F32 = jnp.float32
BF16 = jnp.bfloat16

D_MODEL = 1024
DEPTH = 2
SSD_WIDTH = 1024
SSD_HEAD_DIM = 64
SSD_HEADS = 16
SSD_GROUPS = 2
SSD_STATE = 128
SSD_CONV_DIM = SSD_WIDTH + 2 * SSD_GROUPS * SSD_STATE
GDN_WIDTH = 1024
GDN_HEAD_DIM = 128
GDN_HEADS = 8
GDN_CONV_DIM = 3 * GDN_WIDTH
MIX_WIDTH = SSD_WIDTH + GDN_WIDTH
CONV_K = 4
NORM_EPS = 1e-6
ADA_DIM = 3 * D_MODEL
SMALL = 128
LANES = 128
SUBLANES = 8
INV_BLOCK = 16
VMEM_LIMIT = 56 * 1024 * 1024

_OFF_ZS = 0
_OFF_XBC = _OFF_ZS + SSD_WIDTH
_OFF_DT = _OFF_XBC + SSD_CONV_DIM
_OFF_ZG = _OFF_DT + SSD_HEADS
_OFF_QKV = _OFF_ZG + GDN_WIDTH
_OFF_A = _OFF_QKV + GDN_CONV_DIM
_OFF_B = _OFF_A + GDN_HEADS
_SM_DT = 0
_SM_A = SSD_HEADS
_SM_B = SSD_HEADS + GDN_HEADS


def _bdot(a, b):
    return jnp.dot(a.astype(BF16), b.astype(BF16), preferred_element_type=F32)


def _bdot_nt(a, b):
    return lax.dot_general(a.astype(BF16), b.astype(BF16), (((1,), (1,)), ((), ())),
                           preferred_element_type=F32)


def _terms(x, n):
    out = []
    r = x
    for i in range(n):
        t = r.astype(BF16)
        out.append(t)
        if i + 1 < n:
            r = r - t.astype(F32)
    return out


def _const_dot(c, x, n):
    acc = None
    for t in _terms(x, n):
        p = jnp.dot(c, t, preferred_element_type=F32)
        acc = p if acc is None else acc + p
    return acc


def _dot_const(x, c, n):
    acc = None
    for t in _terms(x, n):
        p = jnp.dot(t, c, preferred_element_type=F32)
        acc = p if acc is None else acc + p
    return acc


def _transpose_via_identity(eye, x, n):
    acc = None
    for t in _terms(x, n):
        p = lax.dot_general(eye, t, (((1,), (1,)), ((), ())), preferred_element_type=F32)
        acc = p if acc is None else acc + p
    return acc


def _softplus(x):
    return jnp.maximum(x, 0.0) + jnp.log1p(jnp.exp(-jnp.abs(x)))


def _silu(x):
    return x * jax.nn.sigmoid(x)


def _ada_kernel(c_ref, w_ref, b_ref, o_ref):
    s = _silu(c_ref[...])
    w = w_ref[0]
    s_hi, s_lo = _terms(s, 2)
    w_hi, w_lo = _terms(w, 2)
    acc = jnp.dot(s_hi, w_hi, preferred_element_type=F32)
    acc = acc + (jnp.dot(s_hi, w_lo, preferred_element_type=F32)
                 + jnp.dot(s_lo, w_hi, preferred_element_type=F32))
    o_ref[0] = acc + b_ref[0]


def _ada_call(c_all, w_ada, b_ada):
    rows = c_all.shape[0]
    tn = ADA_DIM // 2
    return pl.pallas_call(
        _ada_kernel,
        grid=(DEPTH, ADA_DIM // tn),
        in_specs=[
            pl.BlockSpec((rows, D_MODEL), lambda l, j: (0, 0)),
            pl.BlockSpec((1, D_MODEL, tn), lambda l, j: (l, 0, j)),
            pl.BlockSpec((1, 1, tn), lambda l, j: (l, 0, j)),
        ],
        out_specs=pl.BlockSpec((1, rows, tn), lambda l, j: (l, 0, j)),
        out_shape=jax.ShapeDtypeStruct((DEPTH, rows, ADA_DIM), F32),
        compiler_params=pltpu.CompilerParams(
            dimension_semantics=("arbitrary", "arbitrary"), vmem_limit_bytes=VMEM_LIMIT),
        name="ada_mod",
    )(c_all, w_ada, b_ada.reshape(DEPTH, 1, ADA_DIM))


CONV_ROWS = 64
CONV_COLS = 128
PROJ_COLS = 512


def _conv_silu_rows(xp, w_ref, b_ref, out, rows, c0, c1, xp_row0=0, out_row0=0):
    rows_per = min(rows, CONV_ROWS)
    for i in range(rows // rows_per):
        r0 = i * rows_per
        for j in range((c1 - c0) // CONV_COLS):
            cols = slice(c0 + j * CONV_COLS, c0 + (j + 1) * CONV_COLS)
            x = xp[xp_row0 + r0:xp_row0 + r0 + rows_per + SUBLANES, cols]
            acc = x * w_ref[0:1, cols]
            for k in range(1, CONV_K):
                acc = pltpu.roll(acc, 1, 0) + x * w_ref[k:k + 1, cols]
            if b_ref is not None:
                acc = acc + b_ref[:, cols]
            out[out_row0 + r0:out_row0 + r0 + rows_per, cols] = _silu(acc[SUBLANES:, :]).astype(out.dtype)


PERM_CHUNK = SUBLANES * SUBLANES


def _qk_l2_scale(col):
    assert GDN_HEAD_DIM == LANES
    if col < GDN_WIDTH:
        return GDN_HEAD_DIM ** -0.5
    if col < 2 * GDN_WIDTH:
        return 1.0
    return None


def _conv_silu_permuted(xp, w_ref, b_ref, hist, up, out, rows, c0, l2_scale=None):
    sub0 = lax.broadcasted_iota(jnp.int32, (SUBLANES, LANES), 0) == 0
    taps = CONV_K - 1
    for j in range(PROJ_COLS // LANES):
        cols = slice(c0 + j * LANES, c0 + (j + 1) * LANES)
        w = [jnp.broadcast_to(w_ref[k:k + 1, cols], (SUBLANES, LANES)) for k in range(CONV_K)]
        b = None if b_ref is None else jnp.broadcast_to(b_ref[:, cols], (SUBLANES, LANES))
        prev = [hist[i * SUBLANES:(i + 1) * SUBLANES, cols] for i in range(taps)]
        for k in range(rows // PERM_CHUNK):
            base = k * PERM_CHUNK
            xs = [xp[base + a * SUBLANES:base + (a + 1) * SUBLANES, cols] for a in range(SUBLANES)]
            rolled = [pltpu.roll(xs[SUBLANES - taps + i], 1, 0) for i in range(taps)]
            ext = [jnp.where(sub0, prev[i], rolled[i]) for i in range(taps)] + xs
            for a in range(SUBLANES):
                y = ext[a] * w[0]
                for t in range(1, CONV_K):
                    y = y + ext[a + t] * w[t]
                if b is not None:
                    y = y + b
                y = _silu(y)
                scale = None if l2_scale is None else l2_scale(c0 + j * LANES)
                if scale is not None:
                    y = y * (lax.rsqrt(jnp.sum(y * y, axis=-1, keepdims=True) + NORM_EPS) * scale)
                up[j, pl.ds(base + a, SUBLANES, stride=SUBLANES), :] = y
            prev = rolled
        for i in range(taps):
            hist[i * SUBLANES:(i + 1) * SUBLANES, cols] = prev[i]
        out[:, cols] = up[j].astype(out.dtype)


def _inproj_kernel(*refs, tm, steps_per_seq, do_conv):
    x_ref, sc_ref, sh_ref, nw_ref, wzs_ref, wxbc_ref, wzg_ref, wqkv_ref, wsh_ref, wsl_ref = refs[:10]
    if do_conv:
        scw_ref, scb_ref, gcw_ref = refs[10:13]
        (zs_ref, xbc_ref, zg_ref, qkv_ref, sm_ref, ncs_ref, ncg_ref,
         xps, xpg, hist_s, hist_g, hs, hp, up) = refs[13:]
    else:
        zs_ref, xbc_ref, zg_ref, qkv_ref, sm_ref = refs[10:]
    x = x_ref[...]
    ms = jnp.mean(x * x, axis=-1, keepdims=True)
    h = x * lax.rsqrt(ms + NORM_EPS) * nw_ref[...]
    h = h * (1.0 + sc_ref[0]) + sh_ref[0]
    h_hi, h_lo = _terms(h, 2)
    sm = jnp.dot(h_hi, wsh_ref[...], preferred_element_type=F32)
    sm = sm + (jnp.dot(h_hi, wsl_ref[...], preferred_element_type=F32)
               + jnp.dot(h_lo, wsh_ref[...], preferred_element_type=F32))
    sm_ref[...] = sm
    if not do_conv:
        zs_ref[...] = jnp.dot(h_hi, wzs_ref[...], preferred_element_type=F32)
        xbc_ref[...] = jnp.dot(h_hi, wxbc_ref[...], preferred_element_type=F32)
        zg_ref[...] = jnp.dot(h_hi, wzg_ref[...], preferred_element_type=F32)
        qkv_ref[...] = jnp.dot(h_hi, wqkv_ref[...], preferred_element_type=F32)
        return

    @pl.when(pl.program_id(0) % steps_per_seq == 0)
    def _zero_history():
        hist_s[...] = jnp.zeros(hist_s.shape, F32)
        hist_g[...] = jnp.zeros(hist_g.shape, F32)

    assert tm % PERM_CHUNK == 0
    ncb = D_MODEL // LANES
    for cb in range(ncb):
        hs[cb] = h[:, cb * LANES:(cb + 1) * LANES]
    for g in range(0, tm // SUBLANES, 2):
        parts = [jnp.concatenate(
            [hs[cb, pl.ds(PERM_CHUNK * (gg // SUBLANES) + gg % SUBLANES, SUBLANES, stride=SUBLANES), :]
             for cb in range(ncb)], axis=1) for gg in (g, g + 1)]
        hp[g * SUBLANES:(g + 2) * SUBLANES, :] = jnp.concatenate(parts, axis=0).astype(BF16)

    conv_blocks = [(xp, w_ref, cw_ref, cb_ref, hist, out, c0)
                   for xp, w_ref, cw_ref, cb_ref, hist, out, width in (
                       (xps, wxbc_ref, scw_ref, scb_ref, hist_s, xbc_ref, SSD_CONV_DIM),
                       (xpg, wqkv_ref, gcw_ref, None, hist_g, qkv_ref, GDN_CONV_DIM))
                   for c0 in range(0, width, PROJ_COLS)]
    gate_blocks = [(w_ref, out, c0) for w_ref, out, width in ((wzs_ref, zs_ref, SSD_WIDTH),
                                                              (wzg_ref, zg_ref, GDN_WIDTH))
                   for c0 in range(0, width, PROJ_COLS)]

    def project(blk):
        xp, w_ref, _, _, _, _, c0 = blk
        cols = slice(c0, c0 + PROJ_COLS)
        xp[:, cols] = jnp.dot(hp[...], w_ref[:, cols], preferred_element_type=F32)

    def gate_block(i):
        w_ref, out, c0 = gate_blocks[i]
        cols = slice(c0, c0 + PROJ_COLS)
        out[:, cols] = _silu(jnp.dot(h_hi, w_ref[:, cols], preferred_element_type=F32)).astype(out.dtype)

    early_gates = len(gate_blocks) // 2
    for i in range(early_gates):
        gate_block(i)
    project(conv_blocks[0])
    for j, blk in enumerate(conv_blocks):
        if j + 1 < len(conv_blocks):
            project(conv_blocks[j + 1])
        if j % 2 == 0 and early_gates + j // 2 < len(gate_blocks):
            gate_block(early_gates + j // 2)
        xp, _, cw_ref, cb_ref, hist, out, c0 = blk
        _conv_silu_permuted(xp, cw_ref, cb_ref, hist, up, out, tm, c0,
                            l2_scale=_qk_l2_scale if xp is xpg else None)
    assert len(gate_blocks) - early_gates <= (len(conv_blocks) + 1) // 2
    for xp, nst in ((xps, ncs_ref), (xpg, ncg_ref)):
        for i in range(CONV_K - 1):
            row = tm - PERM_CHUNK + SUBLANES * (SUBLANES - (CONV_K - 1) + i) + SUBLANES - 1
            nst[0, i:i + 1, :] = xp[row:row + 1, :]


def _row_mod_spec(per_row, tm, steps_per_group):
    if per_row:
        return pl.BlockSpec((1, tm, D_MODEL), lambda i: (0, i, 0))
    return pl.BlockSpec((1, 1, D_MODEL), lambda i: (i // steps_per_group, 0, 0))


def _resident(shape):
    nd = len(shape)
    return pl.BlockSpec(shape, lambda i: (0,) * nd, pipeline_mode=pl.Buffered(1))


def _inproj_call(x2d, scale, shift, norm_w, w, tm, per_row, steps_per_group, do_conv, act_dtype):
    T = x2d.shape[0]
    assert T % tm == 0
    widths = (SSD_WIDTH, SSD_CONV_DIM, GDN_WIDTH, GDN_CONV_DIM, SMALL)
    dtypes = (act_dtype, act_dtype, act_dtype, act_dtype, F32)
    row = lambda wd: pl.BlockSpec((tm, wd), lambda i: (i, 0))
    in_specs = [
        row(D_MODEL),
        _row_mod_spec(per_row, tm, steps_per_group),
        _row_mod_spec(per_row, tm, steps_per_group),
        _resident((1, D_MODEL)),
        _resident((D_MODEL, SSD_WIDTH)),
        _resident((D_MODEL, SSD_CONV_DIM)),
        _resident((D_MODEL, GDN_WIDTH)),
        _resident((D_MODEL, GDN_CONV_DIM)),
        _resident((D_MODEL, SMALL)),
        _resident((D_MODEL, SMALL)),
    ]
    args = [x2d, scale, shift, norm_w, w["zs"], w["xbc"], w["zg"], w["qkv"], w["sm_hi"], w["sm_lo"]]
    out_specs = [row(wd) for wd in widths]
    out_shape = [jax.ShapeDtypeStruct((T, wd), dt) for wd, dt in zip(widths, dtypes)]
    scratch = []
    if do_conv:
        assert not per_row
        nseq = T // (tm * steps_per_group)
        in_specs += [_resident((CONV_K, SSD_CONV_DIM)), _resident((1, SSD_CONV_DIM)),
                     _resident((CONV_K, GDN_CONV_DIM))]
        args += [w["scw"], w["scb"], w["gcw"]]
        for wd in (SSD_CONV_DIM, GDN_CONV_DIM):
            out_specs.append(pl.BlockSpec((1, CONV_K - 1, wd), lambda i: (i // steps_per_group, 0, 0)))
            out_shape.append(jax.ShapeDtypeStruct((nseq, CONV_K - 1, wd), F32))
            scratch.append(pltpu.VMEM((tm, wd), F32))
        scratch += [pltpu.VMEM(((CONV_K - 1) * SUBLANES, wd), F32) for wd in (SSD_CONV_DIM, GDN_CONV_DIM)]
        scratch += [pltpu.VMEM((D_MODEL // LANES, tm, LANES), F32), pltpu.VMEM((tm, D_MODEL), BF16),
                    pltpu.VMEM((PROJ_COLS // LANES, tm, LANES), F32)]
    return pl.pallas_call(
        functools.partial(_inproj_kernel, tm=tm, steps_per_seq=steps_per_group, do_conv=do_conv),
        grid=(T // tm,),
        in_specs=in_specs,
        out_specs=out_specs,
        out_shape=out_shape,
        scratch_shapes=scratch,
        compiler_params=pltpu.CompilerParams(
            dimension_semantics=("arbitrary",), vmem_limit_bytes=VMEM_LIMIT),
        name="in_proj",
    )(*args)


def _outproj_kernel(x_ref, mix_ref, gate_ref, w_ref, fw_ref, o_ref, *, final_norm):
    y = jnp.dot(mix_ref[...], w_ref[...], preferred_element_type=F32)
    out = x_ref[...] + gate_ref[0] * y
    if final_norm:
        ms = jnp.mean(out * out, axis=-1, keepdims=True)
        out = out * lax.rsqrt(ms + NORM_EPS) * fw_ref[...]
    o_ref[...] = out


def _outproj_call(x2d, mix2d, gate, w_out, final_w, tm, per_row, steps_per_group, final_norm):
    T = x2d.shape[0]
    return pl.pallas_call(
        functools.partial(_outproj_kernel, final_norm=final_norm),
        grid=(T // tm,),
        in_specs=[
            pl.BlockSpec((tm, D_MODEL), lambda i: (i, 0)),
            pl.BlockSpec((tm, MIX_WIDTH), lambda i: (i, 0)),
            _row_mod_spec(per_row, tm, steps_per_group),
            _resident((MIX_WIDTH, D_MODEL)),
            _resident((1, D_MODEL)),
        ],
        out_specs=pl.BlockSpec((tm, D_MODEL), lambda i: (i, 0)),
        out_shape=jax.ShapeDtypeStruct((T, D_MODEL), F32),
        compiler_params=pltpu.CompilerParams(
            dimension_semantics=("arbitrary",), vmem_limit_bytes=VMEM_LIMIT),
        name="out_proj",
    )(x2d, mix2d, gate, w_out, final_w)


def _run_interleaved(*gens):
    live = [g for g in gens if g is not None]
    while live:
        for g in list(live):
            try:
                next(g)
            except StopIteration:
                live.remove(g)


def _chain_gens(gens):
    for g in gens:
        yield from g


def _unit_lower_solve_many(m_list, rhs_list, q, out):
    blk = min(INV_BLOCK, q)
    nb = q // blk
    if nb > 1:
        r = lax.broadcasted_iota(jnp.int32, (q, q), 0) // blk
        c = lax.broadcasted_iota(jnp.int32, (q, q), 1) // blk
        same = r == c
        d_list = [jnp.where(same, m, 0.0) for m in m_list]
        o_list = [m - d for m, d in zip(m_list, d_list)]
    else:
        d_list = m_list
    n_list = [-d for d in d_list]
    e_list = d_list
    for _ in range(int(math.log2(blk)) - 1):
        e_list = [_bdot(e, e) for e in e_list]
        yield
        n_list = [n + e + _bdot(n, e) for n, e in zip(n_list, e_list)]
        yield
    y_list = [rhs + _bdot(n, rhs) for n, rhs in zip(n_list, rhs_list)]
    if nb > 1:
        p_list = [o + _bdot(n, o) for n, o in zip(n_list, o_list)]
        yield
        r_list = [-p for p in p_list]
        e_list = p_list
        for _ in range(int(math.ceil(math.log2(nb))) - 1):
            e_list = [_bdot(e, e) for e in e_list]
            yield
            r_list = [rr + e + _bdot(rr, e) for rr, e in zip(r_list, e_list)]
            yield
        y_list = [y + _bdot(rr, y) for rr, y in zip(r_list, y_list)]
    yield
    out.extend(y_list)


def _mixer_kernel(*refs, names, q, nq, lv, nc, multi_seq, zero_init, do_conv, chains):
    r = dict(zip(names, refs))
    zs_ref, xbc_ref, zg_ref, qkv_ref, sm_ref = r["zs"], r["xbc"], r["zg"], r["qkv"], r["sm"]
    bias_ref, coef_ref, dexp_ref, snw_ref, gnw_ref = r["bias"], r["coef"], r["dexp"], r["snw"], r["gnw"]
    mix_ref, nhs_ref, nsg_ref = r["mix"], r["nhs"], r["nsg"]
    st_s, st_g, y_s, pe_s, xcd_s, bgt_s = r["st_s"], r["st_g"], r["y_s"], r["pe_s"], r["xcd_s"], r["bgt_s"]
    wq_s, u_s, at_s, kdt_s = r["wq_s"], r["u_s"], r["at_s"], r["kdt_s"]

    c_idx = pl.program_id(1)
    rr_ = nq * q
    n_slots = nq if multi_seq else 1
    slot = (lambda c: c) if multi_seq else (lambda c: 0)
    hist = q + SUBLANES
    pair = 2 * SSD_HEAD_DIM
    heads_per_group = SSD_HEADS // SSD_GROUPS
    gw = SSD_WIDTH // SSD_GROUPS
    assert not (do_conv and not multi_seq and nc > 1 and lv != rr_)

    @pl.when(c_idx == 0)
    def _load_state():
        if zero_init:
            st_s[...] = jnp.zeros(st_s.shape, F32)
            st_g[...] = jnp.zeros(st_g.shape, F32)
        else:
            for s in range(n_slots):
                for j in range(SSD_HEADS // 2):
                    blk = r["hs"][0, s, 2 * j:2 * j + 2].reshape(pair, SSD_STATE)
                    st_s[s, :, pair * j:pair * (j + 1)] = blk.T
            st_g[...] = r["sg"][0]
        if do_conv:
            for xp, key, width in ((r["xps"], "cs", SSD_CONV_DIM), (r["xpg"], "cg", GDN_CONV_DIM)):
                for s in range(n_slots):
                    xp[s * hist:s * hist + SUBLANES, :] = jnp.zeros((SUBLANES, width), F32)
                    if not zero_init:
                        xp[s * hist + SUBLANES - (CONV_K - 1):s * hist + SUBLANES, :] = r[key][0, s]

    if do_conv:
        for x_ref, xp, w_ref, b_ref, out, nst, width in (
                (xbc_ref, r["xps"], r["scw"], r["scb"], r["xbc_c"], r["ncs"], SSD_CONV_DIM),
                (qkv_ref, r["xpg"], r["gcw"], None, r["qkv_c"], r["ncg"], GDN_CONV_DIM)):
            if multi_seq:
                for c in range(nq):
                    base = c * hist + SUBLANES
                    xp[base:base + lv, :] = x_ref[c * lv:(c + 1) * lv, :]
                    xp[base + lv:base + q, :] = jnp.zeros((q - lv, width), F32)
                    _conv_silu_rows(xp, w_ref, b_ref, out, q, 0, width, xp_row0=c * hist, out_row0=c * q)
                    nst[0, c] = xp[base + lv - (CONV_K - 1):base + lv, :]
            else:
                xp[SUBLANES:SUBLANES + rr_, :] = x_ref[...]
                _conv_silu_rows(xp, w_ref, b_ref, out, rr_, 0, width)

                @pl.when(c_idx == nc - 1)
                def _store_conv_state():
                    nst[0, 0] = xp[SUBLANES + rr_ - (CONV_K - 1):SUBLANES + rr_, :]

                if nc > 1:
                    xp[0:SUBLANES, :] = xp[rr_:rr_ + SUBLANES, :]
        xbc_c, qkv_c = r["xbc_c"], r["qkv_c"]
        gate = _silu
    else:
        xbc_c, qkv_c = xbc_ref, qkv_ref
        gate = lambda z: z.astype(F32)

    if multi_seq:
        for src, dst, wd in ((zs_ref, r["pzs"], SSD_WIDTH), (zg_ref, r["pzg"], GDN_WIDTH),
                             (sm_ref, r["psm"], SMALL)):
            for c in range(nq):
                dst[c * q:c * q + lv, :] = src[c * lv:(c + 1) * lv, :].astype(F32)
                dst[c * q + lv:(c + 1) * q, :] = jnp.zeros((q - lv, wd), F32)
        zs_v, zg_v, sm = r["pzs"], r["pzg"], r["psm"][...]
    else:
        zs_v, zg_v, sm = zs_ref, zg_ref, sm_ref[...]

    sp = _softplus(sm + bias_ref[...])
    beta_all = jax.nn.sigmoid(sm)
    if multi_seq and lv < q:
        assert q & (q - 1) == 0
        valid = jnp.bitwise_and(lax.broadcasted_iota(jnp.int32, (rr_, SMALL), 0), q - 1) < lv
        sp = jnp.where(valid, sp, 0.0)
        beta_all = jnp.where(valid, beta_all, 0.0)
    ag = sp * coef_ref[...]
    r_all = lax.broadcasted_iota(jnp.int32, (rr_, rr_), 0)
    c_all = lax.broadcasted_iota(jnp.int32, (rr_, rr_), 1)
    chunk_start = (r_all // q) * q
    tril_bd = jnp.where(r_all >= c_all, jnp.where(c_all >= chunk_start, 1.0, 0.0), 0.0).astype(BF16)
    rq = lax.broadcasted_iota(jnp.int32, (q, q), 0)
    cq = lax.broadcasted_iota(jnp.int32, (q, q), 1)
    incl = rq >= cq
    strict = rq > cq
    r128 = lax.broadcasted_iota(jnp.int32, (LANES, LANES), 0)
    c128 = lax.broadcasted_iota(jnp.int32, (LANES, LANES), 1)
    eye = jnp.where(r128 == c128, 1.0, 0.0).astype(BF16)
    cum = _const_dot(tril_bd, ag, 3)
    if q == PERM_CHUNK:
        cum_t, sp_t = cum.T, sp.T
    else:
        cum_t = _transpose_via_identity(eye, cum, 3)
        sp_t = _transpose_via_identity(eye, sp, 3)
    ecum = jnp.exp(cum)
    if nq > 1:
        cum_last = jnp.concatenate(
            [jnp.broadcast_to(cum[(c + 1) * q - 1:(c + 1) * q, :], (q, SMALL)) for c in range(nq)], axis=0)
    else:
        cum_last = cum[q - 1:q, :]
    to_end = jnp.exp(cum_last - cum)
    er = lax.broadcasted_iota(jnp.int32, (SMALL, SSD_WIDTH), 0)
    ec = lax.broadcasted_iota(jnp.int32, (SMALL, SSD_WIDTH), 1) // SSD_HEAD_DIM
    expand = jnp.where(er == ec, 1.0, 0.0).astype(BF16)
    pe_s[...] = _dot_const(ecum, expand, 2)
    p_dtend = _dot_const(sp * to_end, expand, 2)
    xcd_s[...] = (xbc_c[:, 0:SSD_WIDTH].astype(F32) * p_dtend).astype(BF16)

    lane = lax.broadcasted_iota(jnp.int32, (q, pair), 1)
    lo_half = lane < SSD_HEAD_DIM
    scores = {}
    for c in range(nq):
        rows = slice(c * q, (c + 1) * q)
        for g in range(SSD_GROUPS):
            b_g = xbc_c[rows, SSD_WIDTH + g * SSD_STATE:SSD_WIDTH + (g + 1) * SSD_STATE]
            c_g = xbc_c[rows, SSD_WIDTH + (SSD_GROUPS + g) * SSD_STATE:
                        SSD_WIDTH + (SSD_GROUPS + g + 1) * SSD_STATE]
            scores[c, g] = _bdot_nt(c_g, b_g)
            if q == PERM_CHUNK:
                bgt_s[c * SSD_GROUPS + g] = b_g.astype(F32).T.astype(BF16)
            else:
                bgt_s[c * SSD_GROUPS + g] = _transpose_via_identity(eye, b_g, 1).astype(BF16)
    def ssd_diag(items):
        for c in items:
            rows = slice(c * q, (c + 1) * q)
            for j in range(SSD_HEADS // 2):
                g = j // (heads_per_group // 2)
                pcols = slice(pair * j, pair * (j + 1))
                x_pair = xbc_c[rows, pcols].astype(F32)
                y_pair = x_pair * dexp_ref[:, pcols]
                for half in range(2):
                    h = 2 * j + half
                    seg = cum[rows, h:h + 1] - cum_t[h:h + 1, rows]
                    lmat = jnp.exp(jnp.where(incl, seg, -jnp.inf))
                    m_h = scores[c, g] * lmat * sp_t[h:h + 1, rows]
                    x_h = jnp.where(lo_half if half == 0 else jnp.logical_not(lo_half), x_pair, 0.0)
                    y_pair = y_pair + _bdot(m_h, x_h)
                y_s[rows, pcols] = y_pair
                if j % 2 == 1:
                    yield

    def gdn_independent(group):
        m_list, rhs_list = [], []
        for n, (c, h) in enumerate(group):
            rows = slice(c * q, (c + 1) * q)
            i = c * GDN_HEADS + h
            qh = qkv_c[rows, h * GDN_HEAD_DIM:(h + 1) * GDN_HEAD_DIM].astype(F32)
            kh = qkv_c[rows, GDN_WIDTH + h * GDN_HEAD_DIM:GDN_WIDTH + (h + 1) * GDN_HEAD_DIM].astype(F32)
            vh = qkv_c[rows, 2 * GDN_WIDTH + h * GDN_HEAD_DIM:
                       2 * GDN_WIDTH + (h + 1) * GDN_HEAD_DIM].astype(F32)
            if do_conv:
                qh = qh * lax.rsqrt(jnp.sum(qh * qh, axis=-1, keepdims=True) + NORM_EPS) * (GDN_HEAD_DIM ** -0.5)
                kh = kh * lax.rsqrt(jnp.sum(kh * kh, axis=-1, keepdims=True) + NORM_EPS)
            la = _SM_A + h
            bcol = beta_all[rows, _SM_B + h:_SM_B + h + 1]
            eg = ecum[rows, la:la + 1]
            kb = kh * bcol
            dec = jnp.exp(jnp.where(incl, cum[rows, la:la + 1] - cum_t[la:la + 1, rows], -jnp.inf))
            kq = _bdot_nt(jnp.concatenate([kb, qh], axis=0), kh)
            m_list.append(jnp.where(strict, kq[0:q] * dec, 0.0))
            at_s[i] = (kq[q:2 * q] * dec).astype(BF16)
            rhs_list.append(jnp.concatenate([vh * bcol, kb * eg], axis=1))
            wq_s[i, q:2 * q, :] = (qh * eg).astype(BF16)
            kd = kh * to_end[rows, la:la + 1]
            if q == PERM_CHUNK:
                kdt_s[i] = kd.T.astype(BF16)
            else:
                kdt_s[i] = _transpose_via_identity(eye, kd, 1).astype(BF16)
            if n % 4 == 3:
                yield
        sol_list = []
        yield from _unit_lower_solve_many(m_list, rhs_list, q, sol_list)
        for (c, h), sol in zip(group, sol_list):
            i = c * GDN_HEADS + h
            u_s[i] = sol[:, 0:GDN_HEAD_DIM]
            wq_s[i, 0:q, :] = sol[:, GDN_HEAD_DIM:2 * GDN_HEAD_DIM].astype(BF16)
        yield

    def store_mix(c, cols, val):
        if multi_seq:
            mix_ref[c * lv:(c + 1) * lv, cols] = val[0:lv].astype(mix_ref.dtype)
        else:
            mix_ref[c * q:(c + 1) * q, cols] = val.astype(mix_ref.dtype)

    def recurrences(items):
        for c in items:
            rows = slice(c * q, (c + 1) * q)
            last = (c + 1) * q - 1
            s = slot(c)
            for g in range(SSD_GROUPS):
                gcols = slice(g * gw, (g + 1) * gw)
                c_g = xbc_c[rows, SSD_WIDTH + (SSD_GROUPS + g) * SSD_STATE:
                            SSD_WIDTH + (SSD_GROUPS + g + 1) * SSD_STATE]
                state = st_s[s, :, gcols]
                y_g = y_s[rows, gcols] + _bdot(c_g, state) * pe_s[rows, gcols]
                st_s[s, :, gcols] = state * pe_s[last:last + 1, gcols] + jnp.dot(
                    bgt_s[c * SSD_GROUPS + g], xcd_s[rows, gcols], preferred_element_type=F32)
                y_g = y_g * gate(zs_v[rows, gcols])
                ms = jnp.mean(y_g * y_g, axis=-1, keepdims=True)
                y_g = y_g * lax.rsqrt(ms + NORM_EPS) * snw_ref[:, gcols]
                store_mix(c, gcols, y_g)
            yield
        pairs = [(c, h) for c in items for h in range(GDN_HEADS)]
        ws = {}
        for (c, h) in pairs:
            ws[c, h] = jnp.dot(wq_s[c * GDN_HEADS + h], st_g[slot(c), h].astype(BF16),
                               preferred_element_type=F32)
        yield
        vn = {(c, h): (u_s[c * GDN_HEADS + h] - ws[c, h][0:q]).astype(BF16) for (c, h) in pairs}
        for (c, h) in pairs:
            i = c * GDN_HEADS + h
            la = _SM_A + h
            last = (c + 1) * q - 1
            s = slot(c)
            o = ws[c, h][q:2 * q] + jnp.dot(at_s[i], vn[c, h], preferred_element_type=F32)
            st_g[s, h] = st_g[s, h] * ecum[last:last + 1, la:la + 1] + jnp.dot(
                kdt_s[i], vn[c, h], preferred_element_type=F32)
            hc = slice(h * GDN_HEAD_DIM, (h + 1) * GDN_HEAD_DIM)
            ms = jnp.mean(o * o, axis=-1, keepdims=True)
            o = o * lax.rsqrt(ms + NORM_EPS) * gnw_ref[...] * gate(zg_v[c * q:(c + 1) * q, hc])
            store_mix(c, slice(SSD_WIDTH + h * GDN_HEAD_DIM, SSD_WIDTH + (h + 1) * GDN_HEAD_DIM), o)
        yield

    all_ch = [(c, h) for c in range(nq) for h in range(GDN_HEADS)]
    groups = [all_ch[i:i + chains] for i in range(0, len(all_ch), chains)]
    batches = [list(range(nq))] if multi_seq else [[c] for c in range(nq)]
    assert chains % GDN_HEADS == 0
    pending = list(batches)
    done_items = set()
    if multi_seq:
        for _ in ssd_diag(range(nq)):
            pass
    for grp in groups:
        runnable = [b for b in pending if set(b) <= done_items]
        for b in runnable:
            pending.remove(b)
        others = [_chain_gens([recurrences(b) for b in runnable])] if runnable else []
        if not multi_seq:
            others.append(ssd_diag(sorted({c for c, _ in grp})))
        _run_interleaved(gdn_independent(grp), *others)
        done_items |= {c for c, _ in grp}
    for b in pending:
        for _ in recurrences(b):
            pass

    @pl.when(c_idx == nc - 1)
    def _store_state():
        for s in range(n_slots):
            for j in range(SSD_HEADS // 2):
                blk = st_s[s, :, pair * j:pair * (j + 1)].T
                nhs_ref[0, s, 2 * j:2 * j + 2] = blk.reshape(2, SSD_HEAD_DIM, SSD_STATE)
        nsg_ref[0] = st_g[...]


_STATE_TAILS = (
    (CONV_K - 1, SSD_CONV_DIM),
    (SSD_HEADS, SSD_HEAD_DIM, SSD_STATE),
    (CONV_K - 1, GDN_CONV_DIM),
    (GDN_HEADS, GDN_HEAD_DIM, GDN_HEAD_DIM),
)


_STATE_NAMES = ("cs", "hs", "cg", "sg")


def _mixer_call(zs, xbc, zg, qkv, sm, B, L, states_in, prev_out, p, q, nq, multi_seq, layer, chains, do_conv):
    assert zs.shape[0] == B * L
    rows = nq * q
    assert q % min(INV_BLOCK, q) == 0 and q & (q - 1) == 0
    zero_init = states_in is None
    if multi_seq:
        assert CONV_K - 1 <= L <= q and B % nq == 0 and do_conv
        nc, lv, bb = 1, L, nq
        tok = lambda wd: pl.BlockSpec((nq * L, wd), lambda b, c: (b, 0))
    else:
        assert L % rows == 0
        nc, lv, bb = L // rows, rows, 1
        tok = lambda wd: pl.BlockSpec((rows, wd), lambda b, c: (b * nc + c, 0))
    n_slots = bb
    per_b = lambda shape: pl.BlockSpec((1, bb) + shape, lambda b, c: (layer, b) + (0,) * len(shape))
    const = lambda shape: pl.BlockSpec(shape, lambda b, c: (0,) * len(shape))
    nch = nq * GDN_HEADS
    kept = [k for k in range(4) if do_conv or k in (1, 3)]

    names = ["zs", "xbc", "zg", "qkv", "sm"]
    in_specs = [tok(SSD_WIDTH), tok(SSD_CONV_DIM), tok(GDN_WIDTH), tok(GDN_CONV_DIM), tok(SMALL)]
    args = [zs, xbc, zg, qkv, sm]
    if not zero_init:
        for k in kept:
            names.append(_STATE_NAMES[k])
            in_specs.append(per_b(_STATE_TAILS[k]))
            args.append(states_in[k])
    consts = [("bias", (1, SMALL)), ("coef", (1, SMALL)), ("dexp", (1, SSD_WIDTH)), ("snw", (1, SSD_WIDTH)),
              ("gnw", (1, GDN_HEAD_DIM))]
    if do_conv:
        consts += [("scw", (CONV_K, SSD_CONV_DIM)), ("scb", (1, SSD_CONV_DIM)), ("gcw", (CONV_K, GDN_CONV_DIM))]
    for nm, shape in consts:
        names.append(nm)
        in_specs.append(const(shape))
        args.append(p[nm])
    aliases = {}
    if prev_out is not None:
        for k, a in enumerate(prev_out):
            names.append("alias%d" % k)
            aliases[len(args)] = 1 + k
            in_specs.append(pl.BlockSpec(memory_space=pl.ANY))
            args.append(a)
    names += ["mix"] + ["n" + _STATE_NAMES[k] for k in kept]
    out_specs = [tok(MIX_WIDTH)] + [per_b(_STATE_TAILS[k]) for k in kept]
    out_shape = [jax.ShapeDtypeStruct((B * L, MIX_WIDTH), BF16)] + [
        jax.ShapeDtypeStruct((DEPTH, B) + _STATE_TAILS[k], F32) for k in kept]
    scratch = [
        ("st_s", pltpu.VMEM((n_slots, SSD_STATE, SSD_WIDTH), F32)),
        ("st_g", pltpu.VMEM((n_slots, GDN_HEADS, GDN_HEAD_DIM, GDN_HEAD_DIM), F32)),
        ("y_s", pltpu.VMEM((rows, SSD_WIDTH), F32)),
        ("pe_s", pltpu.VMEM((rows, SSD_WIDTH), F32)),
        ("xcd_s", pltpu.VMEM((rows, SSD_WIDTH), BF16)),
        ("bgt_s", pltpu.VMEM((nq * SSD_GROUPS, SSD_STATE, q), BF16)),
        ("wq_s", pltpu.VMEM((nch, 2 * q, GDN_HEAD_DIM), BF16)),
        ("u_s", pltpu.VMEM((nch, q, GDN_HEAD_DIM), F32)),
        ("at_s", pltpu.VMEM((nch, q, q), BF16)),
        ("kdt_s", pltpu.VMEM((nch, GDN_HEAD_DIM, q), BF16)),
    ]
    if do_conv:
        xp_rows = nq * (q + SUBLANES) if multi_seq else rows + SUBLANES
        scratch += [
            ("xps", pltpu.VMEM((xp_rows, SSD_CONV_DIM), F32)),
            ("xpg", pltpu.VMEM((xp_rows, GDN_CONV_DIM), F32)),
            ("xbc_c", pltpu.VMEM((rows, SSD_CONV_DIM), F32)),
            ("qkv_c", pltpu.VMEM((rows, GDN_CONV_DIM), F32)),
        ]
    if multi_seq:
        scratch += [
            ("pzs", pltpu.VMEM((rows, SSD_WIDTH), F32)),
            ("pzg", pltpu.VMEM((rows, GDN_WIDTH), F32)),
            ("psm", pltpu.VMEM((rows, SMALL), F32)),
        ]
    names += [nm for nm, _ in scratch]
    return pl.pallas_call(
        functools.partial(_mixer_kernel, names=tuple(names), q=q, nq=nq, lv=lv, nc=nc, multi_seq=multi_seq,
                          zero_init=zero_init, do_conv=do_conv, chains=chains),
        grid=(B // bb, nc),
        in_specs=in_specs,
        out_specs=out_specs,
        out_shape=out_shape,
        input_output_aliases=aliases,
        scratch_shapes=[s for _, s in scratch],
        compiler_params=pltpu.CompilerParams(
            dimension_semantics=("arbitrary", "arbitrary"), vmem_limit_bytes=VMEM_LIMIT),
        name="mixer",
    )(*args)


def _prep_layer(l, norm_w, w_in, ssd_conv_w, ssd_conv_b, ssd_dt_bias, ssd_a_log, ssd_d, ssd_norm_w,
                gdn_conv_w, gdn_dt_bias, gdn_a_log, gdn_norm_w, w_out):
    wi = w_in[l]
    small = jnp.concatenate(
        [wi[:, _OFF_DT:_OFF_DT + SSD_HEADS], wi[:, _OFF_A:_OFF_A + GDN_HEADS], wi[:, _OFF_B:_OFF_B + GDN_HEADS],
         jnp.zeros((D_MODEL, SMALL - SSD_HEADS - 2 * GDN_HEADS), F32)], axis=1)
    sm_hi = small.astype(BF16)
    sm_lo = (small - sm_hi.astype(F32)).astype(BF16)
    pad = jnp.zeros((SMALL - SSD_HEADS - GDN_HEADS,), F32)
    return {
        "norm_w": norm_w[l].reshape(1, D_MODEL),
        "zs": wi[:, _OFF_ZS:_OFF_ZS + SSD_WIDTH].astype(BF16),
        "xbc": wi[:, _OFF_XBC:_OFF_XBC + SSD_CONV_DIM].astype(BF16),
        "zg": wi[:, _OFF_ZG:_OFF_ZG + GDN_WIDTH].astype(BF16),
        "qkv": wi[:, _OFF_QKV:_OFF_QKV + GDN_CONV_DIM].astype(BF16),
        "sm_hi": sm_hi,
        "sm_lo": sm_lo,
        "scw": ssd_conv_w[l],
        "scb": ssd_conv_b[l].reshape(1, SSD_CONV_DIM),
        "gcw": gdn_conv_w[l],
        "bias": jnp.concatenate([ssd_dt_bias[l], gdn_dt_bias[l], pad]).reshape(1, SMALL),
        "coef": jnp.concatenate([-jnp.exp(ssd_a_log[l]), -jnp.exp(gdn_a_log[l]), pad]).reshape(1, SMALL),
        "dexp": jnp.repeat(ssd_d[l], SSD_HEAD_DIM).reshape(1, SSD_WIDTH),
        "snw": ssd_norm_w[l].reshape(1, SSD_WIDTH),
        "gnw": gdn_norm_w[l].reshape(1, GDN_HEAD_DIM),
        "w_out": w_out[l].astype(BF16),
    }


def _trunk(x, mod, states, params, final_w, q, nq, tm, tm_out, chains, act_dtype):
    B, L, _ = x.shape
    T = B * L
    per_row = L < SUBLANES
    steps_per_group = max(L // tm, 1)
    x2d = x.reshape(T, D_MODEL)
    conv_in_proj = states is None and not per_row
    new_states = None
    conv_states = ([], [])
    for l in range(DEPTH):
        p = params[l]
        shift, scale, gate = (mod[l, :, i * D_MODEL:(i + 1) * D_MODEL] for i in range(3))
        if per_row:
            rep = lambda a: jnp.repeat(a, L, axis=0).reshape(1, T, D_MODEL)
        else:
            rep = lambda a: a.reshape(B, 1, D_MODEL)
        zs, xbc, zg, qkv, sm, *cst = _inproj_call(x2d, rep(scale), rep(shift), p["norm_w"], p, tm, per_row,
                                                  steps_per_group, conv_in_proj, act_dtype)
        mix, *new_states = _mixer_call(zs, xbc, zg, qkv, sm, B, L, states, new_states, p,
                                       q, nq, per_row, l, chains, not conv_in_proj)
        x2d = _outproj_call(x2d, mix, rep(gate), p["w_out"], final_w, tm_out, per_row,
                            max(L // tm_out, 1), final_norm=(l == DEPTH - 1))
        for lst, s in zip(conv_states, cst):
            lst.append(s)
    if conv_in_proj:
        new_states = [jnp.stack(conv_states[0]), new_states[0], jnp.stack(conv_states[1]), new_states[1]]
    return x2d.reshape(B, L, D_MODEL), new_states


def kernel(x_prompt, x_sample, state_ssd_conv, state_ssm, state_gdn_conv, state_gdn, c_prompt, c_sample,
           norm_w, w_ada, b_ada, w_in, ssd_conv_w, ssd_conv_b, ssd_dt_bias, ssd_a_log, ssd_d, ssd_norm_w,
           gdn_conv_w, gdn_dt_bias, gdn_a_log, gdn_norm_w, w_out, final_norm_w):
    bp = x_prompt.shape[0]
    params = [_prep_layer(l, norm_w, w_in, ssd_conv_w, ssd_conv_b, ssd_dt_bias, ssd_a_log, ssd_d, ssd_norm_w,
                          gdn_conv_w, gdn_dt_bias, gdn_a_log, gdn_norm_w, w_out) for l in range(DEPTH)]
    final_w = final_norm_w.reshape(1, D_MODEL)
    mod = _ada_call(jnp.concatenate([c_prompt, c_sample], axis=0), w_ada, b_ada)
    y_p, sp = _trunk(x_prompt, mod[:, :bp], None, params, final_w, q=64, nq=4, tm=512, tm_out=1024, chains=16,
                     act_dtype=BF16)
    ts = x_sample.shape[0] * x_sample.shape[1]
    y_s, ss = _trunk(x_sample, mod[:, bp:], (state_ssd_conv, state_ssm, state_gdn_conv, state_gdn),
                     params, final_w, q=8, nq=8, tm=ts, tm_out=ts, chains=16, act_dtype=F32)
    return (y_p, y_s, sp[0], sp[1], sp[2], sp[3], ss[0], ss[1], ss[2], ss[3])
```

```python
import functools
import math

import jax
import jax.numpy as jnp
from jax import lax
from jax.experimental import pallas as pl
from jax.experimental.pallas import tpu as pltpu

F32 = jnp.float32
BF16 = jnp.bfloat16

D_MODEL = 1024
DEPTH = 2
SSD_WIDTH = 1024
SSD_HEAD_DIM = 64
SSD_HEADS = 16
SSD_GROUPS = 2
SSD_STATE = 128
SSD_CONV_DIM = SSD_WIDTH + 2 * SSD_GROUPS * SSD_STATE
GDN_WIDTH = 1024
GDN_HEAD_DIM = 128
GDN_HEADS = 8
GDN_CONV_DIM = 3 * GDN_WIDTH
MIX_WIDTH = SSD_WIDTH + GDN_WIDTH
CONV_K = 4
NORM_EPS = 1e-6
ADA_DIM = 3 * D_MODEL
SMALL = 128
LANES = 128
SUBLANES = 8
INV_BLOCK = 16
VMEM_LIMIT = 56 * 1024 * 1024

_OFF_ZS = 0
_OFF_XBC = _OFF_ZS + SSD_WIDTH
_OFF_DT = _OFF_XBC + SSD_CONV_DIM
_OFF_ZG = _OFF_DT + SSD_HEADS
_OFF_QKV = _OFF_ZG + GDN_WIDTH
_OFF_A = _OFF_QKV + GDN_CONV_DIM
_OFF_B = _OFF_A + GDN_HEADS
_SM_DT = 0
_SM_A = SSD_HEADS
_SM_B = SSD_HEADS + GDN_HEADS


def _bdot(a, b):
    return jnp.dot(a.astype(BF16), b.astype(BF16), preferred_element_type=F32)


def _bdot_nt(a, b):
    return lax.dot_general(a.astype(BF16), b.astype(BF16), (((1,), (1,)), ((), ())),
                           preferred_element_type=F32)


def _terms(x, n):
    out = []
    r = x
    for i in range(n):
        t = r.astype(BF16)
        out.append(t)
        if i + 1 < n:
            r = r - t.astype(F32)
    return out


def _const_dot(c, x, n):
    acc = None
    for t in _terms(x, n):
        p = jnp.dot(c, t, preferred_element_type=F32)
        acc = p if acc is None else acc + p
    return acc


def _dot_const(x, c, n):
    acc = None
    for t in _terms(x, n):
        p = jnp.dot(t, c, preferred_element_type=F32)
        acc = p if acc is None else acc + p
    return acc


def _transpose_via_identity(eye, x, n):
    acc = None
    for t in _terms(x, n):
        p = lax.dot_general(eye, t, (((1,), (1,)), ((), ())), preferred_element_type=F32)
        acc = p if acc is None else acc + p
    return acc


def _softplus(x):
    return jnp.maximum(x, 0.0) + jnp.log1p(jnp.exp(-jnp.abs(x)))


def _silu(x):
    return x * jax.nn.sigmoid(x)


def _ada_kernel(c_ref, w_ref, b_ref, o_ref):
    s = _silu(c_ref[...])
    w = w_ref[0]
    s_hi, s_lo = _terms(s, 2)
    w_hi, w_lo = _terms(w, 2)
    acc = jnp.dot(s_hi, w_hi, preferred_element_type=F32)
    acc = acc + (jnp.dot(s_hi, w_lo, preferred_element_type=F32)
                 + jnp.dot(s_lo, w_hi, preferred_element_type=F32))
    o_ref[0] = acc + b_ref[0]


def _ada_call(c_all, w_ada, b_ada):
    rows = c_all.shape[0]
    tn = ADA_DIM // 2
    return pl.pallas_call(
        _ada_kernel,
        grid=(DEPTH, ADA_DIM // tn),
        in_specs=[
            pl.BlockSpec((rows, D_MODEL), lambda l, j: (0, 0)),
            pl.BlockSpec((1, D_MODEL, tn), lambda l, j: (l, 0, j)),
            pl.BlockSpec((1, 1, tn), lambda l, j: (l, 0, j)),
        ],
        out_specs=pl.BlockSpec((1, rows, tn), lambda l, j: (l, 0, j)),
        out_shape=jax.ShapeDtypeStruct((DEPTH, rows, ADA_DIM), F32),
        compiler_params=pltpu.CompilerParams(
            dimension_semantics=("arbitrary", "arbitrary"), vmem_limit_bytes=VMEM_LIMIT),
        name="ada_mod",
    )(c_all, w_ada, b_ada.reshape(DEPTH, 1, ADA_DIM))


CONV_ROWS = 64
CONV_COLS = 128
PROJ_COLS = 512


def _conv_silu_rows(xp, w_ref, b_ref, out, rows, c0, c1, xp_row0=0, out_row0=0):
    rows_per = min(rows, CONV_ROWS)
    for i in range(rows // rows_per):
        r0 = i * rows_per
        for j in range((c1 - c0) // CONV_COLS):
            cols = slice(c0 + j * CONV_COLS, c0 + (j + 1) * CONV_COLS)
            x = xp[xp_row0 + r0:xp_row0 + r0 + rows_per + SUBLANES, cols]
            acc = x * w_ref[0:1, cols]
            for k in range(1, CONV_K):
                acc = pltpu.roll(acc, 1, 0) + x * w_ref[k:k + 1, cols]
            if b_ref is not None:
                acc = acc + b_ref[:, cols]
            out[out_row0 + r0:out_row0 + r0 + rows_per, cols] = _silu(acc[SUBLANES:, :]).astype(out.dtype)


PERM_CHUNK = SUBLANES * SUBLANES


def _qk_l2_scale(col):
    assert GDN_HEAD_DIM == LANES
    if col < GDN_WIDTH:
        return GDN_HEAD_DIM ** -0.5
    if col < 2 * GDN_WIDTH:
        return 1.0
    return None


def _conv_silu_permuted(xp, w_ref, b_ref, hist, up, out, rows, c0, l2_scale=None):
    sub0 = lax.broadcasted_iota(jnp.int32, (SUBLANES, LANES), 0) == 0
    taps = CONV_K - 1
    for j in range(PROJ_COLS // LANES):
        cols = slice(c0 + j * LANES, c0 + (j + 1) * LANES)
        w = [jnp.broadcast_to(w_ref[k:k + 1, cols], (SUBLANES, LANES)) for k in range(CONV_K)]
        b = None if b_ref is None else jnp.broadcast_to(b_ref[:, cols], (SUBLANES, LANES))
        prev = [hist[i * SUBLANES:(i + 1) * SUBLANES, cols] for i in range(taps)]
        for k in range(rows // PERM_CHUNK):
            base = k * PERM_CHUNK
            xs = [xp[base + a * SUBLANES:base + (a + 1) * SUBLANES, cols] for a in range(SUBLANES)]
            rolled = [pltpu.roll(xs[SUBLANES - taps + i], 1, 0) for i in range(taps)]
            ext = [jnp.where(sub0, prev[i], rolled[i]) for i in range(taps)] + xs
            for a in range(SUBLANES):
                y = ext[a] * w[0]
                for t in range(1, CONV_K):
                    y = y + ext[a + t] * w[t]
                if b is not None:
                    y = y + b
                y = _silu(y)
                scale = None if l2_scale is None else l2_scale(c0 + j * LANES)
                if scale is not None:
                    y = y * (lax.rsqrt(jnp.sum(y * y, axis=-1, keepdims=True) + NORM_EPS) * scale)
                up[j, pl.ds(base + a, SUBLANES, stride=SUBLANES), :] = y
            prev = rolled
        for i in range(taps):
            hist[i * SUBLANES:(i + 1) * SUBLANES, cols] = prev[i]
        out[:, cols] = up[j].astype(out.dtype)


def _inproj_kernel(*refs, tm, steps_per_seq, do_conv):
    x_ref, sc_ref, sh_ref, nw_ref, wzs_ref, wxbc_ref, wzg_ref, wqkv_ref, wsh_ref, wsl_ref = refs[:10]
    if do_conv:
        scw_ref, scb_ref, gcw_ref = refs[10:13]
        (zs_ref, xbc_ref, zg_ref, qkv_ref, sm_ref, ncs_ref, ncg_ref,
         xps, xpg, hist_s, hist_g, hs, hp, up) = refs[13:]
    else:
        zs_ref, xbc_ref, zg_ref, qkv_ref, sm_ref = refs[10:]
    x = x_ref[...]
    ms = jnp.mean(x * x, axis=-1, keepdims=True)
    h = x * lax.rsqrt(ms + NORM_EPS) * nw_ref[...]
    h = h * (1.0 + sc_ref[0]) + sh_ref[0]
    h_hi, h_lo = _terms(h, 2)
    sm = jnp.dot(h_hi, wsh_ref[...], preferred_element_type=F32)
    sm = sm + (jnp.dot(h_hi, wsl_ref[...], preferred_element_type=F32)
               + jnp.dot(h_lo, wsh_ref[...], preferred_element_type=F32))
    sm_ref[...] = sm
    if not do_conv:
        zs_ref[...] = jnp.dot(h_hi, wzs_ref[...], preferred_element_type=F32)
        xbc_ref[...] = jnp.dot(h_hi, wxbc_ref[...], preferred_element_type=F32)
        zg_ref[...] = jnp.dot(h_hi, wzg_ref[...], preferred_element_type=F32)
        qkv_ref[...] = jnp.dot(h_hi, wqkv_ref[...], preferred_element_type=F32)
        return

    @pl.when(pl.program_id(0) % steps_per_seq == 0)
    def _zero_history():
        hist_s[...] = jnp.zeros(hist_s.shape, F32)
        hist_g[...] = jnp.zeros(hist_g.shape, F32)

    assert tm % PERM_CHUNK == 0
    ncb = D_MODEL // LANES
    for cb in range(ncb):
        hs[cb] = h[:, cb * LANES:(cb + 1) * LANES]
    for g in range(0, tm // SUBLANES, 2):
        parts = [jnp.concatenate(
            [hs[cb, pl.ds(PERM_CHUNK * (gg // SUBLANES) + gg % SUBLANES, SUBLANES, stride=SUBLANES), :]
             for cb in range(ncb)], axis=1) for gg in (g, g + 1)]
        hp[g * SUBLANES:(g + 2) * SUBLANES, :] = jnp.concatenate(parts, axis=0).astype(BF16)

    conv_blocks = [(xp, w_ref, cw_ref, cb_ref, hist, out, c0)
                   for xp, w_ref, cw_ref, cb_ref, hist, out, width in (
                       (xps, wxbc_ref, scw_ref, scb_ref, hist_s, xbc_ref, SSD_CONV_DIM),
                       (xpg, wqkv_ref, gcw_ref, None, hist_g, qkv_ref, GDN_CONV_DIM))
                   for c0 in range(0, width, PROJ_COLS)]
    gate_blocks = [(w_ref, out, c0) for w_ref, out, width in ((wzs_ref, zs_ref, SSD_WIDTH),
                                                              (wzg_ref, zg_ref, GDN_WIDTH))
                   for c0 in range(0, width, PROJ_COLS)]

    def project(blk):
        xp, w_ref, _, _, _, _, c0 = blk
        cols = slice(c0, c0 + PROJ_COLS)
        xp[:, cols] = jnp.dot(hp[...], w_ref[:, cols], preferred_element_type=F32)

    def gate_block(i):
        w_ref, out, c0 = gate_blocks[i]
        cols = slice(c0, c0 + PROJ_COLS)
        out[:, cols] = _silu(jnp.dot(h_hi, w_ref[:, cols], preferred_element_type=F32)).astype(out.dtype)

    early_gates = len(gate_blocks) // 2
    for i in range(early_gates):
        gate_block(i)
    project(conv_blocks[0])
    for j, blk in enumerate(conv_blocks):
        if j + 1 < len(conv_blocks):
            project(conv_blocks[j + 1])
        if j % 2 == 0 and early_gates + j // 2 < len(gate_blocks):
            gate_block(early_gates + j // 2)
        xp, _, cw_ref, cb_ref, hist, out, c0 = blk
        _conv_silu_permuted(xp, cw_ref, cb_ref, hist, up, out, tm, c0,
                            l2_scale=_qk_l2_scale if xp is xpg else None)
    assert len(gate_blocks) - early_gates <= (len(conv_blocks) + 1) // 2
    for xp, nst in ((xps, ncs_ref), (xpg, ncg_ref)):
        for i in range(CONV_K - 1):
            row = tm - PERM_CHUNK + SUBLANES * (SUBLANES - (CONV_K - 1) + i) + SUBLANES - 1
            nst[0, i:i + 1, :] = xp[row:row + 1, :]


def _row_mod_spec(per_row, tm, steps_per_group):
    if per_row:
        return pl.BlockSpec((1, tm, D_MODEL), lambda i: (0, i, 0))
    return pl.BlockSpec((1, 1, D_MODEL), lambda i: (i // steps_per_group, 0, 0))


def _resident(shape):
    nd = len(shape)
    return pl.BlockSpec(shape, lambda i: (0,) * nd, pipeline_mode=pl.Buffered(1))


def _inproj_call(x2d, scale, shift, norm_w, w, tm, per_row, steps_per_group, do_conv, act_dtype):
    T = x2d.shape[0]
    assert T % tm == 0
    widths = (SSD_WIDTH, SSD_CONV_DIM, GDN_WIDTH, GDN_CONV_DIM, SMALL)
    dtypes = (act_dtype, act_dtype, act_dtype, act_dtype, F32)
    row = lambda wd: pl.BlockSpec((tm, wd), lambda i: (i, 0))
    in_specs = [
        row(D_MODEL),
        _row_mod_spec(per_row, tm, steps_per_group),
        _row_mod_spec(per_row, tm, steps_per_group),
        _resident((1, D_MODEL)),
        _resident((D_MODEL, SSD_WIDTH)),
        _resident((D_MODEL, SSD_CONV_DIM)),
        _resident((D_MODEL, GDN_WIDTH)),
        _resident((D_MODEL, GDN_CONV_DIM)),
        _resident((D_MODEL, SMALL)),
        _resident((D_MODEL, SMALL)),
    ]
    args = [x2d, scale, shift, norm_w, w["zs"], w["xbc"], w["zg"], w["qkv"], w["sm_hi"], w["sm_lo"]]
    out_specs = [row(wd) for wd in widths]
    out_shape = [jax.ShapeDtypeStruct((T, wd), dt) for wd, dt in zip(widths, dtypes)]
    scratch = []
    if do_conv:
        assert not per_row
        nseq = T // (tm * steps_per_group)
        in_specs += [_resident((CONV_K, SSD_CONV_DIM)), _resident((1, SSD_CONV_DIM)),
                     _resident((CONV_K, GDN_CONV_DIM))]
        args += [w["scw"], w["scb"], w["gcw"]]
        for wd in (SSD_CONV_DIM, GDN_CONV_DIM):
            out_specs.append(pl.BlockSpec((1, CONV_K - 1, wd), lambda i: (i // steps_per_group, 0, 0)))
            out_shape.append(jax.ShapeDtypeStruct((nseq, CONV_K - 1, wd), F32))
            scratch.append(pltpu.VMEM((tm, wd), F32))
        scratch += [pltpu.VMEM(((CONV_K - 1) * SUBLANES, wd), F32) for wd in (SSD_CONV_DIM, GDN_CONV_DIM)]
        scratch += [pltpu.VMEM((D_MODEL // LANES, tm, LANES), F32), pltpu.VMEM((tm, D_MODEL), BF16),
                    pltpu.VMEM((PROJ_COLS // LANES, tm, LANES), F32)]
    return pl.pallas_call(
        functools.partial(_inproj_kernel, tm=tm, steps_per_seq=steps_per_group, do_conv=do_conv),
        grid=(T // tm,),
        in_specs=in_specs,
        out_specs=out_specs,
        out_shape=out_shape,
        scratch_shapes=scratch,
        compiler_params=pltpu.CompilerParams(
            dimension_semantics=("arbitrary",), vmem_limit_bytes=VMEM_LIMIT),
        name="in_proj",
    )(*args)


def _outproj_kernel(x_ref, mix_ref, gate_ref, w_ref, fw_ref, o_ref, *, final_norm):
    y = jnp.dot(mix_ref[...], w_ref[...], preferred_element_type=F32)
    out = x_ref[...] + gate_ref[0] * y
    if final_norm:
        ms = jnp.mean(out * out, axis=-1, keepdims=True)
        out = out * lax.rsqrt(ms + NORM_EPS) * fw_ref[...]
    o_ref[...] = out


def _outproj_call(x2d, mix2d, gate, w_out, final_w, tm, per_row, steps_per_group, final_norm):
    T = x2d.shape[0]
    return pl.pallas_call(
        functools.partial(_outproj_kernel, final_norm=final_norm),
        grid=(T // tm,),
        in_specs=[
            pl.BlockSpec((tm, D_MODEL), lambda i: (i, 0)),
            pl.BlockSpec((tm, MIX_WIDTH), lambda i: (i, 0)),
            _row_mod_spec(per_row, tm, steps_per_group),
            _resident((MIX_WIDTH, D_MODEL)),
            _resident((1, D_MODEL)),
        ],
        out_specs=pl.BlockSpec((tm, D_MODEL), lambda i: (i, 0)),
        out_shape=jax.ShapeDtypeStruct((T, D_MODEL), F32),
        compiler_params=pltpu.CompilerParams(
            dimension_semantics=("arbitrary",), vmem_limit_bytes=VMEM_LIMIT),
        name="out_proj",
    )(x2d, mix2d, gate, w_out, final_w)


def _run_interleaved(*gens):
    live = [g for g in gens if g is not None]
    while live:
        for g in list(live):
            try:
                next(g)
            except StopIteration:
                live.remove(g)


def _chain_gens(gens):
    for g in gens:
        yield from g


def _unit_lower_solve_many(m_list, rhs_list, q, out):
    blk = min(INV_BLOCK, q)
    nb = q // blk
    if nb > 1:
        r = lax.broadcasted_iota(jnp.int32, (q, q), 0) // blk
        c = lax.broadcasted_iota(jnp.int32, (q, q), 1) // blk
        same = r == c
        d_list = [jnp.where(same, m, 0.0) for m in m_list]
        o_list = [m - d for m, d in zip(m_list, d_list)]
    else:
        d_list = m_list
    n_list = [-d for d in d_list]
    e_list = d_list
    for _ in range(int(math.log2(blk)) - 1):
        e_list = [_bdot(e, e) for e in e_list]
        yield
        n_list = [n + e + _bdot(n, e) for n, e in zip(n_list, e_list)]
        yield
    y_list = [rhs + _bdot(n, rhs) for n, rhs in zip(n_list, rhs_list)]
    if nb > 1:
        p_list = [o + _bdot(n, o) for n, o in zip(n_list, o_list)]
        yield
        r_list = [-p for p in p_list]
        e_list = p_list
        for _ in range(int(math.ceil(math.log2(nb))) - 1):
            e_list = [_bdot(e, e) for e in e_list]
            yield
            r_list = [rr + e + _bdot(rr, e) for rr, e in zip(r_list, e_list)]
            yield
        y_list = [y + _bdot(rr, y) for rr, y in zip(r_list, y_list)]
    yield
    out.extend(y_list)


def _block_diag_pair(x, lo):
    return jnp.concatenate([jnp.where(lo, x, 0.0), jnp.where(lo, 0.0, x)], axis=0)


def _stack_diag(a, b):
    z = jnp.zeros_like(a)
    return jnp.concatenate([jnp.concatenate([a, z], axis=1), jnp.concatenate([z, b], axis=1)], axis=0)


def _unit_lower_solve_pairs(m_list, rhs_list, q, out):
    blk = min(INV_BLOCK, q)
    nb = q // blk
    rowi = lax.broadcasted_iota(jnp.int32, (q, 2 * q), 0)
    coli = lax.broadcasted_iota(jnp.int32, (q, 2 * q), 1)
    lo = coli < q
    n_half = rhs_list[0].shape[1] // 2
    halves = lambda y: _stack_diag(y[:, 0:n_half], y[:, n_half:2 * n_half])
    if nb > 1:
        same = (rowi // blk) == (jnp.bitwise_and(coli, q - 1) // blk)
        d_list = [jnp.where(same, m, 0.0) for m in m_list]
        o_list = [m - d for m, d in zip(m_list, d_list)]
    else:
        d_list = m_list
    n_list = [-d for d in d_list]
    e_list = d_list
    for _ in range(int(math.log2(blk)) - 1):
        e_list = [_bdot(e, _block_diag_pair(e, lo)) for e in e_list]
        yield
        n_list = [n + e + _bdot(n, _block_diag_pair(e, lo)) for n, e in zip(n_list, e_list)]
        yield
    y_list = [rhs + _bdot(n, halves(rhs)) for n, rhs in zip(n_list, rhs_list)]
    if nb > 1:
        p_list = [o + _bdot(n, _block_diag_pair(o, lo)) for n, o in zip(n_list, o_list)]
        yield
        r_list = [-p for p in p_list]
        e_list = p_list
        for _ in range(int(math.ceil(math.log2(nb))) - 1):
            e_list = [_bdot(e, _block_diag_pair(e, lo)) for e in e_list]
            yield
            r_list = [rr + e + _bdot(rr, _block_diag_pair(e, lo)) for rr, e in zip(r_list, e_list)]
            yield
        y_list = [y + _bdot(rr, halves(y)) for rr, y in zip(r_list, y_list)]
    yield
    out.extend(y_list)


def _mixer_kernel(*refs, names, q, nq, lv, nc, multi_seq, zero_init, do_conv, chains, pack):
    r = dict(zip(names, refs))
    zs_ref, xbc_ref, zg_ref, qkv_ref, sm_ref = r["zs"], r["xbc"], r["zg"], r["qkv"], r["sm"]
    bias_ref, coef_ref, dexp_ref, snw_ref, gnw_ref = r["bias"], r["coef"], r["dexp"], r["snw"], r["gnw"]
    mix_ref, nhs_ref, nsg_ref = r["mix"], r["nhs"], r["nsg"]
    st_s, st_g, y_s, pe_s, xcd_s, bgt_s = r["st_s"], r["st_g"], r["y_s"], r["pe_s"], r["xcd_s"], r["bgt_s"]
    wq_s, u_s, at_s, kdt_s = r["wq_s"], r["u_s"], r["at_s"], r["kdt_s"]

    c_idx = pl.program_id(1)
    rr_ = nq * q
    n_slots = nq if multi_seq else 1
    slot = (lambda c: c) if multi_seq else (lambda c: 0)
    hist = q + SUBLANES
    pair = 2 * SSD_HEAD_DIM
    heads_per_group = SSD_HEADS // SSD_GROUPS
    gw = SSD_WIDTH // SSD_GROUPS
    assert not (do_conv and not multi_seq and nc > 1 and lv != rr_)

    @pl.when(c_idx == 0)
    def _load_state():
        if zero_init:
            st_s[...] = jnp.zeros(st_s.shape, F32)
            st_g[...] = jnp.zeros(st_g.shape, F32)
        else:
            for s in range(n_slots):
                for j in range(SSD_HEADS // 2):
                    blk = r["hs"][0, s, 2 * j:2 * j + 2].reshape(pair, SSD_STATE)
                    st_s[s, :, pair * j:pair * (j + 1)] = blk.T
            st_g[...] = r["sg"][0]
        if do_conv:
            for xp, key, width in ((r["xps"], "cs", SSD_CONV_DIM), (r["xpg"], "cg", GDN_CONV_DIM)):
                for s in range(n_slots):
                    xp[s * hist:s * hist + SUBLANES, :] = jnp.zeros((SUBLANES, width), F32)
                    if not zero_init:
                        xp[s * hist + SUBLANES - (CONV_K - 1):s * hist + SUBLANES, :] = r[key][0, s]

    if do_conv:
        for x_ref, xp, w_ref, b_ref, out, nst, width in (
                (xbc_ref, r["xps"], r["scw"], r["scb"], r["xbc_c"], r["ncs"], SSD_CONV_DIM),
                (qkv_ref, r["xpg"], r["gcw"], None, r["qkv_c"], r["ncg"], GDN_CONV_DIM)):
            if multi_seq:
                for c in range(nq):
                    base = c * hist + SUBLANES
                    xp[base:base + lv, :] = x_ref[c * lv:(c + 1) * lv, :]
                    xp[base + lv:base + q, :] = jnp.zeros((q - lv, width), F32)
                    _conv_silu_rows(xp, w_ref, b_ref, out, q, 0, width, xp_row0=c * hist, out_row0=c * q)
                    nst[0, c] = xp[base + lv - (CONV_K - 1):base + lv, :]
            else:
                xp[SUBLANES:SUBLANES + rr_, :] = x_ref[...]
                _conv_silu_rows(xp, w_ref, b_ref, out, rr_, 0, width)

                @pl.when(c_idx == nc - 1)
                def _store_conv_state():
                    nst[0, 0] = xp[SUBLANES + rr_ - (CONV_K - 1):SUBLANES + rr_, :]

                if nc > 1:
                    xp[0:SUBLANES, :] = xp[rr_:rr_ + SUBLANES, :]
        xbc_c, qkv_c = r["xbc_c"], r["qkv_c"]
        gate = _silu
    else:
        xbc_c, qkv_c = xbc_ref, qkv_ref
        gate = lambda z: z.astype(F32)

    if multi_seq:
        for src, dst, wd in ((zs_ref, r["pzs"], SSD_WIDTH), (zg_ref, r["pzg"], GDN_WIDTH),
                             (sm_ref, r["psm"], SMALL)):
            for c in range(nq):
                dst[c * q:c * q + lv, :] = src[c * lv:(c + 1) * lv, :].astype(F32)
                dst[c * q + lv:(c + 1) * q, :] = jnp.zeros((q - lv, wd), F32)
        zs_v, zg_v, sm = r["pzs"], r["pzg"], r["psm"][...]
    else:
        zs_v, zg_v, sm = zs_ref, zg_ref, sm_ref[...]

    sp = _softplus(sm + bias_ref[...])
    beta_all = jax.nn.sigmoid(sm)
    if multi_seq and lv < q:
        assert q & (q - 1) == 0
        valid = jnp.bitwise_and(lax.broadcasted_iota(jnp.int32, (rr_, SMALL), 0), q - 1) < lv
        sp = jnp.where(valid, sp, 0.0)
        beta_all = jnp.where(valid, beta_all, 0.0)
    ag = sp * coef_ref[...]
    r_all = lax.broadcasted_iota(jnp.int32, (rr_, rr_), 0)
    c_all = lax.broadcasted_iota(jnp.int32, (rr_, rr_), 1)
    chunk_start = (r_all // q) * q
    tril_bd = jnp.where(r_all >= c_all, jnp.where(c_all >= chunk_start, 1.0, 0.0), 0.0).astype(BF16)
    rq = lax.broadcasted_iota(jnp.int32, (q, q), 0)
    cq = lax.broadcasted_iota(jnp.int32, (q, q), 1)
    incl = rq >= cq
    strict = rq > cq
    r128 = lax.broadcasted_iota(jnp.int32, (LANES, LANES), 0)
    c128 = lax.broadcasted_iota(jnp.int32, (LANES, LANES), 1)
    eye = jnp.where(r128 == c128, 1.0, 0.0).astype(BF16)
    cum = _const_dot(tril_bd, ag, 3)
    if q == PERM_CHUNK:
        cum_t, sp_t = cum.T, sp.T
    else:
        cum_t = _transpose_via_identity(eye, cum, 3)
        sp_t = _transpose_via_identity(eye, sp, 3)
    ecum = jnp.exp(cum)
    if nq > 1:
        cum_last = jnp.concatenate(
            [jnp.broadcast_to(cum[(c + 1) * q - 1:(c + 1) * q, :], (q, SMALL)) for c in range(nq)], axis=0)
    else:
        cum_last = cum[q - 1:q, :]
    to_end = jnp.exp(cum_last - cum)
    er = lax.broadcasted_iota(jnp.int32, (SMALL, SSD_WIDTH), 0)
    ec = lax.broadcasted_iota(jnp.int32, (SMALL, SSD_WIDTH), 1) // SSD_HEAD_DIM
    expand = jnp.where(er == ec, 1.0, 0.0).astype(BF16)
    pe_s[...] = _dot_const(ecum, expand, 2)
    p_dtend = _dot_const(sp * to_end, expand, 2)
    xcd_s[...] = (xbc_c[:, 0:SSD_WIDTH].astype(F32) * p_dtend).astype(BF16)

    lane = lax.broadcasted_iota(jnp.int32, (q, pair), 1)
    lo_half = lane < SSD_HEAD_DIM
    scores = {}
    for c in range(nq):
        rows = slice(c * q, (c + 1) * q)
        for g in range(SSD_GROUPS):
            b_g = xbc_c[rows, SSD_WIDTH + g * SSD_STATE:SSD_WIDTH + (g + 1) * SSD_STATE]
            c_g = xbc_c[rows, SSD_WIDTH + (SSD_GROUPS + g) * SSD_STATE:
                        SSD_WIDTH + (SSD_GROUPS + g + 1) * SSD_STATE]
            scores[c, g] = _bdot_nt(c_g, b_g)
            if q == PERM_CHUNK:
                bgt_s[c * SSD_GROUPS + g] = b_g.astype(F32).T.astype(BF16)
            else:
                bgt_s[c * SSD_GROUPS + g] = _transpose_via_identity(eye, b_g, 1).astype(BF16)
    def ssd_diag(items):
        for c in items:
            rows = slice(c * q, (c + 1) * q)
            for j in range(SSD_HEADS // 2):
                g = j // (heads_per_group // 2)
                pcols = slice(pair * j, pair * (j + 1))
                x_pair = xbc_c[rows, pcols].astype(F32)
                y_pair = x_pair * dexp_ref[:, pcols]
                for half in range(2):
                    h = 2 * j + half
                    seg = cum[rows, h:h + 1] - cum_t[h:h + 1, rows]
                    lmat = jnp.exp(jnp.where(incl, seg, -jnp.inf))
                    m_h = scores[c, g] * lmat * sp_t[h:h + 1, rows]
                    x_h = jnp.where(lo_half if half == 0 else jnp.logical_not(lo_half), x_pair, 0.0)
                    y_pair = y_pair + _bdot(m_h, x_h)
                y_s[rows, pcols] = y_pair
                if j % 2 == 1:
                    yield

    def gdn_independent(group):
        m_list, rhs_list = [], []
        for n, (c, h) in enumerate(group):
            rows = slice(c * q, (c + 1) * q)
            i = c * GDN_HEADS + h
            qh = qkv_c[rows, h * GDN_HEAD_DIM:(h + 1) * GDN_HEAD_DIM].astype(F32)
            kh = qkv_c[rows, GDN_WIDTH + h * GDN_HEAD_DIM:GDN_WIDTH + (h + 1) * GDN_HEAD_DIM].astype(F32)
            vh = qkv_c[rows, 2 * GDN_WIDTH + h * GDN_HEAD_DIM:
                       2 * GDN_WIDTH + (h + 1) * GDN_HEAD_DIM].astype(F32)
            if do_conv:
                qh = qh * lax.rsqrt(jnp.sum(qh * qh, axis=-1, keepdims=True) + NORM_EPS) * (GDN_HEAD_DIM ** -0.5)
                kh = kh * lax.rsqrt(jnp.sum(kh * kh, axis=-1, keepdims=True) + NORM_EPS)
            la = _SM_A + h
            bcol = beta_all[rows, _SM_B + h:_SM_B + h + 1]
            eg = ecum[rows, la:la + 1]
            kb = kh * bcol
            dec = jnp.exp(jnp.where(incl, cum[rows, la:la + 1] - cum_t[la:la + 1, rows], -jnp.inf))
            kq = _bdot_nt(jnp.concatenate([kb, qh], axis=0), kh)
            m_list.append(jnp.where(strict, kq[0:q] * dec, 0.0))
            at_s[i] = (kq[q:2 * q] * dec).astype(BF16)
            rhs_list.append(jnp.concatenate([vh * bcol, kb * eg], axis=1))
            wq_s[i, q:2 * q, :] = (qh * eg).astype(BF16)
            kd = kh * to_end[rows, la:la + 1]
            if q == PERM_CHUNK:
                kdt_s[i] = kd.T.astype(BF16)
            else:
                kdt_s[i] = _transpose_via_identity(eye, kd, 1).astype(BF16)
            if n % 4 == 3:
                yield
        sol_list = []
        yield from _unit_lower_solve_many(m_list, rhs_list, q, sol_list)
        for (c, h), sol in zip(group, sol_list):
            i = c * GDN_HEADS + h
            u_s[i] = sol[:, 0:GDN_HEAD_DIM]
            wq_s[i, 0:q, :] = sol[:, GDN_HEAD_DIM:2 * GDN_HEAD_DIM].astype(BF16)
        yield

    def gdn_independent_pairs(group):
        assert not do_conv and GDN_HEAD_DIM == LANES and q == PERM_CHUNK
        hd2 = 2 * GDN_HEAD_DIM
        rowp = lax.broadcasted_iota(jnp.int32, (q, 2 * q), 0)
        colp = lax.broadcasted_iota(jnp.int32, (q, 2 * q), 1)
        colm = jnp.bitwise_and(colp, q - 1)
        lo_q = colp < q
        incl2 = rowp >= colm
        strict2 = rowp > colm
        lo_h = lax.broadcasted_iota(jnp.int32, (q, hd2), 1) < GDN_HEAD_DIM
        sel = lambda a, b: jnp.where(lo_h, a, b)
        m_list, rhs_list = [], []
        heads = sorted({(c, h // 2) for c, h in group})
        for n, (c, p2) in enumerate(heads):
            rows = slice(c * q, (c + 1) * q)
            h1, h2 = 2 * p2, 2 * p2 + 1
            la1, la2 = _SM_A + h1, _SM_A + h2
            q_pair = qkv_c[rows, hd2 * p2:hd2 * (p2 + 1)].astype(F32)
            k_pair = qkv_c[rows, GDN_WIDTH + hd2 * p2:GDN_WIDTH + hd2 * (p2 + 1)].astype(F32)
            v_pair = qkv_c[rows, 2 * GDN_WIDTH + hd2 * p2:2 * GDN_WIDTH + hd2 * (p2 + 1)].astype(F32)
            bsel = sel(beta_all[rows, _SM_B + h1:_SM_B + h1 + 1], beta_all[rows, _SM_B + h2:_SM_B + h2 + 1])
            egsel = sel(ecum[rows, la1:la1 + 1], ecum[rows, la2:la2 + 1])
            tesel = sel(to_end[rows, la1:la1 + 1], to_end[rows, la2:la2 + 1])
            kb_pair = k_pair * bsel
            lhs = jnp.concatenate([kb_pair, q_pair], axis=0)
            kq = _bdot_nt(lhs, _block_diag_pair(k_pair, lo_h))
            seg = (jnp.where(lo_q, cum[rows, la1:la1 + 1], cum[rows, la2:la2 + 1])
                   - jnp.concatenate([cum_t[la1:la1 + 1, rows], cum_t[la2:la2 + 1, rows]], axis=1))
            dec = jnp.exp(jnp.where(incl2, seg, -jnp.inf))
            m_list.append(jnp.where(strict2, kq[0:q] * dec, 0.0))
            at_s[c * (GDN_HEADS // 2) + p2] = (kq[q:2 * q] * dec).astype(BF16)
            vb_pair = v_pair * bsel
            kbe_pair = kb_pair * egsel
            rhs_list.append(jnp.concatenate(
                [vb_pair[:, 0:GDN_HEAD_DIM], kbe_pair[:, 0:GDN_HEAD_DIM],
                 vb_pair[:, GDN_HEAD_DIM:hd2], kbe_pair[:, GDN_HEAD_DIM:hd2]], axis=1))
            qd_pair = (q_pair * egsel).astype(BF16)
            kd_pair = k_pair * tesel
            for hh, h in ((0, h1), (1, h2)):
                i = c * GDN_HEADS + h
                hc = slice(hh * GDN_HEAD_DIM, (hh + 1) * GDN_HEAD_DIM)
                wq_s[i, q:2 * q, :] = qd_pair[:, hc]
                kdt_s[i] = kd_pair[:, hc].T.astype(BF16)
            if n % 2 == 1:
                yield
        sol_list = []
        yield from _unit_lower_solve_pairs(m_list, rhs_list, q, sol_list)
        for (c, p2), sol in zip(heads, sol_list):
            for hh in range(2):
                i = c * GDN_HEADS + 2 * p2 + hh
                u_s[i] = sol[:, hh * hd2:hh * hd2 + GDN_HEAD_DIM]
                wq_s[i, 0:q, :] = sol[:, hh * hd2 + GDN_HEAD_DIM:(hh + 1) * hd2].astype(BF16)
        yield

    def store_mix(c, cols, val):
        if multi_seq:
            mix_ref[c * lv:(c + 1) * lv, cols] = val[0:lv].astype(mix_ref.dtype)
        else:
            mix_ref[c * q:(c + 1) * q, cols] = val.astype(mix_ref.dtype)

    def recurrences(items):
        for c in items:
            rows = slice(c * q, (c + 1) * q)
            last = (c + 1) * q - 1
            s = slot(c)
            for g in range(SSD_GROUPS):
                gcols = slice(g * gw, (g + 1) * gw)
                c_g = xbc_c[rows, SSD_WIDTH + (SSD_GROUPS + g) * SSD_STATE:
                            SSD_WIDTH + (SSD_GROUPS + g + 1) * SSD_STATE]
                state = st_s[s, :, gcols]
                y_g = y_s[rows, gcols] + _bdot(c_g, state) * pe_s[rows, gcols]
                st_s[s, :, gcols] = state * pe_s[last:last + 1, gcols] + jnp.dot(
                    bgt_s[c * SSD_GROUPS + g], xcd_s[rows, gcols], preferred_element_type=F32)
                y_g = y_g * gate(zs_v[rows, gcols])
                ms = jnp.mean(y_g * y_g, axis=-1, keepdims=True)
                y_g = y_g * lax.rsqrt(ms + NORM_EPS) * snw_ref[:, gcols]
                store_mix(c, gcols, y_g)
            yield
        pairs = [(c, h) for c in items for h in range(GDN_HEADS)]
        ws = {}
        for (c, h) in pairs:
            ws[c, h] = jnp.dot(wq_s[c * GDN_HEADS + h], st_g[slot(c), h].astype(BF16),
                               preferred_element_type=F32)
        yield
        vn = {(c, h): (u_s[c * GDN_HEADS + h] - ws[c, h][0:q]).astype(BF16) for (c, h) in pairs}
        av = {}
        if pack:
            for c in items:
                for p2 in range(GDN_HEADS // 2):
                    both = jnp.dot(at_s[c * (GDN_HEADS // 2) + p2],
                                   _stack_diag(vn[c, 2 * p2], vn[c, 2 * p2 + 1]), preferred_element_type=F32)
                    av[c, 2 * p2] = both[:, 0:GDN_HEAD_DIM]
                    av[c, 2 * p2 + 1] = both[:, GDN_HEAD_DIM:2 * GDN_HEAD_DIM]
        for (c, h) in pairs:
            i = c * GDN_HEADS + h
            la = _SM_A + h
            last = (c + 1) * q - 1
            s = slot(c)
            if not pack:
                av[c, h] = jnp.dot(at_s[i], vn[c, h], preferred_element_type=F32)
            o = ws[c, h][q:2 * q] + av[c, h]
            st_g[s, h] = st_g[s, h] * ecum[last:last + 1, la:la + 1] + jnp.dot(
                kdt_s[i], vn[c, h], preferred_element_type=F32)
            hc = slice(h * GDN_HEAD_DIM, (h + 1) * GDN_HEAD_DIM)
            ms = jnp.mean(o * o, axis=-1, keepdims=True)
            o = o * lax.rsqrt(ms + NORM_EPS) * gnw_ref[...] * gate(zg_v[c * q:(c + 1) * q, hc])
            store_mix(c, slice(SSD_WIDTH + h * GDN_HEAD_DIM, SSD_WIDTH + (h + 1) * GDN_HEAD_DIM), o)
        yield

    all_ch = [(c, h) for c in range(nq) for h in range(GDN_HEADS)]
    groups = [all_ch[i:i + chains] for i in range(0, len(all_ch), chains)]
    batches = [list(range(nq))] if multi_seq else [[c] for c in range(nq)]
    assert chains % GDN_HEADS == 0
    pending = list(batches)
    done_items = set()
    if multi_seq:
        for _ in ssd_diag(range(nq)):
            pass
    for grp in groups:
        runnable = [b for b in pending if set(b) <= done_items]
        for b in runnable:
            pending.remove(b)
        others = [_chain_gens([recurrences(b) for b in runnable])] if runnable else []
        if not multi_seq:
            others.append(ssd_diag(sorted({c for c, _ in grp})))
        _run_interleaved((gdn_independent_pairs if pack else gdn_independent)(grp), *others)
        done_items |= {c for c, _ in grp}
    for b in pending:
        for _ in recurrences(b):
            pass

    @pl.when(c_idx == nc - 1)
    def _store_state():
        for s in range(n_slots):
            for j in range(SSD_HEADS // 2):
                blk = st_s[s, :, pair * j:pair * (j + 1)].T
                nhs_ref[0, s, 2 * j:2 * j + 2] = blk.reshape(2, SSD_HEAD_DIM, SSD_STATE)
        nsg_ref[0] = st_g[...]


_STATE_TAILS = (
    (CONV_K - 1, SSD_CONV_DIM),
    (SSD_HEADS, SSD_HEAD_DIM, SSD_STATE),
    (CONV_K - 1, GDN_CONV_DIM),
    (GDN_HEADS, GDN_HEAD_DIM, GDN_HEAD_DIM),
)


_STATE_NAMES = ("cs", "hs", "cg", "sg")


def _mixer_call(zs, xbc, zg, qkv, sm, B, L, states_in, prev_out, p, q, nq, multi_seq, layer, chains, do_conv):
    assert zs.shape[0] == B * L
    rows = nq * q
    assert q % min(INV_BLOCK, q) == 0 and q & (q - 1) == 0
    pack = not do_conv and q == PERM_CHUNK
    zero_init = states_in is None
    if multi_seq:
        assert CONV_K - 1 <= L <= q and B % nq == 0 and do_conv
        nc, lv, bb = 1, L, nq
        tok = lambda wd: pl.BlockSpec((nq * L, wd), lambda b, c: (b, 0))
    else:
        assert L % rows == 0
        nc, lv, bb = L // rows, rows, 1
        tok = lambda wd: pl.BlockSpec((rows, wd), lambda b, c: (b * nc + c, 0))
    n_slots = bb
    per_b = lambda shape: pl.BlockSpec((1, bb) + shape, lambda b, c: (layer, b) + (0,) * len(shape))
    const = lambda shape: pl.BlockSpec(shape, lambda b, c: (0,) * len(shape))
    nch = nq * GDN_HEADS
    kept = [k for k in range(4) if do_conv or k in (1, 3)]

    names = ["zs", "xbc", "zg", "qkv", "sm"]
    in_specs = [tok(SSD_WIDTH), tok(SSD_CONV_DIM), tok(GDN_WIDTH), tok(GDN_CONV_DIM), tok(SMALL)]
    args = [zs, xbc, zg, qkv, sm]
    if not zero_init:
        for k in kept:
            names.append(_STATE_NAMES[k])
            in_specs.append(per_b(_STATE_TAILS[k]))
            args.append(states_in[k])
    consts = [("bias", (1, SMALL)), ("coef", (1, SMALL)), ("dexp", (1, SSD_WIDTH)), ("snw", (1, SSD_WIDTH)),
              ("gnw", (1, GDN_HEAD_DIM))]
    if do_conv:
        consts += [("scw", (CONV_K, SSD_CONV_DIM)), ("scb", (1, SSD_CONV_DIM)), ("gcw", (CONV_K, GDN_CONV_DIM))]
    for nm, shape in consts:
        names.append(nm)
        in_specs.append(const(shape))
        args.append(p[nm])
    aliases = {}
    if prev_out is not None:
        for k, a in enumerate(prev_out):
            names.append("alias%d" % k)
            aliases[len(args)] = 1 + k
            in_specs.append(pl.BlockSpec(memory_space=pl.ANY))
            args.append(a)
    names += ["mix"] + ["n" + _STATE_NAMES[k] for k in kept]
    out_specs = [tok(MIX_WIDTH)] + [per_b(_STATE_TAILS[k]) for k in kept]
    out_shape = [jax.ShapeDtypeStruct((B * L, MIX_WIDTH), BF16)] + [
        jax.ShapeDtypeStruct((DEPTH, B) + _STATE_TAILS[k], F32) for k in kept]
    scratch = [
        ("st_s", pltpu.VMEM((n_slots, SSD_STATE, SSD_WIDTH), F32)),
        ("st_g", pltpu.VMEM((n_slots, GDN_HEADS, GDN_HEAD_DIM, GDN_HEAD_DIM), F32)),
        ("y_s", pltpu.VMEM((rows, SSD_WIDTH), F32)),
        ("pe_s", pltpu.VMEM((rows, SSD_WIDTH), F32)),
        ("xcd_s", pltpu.VMEM((rows, SSD_WIDTH), BF16)),
        ("bgt_s", pltpu.VMEM((nq * SSD_GROUPS, SSD_STATE, q), BF16)),
        ("wq_s", pltpu.VMEM((nch, 2 * q, GDN_HEAD_DIM), BF16)),
        ("u_s", pltpu.VMEM((nch, q, GDN_HEAD_DIM), F32)),
        ("at_s", pltpu.VMEM((nch // 2, q, 2 * q) if pack else (nch, q, q), BF16)),
        ("kdt_s", pltpu.VMEM((nch, GDN_HEAD_DIM, q), BF16)),
    ]
    if do_conv:
        xp_rows = nq * (q + SUBLANES) if multi_seq else rows + SUBLANES
        scratch += [
            ("xps", pltpu.VMEM((xp_rows, SSD_CONV_DIM), F32)),
            ("xpg", pltpu.VMEM((xp_rows, GDN_CONV_DIM), F32)),
            ("xbc_c", pltpu.VMEM((rows, SSD_CONV_DIM), F32)),
            ("qkv_c", pltpu.VMEM((rows, GDN_CONV_DIM), F32)),
        ]
    if multi_seq:
        scratch += [
            ("pzs", pltpu.VMEM((rows, SSD_WIDTH), F32)),
            ("pzg", pltpu.VMEM((rows, GDN_WIDTH), F32)),
            ("psm", pltpu.VMEM((rows, SMALL), F32)),
        ]
    names += [nm for nm, _ in scratch]
    return pl.pallas_call(
        functools.partial(_mixer_kernel, names=tuple(names), q=q, nq=nq, lv=lv, nc=nc, multi_seq=multi_seq,
                          zero_init=zero_init, do_conv=do_conv, chains=chains, pack=pack),
        grid=(B // bb, nc),
        in_specs=in_specs,
        out_specs=out_specs,
        out_shape=out_shape,
        input_output_aliases=aliases,
        scratch_shapes=[s for _, s in scratch],
        compiler_params=pltpu.CompilerParams(
            dimension_semantics=("arbitrary", "arbitrary"), vmem_limit_bytes=VMEM_LIMIT),
        name="mixer",
    )(*args)


def _prep_layer(l, norm_w, w_in, ssd_conv_w, ssd_conv_b, ssd_dt_bias, ssd_a_log, ssd_d, ssd_norm_w,
                gdn_conv_w, gdn_dt_bias, gdn_a_log, gdn_norm_w, w_out):
    wi = w_in[l]
    small = jnp.concatenate(
        [wi[:, _OFF_DT:_OFF_DT + SSD_HEADS], wi[:, _OFF_A:_OFF_A + GDN_HEADS], wi[:, _OFF_B:_OFF_B + GDN_HEADS],
         jnp.zeros((D_MODEL, SMALL - SSD_HEADS - 2 * GDN_HEADS), F32)], axis=1)
    sm_hi = small.astype(BF16)
    sm_lo = (small - sm_hi.astype(F32)).astype(BF16)
    pad = jnp.zeros((SMALL - SSD_HEADS - GDN_HEADS,), F32)
    return {
        "norm_w": norm_w[l].reshape(1, D_MODEL),
        "zs": wi[:, _OFF_ZS:_OFF_ZS + SSD_WIDTH].astype(BF16),
        "xbc": wi[:, _OFF_XBC:_OFF_XBC + SSD_CONV_DIM].astype(BF16),
        "zg": wi[:, _OFF_ZG:_OFF_ZG + GDN_WIDTH].astype(BF16),
        "qkv": wi[:, _OFF_QKV:_OFF_QKV + GDN_CONV_DIM].astype(BF16),
        "sm_hi": sm_hi,
        "sm_lo": sm_lo,
        "scw": ssd_conv_w[l],
        "scb": ssd_conv_b[l].reshape(1, SSD_CONV_DIM),
        "gcw": gdn_conv_w[l],
        "bias": jnp.concatenate([ssd_dt_bias[l], gdn_dt_bias[l], pad]).reshape(1, SMALL),
        "coef": jnp.concatenate([-jnp.exp(ssd_a_log[l]), -jnp.exp(gdn_a_log[l]), pad]).reshape(1, SMALL),
        "dexp": jnp.repeat(ssd_d[l], SSD_HEAD_DIM).reshape(1, SSD_WIDTH),
        "snw": ssd_norm_w[l].reshape(1, SSD_WIDTH),
        "gnw": gdn_norm_w[l].reshape(1, GDN_HEAD_DIM),
        "w_out": w_out[l].astype(BF16),
    }


def _trunk(x, mod, states, params, final_w, q, nq, tm, tm_out, chains, act_dtype):
    B, L, _ = x.shape
    T = B * L
    per_row = L < SUBLANES
    steps_per_group = max(L // tm, 1)
    x2d = x.reshape(T, D_MODEL)
    conv_in_proj = states is None and not per_row
    new_states = None
    conv_states = ([], [])
    for l in range(DEPTH):
        p = params[l]
        shift, scale, gate = (mod[l, :, i * D_MODEL:(i + 1) * D_MODEL] for i in range(3))
        if per_row:
            rep = lambda a: jnp.repeat(a, L, axis=0).reshape(1, T, D_MODEL)
        else:
            rep = lambda a: a.reshape(B, 1, D_MODEL)
        zs, xbc, zg, qkv, sm, *cst = _inproj_call(x2d, rep(scale), rep(shift), p["norm_w"], p, tm, per_row,
                                                  steps_per_group, conv_in_proj, act_dtype)
        mix, *new_states = _mixer_call(zs, xbc, zg, qkv, sm, B, L, states, new_states, p,
                                       q, nq, per_row, l, chains, not conv_in_proj)
        x2d = _outproj_call(x2d, mix, rep(gate), p["w_out"], final_w, tm_out, per_row,
                            max(L // tm_out, 1), final_norm=(l == DEPTH - 1))
        for lst, s in zip(conv_states, cst):
            lst.append(s)
    if conv_in_proj:
        new_states = [jnp.stack(conv_states[0]), new_states[0], jnp.stack(conv_states[1]), new_states[1]]
    return x2d.reshape(B, L, D_MODEL), new_states


def kernel(x_prompt, x_sample, state_ssd_conv, state_ssm, state_gdn_conv, state_gdn, c_prompt, c_sample,
           norm_w, w_ada, b_ada, w_in, ssd_conv_w, ssd_conv_b, ssd_dt_bias, ssd_a_log, ssd_d, ssd_norm_w,
           gdn_conv_w, gdn_dt_bias, gdn_a_log, gdn_norm_w, w_out, final_norm_w):
    bp = x_prompt.shape[0]
    params = [_prep_layer(l, norm_w, w_in, ssd_conv_w, ssd_conv_b, ssd_dt_bias, ssd_a_log, ssd_d, ssd_norm_w,
                          gdn_conv_w, gdn_dt_bias, gdn_a_log, gdn_norm_w, w_out) for l in range(DEPTH)]
    final_w = final_norm_w.reshape(1, D_MODEL)
    mod = _ada_call(jnp.concatenate([c_prompt, c_sample], axis=0), w_ada, b_ada)
    y_p, sp = _trunk(x_prompt, mod[:, :bp], None, params, final_w, q=64, nq=4, tm=512, tm_out=1024, chains=16,
                     act_dtype=BF16)
    ts = x_sample.shape[0] * x_sample.shape[1]
    y_s, ss = _trunk(x_sample, mod[:, bp:], (state_ssd_conv, state_ssm, state_gdn_conv, state_gdn),
                     params, final_w, q=8, nq=8, tm=ts, tm_out=ts, chains=16, act_dtype=F32)
    return (y_p, y_s, sp[0], sp[1], sp[2], sp[3], ss[0], ss[1], ss[2], ss[3])
```

```python
import functools
import math

import jax
import jax.numpy as jnp
from jax import lax
from jax.experimental import pallas as pl
from jax.experimental.pallas import tpu as pltpu

F32 = jnp.float32
BF16 = jnp.bfloat16

D_MODEL = 1024
DEPTH = 2
SSD_WIDTH = 1024
SSD_HEAD_DIM = 64
SSD_HEADS = 16
SSD_GROUPS = 2
SSD_STATE = 128
SSD_CONV_DIM = SSD_WIDTH + 2 * SSD_GROUPS * SSD_STATE
GDN_WIDTH = 1024
GDN_HEAD_DIM = 128
GDN_HEADS = 8
GDN_CONV_DIM = 3 * GDN_WIDTH
MIX_WIDTH = SSD_WIDTH + GDN_WIDTH
CONV_K = 4
NORM_EPS = 1e-6
ADA_DIM = 3 * D_MODEL
SMALL = 128
LANES = 128
SUBLANES = 8
INV_BLOCK = 16
VMEM_LIMIT = 56 * 1024 * 1024

_OFF_ZS = 0
_OFF_XBC = _OFF_ZS + SSD_WIDTH
_OFF_DT = _OFF_XBC + SSD_CONV_DIM
_OFF_ZG = _OFF_DT + SSD_HEADS
_OFF_QKV = _OFF_ZG + GDN_WIDTH
_OFF_A = _OFF_QKV + GDN_CONV_DIM
_OFF_B = _OFF_A + GDN_HEADS
_SM_DT = 0
_SM_A = SSD_HEADS
_SM_B = SSD_HEADS + GDN_HEADS


def _bdot(a, b):
    return jnp.dot(a.astype(BF16), b.astype(BF16), preferred_element_type=F32)


def _bdot_nt(a, b):
    return lax.dot_general(a.astype(BF16), b.astype(BF16), (((1,), (1,)), ((), ())),
                           preferred_element_type=F32)


def _terms(x, n):
    out = []
    r = x
    for i in range(n):
        t = r.astype(BF16)
        out.append(t)
        if i + 1 < n:
            r = r - t.astype(F32)
    return out


def _const_dot(c, x, n):
    acc = None
    for t in _terms(x, n):
        p = jnp.dot(c, t, preferred_element_type=F32)
        acc = p if acc is None else acc + p
    return acc


def _dot_const(x, c, n):
    acc = None
    for t in _terms(x, n):
        p = jnp.dot(t, c, preferred_element_type=F32)
        acc = p if acc is None else acc + p
    return acc


def _transpose_via_identity(eye, x, n):
    acc = None
    for t in _terms(x, n):
        p = lax.dot_general(eye, t, (((1,), (1,)), ((), ())), preferred_element_type=F32)
        acc = p if acc is None else acc + p
    return acc


def _softplus(x):
    return jnp.maximum(x, 0.0) + jnp.log1p(jnp.exp(-jnp.abs(x)))


def _silu(x):
    return x * jax.nn.sigmoid(x)


def _ada_kernel(c_ref, w_ref, b_ref, o_ref):
    s = _silu(c_ref[...])
    w = w_ref[0]
    s_hi, s_lo = _terms(s, 2)
    w_hi, w_lo = _terms(w, 2)
    acc = jnp.dot(s_hi, w_hi, preferred_element_type=F32)
    acc = acc + (jnp.dot(s_hi, w_lo, preferred_element_type=F32)
                 + jnp.dot(s_lo, w_hi, preferred_element_type=F32))
    o_ref[0] = acc + b_ref[0]


def _ada_call(c_all, w_ada, b_ada):
    rows = c_all.shape[0]
    tn = ADA_DIM // 2
    return pl.pallas_call(
        _ada_kernel,
        grid=(DEPTH, ADA_DIM // tn),
        in_specs=[
            pl.BlockSpec((rows, D_MODEL), lambda l, j: (0, 0)),
            pl.BlockSpec((1, D_MODEL, tn), lambda l, j: (l, 0, j)),
            pl.BlockSpec((1, 1, tn), lambda l, j: (l, 0, j)),
        ],
        out_specs=pl.BlockSpec((1, rows, tn), lambda l, j: (l, 0, j)),
        out_shape=jax.ShapeDtypeStruct((DEPTH, rows, ADA_DIM), F32),
        compiler_params=pltpu.CompilerParams(
            dimension_semantics=("arbitrary", "arbitrary"), vmem_limit_bytes=VMEM_LIMIT),
        name="ada_mod",
    )(c_all, w_ada, b_ada.reshape(DEPTH, 1, ADA_DIM))


CONV_ROWS = 64
CONV_COLS = 128
PROJ_COLS = 512


def _conv_silu_rows(xp, w_ref, b_ref, out, rows, c0, c1, xp_row0=0, out_row0=0):
    rows_per = min(rows, CONV_ROWS)
    for i in range(rows // rows_per):
        r0 = i * rows_per
        for j in range((c1 - c0) // CONV_COLS):
            cols = slice(c0 + j * CONV_COLS, c0 + (j + 1) * CONV_COLS)
            x = xp[xp_row0 + r0:xp_row0 + r0 + rows_per + SUBLANES, cols]
            acc = x * w_ref[0:1, cols]
            for k in range(1, CONV_K):
                acc = pltpu.roll(acc, 1, 0) + x * w_ref[k:k + 1, cols]
            if b_ref is not None:
                acc = acc + b_ref[:, cols]
            out[out_row0 + r0:out_row0 + r0 + rows_per, cols] = _silu(acc[SUBLANES:, :]).astype(out.dtype)


PERM_CHUNK = SUBLANES * SUBLANES


def _qk_l2_scale(col):
    assert GDN_HEAD_DIM == LANES
    if col < GDN_WIDTH:
        return GDN_HEAD_DIM ** -0.5
    if col < 2 * GDN_WIDTH:
        return 1.0
    return None


def _conv_silu_permuted(xp, w_ref, b_ref, hist, up, out, rows, c0, l2_scale=None):
    sub0 = lax.broadcasted_iota(jnp.int32, (SUBLANES, LANES), 0) == 0
    taps = CONV_K - 1
    for j in range(PROJ_COLS // LANES):
        cols = slice(c0 + j * LANES, c0 + (j + 1) * LANES)
        w = [jnp.broadcast_to(w_ref[k:k + 1, cols], (SUBLANES, LANES)) for k in range(CONV_K)]
        b = None if b_ref is None else jnp.broadcast_to(b_ref[:, cols], (SUBLANES, LANES))
        prev = [hist[i * SUBLANES:(i + 1) * SUBLANES, cols] for i in range(taps)]
        for k in range(rows // PERM_CHUNK):
            base = k * PERM_CHUNK
            xs = [xp[base + a * SUBLANES:base + (a + 1) * SUBLANES, cols] for a in range(SUBLANES)]
            rolled = [pltpu.roll(xs[SUBLANES - taps + i], 1, 0) for i in range(taps)]
            ext = [jnp.where(sub0, prev[i], rolled[i]) for i in range(taps)] + xs
            for a in range(SUBLANES):
                y = ext[a] * w[0]
                for t in range(1, CONV_K):
                    y = y + ext[a + t] * w[t]
                if b is not None:
                    y = y + b
                y = _silu(y)
                scale = None if l2_scale is None else l2_scale(c0 + j * LANES)
                if scale is not None:
                    y = y * (lax.rsqrt(jnp.sum(y * y, axis=-1, keepdims=True) + NORM_EPS) * scale)
                up[j, pl.ds(base + a, SUBLANES, stride=SUBLANES), :] = y
            prev = rolled
        for i in range(taps):
            hist[i * SUBLANES:(i + 1) * SUBLANES, cols] = prev[i]
        out[:, cols] = up[j].astype(out.dtype)


def _inproj_kernel(*refs, tm, steps_per_seq, do_conv):
    x_ref, sc_ref, sh_ref, nw_ref, wzs_ref, wxbc_ref, wzg_ref, wqkv_ref, wsh_ref, wsl_ref = refs[:10]
    if do_conv:
        scw_ref, scb_ref, gcw_ref = refs[10:13]
        (zs_ref, xbc_ref, zg_ref, qkv_ref, sm_ref, ncs_ref, ncg_ref,
         xps, xpg, hist_s, hist_g, hs, hp, up) = refs[13:]
    else:
        zs_ref, xbc_ref, zg_ref, qkv_ref, sm_ref = refs[10:]
    x = x_ref[...]
    ms = jnp.mean(x * x, axis=-1, keepdims=True)
    h = x * lax.rsqrt(ms + NORM_EPS) * nw_ref[...]
    h = h * (1.0 + sc_ref[0]) + sh_ref[0]
    h_hi, h_lo = _terms(h, 2)
    sm = jnp.dot(h_hi, wsh_ref[...], preferred_element_type=F32)
    sm = sm + (jnp.dot(h_hi, wsl_ref[...], preferred_element_type=F32)
               + jnp.dot(h_lo, wsh_ref[...], preferred_element_type=F32))
    sm_ref[...] = sm
    if not do_conv:
        zs_ref[...] = jnp.dot(h_hi, wzs_ref[...], preferred_element_type=F32)
        xbc_ref[...] = jnp.dot(h_hi, wxbc_ref[...], preferred_element_type=F32)
        zg_ref[...] = jnp.dot(h_hi, wzg_ref[...], preferred_element_type=F32)
        qkv_ref[...] = jnp.dot(h_hi, wqkv_ref[...], preferred_element_type=F32)
        return

    @pl.when(pl.program_id(0) % steps_per_seq == 0)
    def _zero_history():
        hist_s[...] = jnp.zeros(hist_s.shape, F32)
        hist_g[...] = jnp.zeros(hist_g.shape, F32)

    assert tm % PERM_CHUNK == 0
    ncb = D_MODEL // LANES
    for cb in range(ncb):
        hs[cb] = h[:, cb * LANES:(cb + 1) * LANES]
    for g in range(0, tm // SUBLANES, 2):
        parts = [jnp.concatenate(
            [hs[cb, pl.ds(PERM_CHUNK * (gg // SUBLANES) + gg % SUBLANES, SUBLANES, stride=SUBLANES), :]
             for cb in range(ncb)], axis=1) for gg in (g, g + 1)]
        hp[g * SUBLANES:(g + 2) * SUBLANES, :] = jnp.concatenate(parts, axis=0).astype(BF16)

    conv_blocks = [(xp, w_ref, cw_ref, cb_ref, hist, out, c0)
                   for xp, w_ref, cw_ref, cb_ref, hist, out, width in (
                       (xps, wxbc_ref, scw_ref, scb_ref, hist_s, xbc_ref, SSD_CONV_DIM),
                       (xpg, wqkv_ref, gcw_ref, None, hist_g, qkv_ref, GDN_CONV_DIM))
                   for c0 in range(0, width, PROJ_COLS)]
    gate_blocks = [(w_ref, out, c0) for w_ref, out, width in ((wzs_ref, zs_ref, SSD_WIDTH),
                                                              (wzg_ref, zg_ref, GDN_WIDTH))
                   for c0 in range(0, width, PROJ_COLS)]

    def project(blk):
        xp, w_ref, _, _, _, _, c0 = blk
        cols = slice(c0, c0 + PROJ_COLS)
        xp[:, cols] = jnp.dot(hp[...], w_ref[:, cols], preferred_element_type=F32)

    def gate_block(i):
        w_ref, out, c0 = gate_blocks[i]
        cols = slice(c0, c0 + PROJ_COLS)
        out[:, cols] = _silu(jnp.dot(h_hi, w_ref[:, cols], preferred_element_type=F32)).astype(out.dtype)

    early_gates = len(gate_blocks) // 2
    for i in range(early_gates):
        gate_block(i)
    project(conv_blocks[0])
    for j, blk in enumerate(conv_blocks):
        if j + 1 < len(conv_blocks):
            project(conv_blocks[j + 1])
        if j % 2 == 0 and early_gates + j // 2 < len(gate_blocks):
            gate_block(early_gates + j // 2)
        xp, _, cw_ref, cb_ref, hist, out, c0 = blk
        _conv_silu_permuted(xp, cw_ref, cb_ref, hist, up, out, tm, c0,
                            l2_scale=_qk_l2_scale if xp is xpg else None)
    assert len(gate_blocks) - early_gates <= (len(conv_blocks) + 1) // 2
    for xp, nst in ((xps, ncs_ref), (xpg, ncg_ref)):
        for i in range(CONV_K - 1):
            row = tm - PERM_CHUNK + SUBLANES * (SUBLANES - (CONV_K - 1) + i) + SUBLANES - 1
            nst[0, i:i + 1, :] = xp[row:row + 1, :]


def _row_mod_spec(per_row, tm, steps_per_group):
    if per_row:
        return pl.BlockSpec((1, tm, D_MODEL), lambda i: (0, i, 0))
    return pl.BlockSpec((1, 1, D_MODEL), lambda i: (i // steps_per_group, 0, 0))


def _resident(shape):
    nd = len(shape)
    return pl.BlockSpec(shape, lambda i: (0,) * nd, pipeline_mode=pl.Buffered(1))


def _inproj_call(x2d, scale, shift, norm_w, w, tm, per_row, steps_per_group, do_conv, act_dtype):
    T = x2d.shape[0]
    assert T % tm == 0
    widths = (SSD_WIDTH, SSD_CONV_DIM, GDN_WIDTH, GDN_CONV_DIM, SMALL)
    dtypes = (act_dtype, act_dtype, act_dtype, act_dtype, F32)
    row = lambda wd: pl.BlockSpec((tm, wd), lambda i: (i, 0))
    in_specs = [
        row(D_MODEL),
        _row_mod_spec(per_row, tm, steps_per_group),
        _row_mod_spec(per_row, tm, steps_per_group),
        _resident((1, D_MODEL)),
        _resident((D_MODEL, SSD_WIDTH)),
        _resident((D_MODEL, SSD_CONV_DIM)),
        _resident((D_MODEL, GDN_WIDTH)),
        _resident((D_MODEL, GDN_CONV_DIM)),
        _resident((D_MODEL, SMALL)),
        _resident((D_MODEL, SMALL)),
    ]
    args = [x2d, scale, shift, norm_w, w["zs"], w["xbc"], w["zg"], w["qkv"], w["sm_hi"], w["sm_lo"]]
    out_specs = [row(wd) for wd in widths]
    out_shape = [jax.ShapeDtypeStruct((T, wd), dt) for wd, dt in zip(widths, dtypes)]
    scratch = []
    if do_conv:
        assert not per_row
        nseq = T // (tm * steps_per_group)
        in_specs += [_resident((CONV_K, SSD_CONV_DIM)), _resident((1, SSD_CONV_DIM)),
                     _resident((CONV_K, GDN_CONV_DIM))]
        args += [w["scw"], w["scb"], w["gcw"]]
        for wd in (SSD_CONV_DIM, GDN_CONV_DIM):
            out_specs.append(pl.BlockSpec((1, CONV_K - 1, wd), lambda i: (i // steps_per_group, 0, 0)))
            out_shape.append(jax.ShapeDtypeStruct((nseq, CONV_K - 1, wd), F32))
            scratch.append(pltpu.VMEM((tm, wd), F32))
        scratch += [pltpu.VMEM(((CONV_K - 1) * SUBLANES, wd), F32) for wd in (SSD_CONV_DIM, GDN_CONV_DIM)]
        scratch += [pltpu.VMEM((D_MODEL // LANES, tm, LANES), F32), pltpu.VMEM((tm, D_MODEL), BF16),
                    pltpu.VMEM((PROJ_COLS // LANES, tm, LANES), F32)]
    return pl.pallas_call(
        functools.partial(_inproj_kernel, tm=tm, steps_per_seq=steps_per_group, do_conv=do_conv),
        grid=(T // tm,),
        in_specs=in_specs,
        out_specs=out_specs,
        out_shape=out_shape,
        scratch_shapes=scratch,
        compiler_params=pltpu.CompilerParams(
            dimension_semantics=("arbitrary",), vmem_limit_bytes=VMEM_LIMIT,
            allow_input_fusion=[4 <= k < 8 for k in range(len(args))]),
        name="in_proj",
    )(*args)


def _outproj_kernel(x_ref, mix_ref, gate_ref, w_ref, fw_ref, o_ref, *, final_norm):
    y = jnp.dot(mix_ref[...], w_ref[...], preferred_element_type=F32)
    out = x_ref[...] + gate_ref[0] * y
    if final_norm:
        ms = jnp.mean(out * out, axis=-1, keepdims=True)
        out = out * lax.rsqrt(ms + NORM_EPS) * fw_ref[...]
    o_ref[...] = out


def _outproj_call(x2d, mix2d, gate, w_out, final_w, tm, per_row, steps_per_group, final_norm):
    T = x2d.shape[0]
    return pl.pallas_call(
        functools.partial(_outproj_kernel, final_norm=final_norm),
        grid=(T // tm,),
        in_specs=[
            pl.BlockSpec((tm, D_MODEL), lambda i: (i, 0)),
            pl.BlockSpec((tm, MIX_WIDTH), lambda i: (i, 0)),
            _row_mod_spec(per_row, tm, steps_per_group),
            _resident((MIX_WIDTH, D_MODEL)),
            _resident((1, D_MODEL)),
        ],
        out_specs=pl.BlockSpec((tm, D_MODEL), lambda i: (i, 0)),
        out_shape=jax.ShapeDtypeStruct((T, D_MODEL), F32),
        compiler_params=pltpu.CompilerParams(
            dimension_semantics=("arbitrary",), vmem_limit_bytes=VMEM_LIMIT),
        name="out_proj",
    )(x2d, mix2d, gate, w_out, final_w)


def _run_interleaved(*gens):
    live = [g for g in gens if g is not None]
    while live:
        for g in list(live):
            try:
                next(g)
            except StopIteration:
                live.remove(g)


def _chain_gens(gens):
    for g in gens:
        yield from g


def _unit_lower_solve_many(m_list, rhs_list, q, out):
    blk = min(INV_BLOCK, q)
    nb = q // blk
    if nb > 1:
        r = lax.broadcasted_iota(jnp.int32, (q, q), 0) // blk
        c = lax.broadcasted_iota(jnp.int32, (q, q), 1) // blk
        same = r == c
        d_list = [jnp.where(same, m, 0.0) for m in m_list]
        o_list = [m - d for m, d in zip(m_list, d_list)]
    else:
        d_list = m_list
    n_list = [-d for d in d_list]
    e_list = d_list
    for _ in range(int(math.log2(blk)) - 1):
        e_list = [_bdot(e, e) for e in e_list]
        yield
        n_list = [n + e + _bdot(n, e) for n, e in zip(n_list, e_list)]
        yield
    y_list = [rhs + _bdot(n, rhs) for n, rhs in zip(n_list, rhs_list)]
    if nb > 1:
        p_list = [o + _bdot(n, o) for n, o in zip(n_list, o_list)]
        yield
        r_list = [-p for p in p_list]
        e_list = p_list
        for _ in range(int(math.ceil(math.log2(nb))) - 1):
            e_list = [_bdot(e, e) for e in e_list]
            yield
            r_list = [rr + e + _bdot(rr, e) for rr, e in zip(r_list, e_list)]
            yield
        y_list = [y + _bdot(rr, y) for rr, y in zip(r_list, y_list)]
    yield
    out.extend(y_list)


def _mixer_kernel(*refs, names, q, nq, lv, nc, multi_seq, zero_init, do_conv, chains):
    r = dict(zip(names, refs))
    zs_ref, xbc_ref, zg_ref, qkv_ref, sm_ref = r["zs"], r["xbc"], r["zg"], r["qkv"], r["sm"]
    bias_ref, coef_ref, dexp_ref, snw_ref, gnw_ref = r["bias"], r["coef"], r["dexp"], r["snw"], r["gnw"]
    mix_ref, nhs_ref, nsg_ref = r["mix"], r["nhs"], r["nsg"]
    st_s, st_g, y_s, pe_s, xcd_s, bgt_s = r["st_s"], r["st_g"], r["y_s"], r["pe_s"], r["xcd_s"], r["bgt_s"]
    wq_s, u_s, at_s, kdt_s = r["wq_s"], r["u_s"], r["at_s"], r["kdt_s"]

    c_idx = pl.program_id(1)
    rr_ = nq * q
    n_slots = nq if multi_seq else 1
    slot = (lambda c: c) if multi_seq else (lambda c: 0)
    hist = q + SUBLANES
    pair = 2 * SSD_HEAD_DIM
    heads_per_group = SSD_HEADS // SSD_GROUPS
    gw = SSD_WIDTH // SSD_GROUPS
    assert not (do_conv and not multi_seq and nc > 1 and lv != rr_)

    @pl.when(c_idx == 0)
    def _load_state():
        if zero_init:
            st_s[...] = jnp.zeros(st_s.shape, F32)
            st_g[...] = jnp.zeros(st_g.shape, F32)
        else:
            for s in range(n_slots):
                for j in range(SSD_HEADS // 2):
                    blk = r["hs"][0, s, 2 * j:2 * j + 2].reshape(pair, SSD_STATE)
                    st_s[s, :, pair * j:pair * (j + 1)] = blk.T
            st_g[...] = r["sg"][0]
        if do_conv:
            for xp, key, width in ((r["xps"], "cs", SSD_CONV_DIM), (r["xpg"], "cg", GDN_CONV_DIM)):
                for s in range(n_slots):
                    xp[s * hist:s * hist + SUBLANES, :] = jnp.zeros((SUBLANES, width), F32)
                    if not zero_init:
                        xp[s * hist + SUBLANES - (CONV_K - 1):s * hist + SUBLANES, :] = r[key][0, s]

    if do_conv:
        for x_ref, xp, w_ref, b_ref, out, nst, width in (
                (xbc_ref, r["xps"], r["scw"], r["scb"], r["xbc_c"], r["ncs"], SSD_CONV_DIM),
                (qkv_ref, r["xpg"], r["gcw"], None, r["qkv_c"], r["ncg"], GDN_CONV_DIM)):
            if multi_seq:
                for c in range(nq):
                    base = c * hist + SUBLANES
                    xp[base:base + lv, :] = x_ref[c * lv:(c + 1) * lv, :]
                    xp[base + lv:base + q, :] = jnp.zeros((q - lv, width), F32)
                    _conv_silu_rows(xp, w_ref, b_ref, out, q, 0, width, xp_row0=c * hist, out_row0=c * q)
                    nst[0, c] = xp[base + lv - (CONV_K - 1):base + lv, :]
            else:
                xp[SUBLANES:SUBLANES + rr_, :] = x_ref[...]
                _conv_silu_rows(xp, w_ref, b_ref, out, rr_, 0, width)

                @pl.when(c_idx == nc - 1)
                def _store_conv_state():
                    nst[0, 0] = xp[SUBLANES + rr_ - (CONV_K - 1):SUBLANES + rr_, :]

                if nc > 1:
                    xp[0:SUBLANES, :] = xp[rr_:rr_ + SUBLANES, :]
        xbc_c, qkv_c = r["xbc_c"], r["qkv_c"]
        gate = _silu
    else:
        xbc_c, qkv_c = xbc_ref, qkv_ref
        gate = lambda z: z.astype(F32)

    if multi_seq:
        for src, dst, wd in ((zs_ref, r["pzs"], SSD_WIDTH), (zg_ref, r["pzg"], GDN_WIDTH),
                             (sm_ref, r["psm"], SMALL)):
            for c in range(nq):
                dst[c * q:c * q + lv, :] = src[c * lv:(c + 1) * lv, :].astype(F32)
                dst[c * q + lv:(c + 1) * q, :] = jnp.zeros((q - lv, wd), F32)
        zs_v, zg_v, sm = r["pzs"], r["pzg"], r["psm"][...]
    else:
        zs_v, zg_v, sm = zs_ref, zg_ref, sm_ref[...]

    sp = _softplus(sm + bias_ref[...])
    beta_all = jax.nn.sigmoid(sm)
    if multi_seq and lv < q:
        assert q & (q - 1) == 0
        valid = jnp.bitwise_and(lax.broadcasted_iota(jnp.int32, (rr_, SMALL), 0), q - 1) < lv
        sp = jnp.where(valid, sp, 0.0)
        beta_all = jnp.where(valid, beta_all, 0.0)
    ag = sp * coef_ref[...]
    r_all = lax.broadcasted_iota(jnp.int32, (rr_, rr_), 0)
    c_all = lax.broadcasted_iota(jnp.int32, (rr_, rr_), 1)
    chunk_start = (r_all // q) * q
    tril_bd = jnp.where(r_all >= c_all, jnp.where(c_all >= chunk_start, 1.0, 0.0), 0.0).astype(BF16)
    rq = lax.broadcasted_iota(jnp.int32, (q, q), 0)
    cq = lax.broadcasted_iota(jnp.int32, (q, q), 1)
    incl = rq >= cq
    strict = rq > cq
    r128 = lax.broadcasted_iota(jnp.int32, (LANES, LANES), 0)
    c128 = lax.broadcasted_iota(jnp.int32, (LANES, LANES), 1)
    eye = jnp.where(r128 == c128, 1.0, 0.0).astype(BF16)
    cum = _const_dot(tril_bd, ag, 3)
    if q == PERM_CHUNK:
        cum_t, sp_t = cum.T, sp.T
    else:
        cum_t = _transpose_via_identity(eye, cum, 3)
        sp_t = _transpose_via_identity(eye, sp, 3)
    ecum = jnp.exp(cum)
    if nq > 1:
        cum_last = jnp.concatenate(
            [jnp.broadcast_to(cum[(c + 1) * q - 1:(c + 1) * q, :], (q, SMALL)) for c in range(nq)], axis=0)
    else:
        cum_last = cum[q - 1:q, :]
    to_end = jnp.exp(cum_last - cum)
    er = lax.broadcasted_iota(jnp.int32, (SMALL, SSD_WIDTH), 0)
    ec = lax.broadcasted_iota(jnp.int32, (SMALL, SSD_WIDTH), 1) // SSD_HEAD_DIM
    expand = jnp.where(er == ec, 1.0, 0.0).astype(BF16)
    pe_s[...] = _dot_const(ecum, expand, 2)
    p_dtend = _dot_const(sp * to_end, expand, 2)
    xcd_s[...] = (xbc_c[:, 0:SSD_WIDTH].astype(F32) * p_dtend).astype(BF16)

    lane = lax.broadcasted_iota(jnp.int32, (q, pair), 1)
    lo_half = lane < SSD_HEAD_DIM
    scores = {}
    for c in range(nq):
        rows = slice(c * q, (c + 1) * q)
        for g in range(SSD_GROUPS):
            b_g = xbc_c[rows, SSD_WIDTH + g * SSD_STATE:SSD_WIDTH + (g + 1) * SSD_STATE]
            c_g = xbc_c[rows, SSD_WIDTH + (SSD_GROUPS + g) * SSD_STATE:
                        SSD_WIDTH + (SSD_GROUPS + g + 1) * SSD_STATE]
            scores[c, g] = _bdot_nt(c_g, b_g)
            if q == PERM_CHUNK:
                bgt_s[c * SSD_GROUPS + g] = b_g.astype(F32).T.astype(BF16)
            else:
                bgt_s[c * SSD_GROUPS + g] = _transpose_via_identity(eye, b_g, 1).astype(BF16)
    def ssd_diag(items):
        for c in items:
            rows = slice(c * q, (c + 1) * q)
            for j in range(SSD_HEADS // 2):
                g = j // (heads_per_group // 2)
                pcols = slice(pair * j, pair * (j + 1))
                x_pair = xbc_c[rows, pcols].astype(F32)
                y_pair = x_pair * dexp_ref[:, pcols]
                for half in range(2):
                    h = 2 * j + half
                    seg = cum[rows, h:h + 1] - cum_t[h:h + 1, rows]
                    lmat = jnp.exp(jnp.where(incl, seg, -jnp.inf))
                    m_h = scores[c, g] * lmat * sp_t[h:h + 1, rows]
                    x_h = jnp.where(lo_half if half == 0 else jnp.logical_not(lo_half), x_pair, 0.0)
                    y_pair = y_pair + _bdot(m_h, x_h)
                y_s[rows, pcols] = y_pair
                if j % 2 == 1:
                    yield

    def gdn_independent(group):
        m_list, rhs_list = [], []
        for n, (c, h) in enumerate(group):
            rows = slice(c * q, (c + 1) * q)
            i = c * GDN_HEADS + h
            qh = qkv_c[rows, h * GDN_HEAD_DIM:(h + 1) * GDN_HEAD_DIM].astype(F32)
            kh = qkv_c[rows, GDN_WIDTH + h * GDN_HEAD_DIM:GDN_WIDTH + (h + 1) * GDN_HEAD_DIM].astype(F32)
            vh = qkv_c[rows, 2 * GDN_WIDTH + h * GDN_HEAD_DIM:
                       2 * GDN_WIDTH + (h + 1) * GDN_HEAD_DIM].astype(F32)
            if do_conv:
                qh = qh * lax.rsqrt(jnp.sum(qh * qh, axis=-1, keepdims=True) + NORM_EPS) * (GDN_HEAD_DIM ** -0.5)
                kh = kh * lax.rsqrt(jnp.sum(kh * kh, axis=-1, keepdims=True) + NORM_EPS)
            la = _SM_A + h
            bcol = beta_all[rows, _SM_B + h:_SM_B + h + 1]
            eg = ecum[rows, la:la + 1]
            kb = kh * bcol
            dec = jnp.exp(jnp.where(incl, cum[rows, la:la + 1] - cum_t[la:la + 1, rows], -jnp.inf))
            kq = _bdot_nt(jnp.concatenate([kb, qh], axis=0), kh)
            m_list.append(jnp.where(strict, kq[0:q] * dec, 0.0))
            at_s[i] = (kq[q:2 * q] * dec).astype(BF16)
            rhs_list.append(jnp.concatenate([vh * bcol, kb * eg], axis=1))
            wq_s[i, q:2 * q, :] = (qh * eg).astype(BF16)
            kd = kh * to_end[rows, la:la + 1]
            if q == PERM_CHUNK:
                kdt_s[i] = kd.T.astype(BF16)
            else:
                kdt_s[i] = _transpose_via_identity(eye, kd, 1).astype(BF16)
            if n % 4 == 3:
                yield
        sol_list = []
        yield from _unit_lower_solve_many(m_list, rhs_list, q, sol_list)
        for (c, h), sol in zip(group, sol_list):
            i = c * GDN_HEADS + h
            u_s[i] = sol[:, 0:GDN_HEAD_DIM]
            wq_s[i, 0:q, :] = sol[:, GDN_HEAD_DIM:2 * GDN_HEAD_DIM].astype(BF16)
        yield

    def store_mix(c, cols, val):
        if multi_seq:
            mix_ref[c * lv:(c + 1) * lv, cols] = val[0:lv].astype(mix_ref.dtype)
        else:
            mix_ref[c * q:(c + 1) * q, cols] = val.astype(mix_ref.dtype)

    def recurrences(items):
        for c in items:
            rows = slice(c * q, (c + 1) * q)
            last = (c + 1) * q - 1
            s = slot(c)
            for g in range(SSD_GROUPS):
                gcols = slice(g * gw, (g + 1) * gw)
                c_g = xbc_c[rows, SSD_WIDTH + (SSD_GROUPS + g) * SSD_STATE:
                            SSD_WIDTH + (SSD_GROUPS + g + 1) * SSD_STATE]
                state = st_s[s, :, gcols]
                y_g = y_s[rows, gcols] + _bdot(c_g, state) * pe_s[rows, gcols]
                st_s[s, :, gcols] = state * pe_s[last:last + 1, gcols] + jnp.dot(
                    bgt_s[c * SSD_GROUPS + g], xcd_s[rows, gcols], preferred_element_type=F32)
                y_g = y_g * gate(zs_v[rows, gcols])
                ms = jnp.mean(y_g * y_g, axis=-1, keepdims=True)
                y_g = y_g * lax.rsqrt(ms + NORM_EPS) * snw_ref[:, gcols]
                store_mix(c, gcols, y_g)
            yield
        pairs = [(c, h) for c in items for h in range(GDN_HEADS)]
        ws = {}
        for (c, h) in pairs:
            ws[c, h] = jnp.dot(wq_s[c * GDN_HEADS + h], st_g[slot(c), h].astype(BF16),
                               preferred_element_type=F32)
        yield
        vn = {(c, h): (u_s[c * GDN_HEADS + h] - ws[c, h][0:q]).astype(BF16) for (c, h) in pairs}
        for (c, h) in pairs:
            i = c * GDN_HEADS + h
            la = _SM_A + h
            last = (c + 1) * q - 1
            s = slot(c)
            o = ws[c, h][q:2 * q] + jnp.dot(at_s[i], vn[c, h], preferred_element_type=F32)
            st_g[s, h] = st_g[s, h] * ecum[last:last + 1, la:la + 1] + jnp.dot(
                kdt_s[i], vn[c, h], preferred_element_type=F32)
            hc = slice(h * GDN_HEAD_DIM, (h + 1) * GDN_HEAD_DIM)
            ms = jnp.mean(o * o, axis=-1, keepdims=True)
            o = o * lax.rsqrt(ms + NORM_EPS) * gnw_ref[...] * gate(zg_v[c * q:(c + 1) * q, hc])
            store_mix(c, slice(SSD_WIDTH + h * GDN_HEAD_DIM, SSD_WIDTH + (h + 1) * GDN_HEAD_DIM), o)
        yield

    all_ch = [(c, h) for c in range(nq) for h in range(GDN_HEADS)]
    groups = [all_ch[i:i + chains] for i in range(0, len(all_ch), chains)]
    batches = [list(range(nq))] if multi_seq else [[c] for c in range(nq)]
    assert chains % GDN_HEADS == 0
    pending = list(batches)
    done_items = set()
    if multi_seq:
        for _ in ssd_diag(range(nq)):
            pass
    for grp in groups:
        runnable = [b for b in pending if set(b) <= done_items]
        for b in runnable:
            pending.remove(b)
        others = [_chain_gens([recurrences(b) for b in runnable])] if runnable else []
        if not multi_seq:
            others.append(ssd_diag(sorted({c for c, _ in grp})))
        _run_interleaved(gdn_independent(grp), *others)
        done_items |= {c for c, _ in grp}
    for b in pending:
        for _ in recurrences(b):
            pass

    @pl.when(c_idx == nc - 1)
    def _store_state():
        for s in range(n_slots):
            for j in range(SSD_HEADS // 2):
                blk = st_s[s, :, pair * j:pair * (j + 1)].T
                nhs_ref[0, s, 2 * j:2 * j + 2] = blk.reshape(2, SSD_HEAD_DIM, SSD_STATE)
        nsg_ref[0] = st_g[...]


_STATE_TAILS = (
    (CONV_K - 1, SSD_CONV_DIM),
    (SSD_HEADS, SSD_HEAD_DIM, SSD_STATE),
    (CONV_K - 1, GDN_CONV_DIM),
    (GDN_HEADS, GDN_HEAD_DIM, GDN_HEAD_DIM),
)


_STATE_NAMES = ("cs", "hs", "cg", "sg")


def _mixer_call(zs, xbc, zg, qkv, sm, B, L, states_in, prev_out, p, q, nq, multi_seq, layer, chains, do_conv):
    assert zs.shape[0] == B * L
    rows = nq * q
    assert q % min(INV_BLOCK, q) == 0 and q & (q - 1) == 0
    zero_init = states_in is None
    if multi_seq:
        assert CONV_K - 1 <= L <= q and B % nq == 0 and do_conv
        nc, lv, bb = 1, L, nq
        tok = lambda wd: pl.BlockSpec((nq * L, wd), lambda b, c: (b, 0))
    else:
        assert L % rows == 0
        nc, lv, bb = L // rows, rows, 1
        tok = lambda wd: pl.BlockSpec((rows, wd), lambda b, c: (b * nc + c, 0))
    n_slots = bb
    per_b = lambda shape: pl.BlockSpec((1, bb) + shape, lambda b, c: (layer, b) + (0,) * len(shape))
    const = lambda shape: pl.BlockSpec(shape, lambda b, c: (0,) * len(shape))
    nch = nq * GDN_HEADS
    kept = [k for k in range(4) if do_conv or k in (1, 3)]

    names = ["zs", "xbc", "zg", "qkv", "sm"]
    in_specs = [tok(SSD_WIDTH), tok(SSD_CONV_DIM), tok(GDN_WIDTH), tok(GDN_CONV_DIM), tok(SMALL)]
    args = [zs, xbc, zg, qkv, sm]
    if not zero_init:
        for k in kept:
            names.append(_STATE_NAMES[k])
            in_specs.append(per_b(_STATE_TAILS[k]))
            args.append(states_in[k])
    consts = [("bias", (1, SMALL)), ("coef", (1, SMALL)), ("dexp", (1, SSD_WIDTH)), ("snw", (1, SSD_WIDTH)),
              ("gnw", (1, GDN_HEAD_DIM))]
    if do_conv:
        consts += [("scw", (CONV_K, SSD_CONV_DIM)), ("scb", (1, SSD_CONV_DIM)), ("gcw", (CONV_K, GDN_CONV_DIM))]
    for nm, shape in consts:
        names.append(nm)
        in_specs.append(const(shape))
        args.append(p[nm])
    aliases = {}
    if prev_out is not None:
        for k, a in enumerate(prev_out):
            names.append("alias%d" % k)
            aliases[len(args)] = 1 + k
            in_specs.append(pl.BlockSpec(memory_space=pl.ANY))
            args.append(a)
    names += ["mix"] + ["n" + _STATE_NAMES[k] for k in kept]
    out_specs = [tok(MIX_WIDTH)] + [per_b(_STATE_TAILS[k]) for k in kept]
    out_shape = [jax.ShapeDtypeStruct((B * L, MIX_WIDTH), BF16)] + [
        jax.ShapeDtypeStruct((DEPTH, B) + _STATE_TAILS[k], F32) for k in kept]
    scratch = [
        ("st_s", pltpu.VMEM((n_slots, SSD_STATE, SSD_WIDTH), F32)),
        ("st_g", pltpu.VMEM((n_slots, GDN_HEADS, GDN_HEAD_DIM, GDN_HEAD_DIM), F32)),
        ("y_s", pltpu.VMEM((rows, SSD_WIDTH), F32)),
        ("pe_s", pltpu.VMEM((rows, SSD_WIDTH), F32)),
        ("xcd_s", pltpu.VMEM((rows, SSD_WIDTH), BF16)),
        ("bgt_s", pltpu.VMEM((nq * SSD_GROUPS, SSD_STATE, q), BF16)),
        ("wq_s", pltpu.VMEM((nch, 2 * q, GDN_HEAD_DIM), BF16)),
        ("u_s", pltpu.VMEM((nch, q, GDN_HEAD_DIM), F32)),
        ("at_s", pltpu.VMEM((nch, q, q), BF16)),
        ("kdt_s", pltpu.VMEM((nch, GDN_HEAD_DIM, q), BF16)),
    ]
    if do_conv:
        xp_rows = nq * (q + SUBLANES) if multi_seq else rows + SUBLANES
        scratch += [
            ("xps", pltpu.VMEM((xp_rows, SSD_CONV_DIM), F32)),
            ("xpg", pltpu.VMEM((xp_rows, GDN_CONV_DIM), F32)),
            ("xbc_c", pltpu.VMEM((rows, SSD_CONV_DIM), F32)),
            ("qkv_c", pltpu.VMEM((rows, GDN_CONV_DIM), F32)),
        ]
    if multi_seq:
        scratch += [
            ("pzs", pltpu.VMEM((rows, SSD_WIDTH), F32)),
            ("pzg", pltpu.VMEM((rows, GDN_WIDTH), F32)),
            ("psm", pltpu.VMEM((rows, SMALL), F32)),
        ]
    names += [nm for nm, _ in scratch]
    return pl.pallas_call(
        functools.partial(_mixer_kernel, names=tuple(names), q=q, nq=nq, lv=lv, nc=nc, multi_seq=multi_seq,
                          zero_init=zero_init, do_conv=do_conv, chains=chains),
        grid=(B // bb, nc),
        in_specs=in_specs,
        out_specs=out_specs,
        out_shape=out_shape,
        input_output_aliases=aliases,
        scratch_shapes=[s for _, s in scratch],
        compiler_params=pltpu.CompilerParams(
            dimension_semantics=("arbitrary", "arbitrary"), vmem_limit_bytes=VMEM_LIMIT),
        name="mixer",
    )(*args)


def _prep_layer(l, norm_w, w_in, ssd_conv_w, ssd_conv_b, ssd_dt_bias, ssd_a_log, ssd_d, ssd_norm_w,
                gdn_conv_w, gdn_dt_bias, gdn_a_log, gdn_norm_w, w_out):
    wi = w_in[l]
    small = jnp.concatenate(
        [wi[:, _OFF_DT:_OFF_DT + SSD_HEADS], wi[:, _OFF_A:_OFF_A + GDN_HEADS], wi[:, _OFF_B:_OFF_B + GDN_HEADS],
         jnp.zeros((D_MODEL, SMALL - SSD_HEADS - 2 * GDN_HEADS), F32)], axis=1)
    sm_hi = small.astype(BF16)
    sm_lo = (small - sm_hi.astype(F32)).astype(BF16)
    pad = jnp.zeros((SMALL - SSD_HEADS - GDN_HEADS,), F32)
    return {
        "norm_w": norm_w[l].reshape(1, D_MODEL),
        "zs": wi[:, _OFF_ZS:_OFF_ZS + SSD_WIDTH].astype(BF16),
        "xbc": wi[:, _OFF_XBC:_OFF_XBC + SSD_CONV_DIM].astype(BF16),
        "zg": wi[:, _OFF_ZG:_OFF_ZG + GDN_WIDTH].astype(BF16),
        "qkv": wi[:, _OFF_QKV:_OFF_QKV + GDN_CONV_DIM].astype(BF16),
        "sm_hi": sm_hi,
        "sm_lo": sm_lo,
        "scw": ssd_conv_w[l],
        "scb": ssd_conv_b[l].reshape(1, SSD_CONV_DIM),
        "gcw": gdn_conv_w[l],
        "bias": jnp.concatenate([ssd_dt_bias[l], gdn_dt_bias[l], pad]).reshape(1, SMALL),
        "coef": jnp.concatenate([-jnp.exp(ssd_a_log[l]), -jnp.exp(gdn_a_log[l]), pad]).reshape(1, SMALL),
        "dexp": jnp.repeat(ssd_d[l], SSD_HEAD_DIM).reshape(1, SSD_WIDTH),
        "snw": ssd_norm_w[l].reshape(1, SSD_WIDTH),
        "gnw": gdn_norm_w[l].reshape(1, GDN_HEAD_DIM),
        "w_out": w_out[l].astype(BF16),
    }


def _trunk(x, mod, states, params, final_w, q, nq, tm, tm_out, chains, act_dtype):
    B, L, _ = x.shape
    T = B * L
    per_row = L < SUBLANES
    steps_per_group = max(L // tm, 1)
    x2d = x.reshape(T, D_MODEL)
    conv_in_proj = states is None and not per_row
    new_states = None
    conv_states = ([], [])
    for l in range(DEPTH):
        p = params[l]
        shift, scale, gate = (mod[l, :, i * D_MODEL:(i + 1) * D_MODEL] for i in range(3))
        if per_row:
            rep = lambda a: jnp.repeat(a, L, axis=0).reshape(1, T, D_MODEL)
        else:
            rep = lambda a: a.reshape(B, 1, D_MODEL)
        zs, xbc, zg, qkv, sm, *cst = _inproj_call(x2d, rep(scale), rep(shift), p["norm_w"], p, tm, per_row,
                                                  steps_per_group, conv_in_proj, act_dtype)
        mix, *new_states = _mixer_call(zs, xbc, zg, qkv, sm, B, L, states, new_states, p,
                                       q, nq, per_row, l, chains, not conv_in_proj)
        x2d = _outproj_call(x2d, mix, rep(gate), p["w_out"], final_w, tm_out, per_row,
                            max(L // tm_out, 1), final_norm=(l == DEPTH - 1))
        for lst, s in zip(conv_states, cst):
            lst.append(s)
    if conv_in_proj:
        new_states = [jnp.stack(conv_states[0]), new_states[0], jnp.stack(conv_states[1]), new_states[1]]
    return x2d.reshape(B, L, D_MODEL), new_states


def kernel(x_prompt, x_sample, state_ssd_conv, state_ssm, state_gdn_conv, state_gdn, c_prompt, c_sample,
           norm_w, w_ada, b_ada, w_in, ssd_conv_w, ssd_conv_b, ssd_dt_bias, ssd_a_log, ssd_d, ssd_norm_w,
           gdn_conv_w, gdn_dt_bias, gdn_a_log, gdn_norm_w, w_out, final_norm_w):
    bp = x_prompt.shape[0]
    params = [_prep_layer(l, norm_w, w_in, ssd_conv_w, ssd_conv_b, ssd_dt_bias, ssd_a_log, ssd_d, ssd_norm_w,
                          gdn_conv_w, gdn_dt_bias, gdn_a_log, gdn_norm_w, w_out) for l in range(DEPTH)]
    final_w = final_norm_w.reshape(1, D_MODEL)
    mod = _ada_call(jnp.concatenate([c_prompt, c_sample], axis=0), w_ada, b_ada)
    y_p, sp = _trunk(x_prompt, mod[:, :bp], None, params, final_w, q=64, nq=4, tm=512, tm_out=1024, chains=16,
                     act_dtype=BF16)
    ts = x_sample.shape[0] * x_sample.shape[1]
    y_s, ss = _trunk(x_sample, mod[:, bp:], (state_ssd_conv, state_ssm, state_gdn_conv, state_gdn),
                     params, final_w, q=8, nq=8, tm=ts, tm_out=ts, chains=16, act_dtype=F32)
    return (y_p, y_s, sp[0], sp[1], sp[2], sp[3], ss[0], ss[1], ss[2], ss[3])
```

```python
import functools
import math

import jax
import jax.numpy as jnp
from jax import lax
from jax.experimental import pallas as pl
from jax.experimental.pallas import tpu as pltpu

F32 = jnp.float32
BF16 = jnp.bfloat16

D_MODEL = 1024
DEPTH = 2
SSD_WIDTH = 1024
SSD_HEAD_DIM = 64
SSD_HEADS = 16
SSD_GROUPS = 2
SSD_STATE = 128
SSD_CONV_DIM = SSD_WIDTH + 2 * SSD_GROUPS * SSD_STATE
GDN_WIDTH = 1024
GDN_HEAD_DIM = 128
GDN_HEADS = 8
GDN_CONV_DIM = 3 * GDN_WIDTH
MIX_WIDTH = SSD_WIDTH + GDN_WIDTH
CONV_K = 4
NORM_EPS = 1e-6
ADA_DIM = 3 * D_MODEL
SMALL = 128
LANES = 128
SUBLANES = 8
INV_BLOCK = 16
VMEM_LIMIT = 56 * 1024 * 1024

_OFF_ZS = 0
_OFF_XBC = _OFF_ZS + SSD_WIDTH
_OFF_DT = _OFF_XBC + SSD_CONV_DIM
_OFF_ZG = _OFF_DT + SSD_HEADS
_OFF_QKV = _OFF_ZG + GDN_WIDTH
_OFF_A = _OFF_QKV + GDN_CONV_DIM
_OFF_B = _OFF_A + GDN_HEADS
_SM_DT = 0
_SM_A = SSD_HEADS
_SM_B = SSD_HEADS + GDN_HEADS


def _bdot(a, b):
    return jnp.dot(a.astype(BF16), b.astype(BF16), preferred_element_type=F32)


def _bdot_nt(a, b):
    return lax.dot_general(a.astype(BF16), b.astype(BF16), (((1,), (1,)), ((), ())),
                           preferred_element_type=F32)


def _terms(x, n):
    out = []
    r = x
    for i in range(n):
        t = r.astype(BF16)
        out.append(t)
        if i + 1 < n:
            r = r - t.astype(F32)
    return out


def _const_dot(c, x, n):
    acc = None
    for t in _terms(x, n):
        p = jnp.dot(c, t, preferred_element_type=F32)
        acc = p if acc is None else acc + p
    return acc


def _dot_const(x, c, n):
    acc = None
    for t in _terms(x, n):
        p = jnp.dot(t, c, preferred_element_type=F32)
        acc = p if acc is None else acc + p
    return acc


def _transpose_via_identity(eye, x, n):
    acc = None
    for t in _terms(x, n):
        p = lax.dot_general(eye, t, (((1,), (1,)), ((), ())), preferred_element_type=F32)
        acc = p if acc is None else acc + p
    return acc


def _softplus(x):
    return jnp.maximum(x, 0.0) + jnp.log1p(jnp.exp(-jnp.abs(x)))


def _silu(x):
    return x * jax.nn.sigmoid(x)


def _ada_kernel(c_ref, w_ref, b_ref, o_ref):
    s = _silu(c_ref[...])
    w = w_ref[0]
    s_hi, s_lo = _terms(s, 2)
    w_hi, w_lo = _terms(w, 2)
    acc = jnp.dot(s_hi, w_hi, preferred_element_type=F32)
    acc = acc + (jnp.dot(s_hi, w_lo, preferred_element_type=F32)
                 + jnp.dot(s_lo, w_hi, preferred_element_type=F32))
    o_ref[0] = acc + b_ref[0]


def _ada_call(c_all, w_ada, b_ada):
    rows = c_all.shape[0]
    tn = ADA_DIM // 2
    return pl.pallas_call(
        _ada_kernel,
        grid=(DEPTH, ADA_DIM // tn),
        in_specs=[
            pl.BlockSpec((rows, D_MODEL), lambda l, j: (0, 0)),
            pl.BlockSpec((1, D_MODEL, tn), lambda l, j: (l, 0, j)),
            pl.BlockSpec((1, 1, tn), lambda l, j: (l, 0, j)),
        ],
        out_specs=pl.BlockSpec((1, rows, tn), lambda l, j: (l, 0, j)),
        out_shape=jax.ShapeDtypeStruct((DEPTH, rows, ADA_DIM), F32),
        compiler_params=pltpu.CompilerParams(
            dimension_semantics=("arbitrary", "arbitrary"), vmem_limit_bytes=VMEM_LIMIT),
        name="ada_mod",
    )(c_all, w_ada, b_ada.reshape(DEPTH, 1, ADA_DIM))


CONV_ROWS = 64
CONV_COLS = 128
PROJ_COLS = 512


def _conv_silu_rows(xp, w_ref, b_ref, out, rows, c0, c1, xp_row0=0, out_row0=0):
    rows_per = min(rows, CONV_ROWS)
    for i in range(rows // rows_per):
        r0 = i * rows_per
        for j in range((c1 - c0) // CONV_COLS):
            cols = slice(c0 + j * CONV_COLS, c0 + (j + 1) * CONV_COLS)
            x = xp[xp_row0 + r0:xp_row0 + r0 + rows_per + SUBLANES, cols]
            acc = x * w_ref[0:1, cols]
            for k in range(1, CONV_K):
                acc = pltpu.roll(acc, 1, 0) + x * w_ref[k:k + 1, cols]
            if b_ref is not None:
                acc = acc + b_ref[:, cols]
            out[out_row0 + r0:out_row0 + r0 + rows_per, cols] = _silu(acc[SUBLANES:, :]).astype(out.dtype)


PERM_CHUNK = SUBLANES * SUBLANES
NORM_ROWS = 64


def _qk_l2_scale(col):
    assert GDN_HEAD_DIM == LANES
    if col < GDN_WIDTH:
        return GDN_HEAD_DIM ** -0.5
    if col < 2 * GDN_WIDTH:
        return 1.0
    return None


def _conv_silu_permuted(xp, w_ref, b_ref, hist, up, out, rows, c0, l2_scale=None):
    sub0 = lax.broadcasted_iota(jnp.int32, (SUBLANES, LANES), 0) == 0
    taps = CONV_K - 1
    for j in range(PROJ_COLS // LANES):
        cols = slice(c0 + j * LANES, c0 + (j + 1) * LANES)
        w = [jnp.broadcast_to(w_ref[k:k + 1, cols], (SUBLANES, LANES)) for k in range(CONV_K)]
        b = None if b_ref is None else jnp.broadcast_to(b_ref[:, cols], (SUBLANES, LANES))
        prev = [hist[i * SUBLANES:(i + 1) * SUBLANES, cols] for i in range(taps)]
        for k in range(rows // PERM_CHUNK):
            base = k * PERM_CHUNK
            xs = [xp[base + a * SUBLANES:base + (a + 1) * SUBLANES, cols] for a in range(SUBLANES)]
            rolled = [pltpu.roll(xs[SUBLANES - taps + i], 1, 0) for i in range(taps)]
            ext = [jnp.where(sub0, prev[i], rolled[i]) for i in range(taps)] + xs
            for a in range(SUBLANES):
                y = ext[a] * w[0]
                for t in range(1, CONV_K):
                    y = y + ext[a + t] * w[t]
                if b is not None:
                    y = y + b
                y = _silu(y)
                scale = None if l2_scale is None else l2_scale(c0 + j * LANES)
                if scale is not None:
                    y = y * (lax.rsqrt(jnp.sum(y * y, axis=-1, keepdims=True) + NORM_EPS) * scale)
                up[j, pl.ds(base + a, SUBLANES, stride=SUBLANES), :] = y
            prev = rolled
        for i in range(taps):
            hist[i * SUBLANES:(i + 1) * SUBLANES, cols] = prev[i]
        out[:, cols] = up[j].astype(out.dtype)


def _inproj_kernel(*refs, tm, steps_per_seq, do_conv):
    x_ref, sc_ref, sh_ref, nw_ref, wzs_ref, wxbc_ref, wzg_ref, wqkv_ref, wsh_ref, wsl_ref = refs[:10]
    if do_conv:
        scw_ref, scb_ref, gcw_ref = refs[10:13]
        (zs_ref, xbc_ref, zg_ref, qkv_ref, sm_ref, ncs_ref, ncg_ref,
         xps, xpg, hist_s, hist_g, hs, hp, up, hh, hl) = refs[13:]
    else:
        zs_ref, xbc_ref, zg_ref, qkv_ref, sm_ref = refs[10:]

    def normed(x, sc, sh):
        ms = jnp.mean(x * x, axis=-1, keepdims=True)
        return x * lax.rsqrt(ms + NORM_EPS) * nw_ref[...] * (1.0 + sc) + sh

    if do_conv:
        for r0 in range(0, tm, NORM_ROWS):
            rows = slice(r0, r0 + NORM_ROWS)
            h = normed(x_ref[rows, :], sc_ref[0], sh_ref[0])
            hi, lo = _terms(h, 2)
            hh[rows, :] = hi
            hl[rows, :] = lo
            for cb in range(D_MODEL // LANES):
                hs[cb, rows, :] = h[:, cb * LANES:(cb + 1) * LANES]
        h_hi, h_lo = hh[...], hl[...]
    else:
        h = normed(x_ref[...], sc_ref[0], sh_ref[0])
        h_hi, h_lo = _terms(h, 2)
    sm = jnp.dot(h_hi, wsh_ref[...], preferred_element_type=F32)
    sm = sm + (jnp.dot(h_hi, wsl_ref[...], preferred_element_type=F32)
               + jnp.dot(h_lo, wsh_ref[...], preferred_element_type=F32))
    sm_ref[...] = sm
    if not do_conv:
        zs_ref[...] = jnp.dot(h_hi, wzs_ref[...], preferred_element_type=F32)
        xbc_ref[...] = jnp.dot(h_hi, wxbc_ref[...], preferred_element_type=F32)
        zg_ref[...] = jnp.dot(h_hi, wzg_ref[...], preferred_element_type=F32)
        qkv_ref[...] = jnp.dot(h_hi, wqkv_ref[...], preferred_element_type=F32)
        return

    @pl.when(pl.program_id(0) % steps_per_seq == 0)
    def _zero_history():
        hist_s[...] = jnp.zeros(hist_s.shape, F32)
        hist_g[...] = jnp.zeros(hist_g.shape, F32)

    assert tm % PERM_CHUNK == 0
    ncb = D_MODEL // LANES
    for g in range(0, tm // SUBLANES, 2):
        parts = [jnp.concatenate(
            [hs[cb, pl.ds(PERM_CHUNK * (gg // SUBLANES) + gg % SUBLANES, SUBLANES, stride=SUBLANES), :]
             for cb in range(ncb)], axis=1) for gg in (g, g + 1)]
        hp[g * SUBLANES:(g + 2) * SUBLANES, :] = jnp.concatenate(parts, axis=0).astype(BF16)

    conv_blocks = [(xp, w_ref, cw_ref, cb_ref, hist, out, c0)
                   for xp, w_ref, cw_ref, cb_ref, hist, out, width in (
                       (xps, wxbc_ref, scw_ref, scb_ref, hist_s, xbc_ref, SSD_CONV_DIM),
                       (xpg, wqkv_ref, gcw_ref, None, hist_g, qkv_ref, GDN_CONV_DIM))
                   for c0 in range(0, width, PROJ_COLS)]
    gate_blocks = [(w_ref, out, c0) for w_ref, out, width in ((wzs_ref, zs_ref, SSD_WIDTH),
                                                              (wzg_ref, zg_ref, GDN_WIDTH))
                   for c0 in range(0, width, PROJ_COLS)]

    def project(blk):
        xp, w_ref, _, _, _, _, c0 = blk
        cols = slice(c0, c0 + PROJ_COLS)
        xp[:, cols] = jnp.dot(hp[...], w_ref[:, cols], preferred_element_type=F32)

    def gate_block(i):
        w_ref, out, c0 = gate_blocks[i]
        cols = slice(c0, c0 + PROJ_COLS)
        out[:, cols] = _silu(jnp.dot(h_hi, w_ref[:, cols], preferred_element_type=F32)).astype(out.dtype)

    early_gates = len(gate_blocks) // 2
    for i in range(early_gates):
        gate_block(i)
    project(conv_blocks[0])
    for j, blk in enumerate(conv_blocks):
        if j + 1 < len(conv_blocks):
            project(conv_blocks[j + 1])
        if j % 2 == 0 and early_gates + j // 2 < len(gate_blocks):
            gate_block(early_gates + j // 2)
        xp, _, cw_ref, cb_ref, hist, out, c0 = blk
        _conv_silu_permuted(xp, cw_ref, cb_ref, hist, up, out, tm, c0,
                            l2_scale=_qk_l2_scale if xp is xpg else None)
    assert len(gate_blocks) - early_gates <= (len(conv_blocks) + 1) // 2
    for xp, nst in ((xps, ncs_ref), (xpg, ncg_ref)):
        for i in range(CONV_K - 1):
            row = tm - PERM_CHUNK + SUBLANES * (SUBLANES - (CONV_K - 1) + i) + SUBLANES - 1
            nst[0, i:i + 1, :] = xp[row:row + 1, :]


def _row_mod_spec(per_row, tm, steps_per_group):
    if per_row:
        return pl.BlockSpec((1, tm, D_MODEL), lambda i: (0, i, 0))
    return pl.BlockSpec((1, 1, D_MODEL), lambda i: (i // steps_per_group, 0, 0))


def _resident(shape):
    nd = len(shape)
    return pl.BlockSpec(shape, lambda i: (0,) * nd, pipeline_mode=pl.Buffered(1))


def _inproj_call(x2d, scale, shift, norm_w, w, tm, per_row, steps_per_group, do_conv, act_dtype):
    T = x2d.shape[0]
    assert T % tm == 0
    widths = (SSD_WIDTH, SSD_CONV_DIM, GDN_WIDTH, GDN_CONV_DIM, SMALL)
    dtypes = (act_dtype, act_dtype, act_dtype, act_dtype, F32)
    row = lambda wd: pl.BlockSpec((tm, wd), lambda i: (i, 0))
    in_specs = [
        row(D_MODEL),
        _row_mod_spec(per_row, tm, steps_per_group),
        _row_mod_spec(per_row, tm, steps_per_group),
        _resident((1, D_MODEL)),
        _resident((D_MODEL, SSD_WIDTH)),
        _resident((D_MODEL, SSD_CONV_DIM)),
        _resident((D_MODEL, GDN_WIDTH)),
        _resident((D_MODEL, GDN_CONV_DIM)),
        _resident((D_MODEL, SMALL)),
        _resident((D_MODEL, SMALL)),
    ]
    args = [x2d, scale, shift, norm_w, w["zs"], w["xbc"], w["zg"], w["qkv"], w["sm_hi"], w["sm_lo"]]
    out_specs = [row(wd) for wd in widths]
    out_shape = [jax.ShapeDtypeStruct((T, wd), dt) for wd, dt in zip(widths, dtypes)]
    scratch = []
    if do_conv:
        assert not per_row
        nseq = T // (tm * steps_per_group)
        in_specs += [_resident((CONV_K, SSD_CONV_DIM)), _resident((1, SSD_CONV_DIM)),
                     _resident((CONV_K, GDN_CONV_DIM))]
        args += [w["scw"], w["scb"], w["gcw"]]
        for wd in (SSD_CONV_DIM, GDN_CONV_DIM):
            out_specs.append(pl.BlockSpec((1, CONV_K - 1, wd), lambda i: (i // steps_per_group, 0, 0)))
            out_shape.append(jax.ShapeDtypeStruct((nseq, CONV_K - 1, wd), F32))
            scratch.append(pltpu.VMEM((tm, wd), F32))
        scratch += [pltpu.VMEM(((CONV_K - 1) * SUBLANES, wd), F32) for wd in (SSD_CONV_DIM, GDN_CONV_DIM)]
        scratch += [pltpu.VMEM((D_MODEL // LANES, tm, LANES), F32), pltpu.VMEM((tm, D_MODEL), BF16),
                    pltpu.VMEM((PROJ_COLS // LANES, tm, LANES), F32),
                    pltpu.VMEM((tm, D_MODEL), BF16), pltpu.VMEM((tm, D_MODEL), BF16)]
    return pl.pallas_call(
        functools.partial(_inproj_kernel, tm=tm, steps_per_seq=steps_per_group, do_conv=do_conv),
        grid=(T // tm,),
        in_specs=in_specs,
        out_specs=out_specs,
        out_shape=out_shape,
        scratch_shapes=scratch,
        compiler_params=pltpu.CompilerParams(
            dimension_semantics=("arbitrary",), vmem_limit_bytes=VMEM_LIMIT),
        name="in_proj",
    )(*args)


def _outproj_kernel(x_ref, mix_ref, gate_ref, w_ref, fw_ref, o_ref, *, final_norm):
    y = jnp.dot(mix_ref[...], w_ref[...], preferred_element_type=F32)
    out = x_ref[...] + gate_ref[0] * y
    if final_norm:
        ms = jnp.mean(out * out, axis=-1, keepdims=True)
        out = out * lax.rsqrt(ms + NORM_EPS) * fw_ref[...]
    o_ref[...] = out


def _outproj_call(x2d, mix2d, gate, w_out, final_w, tm, per_row, steps_per_group, final_norm):
    T = x2d.shape[0]
    return pl.pallas_call(
        functools.partial(_outproj_kernel, final_norm=final_norm),
        grid=(T // tm,),
        in_specs=[
            pl.BlockSpec((tm, D_MODEL), lambda i: (i, 0)),
            pl.BlockSpec((tm, MIX_WIDTH), lambda i: (i, 0)),
            _row_mod_spec(per_row, tm, steps_per_group),
            _resident((MIX_WIDTH, D_MODEL)),
            _resident((1, D_MODEL)),
        ],
        out_specs=pl.BlockSpec((tm, D_MODEL), lambda i: (i, 0)),
        out_shape=jax.ShapeDtypeStruct((T, D_MODEL), F32),
        compiler_params=pltpu.CompilerParams(
            dimension_semantics=("arbitrary",), vmem_limit_bytes=VMEM_LIMIT),
        name="out_proj",
    )(x2d, mix2d, gate, w_out, final_w)


def _run_interleaved(*gens):
    live = [g for g in gens if g is not None]
    while live:
        for g in list(live):
            try:
                next(g)
            except StopIteration:
                live.remove(g)


def _chain_gens(gens):
    for g in gens:
        yield from g


def _unit_lower_solve_many(m_list, rhs_list, q, out):
    blk = min(INV_BLOCK, q)
    nb = q // blk
    if nb > 1:
        r = lax.broadcasted_iota(jnp.int32, (q, q), 0) // blk
        c = lax.broadcasted_iota(jnp.int32, (q, q), 1) // blk
        same = r == c
        d_list = [jnp.where(same, m, 0.0) for m in m_list]
        o_list = [m - d for m, d in zip(m_list, d_list)]
    else:
        d_list = m_list
    n_list = [-d for d in d_list]
    e_list = d_list
    for _ in range(int(math.log2(blk)) - 1):
        e_list = [_bdot(e, e) for e in e_list]
        yield
        n_list = [n + e + _bdot(n, e) for n, e in zip(n_list, e_list)]
        yield
    y_list = [rhs + _bdot(n, rhs) for n, rhs in zip(n_list, rhs_list)]
    if nb > 1:
        p_list = [o + _bdot(n, o) for n, o in zip(n_list, o_list)]
        yield
        r_list = [-p for p in p_list]
        e_list = p_list
        for _ in range(int(math.ceil(math.log2(nb))) - 1):
            e_list = [_bdot(e, e) for e in e_list]
            yield
            r_list = [rr + e + _bdot(rr, e) for rr, e in zip(r_list, e_list)]
            yield
        y_list = [y + _bdot(rr, y) for rr, y in zip(r_list, y_list)]
    yield
    out.extend(y_list)


def _mixer_kernel(*refs, names, q, nq, lv, nc, multi_seq, zero_init, do_conv, chains):
    r = dict(zip(names, refs))
    zs_ref, xbc_ref, zg_ref, qkv_ref, sm_ref = r["zs"], r["xbc"], r["zg"], r["qkv"], r["sm"]
    bias_ref, coef_ref, dexp_ref, snw_ref, gnw_ref = r["bias"], r["coef"], r["dexp"], r["snw"], r["gnw"]
    mix_ref, nhs_ref, nsg_ref = r["mix"], r["nhs"], r["nsg"]
    st_s, st_g, y_s, pe_s, xcd_s, bgt_s = r["st_s"], r["st_g"], r["y_s"], r["pe_s"], r["xcd_s"], r["bgt_s"]
    wq_s, u_s, at_s, kdt_s = r["wq_s"], r["u_s"], r["at_s"], r["kdt_s"]

    c_idx = pl.program_id(1)
    rr_ = nq * q
    n_slots = nq if multi_seq else 1
    slot = (lambda c: c) if multi_seq else (lambda c: 0)
    hist = q + SUBLANES
    pair = 2 * SSD_HEAD_DIM
    heads_per_group = SSD_HEADS // SSD_GROUPS
    gw = SSD_WIDTH // SSD_GROUPS
    assert not (do_conv and not multi_seq and nc > 1 and lv != rr_)

    @pl.when(c_idx == 0)
    def _load_state():
        if zero_init:
            st_s[...] = jnp.zeros(st_s.shape, F32)
            st_g[...] = jnp.zeros(st_g.shape, F32)
        else:
            for s in range(n_slots):
                for j in range(SSD_HEADS // 2):
                    blk = r["hs"][0, s, 2 * j:2 * j + 2].reshape(pair, SSD_STATE)
                    st_s[s, :, pair * j:pair * (j + 1)] = blk.T
            st_g[...] = r["sg"][0]
        if do_conv:
            for xp, key, width in ((r["xps"], "cs", SSD_CONV_DIM), (r["xpg"], "cg", GDN_CONV_DIM)):
                for s in range(n_slots):
                    xp[s * hist:s * hist + SUBLANES, :] = jnp.zeros((SUBLANES, width), F32)
                    if not zero_init:
                        xp[s * hist + SUBLANES - (CONV_K - 1):s * hist + SUBLANES, :] = r[key][0, s]

    if do_conv:
        for x_ref, xp, w_ref, b_ref, out, nst, width in (
                (xbc_ref, r["xps"], r["scw"], r["scb"], r["xbc_c"], r["ncs"], SSD_CONV_DIM),
                (qkv_ref, r["xpg"], r["gcw"], None, r["qkv_c"], r["ncg"], GDN_CONV_DIM)):
            if multi_seq:
                for c in range(nq):
                    base = c * hist + SUBLANES
                    xp[base:base + lv, :] = x_ref[c * lv:(c + 1) * lv, :]
                    xp[base + lv:base + q, :] = jnp.zeros((q - lv, width), F32)
                    _conv_silu_rows(xp, w_ref, b_ref, out, q, 0, width, xp_row0=c * hist, out_row0=c * q)
                    nst[0, c] = xp[base + lv - (CONV_K - 1):base + lv, :]
            else:
                xp[SUBLANES:SUBLANES + rr_, :] = x_ref[...]
                _conv_silu_rows(xp, w_ref, b_ref, out, rr_, 0, width)

                @pl.when(c_idx == nc - 1)
                def _store_conv_state():
                    nst[0, 0] = xp[SUBLANES + rr_ - (CONV_K - 1):SUBLANES + rr_, :]

                if nc > 1:
                    xp[0:SUBLANES, :] = xp[rr_:rr_ + SUBLANES, :]
        xbc_c, qkv_c = r["xbc_c"], r["qkv_c"]
        gate = _silu
    else:
        xbc_c, qkv_c = xbc_ref, qkv_ref
        gate = lambda z: z.astype(F32)

    if multi_seq:
        for src, dst, wd in ((zs_ref, r["pzs"], SSD_WIDTH), (zg_ref, r["pzg"], GDN_WIDTH),
                             (sm_ref, r["psm"], SMALL)):
            for c in range(nq):
                dst[c * q:c * q + lv, :] = src[c * lv:(c + 1) * lv, :].astype(F32)
                dst[c * q + lv:(c + 1) * q, :] = jnp.zeros((q - lv, wd), F32)
        zs_v, zg_v, sm = r["pzs"], r["pzg"], r["psm"][...]
    else:
        zs_v, zg_v, sm = zs_ref, zg_ref, sm_ref[...]

    sp = _softplus(sm + bias_ref[...])
    beta_all = jax.nn.sigmoid(sm)
    if multi_seq and lv < q:
        assert q & (q - 1) == 0
        valid = jnp.bitwise_and(lax.broadcasted_iota(jnp.int32, (rr_, SMALL), 0), q - 1) < lv
        sp = jnp.where(valid, sp, 0.0)
        beta_all = jnp.where(valid, beta_all, 0.0)
    ag = sp * coef_ref[...]
    r_all = lax.broadcasted_iota(jnp.int32, (rr_, rr_), 0)
    c_all = lax.broadcasted_iota(jnp.int32, (rr_, rr_), 1)
    chunk_start = (r_all // q) * q
    tril_bd = jnp.where(r_all >= c_all, jnp.where(c_all >= chunk_start, 1.0, 0.0), 0.0).astype(BF16)
    rq = lax.broadcasted_iota(jnp.int32, (q, q), 0)
    cq = lax.broadcasted_iota(jnp.int32, (q, q), 1)
    incl = rq >= cq
    strict = rq > cq
    r128 = lax.broadcasted_iota(jnp.int32, (LANES, LANES), 0)
    c128 = lax.broadcasted_iota(jnp.int32, (LANES, LANES), 1)
    eye = jnp.where(r128 == c128, 1.0, 0.0).astype(BF16)
    cum = _const_dot(tril_bd, ag, 3)
    if q == PERM_CHUNK:
        cum_t, sp_t = cum.T, sp.T
    else:
        cum_t = _transpose_via_identity(eye, cum, 3)
        sp_t = _transpose_via_identity(eye, sp, 3)
    ecum = jnp.exp(cum)
    if nq > 1:
        cum_last = jnp.concatenate(
            [jnp.broadcast_to(cum[(c + 1) * q - 1:(c + 1) * q, :], (q, SMALL)) for c in range(nq)], axis=0)
    else:
        cum_last = cum[q - 1:q, :]
    to_end = jnp.exp(cum_last - cum)
    er = lax.broadcasted_iota(jnp.int32, (SMALL, SSD_WIDTH), 0)
    ec = lax.broadcasted_iota(jnp.int32, (SMALL, SSD_WIDTH), 1) // SSD_HEAD_DIM
    expand = jnp.where(er == ec, 1.0, 0.0).astype(BF16)
    pe_s[...] = _dot_const(ecum, expand, 2)
    p_dtend = _dot_const(sp * to_end, expand, 2)
    xcd_s[...] = (xbc_c[:, 0:SSD_WIDTH].astype(F32) * p_dtend).astype(BF16)

    lane = lax.broadcasted_iota(jnp.int32, (q, pair), 1)
    lo_half = lane < SSD_HEAD_DIM
    scores = {}
    for c in range(nq):
        rows = slice(c * q, (c + 1) * q)
        for g in range(SSD_GROUPS):
            b_g = xbc_c[rows, SSD_WIDTH + g * SSD_STATE:SSD_WIDTH + (g + 1) * SSD_STATE]
            c_g = xbc_c[rows, SSD_WIDTH + (SSD_GROUPS + g) * SSD_STATE:
                        SSD_WIDTH + (SSD_GROUPS + g + 1) * SSD_STATE]
            scores[c, g] = _bdot_nt(c_g, b_g)
            if q == PERM_CHUNK:
                bgt_s[c * SSD_GROUPS + g] = b_g.astype(F32).T.astype(BF16)
            else:
                bgt_s[c * SSD_GROUPS + g] = _transpose_via_identity(eye, b_g, 1).astype(BF16)
    def ssd_diag(items):
        for c in items:
            rows = slice(c * q, (c + 1) * q)
            for j in range(SSD_HEADS // 2):
                g = j // (heads_per_group // 2)
                pcols = slice(pair * j, pair * (j + 1))
                x_pair = xbc_c[rows, pcols].astype(F32)
                y_pair = x_pair * dexp_ref[:, pcols]
                for half in range(2):
                    h = 2 * j + half
                    seg = cum[rows, h:h + 1] - cum_t[h:h + 1, rows]
                    lmat = jnp.exp(jnp.where(incl, seg, -jnp.inf))
                    m_h = scores[c, g] * lmat * sp_t[h:h + 1, rows]
                    x_h = jnp.where(lo_half if half == 0 else jnp.logical_not(lo_half), x_pair, 0.0)
                    y_pair = y_pair + _bdot(m_h, x_h)
                y_s[rows, pcols] = y_pair
                if j % 2 == 1:
                    yield

    def gdn_independent(group):
        m_list, rhs_list = [], []
        for n, (c, h) in enumerate(group):
            rows = slice(c * q, (c + 1) * q)
            i = c * GDN_HEADS + h
            qh = qkv_c[rows, h * GDN_HEAD_DIM:(h + 1) * GDN_HEAD_DIM].astype(F32)
            kh = qkv_c[rows, GDN_WIDTH + h * GDN_HEAD_DIM:GDN_WIDTH + (h + 1) * GDN_HEAD_DIM].astype(F32)
            vh = qkv_c[rows, 2 * GDN_WIDTH + h * GDN_HEAD_DIM:
                       2 * GDN_WIDTH + (h + 1) * GDN_HEAD_DIM].astype(F32)
            if do_conv:
                qh = qh * lax.rsqrt(jnp.sum(qh * qh, axis=-1, keepdims=True) + NORM_EPS) * (GDN_HEAD_DIM ** -0.5)
                kh = kh * lax.rsqrt(jnp.sum(kh * kh, axis=-1, keepdims=True) + NORM_EPS)
            la = _SM_A + h
            bcol = beta_all[rows, _SM_B + h:_SM_B + h + 1]
            eg = ecum[rows, la:la + 1]
            kb = kh * bcol
            dec = jnp.exp(jnp.where(incl, cum[rows, la:la + 1] - cum_t[la:la + 1, rows], -jnp.inf))
            kq = _bdot_nt(jnp.concatenate([kb, qh], axis=0), kh)
            m_list.append(jnp.where(strict, kq[0:q] * dec, 0.0))
            at_s[i] = (kq[q:2 * q] * dec).astype(BF16)
            rhs_list.append(jnp.concatenate([vh * bcol, kb * eg], axis=1))
            wq_s[i, q:2 * q, :] = (qh * eg).astype(BF16)
            kd = kh * to_end[rows, la:la + 1]
            if q == PERM_CHUNK:
                kdt_s[i] = kd.T.astype(BF16)
            else:
                kdt_s[i] = _transpose_via_identity(eye, kd, 1).astype(BF16)
            if n % 4 == 3:
                yield
        sol_list = []
        yield from _unit_lower_solve_many(m_list, rhs_list, q, sol_list)
        for (c, h), sol in zip(group, sol_list):
            i = c * GDN_HEADS + h
            u_s[i] = sol[:, 0:GDN_HEAD_DIM]
            wq_s[i, 0:q, :] = sol[:, GDN_HEAD_DIM:2 * GDN_HEAD_DIM].astype(BF16)
        yield

    def store_mix(c, cols, val):
        if multi_seq:
            mix_ref[c * lv:(c + 1) * lv, cols] = val[0:lv].astype(mix_ref.dtype)
        else:
            mix_ref[c * q:(c + 1) * q, cols] = val.astype(mix_ref.dtype)

    def recurrences(items):
        for c in items:
            rows = slice(c * q, (c + 1) * q)
            last = (c + 1) * q - 1
            s = slot(c)
            for g in range(SSD_GROUPS):
                gcols = slice(g * gw, (g + 1) * gw)
                c_g = xbc_c[rows, SSD_WIDTH + (SSD_GROUPS + g) * SSD_STATE:
                            SSD_WIDTH + (SSD_GROUPS + g + 1) * SSD_STATE]
                state = st_s[s, :, gcols]
                y_g = y_s[rows, gcols] + _bdot(c_g, state) * pe_s[rows, gcols]
                st_s[s, :, gcols] = state * pe_s[last:last + 1, gcols] + jnp.dot(
                    bgt_s[c * SSD_GROUPS + g], xcd_s[rows, gcols], preferred_element_type=F32)
                y_g = y_g * gate(zs_v[rows, gcols])
                ms = jnp.mean(y_g * y_g, axis=-1, keepdims=True)
                y_g = y_g * lax.rsqrt(ms + NORM_EPS) * snw_ref[:, gcols]
                store_mix(c, gcols, y_g)
            yield
        pairs = [(c, h) for c in items for h in range(GDN_HEADS)]
        ws = {}
        for (c, h) in pairs:
            ws[c, h] = jnp.dot(wq_s[c * GDN_HEADS + h], st_g[slot(c), h].astype(BF16),
                               preferred_element_type=F32)
        yield
        vn = {(c, h): (u_s[c * GDN_HEADS + h] - ws[c, h][0:q]).astype(BF16) for (c, h) in pairs}
        for (c, h) in pairs:
            i = c * GDN_HEADS + h
            la = _SM_A + h
            last = (c + 1) * q - 1
            s = slot(c)
            o = ws[c, h][q:2 * q] + jnp.dot(at_s[i], vn[c, h], preferred_element_type=F32)
            st_g[s, h] = st_g[s, h] * ecum[last:last + 1, la:la + 1] + jnp.dot(
                kdt_s[i], vn[c, h], preferred_element_type=F32)
            hc = slice(h * GDN_HEAD_DIM, (h + 1) * GDN_HEAD_DIM)
            ms = jnp.mean(o * o, axis=-1, keepdims=True)
            o = o * lax.rsqrt(ms + NORM_EPS) * gnw_ref[...] * gate(zg_v[c * q:(c + 1) * q, hc])
            store_mix(c, slice(SSD_WIDTH + h * GDN_HEAD_DIM, SSD_WIDTH + (h + 1) * GDN_HEAD_DIM), o)
        yield

    all_ch = [(c, h) for c in range(nq) for h in range(GDN_HEADS)]
    groups = [all_ch[i:i + chains] for i in range(0, len(all_ch), chains)]
    batches = [list(range(nq))] if multi_seq else [[c] for c in range(nq)]
    assert chains % GDN_HEADS == 0
    pending = list(batches)
    done_items = set()
    if multi_seq:
        for _ in ssd_diag(range(nq)):
            pass
    for grp in groups:
        runnable = [b for b in pending if set(b) <= done_items]
        for b in runnable:
            pending.remove(b)
        others = [_chain_gens([recurrences(b) for b in runnable])] if runnable else []
        if not multi_seq:
            others.append(ssd_diag(sorted({c for c, _ in grp})))
        _run_interleaved(gdn_independent(grp), *others)
        done_items |= {c for c, _ in grp}
    for b in pending:
        for _ in recurrences(b):
            pass

    @pl.when(c_idx == nc - 1)
    def _store_state():
        for s in range(n_slots):
            for j in range(SSD_HEADS // 2):
                blk = st_s[s, :, pair * j:pair * (j + 1)].T
                nhs_ref[0, s, 2 * j:2 * j + 2] = blk.reshape(2, SSD_HEAD_DIM, SSD_STATE)
        nsg_ref[0] = st_g[...]


_STATE_TAILS = (
    (CONV_K - 1, SSD_CONV_DIM),
    (SSD_HEADS, SSD_HEAD_DIM, SSD_STATE),
    (CONV_K - 1, GDN_CONV_DIM),
    (GDN_HEADS, GDN_HEAD_DIM, GDN_HEAD_DIM),
)


_STATE_NAMES = ("cs", "hs", "cg", "sg")


def _mixer_call(zs, xbc, zg, qkv, sm, B, L, states_in, prev_out, p, q, nq, multi_seq, layer, chains, do_conv):
    assert zs.shape[0] == B * L
    rows = nq * q
    assert q % min(INV_BLOCK, q) == 0 and q & (q - 1) == 0
    zero_init = states_in is None
    if multi_seq:
        assert CONV_K - 1 <= L <= q and B % nq == 0 and do_conv
        nc, lv, bb = 1, L, nq
        tok = lambda wd: pl.BlockSpec((nq * L, wd), lambda b, c: (b, 0))
    else:
        assert L % rows == 0
        nc, lv, bb = L // rows, rows, 1
        tok = lambda wd: pl.BlockSpec((rows, wd), lambda b, c: (b * nc + c, 0))
    n_slots = bb
    per_b = lambda shape: pl.BlockSpec((1, bb) + shape, lambda b, c: (layer, b) + (0,) * len(shape))
    const = lambda shape: pl.BlockSpec(shape, lambda b, c: (0,) * len(shape))
    nch = nq * GDN_HEADS
    kept = [k for k in range(4) if do_conv or k in (1, 3)]

    names = ["zs", "xbc", "zg", "qkv", "sm"]
    in_specs = [tok(SSD_WIDTH), tok(SSD_CONV_DIM), tok(GDN_WIDTH), tok(GDN_CONV_DIM), tok(SMALL)]
    args = [zs, xbc, zg, qkv, sm]
    if not zero_init:
        for k in kept:
            names.append(_STATE_NAMES[k])
            in_specs.append(per_b(_STATE_TAILS[k]))
            args.append(states_in[k])
    consts = [("bias", (1, SMALL)), ("coef", (1, SMALL)), ("dexp", (1, SSD_WIDTH)), ("snw", (1, SSD_WIDTH)),
              ("gnw", (1, GDN_HEAD_DIM))]
    if do_conv:
        consts += [("scw", (CONV_K, SSD_CONV_DIM)), ("scb", (1, SSD_CONV_DIM)), ("gcw", (CONV_K, GDN_CONV_DIM))]
    for nm, shape in consts:
        names.append(nm)
        in_specs.append(const(shape))
        args.append(p[nm])
    aliases = {}
    if prev_out is not None:
        for k, a in enumerate(prev_out):
            names.append("alias%d" % k)
            aliases[len(args)] = 1 + k
            in_specs.append(pl.BlockSpec(memory_space=pl.ANY))
            args.append(a)
    names += ["mix"] + ["n" + _STATE_NAMES[k] for k in kept]
    out_specs = [tok(MIX_WIDTH)] + [per_b(_STATE_TAILS[k]) for k in kept]
    out_shape = [jax.ShapeDtypeStruct((B * L, MIX_WIDTH), BF16)] + [
        jax.ShapeDtypeStruct((DEPTH, B) + _STATE_TAILS[k], F32) for k in kept]
    scratch = [
        ("st_s", pltpu.VMEM((n_slots, SSD_STATE, SSD_WIDTH), F32)),
        ("st_g", pltpu.VMEM((n_slots, GDN_HEADS, GDN_HEAD_DIM, GDN_HEAD_DIM), F32)),
        ("y_s", pltpu.VMEM((rows, SSD_WIDTH), F32)),
        ("pe_s", pltpu.VMEM((rows, SSD_WIDTH), F32)),
        ("xcd_s", pltpu.VMEM((rows, SSD_WIDTH), BF16)),
        ("bgt_s", pltpu.VMEM((nq * SSD_GROUPS, SSD_STATE, q), BF16)),
        ("wq_s", pltpu.VMEM((nch, 2 * q, GDN_HEAD_DIM), BF16)),
        ("u_s", pltpu.VMEM((nch, q, GDN_HEAD_DIM), F32)),
        ("at_s", pltpu.VMEM((nch, q, q), BF16)),
        ("kdt_s", pltpu.VMEM((nch, GDN_HEAD_DIM, q), BF16)),
    ]
    if do_conv:
        xp_rows = nq * (q + SUBLANES) if multi_seq else rows + SUBLANES
        scratch += [
            ("xps", pltpu.VMEM((xp_rows, SSD_CONV_DIM), F32)),
            ("xpg", pltpu.VMEM((xp_rows, GDN_CONV_DIM), F32)),
            ("xbc_c", pltpu.VMEM((rows, SSD_CONV_DIM), F32)),
            ("qkv_c", pltpu.VMEM((rows, GDN_CONV_DIM), F32)),
        ]
    if multi_seq:
        scratch += [
            ("pzs", pltpu.VMEM((rows, SSD_WIDTH), F32)),
            ("pzg", pltpu.VMEM((rows, GDN_WIDTH), F32)),
            ("psm", pltpu.VMEM((rows, SMALL), F32)),
        ]
    names += [nm for nm, _ in scratch]
    return pl.pallas_call(
        functools.partial(_mixer_kernel, names=tuple(names), q=q, nq=nq, lv=lv, nc=nc, multi_seq=multi_seq,
                          zero_init=zero_init, do_conv=do_conv, chains=chains),
        grid=(B // bb, nc),
        in_specs=in_specs,
        out_specs=out_specs,
        out_shape=out_shape,
        input_output_aliases=aliases,
        scratch_shapes=[s for _, s in scratch],
        compiler_params=pltpu.CompilerParams(
            dimension_semantics=("arbitrary", "arbitrary"), vmem_limit_bytes=VMEM_LIMIT),
        name="mixer",
    )(*args)


def _prep_layer(l, norm_w, w_in, ssd_conv_w, ssd_conv_b, ssd_dt_bias, ssd_a_log, ssd_d, ssd_norm_w,
                gdn_conv_w, gdn_dt_bias, gdn_a_log, gdn_norm_w, w_out):
    wi = w_in[l]
    small = jnp.concatenate(
        [wi[:, _OFF_DT:_OFF_DT + SSD_HEADS], wi[:, _OFF_A:_OFF_A + GDN_HEADS], wi[:, _OFF_B:_OFF_B + GDN_HEADS],
         jnp.zeros((D_MODEL, SMALL - SSD_HEADS - 2 * GDN_HEADS), F32)], axis=1)
    sm_hi = small.astype(BF16)
    sm_lo = (small - sm_hi.astype(F32)).astype(BF16)
    pad = jnp.zeros((SMALL - SSD_HEADS - GDN_HEADS,), F32)
    return {
        "norm_w": norm_w[l].reshape(1, D_MODEL),
        "zs": wi[:, _OFF_ZS:_OFF_ZS + SSD_WIDTH].astype(BF16),
        "xbc": wi[:, _OFF_XBC:_OFF_XBC + SSD_CONV_DIM].astype(BF16),
        "zg": wi[:, _OFF_ZG:_OFF_ZG + GDN_WIDTH].astype(BF16),
        "qkv": wi[:, _OFF_QKV:_OFF_QKV + GDN_CONV_DIM].astype(BF16),
        "sm_hi": sm_hi,
        "sm_lo": sm_lo,
        "scw": ssd_conv_w[l],
        "scb": ssd_conv_b[l].reshape(1, SSD_CONV_DIM),
        "gcw": gdn_conv_w[l],
        "bias": jnp.concatenate([ssd_dt_bias[l], gdn_dt_bias[l], pad]).reshape(1, SMALL),
        "coef": jnp.concatenate([-jnp.exp(ssd_a_log[l]), -jnp.exp(gdn_a_log[l]), pad]).reshape(1, SMALL),
        "dexp": jnp.repeat(ssd_d[l], SSD_HEAD_DIM).reshape(1, SSD_WIDTH),
        "snw": ssd_norm_w[l].reshape(1, SSD_WIDTH),
        "gnw": gdn_norm_w[l].reshape(1, GDN_HEAD_DIM),
        "w_out": w_out[l].astype(BF16),
    }


def _trunk(x, mod, states, params, final_w, q, nq, tm, tm_out, chains, act_dtype):
    B, L, _ = x.shape
    T = B * L
    per_row = L < SUBLANES
    steps_per_group = max(L // tm, 1)
    x2d = x.reshape(T, D_MODEL)
    conv_in_proj = states is None and not per_row
    new_states = None
    conv_states = ([], [])
    for l in range(DEPTH):
        p = params[l]
        shift, scale, gate = (mod[l, :, i * D_MODEL:(i + 1) * D_MODEL] for i in range(3))
        if per_row:
            rep = lambda a: jnp.repeat(a, L, axis=0).reshape(1, T, D_MODEL)
        else:
            rep = lambda a: a.reshape(B, 1, D_MODEL)
        zs, xbc, zg, qkv, sm, *cst = _inproj_call(x2d, rep(scale), rep(shift), p["norm_w"], p, tm, per_row,
                                                  steps_per_group, conv_in_proj, act_dtype)
        mix, *new_states = _mixer_call(zs, xbc, zg, qkv, sm, B, L, states, new_states, p,
                                       q, nq, per_row, l, chains, not conv_in_proj)
        x2d = _outproj_call(x2d, mix, rep(gate), p["w_out"], final_w, tm_out, per_row,
                            max(L // tm_out, 1), final_norm=(l == DEPTH - 1))
        for lst, s in zip(conv_states, cst):
            lst.append(s)
    if conv_in_proj:
        new_states = [jnp.stack(conv_states[0]), new_states[0], jnp.stack(conv_states[1]), new_states[1]]
    return x2d.reshape(B, L, D_MODEL), new_states


def kernel(x_prompt, x_sample, state_ssd_conv, state_ssm, state_gdn_conv, state_gdn, c_prompt, c_sample,
           norm_w, w_ada, b_ada, w_in, ssd_conv_w, ssd_conv_b, ssd_dt_bias, ssd_a_log, ssd_d, ssd_norm_w,
           gdn_conv_w, gdn_dt_bias, gdn_a_log, gdn_norm_w, w_out, final_norm_w):
    bp = x_prompt.shape[0]
    params = [_prep_layer(l, norm_w, w_in, ssd_conv_w, ssd_conv_b, ssd_dt_bias, ssd_a_log, ssd_d, ssd_norm_w,
                          gdn_conv_w, gdn_dt_bias, gdn_a_log, gdn_norm_w, w_out) for l in range(DEPTH)]
    final_w = final_norm_w.reshape(1, D_MODEL)
    mod = _ada_call(jnp.concatenate([c_prompt, c_sample], axis=0), w_ada, b_ada)
    y_p, sp = _trunk(x_prompt, mod[:, :bp], None, params, final_w, q=64, nq=4, tm=512, tm_out=1024, chains=16,
                     act_dtype=BF16)
    ts = x_sample.shape[0] * x_sample.shape[1]
    y_s, ss = _trunk(x_sample, mod[:, bp:], (state_ssd_conv, state_ssm, state_gdn_conv, state_gdn),
                     params, final_w, q=8, nq=8, tm=ts, tm_out=ts, chains=16, act_dtype=F32)
    return (y_p, y_s, sp[0], sp[1], sp[2], sp[3], ss[0], ss[1], ss[2], ss[3])
```
